```python
import math
import jax, jax.numpy as jnp
from jax import lax
import numpy as np

D_MODEL = 1024
BATCH = 8
SEQ = 8192
DEPTH = 4

POOL_WINDOWS = (2, 4, 8, 16)
N_POOL_GROUPS = len(POOL_WINDOWS)
POOL_GROUP_DIM = D_MODEL // 8
POOL_W = N_POOL_GROUPS * POOL_GROUP_DIM
FOX_HEADS = 8
FOX_HEAD_DIM = 64
FOX_W = FOX_HEADS * FOX_HEAD_DIM
Q_BLOCK = 128
OFF_POOL = 0
OFF_Q = OFF_POOL + POOL_W
OFF_K = OFF_Q + FOX_W
OFF_V = OFF_K + FOX_W
OFF_F = OFF_V + FOX_W
OFF_GP = OFF_F + FOX_HEADS
OFF_GF = OFF_GP + D_MODEL
IN_W = OFF_GF + D_MODEL
N_MEM = 256
X_HEADS = 4
X_HEAD_DIM = 128
X_W = X_HEADS * X_HEAD_DIM
D_FF = 2816
CONV_WIDTH = 3
RMS_EPS = 1e-6

kernel_name = "hybrid_pool_fox_memxattn_convffn"


def rms_norm(x, g):
    xf = x.astype(jnp.float32)
    y = xf * lax.rsqrt(jnp.mean(xf * xf, axis=-1, keepdims=True) + RMS_EPS)
    return (y * g.astype(jnp.float32)).astype(x.dtype)


def shift_right(z, n):
    return jnp.pad(z, ((0, 0), (n, 0), (0, 0)))[:, : z.shape[1]]


def pool_mixer(u, pool_w, pool_scale):
    B, S, _ = u.shape
    uf = u.astype(jnp.float32)
    cs = jnp.cumsum(uf, axis=1)
    t = jnp.arange(S)
    outs = []
    for g, w in enumerate(POOL_WINDOWS):
        sl = slice(g * POOL_GROUP_DIM, (g + 1) * POOL_GROUP_DIM)
        csg = cs[..., sl]
        cnt = jnp.minimum(t + 1, w).astype(jnp.float32)[None, :, None]
        outs.append((csg - shift_right(csg, w)) / cnt - uf[..., sl])
    pooled = jnp.stack(outs, axis=2).astype(u.dtype)
    mixed = jnp.einsum("bsgc,gcd->bsgd", pooled, pool_w).reshape(B, S, POOL_W)
    return mixed * pool_scale


def fox_attention(q, k, v, log_f):
    B, S, H, Dh = q.shape
    nb = S // Q_BLOCK
    scale = 1.0 / math.sqrt(Dh)
    cT = jnp.cumsum(log_f.astype(jnp.float32), axis=1).transpose(0, 2, 1)
    qb = q.reshape(B, nb, Q_BLOCK, H, Dh).transpose(1, 0, 2, 3, 4)
    cb = cT.reshape(B, H, nb, Q_BLOCK).transpose(2, 0, 1, 3)
    kpos = jnp.arange(S)

    def one_block(args):
        qi, ci, bi = args
        s = jnp.einsum("bqhd,bkhd->bhqk", qi, k, preferred_element_type=jnp.float32) * scale
        s = s + ci[..., :, None] - cT[:, :, None, :]
        qpos = bi * Q_BLOCK + jnp.arange(Q_BLOCK)
        s = jnp.where(kpos[None, :] <= qpos[:, None], s, -jnp.inf)
        p = jax.nn.softmax(s, axis=-1)
        return jnp.einsum("bhqk,bkhd->bqhd", p.astype(v.dtype), v)

    out = lax.map(one_block, (qb, cb, jnp.arange(nb)))
    return out.transpose(1, 0, 2, 3, 4).reshape(B, S, H * Dh)


def mem_attention(h, mem_n, w_xq, w_xkv, w_xo):
    B, S, _ = h.shape
    M = mem_n.shape[1]
    q = (h @ w_xq).reshape(B, S, X_HEADS, X_HEAD_DIM)
    kv = mem_n @ w_xkv
    k = kv[..., :X_W].reshape(B, M, X_HEADS, X_HEAD_DIM)
    v = kv[..., X_W:].reshape(B, M, X_HEADS, X_HEAD_DIM)
    s = jnp.einsum("bqhd,bmhd->bhqm", q, k, preferred_element_type=jnp.float32) / math.sqrt(X_HEAD_DIM)
    p = jax.nn.softmax(s, axis=-1)
    o = jnp.einsum("bhqm,bmhd->bqhd", p.astype(v.dtype), v).reshape(B, S, X_W)
    return o @ w_xo


def conv_ffn(h, w_up, conv_w, conv_b, w_down):
    z = h @ w_up
    zc = conv_w[2] * z + conv_w[1] * shift_right(z, 1) + conv_w[0] * shift_right(z, 2) + conv_b
    g, u = zc[..., :D_FF], zc[..., D_FF:]
    return (jax.nn.gelu(g, approximate=True) * u) @ w_down


def _fwd_setup_inputs(seed: int = 0) -> dict:
    key = jax.random.key(seed)
    ks = jax.random.split(key, 32)
    L, D = DEPTH, D_MODEL
    f32 = jnp.float32

    def nrm(k, shape, fan_in):
        return jax.random.normal(k, shape, f32) * (fan_in ** -0.5)

    def gain(k):
        return 1.0 + 0.05 * jax.random.normal(k, (L, D), f32)

    b_forget = (jnp.linspace(1.0, 5.0, FOX_HEADS, dtype=f32)[None, :]
                + 0.1 * jax.random.normal(ks[5], (L, FOX_HEADS), f32))
    return {
        "x": jax.random.normal(ks[0], (BATCH, SEQ, D), f32),
        "mem": jax.random.normal(ks[1], (BATCH, N_MEM, D), f32),
        "mix_pre_g": gain(ks[2]),
        "mix_post_g": gain(ks[3]),
        "w_in": nrm(ks[4], (L, D, IN_W), D),
        "b_forget": b_forget,
        "pool_w": nrm(ks[6], (L, N_POOL_GROUPS, POOL_GROUP_DIM, POOL_GROUP_DIM), POOL_GROUP_DIM),
        "pool_scale": 1.0 + 0.1 * jax.random.normal(ks[7], (L, POOL_W), f32),
        "w_pool_br": nrm(ks[8], (L, POOL_W, D), POOL_W),
        "w_fox_br": nrm(ks[9], (L, FOX_W, D), FOX_W),
        "w_mix_out": nrm(ks[10], (L, D, D), D),
        "xa_pre_g": gain(ks[11]),
        "xa_post_g": gain(ks[12]),
        "mem_g": gain(ks[13]),
        "w_xq": nrm(ks[14], (L, D, X_W), D),
        "w_xkv": nrm(ks[15], (L, D, 2 * X_W), D),
        "w_xo": nrm(ks[16], (L, X_W, D), X_W),
        "ffn_pre_g": gain(ks[17]),
        "ffn_post_g": gain(ks[18]),
        "w_up": nrm(ks[19], (L, D, 2 * D_FF), D),
        "conv_w": nrm(ks[20], (L, CONV_WIDTH, 2 * D_FF), CONV_WIDTH),
        "conv_b": 0.01 * jax.random.normal(ks[21], (L, 2 * D_FF), f32),
        "w_down": nrm(ks[22], (L, D_FF, D), D_FF),
    }


def _fwd_reference(x, mem, mix_pre_g, mix_post_g, w_in, b_forget, pool_w, pool_scale,
              w_pool_br, w_fox_br, w_mix_out, xa_pre_g, xa_post_g, mem_g, w_xq, w_xkv,
              w_xo, ffn_pre_g, ffn_post_g, w_up, conv_w, conv_b, w_down):
    B, S, _ = x.shape
    for l in range(DEPTH):
        h = rms_norm(x, mix_pre_g[l])
        z = h @ w_in[l]
        u_pool = z[..., OFF_POOL:OFF_Q]
        q = z[..., OFF_Q:OFF_K].reshape(B, S, FOX_HEADS, FOX_HEAD_DIM)
        k = z[..., OFF_K:OFF_V].reshape(B, S, FOX_HEADS, FOX_HEAD_DIM)
        v = z[..., OFF_V:OFF_F].reshape(B, S, FOX_HEADS, FOX_HEAD_DIM)
        log_f = jax.nn.log_sigmoid((z[..., OFF_F:OFF_GP] + b_forget[l]).astype(jnp.float32))
        gate_pool = jax.nn.sigmoid(z[..., OFF_GP:OFF_GF])
        gate_fox = jax.nn.sigmoid(z[..., OFF_GF:])
        y_pool = pool_mixer(u_pool, pool_w[l], pool_scale[l]) @ w_pool_br[l]
        y_fox = fox_attention(q, k, v, log_f) @ w_fox_br[l]
        merged = gate_pool * y_pool + gate_fox * y_fox
        x = x + rms_norm(merged @ w_mix_out[l], mix_post_g[l])
        h = rms_norm(x, xa_pre_g[l])
        mem_n = rms_norm(mem, mem_g[l])
        x = x + rms_norm(mem_attention(h, mem_n, w_xq[l], w_xkv[l], w_xo[l]), xa_post_g[l])
        h = rms_norm(x, ffn_pre_g[l])
        x = x + rms_norm(conv_ffn(h, w_up[l], conv_w[l], conv_b[l], w_down[l]), ffn_post_g[l])
    return x


import jax as _jax
import jax.numpy as _jnp

TWIN_FORMAT = 'train_step'
FWD_PARAMS = ['x', 'mem', 'mix_pre_g', 'mix_post_g', 'w_in', 'b_forget', 'pool_w', 'pool_scale', 'w_pool_br', 'w_fox_br', 'w_mix_out', 'xa_pre_g', 'xa_post_g', 'mem_g', 'w_xq', 'w_xkv', 'w_xo', 'ffn_pre_g', 'ffn_post_g', 'w_up', 'conv_w', 'conv_b', 'w_down']
TWIN_WEIGHTS = ['mix_pre_g', 'mix_post_g', 'w_in', 'b_forget', 'pool_w', 'pool_scale', 'w_pool_br', 'w_fox_br', 'w_mix_out', 'xa_pre_g', 'xa_post_g', 'mem_g', 'w_xq', 'w_xkv', 'w_xo', 'ffn_pre_g', 'ffn_post_g', 'w_up', 'conv_w', 'conv_b', 'w_down']
TWIN_DIFF_INPUT = 'x'
TWIN_INPUTS = ['x', 'mem', 'mix_pre_g', 'mix_post_g', 'w_in', 'b_forget', 'pool_w', 'pool_scale', 'w_pool_br', 'w_fox_br', 'w_mix_out', 'xa_pre_g', 'xa_post_g', 'mem_g', 'w_xq', 'w_xkv', 'w_xo', 'ffn_pre_g', 'ffn_post_g', 'w_up', 'conv_w', 'conv_b', 'w_down', 'loss_target', 'm_mix_pre_g', 'm_mix_post_g', 'm_w_in', 'm_b_forget', 'm_pool_w', 'm_pool_scale', 'm_w_pool_br', 'm_w_fox_br', 'm_w_mix_out', 'm_xa_pre_g', 'm_xa_post_g', 'm_mem_g', 'm_w_xq', 'm_w_xkv', 'm_w_xo', 'm_ffn_pre_g', 'm_ffn_post_g', 'm_w_up', 'm_conv_w', 'm_conv_b', 'm_w_down', 'v_mix_pre_g', 'v_mix_post_g', 'v_w_in', 'v_b_forget', 'v_pool_w', 'v_pool_scale', 'v_w_pool_br', 'v_w_fox_br', 'v_w_mix_out', 'v_xa_pre_g', 'v_xa_post_g', 'v_mem_g', 'v_w_xq', 'v_w_xkv', 'v_w_xo', 'v_ffn_pre_g', 'v_ffn_post_g', 'v_w_up', 'v_conv_w', 'v_conv_b', 'v_w_down']
TWIN_OUTPUTS = ['loss', 'grad_x', 'grad_mix_pre_g', 'grad_mix_post_g', 'grad_w_in', 'grad_b_forget', 'grad_pool_w', 'grad_pool_scale', 'grad_w_pool_br', 'grad_w_fox_br', 'grad_w_mix_out', 'grad_xa_pre_g', 'grad_xa_post_g', 'grad_mem_g', 'grad_w_xq', 'grad_w_xkv', 'grad_w_xo', 'grad_ffn_pre_g', 'grad_ffn_post_g', 'grad_w_up', 'grad_conv_w', 'grad_conv_b', 'grad_w_down', 'delta_mix_pre_g', 'delta_mix_post_g', 'delta_w_in', 'delta_b_forget', 'delta_pool_w', 'delta_pool_scale', 'delta_w_pool_br', 'delta_w_fox_br', 'delta_w_mix_out', 'delta_xa_pre_g', 'delta_xa_post_g', 'delta_mem_g', 'delta_w_xq', 'delta_w_xkv', 'delta_w_xo', 'delta_ffn_pre_g', 'delta_ffn_post_g', 'delta_w_up', 'delta_conv_w', 'delta_conv_b', 'delta_w_down', 'new_m_mix_pre_g', 'new_m_mix_post_g', 'new_m_w_in', 'new_m_b_forget', 'new_m_pool_w', 'new_m_pool_scale', 'new_m_w_pool_br', 'new_m_w_fox_br', 'new_m_w_mix_out', 'new_m_xa_pre_g', 'new_m_xa_post_g', 'new_m_mem_g', 'new_m_w_xq', 'new_m_w_xkv', 'new_m_w_xo', 'new_m_ffn_pre_g', 'new_m_ffn_post_g', 'new_m_w_up', 'new_m_conv_w', 'new_m_conv_b', 'new_m_w_down', 'new_v_mix_pre_g', 'new_v_mix_post_g', 'new_v_w_in', 'new_v_b_forget', 'new_v_pool_w', 'new_v_pool_scale', 'new_v_w_pool_br', 'new_v_w_fox_br', 'new_v_w_mix_out', 'new_v_xa_pre_g', 'new_v_xa_post_g', 'new_v_mem_g', 'new_v_w_xq', 'new_v_w_xkv', 'new_v_w_xo', 'new_v_ffn_pre_g', 'new_v_ffn_post_g', 'new_v_w_up', 'new_v_conv_w', 'new_v_conv_b', 'new_v_w_down']
TWIN_LEAF_KINDS = {'loss': 'loss', 'grad_x': 'grad_x', 'grad_mix_pre_g': 'grad_w', 'grad_mix_post_g': 'grad_w', 'grad_w_in': 'grad_w', 'grad_b_forget': 'grad_w', 'grad_pool_w': 'grad_w', 'grad_pool_scale': 'grad_w', 'grad_w_pool_br': 'grad_w', 'grad_w_fox_br': 'grad_w', 'grad_w_mix_out': 'grad_w', 'grad_xa_pre_g': 'grad_w', 'grad_xa_post_g': 'grad_w', 'grad_mem_g': 'grad_w', 'grad_w_xq': 'grad_w', 'grad_w_xkv': 'grad_w', 'grad_w_xo': 'grad_w', 'grad_ffn_pre_g': 'grad_w', 'grad_ffn_post_g': 'grad_w', 'grad_w_up': 'grad_w', 'grad_conv_w': 'grad_w', 'grad_conv_b': 'grad_w', 'grad_w_down': 'grad_w', 'delta_mix_pre_g': 'delta_w', 'delta_mix_post_g': 'delta_w', 'delta_w_in': 'delta_w', 'delta_b_forget': 'delta_w', 'delta_pool_w': 'delta_w', 'delta_pool_scale': 'delta_w', 'delta_w_pool_br': 'delta_w', 'delta_w_fox_br': 'delta_w', 'delta_w_mix_out': 'delta_w', 'delta_xa_pre_g': 'delta_w', 'delta_xa_post_g': 'delta_w', 'delta_mem_g': 'delta_w', 'delta_w_xq': 'delta_w', 'delta_w_xkv': 'delta_w', 'delta_w_xo': 'delta_w', 'delta_ffn_pre_g': 'delta_w', 'delta_ffn_post_g': 'delta_w', 'delta_w_up': 'delta_w', 'delta_conv_w': 'delta_w', 'delta_conv_b': 'delta_w', 'delta_w_down': 'delta_w', 'new_m_mix_pre_g': 'new_m', 'new_m_mix_post_g': 'new_m', 'new_m_w_in': 'new_m', 'new_m_b_forget': 'new_m', 'new_m_pool_w': 'new_m', 'new_m_pool_scale': 'new_m', 'new_m_w_pool_br': 'new_m', 'new_m_w_fox_br': 'new_m', 'new_m_w_mix_out': 'new_m', 'new_m_xa_pre_g': 'new_m', 'new_m_xa_post_g': 'new_m', 'new_m_mem_g': 'new_m', 'new_m_w_xq': 'new_m', 'new_m_w_xkv': 'new_m', 'new_m_w_xo': 'new_m', 'new_m_ffn_pre_g': 'new_m', 'new_m_ffn_post_g': 'new_m', 'new_m_w_up': 'new_m', 'new_m_conv_w': 'new_m', 'new_m_conv_b': 'new_m', 'new_m_w_down': 'new_m', 'new_v_mix_pre_g': 'new_v', 'new_v_mix_post_g': 'new_v', 'new_v_w_in': 'new_v', 'new_v_b_forget': 'new_v', 'new_v_pool_w': 'new_v', 'new_v_pool_scale': 'new_v', 'new_v_w_pool_br': 'new_v', 'new_v_w_fox_br': 'new_v', 'new_v_w_mix_out': 'new_v', 'new_v_xa_pre_g': 'new_v', 'new_v_xa_post_g': 'new_v', 'new_v_mem_g': 'new_v', 'new_v_w_xq': 'new_v', 'new_v_w_xkv': 'new_v', 'new_v_w_xo': 'new_v', 'new_v_ffn_pre_g': 'new_v', 'new_v_ffn_post_g': 'new_v', 'new_v_w_up': 'new_v', 'new_v_conv_w': 'new_v', 'new_v_conv_b': 'new_v', 'new_v_w_down': 'new_v'}


def _forward(args):
    return _fwd_reference(*[args[k] for k in FWD_PARAMS])


def _output_shape():
    out = _jax.eval_shape(lambda: _forward(_fwd_setup_inputs(0)))
    return out.shape, out.dtype

N_MICROBATCH = 1
ADAM_LR = 0.001
ADAM_B1 = 0.9
ADAM_B2 = 0.999
ADAM_EPS = 1e-08
ADAM_WD = 0.01
ADAM_STEP = 10
PER_EXAMPLE_BATCH_AXIS = {'x': 0, 'mem': 0, 'loss_target': 0}
SHARED_INPUTS = []
_WEIGHT_DTYPES = {'mix_pre_g': _jnp.float32, 'mix_post_g': _jnp.float32, 'w_in': _jnp.float32, 'b_forget': _jnp.float32, 'pool_w': _jnp.float32, 'pool_scale': _jnp.float32, 'w_pool_br': _jnp.float32, 'w_fox_br': _jnp.float32, 'w_mix_out': _jnp.float32, 'xa_pre_g': _jnp.float32, 'xa_post_g': _jnp.float32, 'mem_g': _jnp.float32, 'w_xq': _jnp.float32, 'w_xkv': _jnp.float32, 'w_xo': _jnp.float32, 'ffn_pre_g': _jnp.float32, 'ffn_post_g': _jnp.float32, 'w_up': _jnp.float32, 'conv_w': _jnp.float32, 'conv_b': _jnp.float32, 'w_down': _jnp.float32}
MOMENT_SCALE = {'mix_pre_g': 1.131890e+01, 'mix_post_g': 6.397783e+01, 'w_in': 5.546828e+00, 'b_forget': 4.291644e+00, 'pool_w': 4.521364e+00, 'pool_scale': 4.829946e+00, 'w_pool_br': 3.367712e+00, 'w_fox_br': 1.344993e+01, 'w_mix_out': 1.174261e+01, 'xa_pre_g': 6.738371e+00, 'xa_post_g': 7.389901e+01, 'mem_g': 2.895981e+01, 'w_xq': 9.191586e+00, 'w_xkv': 2.914777e+01, 'w_xo': 3.004747e+01, 'ffn_pre_g': 8.690749e+00, 'ffn_post_g': 6.370698e+01, 'w_up': 3.635007e+00, 'conv_w': 4.359401e+00, 'conv_b': 1.298404e+01, 'w_down': 8.014738e+00}


def _to_microbatches(a, axis):
    t = _jnp.moveaxis(a, axis, 0)
    t = t.reshape((N_MICROBATCH, t.shape[0] // N_MICROBATCH) + t.shape[1:])
    return _jnp.moveaxis(t, 1, axis + 1)


def setup_inputs(seed: int = 0) -> dict:
    inp = _fwd_setup_inputs(seed)
    key = _jax.random.fold_in(_jax.random.key(seed), 7919)
    shape, _ = _output_shape()
    out = dict(inp)
    out["loss_target"] = _jax.random.normal(_jax.random.fold_in(key, 0), shape, _jnp.float32)
    for i, name in enumerate(TWIN_WEIGHTS):
        w = inp[name].astype(_jnp.float32)
        if MOMENT_SCALE is None:
            s = _jnp.sqrt(_jnp.mean(_jnp.square(w)) + 1e-30)
        else:
            s = MOMENT_SCALE[name]
        km, kv = _jax.random.split(_jax.random.fold_in(key, i + 1))
        out[name] = w
        out["m_" + name] = s * _jax.random.normal(km, w.shape, _jnp.float32)
        out["v_" + name] = (s * s) * _jax.random.uniform(kv, w.shape, _jnp.float32, 0.5, 1.5)
    if N_MICROBATCH > 1:
        for name, axis in PER_EXAMPLE_BATCH_AXIS.items():
            out[name] = _to_microbatches(out[name], axis)
    return {'x': out['x'], 'mem': out['mem'], 'mix_pre_g': out['mix_pre_g'], 'mix_post_g': out['mix_post_g'], 'w_in': out['w_in'], 'b_forget': out['b_forget'], 'pool_w': out['pool_w'], 'pool_scale': out['pool_scale'], 'w_pool_br': out['w_pool_br'], 'w_fox_br': out['w_fox_br'], 'w_mix_out': out['w_mix_out'], 'xa_pre_g': out['xa_pre_g'], 'xa_post_g': out['xa_post_g'], 'mem_g': out['mem_g'], 'w_xq': out['w_xq'], 'w_xkv': out['w_xkv'], 'w_xo': out['w_xo'], 'ffn_pre_g': out['ffn_pre_g'], 'ffn_post_g': out['ffn_post_g'], 'w_up': out['w_up'], 'conv_w': out['conv_w'], 'conv_b': out['conv_b'], 'w_down': out['w_down'], 'loss_target': out['loss_target'], 'm_mix_pre_g': out['m_mix_pre_g'], 'm_mix_post_g': out['m_mix_post_g'], 'm_w_in': out['m_w_in'], 'm_b_forget': out['m_b_forget'], 'm_pool_w': out['m_pool_w'], 'm_pool_scale': out['m_pool_scale'], 'm_w_pool_br': out['m_w_pool_br'], 'm_w_fox_br': out['m_w_fox_br'], 'm_w_mix_out': out['m_w_mix_out'], 'm_xa_pre_g': out['m_xa_pre_g'], 'm_xa_post_g': out['m_xa_post_g'], 'm_mem_g': out['m_mem_g'], 'm_w_xq': out['m_w_xq'], 'm_w_xkv': out['m_w_xkv'], 'm_w_xo': out['m_w_xo'], 'm_ffn_pre_g': out['m_ffn_pre_g'], 'm_ffn_post_g': out['m_ffn_post_g'], 'm_w_up': out['m_w_up'], 'm_conv_w': out['m_conv_w'], 'm_conv_b': out['m_conv_b'], 'm_w_down': out['m_w_down'], 'v_mix_pre_g': out['v_mix_pre_g'], 'v_mix_post_g': out['v_mix_post_g'], 'v_w_in': out['v_w_in'], 'v_b_forget': out['v_b_forget'], 'v_pool_w': out['v_pool_w'], 'v_pool_scale': out['v_pool_scale'], 'v_w_pool_br': out['v_w_pool_br'], 'v_w_fox_br': out['v_w_fox_br'], 'v_w_mix_out': out['v_w_mix_out'], 'v_xa_pre_g': out['v_xa_pre_g'], 'v_xa_post_g': out['v_xa_post_g'], 'v_mem_g': out['v_mem_g'], 'v_w_xq': out['v_w_xq'], 'v_w_xkv': out['v_w_xkv'], 'v_w_xo': out['v_w_xo'], 'v_ffn_pre_g': out['v_ffn_pre_g'], 'v_ffn_post_g': out['v_ffn_post_g'], 'v_w_up': out['v_w_up'], 'v_conv_w': out['v_conv_w'], 'v_conv_b': out['v_conv_b'], 'v_w_down': out['v_w_down']}


def _loss(weights, diff, rest, loss_target):
    with _jax.named_scope("forward"):
        args = {**rest, TWIN_DIFF_INPUT: diff, **{k: w.astype(_WEIGHT_DTYPES[k]) for k, w in weights.items()}}
        y = _forward(args)
    with _jax.named_scope("loss_head"):
        err = _jnp.square(y.astype(_jnp.float32) - loss_target)
        return 0.5 * _jnp.sum(_jnp.mean(err, axis=-1)) if err.ndim else 0.5 * err


def _adamw(w, g, m, v):
    m = ADAM_B1 * m + (1.0 - ADAM_B1) * g
    v = ADAM_B2 * v + (1.0 - ADAM_B2) * _jnp.square(g)
    m_hat = m / (1.0 - ADAM_B1 ** ADAM_STEP)
    v_hat = v / (1.0 - ADAM_B2 ** ADAM_STEP)
    delta = -ADAM_LR * (m_hat / (_jnp.sqrt(v_hat) + ADAM_EPS) + ADAM_WD * w)
    return delta, m, v


def reference(x, mem, mix_pre_g, mix_post_g, w_in, b_forget, pool_w, pool_scale, w_pool_br, w_fox_br, w_mix_out, xa_pre_g, xa_post_g, mem_g, w_xq, w_xkv, w_xo, ffn_pre_g, ffn_post_g, w_up, conv_w, conv_b, w_down, loss_target, m_mix_pre_g, m_mix_post_g, m_w_in, m_b_forget, m_pool_w, m_pool_scale, m_w_pool_br, m_w_fox_br, m_w_mix_out, m_xa_pre_g, m_xa_post_g, m_mem_g, m_w_xq, m_w_xkv, m_w_xo, m_ffn_pre_g, m_ffn_post_g, m_w_up, m_conv_w, m_conv_b, m_w_down, v_mix_pre_g, v_mix_post_g, v_w_in, v_b_forget, v_pool_w, v_pool_scale, v_w_pool_br, v_w_fox_br, v_w_mix_out, v_xa_pre_g, v_xa_post_g, v_mem_g, v_w_xq, v_w_xkv, v_w_xo, v_ffn_pre_g, v_ffn_post_g, v_w_up, v_conv_w, v_conv_b, v_w_down):
    given = dict(x=x, mem=mem, mix_pre_g=mix_pre_g, mix_post_g=mix_post_g, w_in=w_in, b_forget=b_forget, pool_w=pool_w, pool_scale=pool_scale, w_pool_br=w_pool_br, w_fox_br=w_fox_br, w_mix_out=w_mix_out, xa_pre_g=xa_pre_g, xa_post_g=xa_post_g, mem_g=mem_g, w_xq=w_xq, w_xkv=w_xkv, w_xo=w_xo, ffn_pre_g=ffn_pre_g, ffn_post_g=ffn_post_g, w_up=w_up, conv_w=conv_w, conv_b=conv_b, w_down=w_down, loss_target=loss_target, m_mix_pre_g=m_mix_pre_g, m_mix_post_g=m_mix_post_g, m_w_in=m_w_in, m_b_forget=m_b_forget, m_pool_w=m_pool_w, m_pool_scale=m_pool_scale, m_w_pool_br=m_w_pool_br, m_w_fox_br=m_w_fox_br, m_w_mix_out=m_w_mix_out, m_xa_pre_g=m_xa_pre_g, m_xa_post_g=m_xa_post_g, m_mem_g=m_mem_g, m_w_xq=m_w_xq, m_w_xkv=m_w_xkv, m_w_xo=m_w_xo, m_ffn_pre_g=m_ffn_pre_g, m_ffn_post_g=m_ffn_post_g, m_w_up=m_w_up, m_conv_w=m_conv_w, m_conv_b=m_conv_b, m_w_down=m_w_down, v_mix_pre_g=v_mix_pre_g, v_mix_post_g=v_mix_post_g, v_w_in=v_w_in, v_b_forget=v_b_forget, v_pool_w=v_pool_w, v_pool_scale=v_pool_scale, v_w_pool_br=v_w_pool_br, v_w_fox_br=v_w_fox_br, v_w_mix_out=v_w_mix_out, v_xa_pre_g=v_xa_pre_g, v_xa_post_g=v_xa_post_g, v_mem_g=v_mem_g, v_w_xq=v_w_xq, v_w_xkv=v_w_xkv, v_w_xo=v_w_xo, v_ffn_pre_g=v_ffn_pre_g, v_ffn_post_g=v_ffn_post_g, v_w_up=v_w_up, v_conv_w=v_conv_w, v_conv_b=v_conv_b, v_w_down=v_w_down)
    weights = {n: given[n] for n in TWIN_WEIGHTS}
    shared = {n: given[n] for n in SHARED_INPUTS}
    per_example = {n: given[n] for n in ['x', 'mem']}
    grad_fn = _jax.value_and_grad(_loss, argnums=(0, 1))

    def one_microbatch(ex, loss_target):
        ex = dict(ex)
        diff = ex.pop(TWIN_DIFF_INPUT)
        return grad_fn(weights, diff, {**shared, **ex}, loss_target)

    if N_MICROBATCH == 1:
        loss, (grad_w, grad_x) = one_microbatch(per_example, given["loss_target"])
    else:
        def body(carry, xs):
            loss_sum, grad_sum = carry
            l_k, (gw_k, gx_k) = one_microbatch(xs[0], xs[1])
            with _jax.named_scope("update"):
                return (loss_sum + l_k, _jax.tree.map(_jnp.add, grad_sum, gw_k)), gx_k

        init = (_jnp.zeros((), _jnp.float32), _jax.tree.map(_jnp.zeros_like, weights))
        (loss, grad_w), grad_x = _jax.lax.scan(body, init, (per_example, given["loss_target"]))
    with _jax.named_scope("update"):
        delta_w, new_m, new_v = {}, {}, {}
        for n in TWIN_WEIGHTS:
            delta_w[n], new_m[n], new_v[n] = _adamw(weights[n], grad_w[n], given["m_" + n], given["v_" + n])
    return (loss, grad_x, *[grad_w[n] for n in TWIN_WEIGHTS], *[delta_w[n] for n in TWIN_WEIGHTS],
            *[new_m[n] for n in TWIN_WEIGHTS], *[new_v[n] for n in TWIN_WEIGHTS])
```

```python
import functools
import math

import jax
import jax.numpy as jnp
from jax import lax
from jax.experimental import pallas as pl
from jax.experimental.pallas import tpu as pltpu

F32 = jnp.float32
BF16 = jnp.bfloat16
MESH = pl.DeviceIdType.MESH

RMS_EPS = 1e-6
POOL_WINDOWS = (2, 4, 8, 16)
POOL_GROUP = 128
POOL_W = 512
FOX_HEADS = 8
FOX_DH = 64
FOX_W = 512
X_HEADS = 4
X_DH = 128
X_W = 512
N_FGATE = 8
LANES = 128
HALO = 16

ADAM_LR = 0.001
ADAM_B1 = 0.9
ADAM_B2 = 0.999
ADAM_EPS = 1e-08
ADAM_WD = 0.01
ADAM_STEP = 10

VMEM_LIMIT_BYTES = 56 * 1024 * 1024


def _cparams(*sem):
    return pltpu.CompilerParams(dimension_semantics=sem, vmem_limit_bytes=VMEM_LIMIT_BYTES)


def _pick(n, cap, align=LANES):
    if n <= cap:
        return n
    best = None
    for t in range(align, cap + 1, align):
        if n % t == 0:
            best = t
    assert best is not None, (n, cap, align)
    return best


def _sigmoid(x):
    return 1.0 / (1.0 + jnp.exp(-x))


def _matmul(a, b, *, ta=False, tb=False, out_dtype=F32, name):
    if ta:
        K, M = a.shape
    else:
        M, K = a.shape
    if tb:
        N, Kb = b.shape
    else:
        Kb, N = b.shape
    assert K == Kb, (a.shape, b.shape, ta, tb)
    tn = _pick(N, 1408)
    tm = _pick(M, 1024 if tn <= 1024 else 512)
    tk = _pick(K, 2048 if (tm * tn <= 512 * 1024) else 1024)
    if K <= 1536:
        tk = K
    nk = K // tk
    dims = (((0 if ta else 1,), (1 if tb else 0,)), ((), ()))

    def body(a_ref, b_ref, o_ref, *scratch):
        p = lax.dot_general(a_ref[...].astype(BF16), b_ref[...].astype(BF16), dims,
                            preferred_element_type=F32)
        if nk == 1:
            o_ref[...] = p.astype(out_dtype)
        else:
            acc_ref, = scratch
            k = pl.program_id(2)

            @pl.when(k == 0)
            def _():
                acc_ref[...] = p

            @pl.when(k > 0)
            def _():
                acc_ref[...] += p

            @pl.when(k == nk - 1)
            def _():
                o_ref[...] = acc_ref[...].astype(out_dtype)

    a_spec = pl.BlockSpec((tk, tm), lambda i, j, k: (k, i)) if ta else pl.BlockSpec((tm, tk), lambda i, j, k: (i, k))
    b_spec = pl.BlockSpec((tn, tk), lambda i, j, k: (j, k)) if tb else pl.BlockSpec((tk, tn), lambda i, j, k: (k, j))
    return pl.pallas_call(
        body, name=name,
        out_shape=jax.ShapeDtypeStruct((M, N), out_dtype),
        grid=(M // tm, N // tn, nk),
        in_specs=[a_spec, b_spec],
        out_specs=pl.BlockSpec((tm, tn), lambda i, j, k: (i, j)),
        scratch_shapes=[pltpu.VMEM((tm, tn), F32)] if nk > 1 else [],
        compiler_params=_cparams("parallel", "parallel", "arbitrary"),
    )(a, b)


def _row_block(S, D, cap_bytes=2 * 1024 * 1024):
    ts = max(8, min(S, cap_bytes // (4 * D)))
    return _pick(S, ts, 8)


def _rms_fwd(x, g, *, out_dtype, name):
    S, D = x.shape
    ts = _row_block(S, D)

    def body(x_ref, g_ref, o_ref):
        xf = x_ref[...]
        r = lax.rsqrt(jnp.mean(xf * xf, axis=-1, keepdims=True) + RMS_EPS)
        o_ref[...] = (xf * r * g_ref[...]).astype(out_dtype)

    return pl.pallas_call(
        body, name=name, out_shape=jax.ShapeDtypeStruct((S, D), out_dtype), grid=(S // ts,),
        in_specs=[pl.BlockSpec((ts, D), lambda i: (i, 0)), pl.BlockSpec((1, D), lambda i: (0, 0))],
        out_specs=pl.BlockSpec((ts, D), lambda i: (i, 0)),
        compiler_params=_cparams("parallel"),
    )(x, g)


def _add_rms(x, r, g, *, name):
    S, D = x.shape
    ts = _row_block(S, D)

    def body(x_ref, r_ref, g_ref, o_ref):
        rf = r_ref[...]
        s = lax.rsqrt(jnp.mean(rf * rf, axis=-1, keepdims=True) + RMS_EPS)
        o_ref[...] = x_ref[...] + rf * s * g_ref[...]

    return pl.pallas_call(
        body, name=name, out_shape=jax.ShapeDtypeStruct((S, D), F32), grid=(S // ts,),
        in_specs=[pl.BlockSpec((ts, D), lambda i: (i, 0)), pl.BlockSpec((ts, D), lambda i: (i, 0)),
                  pl.BlockSpec((1, D), lambda i: (0, 0))],
        out_specs=pl.BlockSpec((ts, D), lambda i: (i, 0)),
        compiler_params=_cparams("parallel"),
    )(x, r, g)


def _rms_bwd(x, g, dy, res=None, *, out_dtype, name):
    S, D = x.shape
    ts = _row_block(S, D)
    has_res = res is not None

    def body(*refs):
        if has_res:
            x_ref, g_ref, dy_ref, res_ref, dx_ref, dg_ref = refs
        else:
            x_ref, g_ref, dy_ref, dx_ref, dg_ref = refs
        i = pl.program_id(0)
        xf = x_ref[...]
        dyf = dy_ref[...].astype(F32)
        r = lax.rsqrt(jnp.mean(xf * xf, axis=-1, keepdims=True) + RMS_EPS)
        n = xf * r
        dn = dyf * g_ref[...]
        dx = r * (dn - n * jnp.mean(dn * n, axis=-1, keepdims=True))
        if has_res:
            dx = dx + res_ref[...]
        dx_ref[...] = dx.astype(out_dtype)
        part = jnp.sum((dyf * n).reshape(ts // 8, 8, D), axis=0)

        @pl.when(i == 0)
        def _():
            dg_ref[...] = part

        @pl.when(i > 0)
        def _():
            dg_ref[...] += part

    row = pl.BlockSpec((ts, D), lambda i: (i, 0))
    in_specs = [row, pl.BlockSpec((1, D), lambda i: (0, 0)), row] + ([row] if has_res else [])
    args = (x, g, dy) + ((res,) if has_res else ())
    dx, dg = pl.pallas_call(
        body, name=name,
        out_shape=(jax.ShapeDtypeStruct((S, D), out_dtype), jax.ShapeDtypeStruct((8, D), F32)),
        grid=(S // ts,), in_specs=in_specs,
        out_specs=(row, pl.BlockSpec((8, D), lambda i: (0, 0))),
        compiler_params=_cparams("arbitrary"),
    )(*args)
    return dx, dg


def _loss_head(y, t, *, name):
    S, D = y.shape
    ts = _row_block(S, D)

    def body(y_ref, t_ref, dy_ref, sq_ref):
        i = pl.program_id(0)
        e = y_ref[...] - t_ref[...]
        dy_ref[...] = e / D
        part = jnp.sum((e * e).reshape(ts // 8, 8, D), axis=0)

        @pl.when(i == 0)
        def _():
            sq_ref[...] = part

        @pl.when(i > 0)
        def _():
            sq_ref[...] += part

    row = pl.BlockSpec((ts, D), lambda i: (i, 0))
    return pl.pallas_call(
        body, name=name,
        out_shape=(jax.ShapeDtypeStruct((S, D), F32), jax.ShapeDtypeStruct((8, D), F32)),
        grid=(S // ts,), in_specs=[row, row],
        out_specs=(row, pl.BlockSpec((8, D), lambda i: (0, 0))),
        compiler_params=_cparams("arbitrary"),
    )(y, t)


def _window_counts(i, T, w):
    t = i * T + lax.broadcasted_iota(jnp.int32, (T, 1), 0)
    return jnp.minimum(t + 1, w).astype(F32)


def _pool_fwd(zu, pool_w, pool_scale, *, name):
    S, W = zu.shape
    T = _pick(S, 1024, 8)

    def body(u_ref, pw_ref, sc_ref, pooled_ref, mixed_ref, halo_ref):
        i = pl.program_id(0)

        @pl.when(i == 0)
        def _():
            halo_ref[...] = jnp.zeros_like(halo_ref)

        u = u_ref[...]
        ext = jnp.concatenate([halo_ref[...], u], axis=0)
        halo_ref[...] = u[T - HALO:, :]
        for g, w in enumerate(POOL_WINDOWS):
            cols = slice(g * POOL_GROUP, (g + 1) * POOL_GROUP)
            s = ext[:, cols]
            sh = 1
            while sh < w:
                s = s + pltpu.roll(s, sh, 0)
                sh *= 2
            pooled = s[HALO:, :] / _window_counts(i, T, w) - u[:, cols]
            pooled_bf = pooled.astype(BF16)
            pm = jnp.dot(pooled_bf, pw_ref[g].astype(BF16), preferred_element_type=F32)
            pooled_ref[:, cols] = pooled_bf
            mixed_ref[:, cols] = (pm * sc_ref[:, cols]).astype(BF16)

    row = pl.BlockSpec((T, W), lambda i: (i, 0))
    return pl.pallas_call(
        body, name=name,
        out_shape=(jax.ShapeDtypeStruct((S, W), BF16), jax.ShapeDtypeStruct((S, W), BF16)),
        grid=(S // T,),
        in_specs=[row, pl.BlockSpec(pool_w.shape, lambda i: (0, 0, 0)), pl.BlockSpec((1, W), lambda i: (0, 0))],
        out_specs=(row, row),
        scratch_shapes=[pltpu.VMEM((HALO, W), F32)],
        compiler_params=_cparams("arbitrary"),
    )(zu, pool_w, pool_scale)


def _pool_bwd(pooled, dmixed, pool_w, pool_scale, *, name):
    S, W = pooled.shape
    T = _pick(S, 1024, 8)
    nb = S // T

    def body(p_ref, dm_ref, pw_ref, sc_ref, dzu_ref, dpw_ref, dsc_ref, halo_ref):
        i = pl.program_id(0)
        blk = nb - 1 - i

        @pl.when(i == 0)
        def _():
            halo_ref[...] = jnp.zeros_like(halo_ref)
            dpw_ref[...] = jnp.zeros_like(dpw_ref)
            dsc_ref[...] = jnp.zeros_like(dsc_ref)

        for g, w in enumerate(POOL_WINDOWS):
            cols = slice(g * POOL_GROUP, (g + 1) * POOL_GROUP)
            p = p_ref[:, cols]
            dm = dm_ref[:, cols]
            pw = pw_ref[g].astype(BF16)
            pm = jnp.dot(p, pw, preferred_element_type=F32)
            dsc_ref[:, cols] += jnp.sum((dm * pm).reshape(T // 8, 8, POOL_GROUP), axis=0)
            dpm = (dm * sc_ref[:, cols]).astype(BF16)
            dpw_ref[g] += lax.dot_general(p, dpm, (((0,), (0,)), ((), ())), preferred_element_type=F32)
            dpooled = lax.dot_general(dpm, pw, (((1,), (1,)), ((), ())), preferred_element_type=F32)
            e = dpooled / _window_counts(blk, T, w)
            ext = jnp.concatenate([e, halo_ref[:, cols]], axis=0)
            halo_ref[:, cols] = e[:HALO, :]
            s = ext
            sh = 1
            while sh < w:
                s = s + pltpu.roll(s, T + HALO - sh, 0)
                sh *= 2
            dzu_ref[:, cols] = (s[:T, :] - dpooled).astype(BF16)

    row = pl.BlockSpec((T, W), lambda i: (nb - 1 - i, 0))
    return pl.pallas_call(
        body, name=name,
        out_shape=(jax.ShapeDtypeStruct((S, W), BF16), jax.ShapeDtypeStruct(pool_w.shape, F32),
                   jax.ShapeDtypeStruct((8, W), F32)),
        grid=(nb,),
        in_specs=[row, row, pl.BlockSpec(pool_w.shape, lambda i: (0, 0, 0)), pl.BlockSpec((1, W), lambda i: (0, 0))],
        out_specs=(row, pl.BlockSpec(pool_w.shape, lambda i: (0, 0, 0)), pl.BlockSpec((8, W), lambda i: (0, 0))),
        scratch_shapes=[pltpu.VMEM((HALO, W), F32)],
        compiler_params=_cparams("arbitrary"),
    )(pooled, dmixed, pool_w, pool_scale)


def _fgate_fwd(zf, bf, *, name):
    S, W = zf.shape
    T = _pick(S, 512, 8)

    def body(z_ref, b_ref, c_ref, carry_ref):
        i = pl.program_id(0)

        @pl.when(i == 0)
        def _():
            carry_ref[...] = jnp.zeros_like(carry_ref)

        a = z_ref[...] + b_ref[...]
        s = jnp.minimum(a, 0.0) - jnp.log(1.0 + jnp.exp(-jnp.abs(a)))
        row = lax.broadcasted_iota(jnp.int32, (T, W), 0)
        sh = 1
        while sh < T:
            s = s + jnp.where(row >= sh, pltpu.roll(s, sh, 0), 0.0)
            sh *= 2
        c = s + carry_ref[0:1, :]
        c_ref[...] = c
        carry_ref[...] = jnp.broadcast_to(c[T - 1:T, :], carry_ref.shape)

    return pl.pallas_call(
        body, name=name, out_shape=jax.ShapeDtypeStruct((S, W), F32), grid=(S // T,),
        in_specs=[pl.BlockSpec((T, W), lambda i: (i, 0)), pl.BlockSpec((1, W), lambda i: (0, 0))],
        out_specs=pl.BlockSpec((T, W), lambda i: (i, 0)),
        scratch_shapes=[pltpu.VMEM((8, W), F32)],
        compiler_params=_cparams("arbitrary"),
    )(zf, bf)


def _fgate_bwd(zf, bf, dc, *, name):
    S, W = zf.shape
    T = _pick(S, 512, 8)
    nb = S // T

    def body(z_ref, b_ref, dc_ref, dz_ref, db_ref, carry_ref):
        i = pl.program_id(0)

        @pl.when(i == 0)
        def _():
            carry_ref[...] = jnp.zeros_like(carry_ref)
            db_ref[...] = jnp.zeros_like(db_ref)

        s = dc_ref[...]
        row = lax.broadcasted_iota(jnp.int32, (T, W), 0)
        sh = 1
        while sh < T:
            s = s + jnp.where(row < T - sh, pltpu.roll(s, T - sh, 0), 0.0)
            sh *= 2
        dlf = s + carry_ref[0:1, :]
        carry_ref[...] = jnp.broadcast_to(dlf[0:1, :], carry_ref.shape)
        dz = dlf * (1.0 - _sigmoid(z_ref[...] + b_ref[...]))
        dz_ref[...] = dz.astype(BF16)
        db_ref[...] += jnp.sum(dz.reshape(T // 8, 8, W), axis=0)

    row_spec = pl.BlockSpec((T, W), lambda i: (nb - 1 - i, 0))
    return pl.pallas_call(
        body, name=name,
        out_shape=(jax.ShapeDtypeStruct((S, W), BF16), jax.ShapeDtypeStruct((8, W), F32)),
        grid=(nb,),
        in_specs=[row_spec, pl.BlockSpec((1, W), lambda i: (0, 0)), row_spec],
        out_specs=(row_spec, pl.BlockSpec((8, W), lambda i: (0, 0))),
        scratch_shapes=[pltpu.VMEM((8, W), F32)],
        compiler_params=_cparams("arbitrary"),
    )(zf, bf, dc)


def _merge_fwd(zg, yp, yf, *, name):
    S, D = yp.shape
    ts = _row_block(S, D, 1024 * 1024)

    def body(zg_ref, yp_ref, yf_ref, o_ref):
        o_ref[...] = (_sigmoid(zg_ref[:, :D]) * yp_ref[...] + _sigmoid(zg_ref[:, D:]) * yf_ref[...]).astype(BF16)

    row = pl.BlockSpec((ts, D), lambda i: (i, 0))
    return pl.pallas_call(
        body, name=name, out_shape=jax.ShapeDtypeStruct((S, D), BF16), grid=(S // ts,),
        in_specs=[pl.BlockSpec((ts, 2 * D), lambda i: (i, 0)), row, row], out_specs=row,
        compiler_params=_cparams("parallel"),
    )(zg, yp, yf)


def _merge_bwd(zg, yp, yf, dmerged, *, name):
    S, D = yp.shape
    ts = _row_block(S, D, 1024 * 1024)

    def body(zg_ref, yp_ref, yf_ref, dm_ref, dyp_ref, dyf_ref, dzg_ref):
        dm = dm_ref[...]
        sp = _sigmoid(zg_ref[:, :D])
        sf = _sigmoid(zg_ref[:, D:])
        dyp_ref[...] = (dm * sp).astype(BF16)
        dyf_ref[...] = (dm * sf).astype(BF16)
        dzg_ref[:, :D] = (dm * yp_ref[...] * (sp * (1.0 - sp))).astype(BF16)
        dzg_ref[:, D:] = (dm * yf_ref[...] * (sf * (1.0 - sf))).astype(BF16)

    row = pl.BlockSpec((ts, D), lambda i: (i, 0))
    wide = pl.BlockSpec((ts, 2 * D), lambda i: (i, 0))
    return pl.pallas_call(
        body, name=name,
        out_shape=(jax.ShapeDtypeStruct((S, D), BF16), jax.ShapeDtypeStruct((S, D), BF16),
                   jax.ShapeDtypeStruct((S, 2 * D), BF16)),
        grid=(S // ts,), in_specs=[wide, row, row, row], out_specs=(row, row, wide),
        compiler_params=_cparams("parallel"),
    )(zg, yp, yf, dmerged)


NEG_BIG = -1e30


def _fox_fwd(qkv, ccol, crow, *, name):
    _, H, S, Dh = qkv.shape
    bq = _pick(S, 512, 128)
    nq = S // bq
    scale = 1.0 / math.sqrt(Dh)

    def body(q_ref, k_ref, v_ref, cq_ref, ck_ref, o_ref, lse_ref):
        i = pl.program_id(1)
        q = q_ref[...] * scale
        cq = cq_ref[...]

        def step(j, carry, masked):
            m, l, acc = carry
            start = pl.multiple_of(j * bq, bq)
            kb = k_ref[pl.ds(start, bq), :]
            vb = v_ref[pl.ds(start, bq), :]
            ck = ck_ref[:, pl.ds(start, bq)]
            s = lax.dot_general(q, kb, (((1,), (1,)), ((), ())), preferred_element_type=F32)
            s = s + (cq - ck)
            if masked:
                r = lax.broadcasted_iota(jnp.int32, (bq, bq), 0)
                c = lax.broadcasted_iota(jnp.int32, (bq, bq), 1)
                s = jnp.where(c <= r, s, NEG_BIG)
            m_new = jnp.maximum(m, jnp.max(s, axis=-1, keepdims=True))
            alpha = jnp.exp(m - m_new)
            p = jnp.exp(s - m_new)
            l = alpha * l + jnp.sum(p, axis=-1, keepdims=True)
            acc = alpha * acc + jnp.dot(p.astype(BF16), vb, preferred_element_type=F32)
            return m_new, l, acc

        init = (jnp.full((bq, 1), NEG_BIG, F32), jnp.zeros((bq, 1), F32), jnp.zeros((bq, Dh), F32))
        carry = lax.fori_loop(0, i, lambda j, c: step(j, c, False), init)
        m, l, acc = step(i, carry, True)
        o_ref[...] = acc / l
        lse_ref[...] = m + jnp.log(l)

    return pl.pallas_call(
        body, name=name,
        out_shape=(jax.ShapeDtypeStruct((H, S, Dh), F32), jax.ShapeDtypeStruct((H, S, 1), F32)),
        grid=(H, nq),
        in_specs=[pl.BlockSpec((None, None, bq, Dh), lambda h, i: (0, h, i, 0)),
                  pl.BlockSpec((None, None, S, Dh), lambda h, i: (1, h, 0, 0)),
                  pl.BlockSpec((None, None, S, Dh), lambda h, i: (2, h, 0, 0)),
                  pl.BlockSpec((None, bq, 1), lambda h, i: (h, i, 0)),
                  pl.BlockSpec((None, 1, S), lambda h, i: (h, 0, 0))],
        out_specs=(pl.BlockSpec((None, bq, Dh), lambda h, i: (h, i, 0)),
                   pl.BlockSpec((None, bq, 1), lambda h, i: (h, i, 0))),
        compiler_params=_cparams("parallel", "arbitrary"),
    )(qkv, qkv, qkv, ccol, crow)


def _rowdot(a, b, *, name):
    H, S, Dh = a.shape
    ts = _pick(S, 2048, 8)

    def body(a_ref, b_ref, o_ref):
        o_ref[...] = jnp.sum(a_ref[...].astype(F32) * b_ref[...].astype(F32), axis=-1, keepdims=True)

    blk = pl.BlockSpec((None, ts, Dh), lambda h, i: (h, i, 0))
    return pl.pallas_call(
        body, name=name, out_shape=jax.ShapeDtypeStruct((H, S, 1), F32), grid=(H, S // ts),
        in_specs=[blk, blk], out_specs=pl.BlockSpec((None, ts, 1), lambda h, i: (h, i, 0)),
        compiler_params=_cparams("parallel", "parallel"),
    )(a, b)


def _fox_bwd(qkv, do, ccol, crow, lse_row, delta_row, *, name):
    _, H, S, Dh = qkv.shape
    bk = _pick(S, 512, 128)
    nk = S // bk
    scale = 1.0 / math.sqrt(Dh)

    def body(q_ref, k_ref, v_ref, do_ref, ck_ref, cq_ref, lse_ref, dl_ref, dq_ref, dk_ref, dv_ref, dc_ref, dcq_ref):
        j = pl.program_id(1)

        @pl.when(j == 0)
        def _():
            dq_ref[...] = jnp.zeros_like(dq_ref)
            dcq_ref[...] = jnp.zeros_like(dcq_ref)

        kb = k_ref[...]
        vb = v_ref[...]
        ck = ck_ref[...]

        def step(i, carry, masked):
            dk, dv, dc = carry
            start = pl.multiple_of(i * bk, bk)
            qb = q_ref[pl.ds(start, bk), :] * scale
            dob = do_ref[pl.ds(start, bk), :]
            cq = cq_ref[:, pl.ds(start, bk)]
            lse = lse_ref[:, pl.ds(start, bk)]
            dl = dl_ref[:, pl.ds(start, bk)]
            st = lax.dot_general(kb, qb, (((1,), (1,)), ((), ())), preferred_element_type=F32)
            st = st + (cq - ck)
            pt = jnp.exp(st - lse)
            if masked:
                r = lax.broadcasted_iota(jnp.int32, (bk, bk), 0)
                c = lax.broadcasted_iota(jnp.int32, (bk, bk), 1)
                pt = jnp.where(c >= r, pt, 0.0)
            dpt = lax.dot_general(vb, dob, (((1,), (1,)), ((), ())), preferred_element_type=F32)
            dst = pt * (dpt - dl)
            pt_bf = pt.astype(BF16)
            dst_bf = dst.astype(BF16)
            dv = dv + jnp.dot(pt_bf, dob, preferred_element_type=F32)
            dk = dk + jnp.dot(dst_bf, qb, preferred_element_type=F32)
            dc = dc - jnp.sum(dst, axis=-1, keepdims=True)
            dcq_ref[:, pl.ds(start, bk)] += jnp.sum(dst, axis=0, keepdims=True)
            dq_ref[pl.ds(start, bk), :] += lax.dot_general(
                dst_bf, kb, (((0,), (0,)), ((), ())), preferred_element_type=F32)
            return dk, dv, dc

        init = (jnp.zeros((bk, Dh), F32), jnp.zeros((bk, Dh), F32), jnp.zeros((bk, 1), F32))
        carry = step(j, init, True)
        dk, dv, dc = lax.fori_loop(j + 1, nk, lambda i, c: step(i, c, False), carry)
        dk_ref[...] = dk.astype(BF16)
        dv_ref[...] = dv.astype(BF16)
        dc_ref[...] = dc

        @pl.when(j == nk - 1)
        def _():
            dq_ref[...] = dq_ref[...] * scale

    full = lambda sel: pl.BlockSpec((None, None, S, Dh), lambda h, j: (sel, h, 0, 0))
    blk = lambda sel: pl.BlockSpec((None, None, bk, Dh), lambda h, j: (sel, h, j, 0))
    rowfull = pl.BlockSpec((None, 1, S), lambda h, j: (h, 0, 0))
    hblk = pl.BlockSpec((None, bk, Dh), lambda h, j: (h, j, 0))
    return pl.pallas_call(
        body, name=name,
        out_shape=(jax.ShapeDtypeStruct((H, S, Dh), F32), jax.ShapeDtypeStruct((H, S, Dh), BF16),
                   jax.ShapeDtypeStruct((H, S, Dh), BF16), jax.ShapeDtypeStruct((H, S, 1), F32),
                   jax.ShapeDtypeStruct((H, 1, S), F32)),
        grid=(H, nk),
        in_specs=[full(0), blk(1), blk(2), pl.BlockSpec((None, S, Dh), lambda h, j: (h, 0, 0)),
                  pl.BlockSpec((None, bk, 1), lambda h, j: (h, j, 0)), rowfull, rowfull, rowfull],
        out_specs=(pl.BlockSpec((None, S, Dh), lambda h, j: (h, 0, 0)), hblk, hblk,
                   pl.BlockSpec((None, bk, 1), lambda h, j: (h, j, 0)), rowfull),
        compiler_params=_cparams("parallel", "arbitrary"),
    )(qkv, qkv, qkv, do, ccol, crow, lse_row, delta_row)


def _xattn_fwd(q, kv, *, name):
    S, W = q.shape
    M = kv.shape[0]
    tq = _pick(S, 512, 8)
    scale = 1.0 / math.sqrt(X_DH)

    def body(q_ref, kv_ref, o_ref):
        for h in range(X_HEADS):
            cols = slice(h * X_DH, (h + 1) * X_DH)
            vcols = slice(W + h * X_DH, W + (h + 1) * X_DH)
            s = lax.dot_general(q_ref[:, cols], kv_ref[:, cols], (((1,), (1,)), ((), ())),
                                preferred_element_type=F32) * scale
            e = jnp.exp(s - jnp.max(s, axis=-1, keepdims=True))
            p = e / jnp.sum(e, axis=-1, keepdims=True)
            o_ref[:, cols] = jnp.dot(p.astype(BF16), kv_ref[:, vcols], preferred_element_type=F32).astype(BF16)

    return pl.pallas_call(
        body, name=name, out_shape=jax.ShapeDtypeStruct((S, W), BF16), grid=(S // tq,),
        in_specs=[pl.BlockSpec((tq, W), lambda i: (i, 0)), pl.BlockSpec((M, 2 * W), lambda i: (0, 0))],
        out_specs=pl.BlockSpec((tq, W), lambda i: (i, 0)),
        compiler_params=_cparams("parallel"),
    )(q, kv)


def _xattn_bwd(q, kv, do, *, name):
    S, W = q.shape
    M = kv.shape[0]
    tq = _pick(S, 512, 8)
    scale = 1.0 / math.sqrt(X_DH)

    def body(q_ref, kv_ref, do_ref, dq_ref, dkv_ref):
        i = pl.program_id(0)

        @pl.when(i == 0)
        def _():
            dkv_ref[...] = jnp.zeros_like(dkv_ref)

        for h in range(X_HEADS):
            cols = slice(h * X_DH, (h + 1) * X_DH)
            vcols = slice(W + h * X_DH, W + (h + 1) * X_DH)
            qh = q_ref[:, cols]
            kh = kv_ref[:, cols]
            vh = kv_ref[:, vcols]
            doh = do_ref[:, cols]
            s = lax.dot_general(qh, kh, (((1,), (1,)), ((), ())), preferred_element_type=F32) * scale
            e = jnp.exp(s - jnp.max(s, axis=-1, keepdims=True))
            p = e / jnp.sum(e, axis=-1, keepdims=True)
            dp = lax.dot_general(doh, vh, (((1,), (1,)), ((), ())), preferred_element_type=F32)
            ds = (p * (dp - jnp.sum(p * dp, axis=-1, keepdims=True)) * scale).astype(BF16)
            dq_ref[:, cols] = jnp.dot(ds, kh, preferred_element_type=F32).astype(BF16)
            dkv_ref[:, cols] += lax.dot_general(ds, qh, (((0,), (0,)), ((), ())), preferred_element_type=F32)
            dkv_ref[:, vcols] += lax.dot_general(p.astype(BF16), doh, (((0,), (0,)), ((), ())),
                                                 preferred_element_type=F32)

    return pl.pallas_call(
        body, name=name,
        out_shape=(jax.ShapeDtypeStruct((S, W), BF16), jax.ShapeDtypeStruct((M, 2 * W), F32)),
        grid=(S // tq,),
        in_specs=[pl.BlockSpec((tq, W), lambda i: (i, 0)), pl.BlockSpec((M, 2 * W), lambda i: (0, 0)),
                  pl.BlockSpec((tq, W), lambda i: (i, 0))],
        out_specs=(pl.BlockSpec((tq, W), lambda i: (i, 0)), pl.BlockSpec((M, 2 * W), lambda i: (0, 0))),
        compiler_params=_cparams("arbitrary"),
    )(q, kv, do)


GELU_C = math.sqrt(2.0 / math.pi)
GELU_A = 0.044715
CONV_HALO = 8


def _gelu_parts(x):
    u = GELU_C * (x + GELU_A * x * x * x)
    t = jnp.tanh(u)
    g = 0.5 * x * (1.0 + t)
    dg = 0.5 * (1.0 + t) + 0.5 * x * (1.0 - t * t) * (GELU_C * (1.0 + 3.0 * GELU_A * x * x))
    return g, dg


def _conv3(ext, w_ref, b_ref):
    return (w_ref[2:3, :] * ext + w_ref[1:2, :] * pltpu.roll(ext, 1, 0)
            + w_ref[0:1, :] * pltpu.roll(ext, 2, 0) + b_ref[...])


def _convglu_fwd(z, conv_w, conv_b, *, name):
    S, F2 = z.shape
    F = F2 // 2
    tc = _pick(F, 1408)
    ncol = F // tc
    T = _pick(S, 512, 8)
    hb = T // CONV_HALO

    def body(zg_ref, zu_ref, zgp_ref, zup_ref, wg_ref, wu_ref, bg_ref, bu_ref, act_ref):
        i = pl.program_id(1)
        first = (i > 0).astype(F32)

        def conv(z_ref, zp_ref, w_ref, b_ref):
            ext = jnp.concatenate([zp_ref[...] * first, z_ref[...]], axis=0)
            return _conv3(ext, w_ref, b_ref)[CONV_HALO:, :]

        gc = conv(zg_ref, zgp_ref, wg_ref, bg_ref)
        uc = conv(zu_ref, zup_ref, wu_ref, bu_ref)
        act_ref[...] = (_gelu_parts(gc)[0] * uc).astype(BF16)

    cur = lambda off: pl.BlockSpec((T, tc), lambda j, i: (i, j + off))
    prev = lambda off: pl.BlockSpec((CONV_HALO, tc), lambda j, i: (jnp.maximum(i * hb - 1, 0), j + off))
    vec = lambda rows, off: pl.BlockSpec((rows, tc), lambda j, i: (0, j + off))
    return pl.pallas_call(
        body, name=name, out_shape=jax.ShapeDtypeStruct((S, F), BF16), grid=(ncol, S // T),
        in_specs=[cur(0), cur(ncol), prev(0), prev(ncol), vec(3, 0), vec(3, ncol), vec(1, 0), vec(1, ncol)],
        out_specs=pl.BlockSpec((T, tc), lambda j, i: (i, j)),
        compiler_params=_cparams("parallel", "parallel"),
    )(z, z, z, z, conv_w, conv_w, conv_b, conv_b)


def _convglu_bwd(z, dact, conv_w, conv_b, *, name):
    S, F2 = z.shape
    F = F2 // 2
    tc = _pick(F, 1408)
    ncol = F // tc
    T = _pick(S, 256, 8)
    nrow = S // T
    hb = T // CONV_HALO
    TE = T + CONV_HALO

    def body(zg_ref, zu_ref, zgp_ref, zup_ref, zgn_ref, zun_ref, da_ref, dan_ref,
             wg_ref, wu_ref, bg_ref, bu_ref, dzg_ref, dzu_ref, dwg_ref, dwu_ref):
        i = pl.program_id(1)
        first = (i > 0).astype(F32)
        last = (i < nrow - 1).astype(F32)

        @pl.when(i == 0)
        def _():
            dwg_ref[...] = jnp.zeros_like(dwg_ref)
            dwu_ref[...] = jnp.zeros_like(dwu_ref)

        extg = jnp.concatenate([zgp_ref[...] * first, zg_ref[...], zgn_ref[...]], axis=0)
        extu = jnp.concatenate([zup_ref[...] * first, zu_ref[...], zun_ref[...]], axis=0)
        gc = _conv3(extg, wg_ref, bg_ref)[CONV_HALO:, :]
        uc = _conv3(extu, wu_ref, bu_ref)[CONV_HALO:, :]
        da = jnp.concatenate([da_ref[...], dan_ref[...] * last], axis=0)
        gl, dgl = _gelu_parts(gc)
        dgc = da * uc * dgl
        duc = da * gl

        def back(d, ext, w_ref, dz_ref, dw_ref):
            dz = w_ref[2:3, :] * d + w_ref[1:2, :] * pltpu.roll(d, TE - 1, 0) + w_ref[0:1, :] * pltpu.roll(d, TE - 2, 0)
            dz_ref[...] = dz[:T, :].astype(BF16)
            dc = d[:T, :]
            z0 = ext[CONV_HALO:CONV_HALO + T, :]
            z1 = pltpu.roll(ext, 1, 0)[CONV_HALO:CONV_HALO + T, :]
            z2 = pltpu.roll(ext, 2, 0)[CONV_HALO:CONV_HALO + T, :]
            rows = [jnp.sum(dc * z2, axis=0, keepdims=True), jnp.sum(dc * z1, axis=0, keepdims=True),
                    jnp.sum(dc * z0, axis=0, keepdims=True), jnp.sum(dc, axis=0, keepdims=True)]
            dw_ref[0:4, :] += jnp.concatenate(rows, axis=0)

        back(dgc, extg, wg_ref, dzg_ref, dwg_ref)
        back(duc, extu, wu_ref, dzu_ref, dwu_ref)

    cur = lambda off: pl.BlockSpec((T, tc), lambda j, i: (i, j + off))
    prev = lambda off: pl.BlockSpec((CONV_HALO, tc), lambda j, i: (jnp.maximum(i * hb - 1, 0), j + off))
    nxt = lambda off: pl.BlockSpec((CONV_HALO, tc), lambda j, i: (jnp.minimum((i + 1) * hb, S // CONV_HALO - 1), j + off))
    vec = lambda rows, off: pl.BlockSpec((rows, tc), lambda j, i: (0, j + off))
    dzg, dzu, dwg, dwu = pl.pallas_call(
        body, name=name,
        out_shape=(jax.ShapeDtypeStruct((S, F), BF16), jax.ShapeDtypeStruct((S, F), BF16),
                   jax.ShapeDtypeStruct((8, F), F32), jax.ShapeDtypeStruct((8, F), F32)),
        grid=(ncol, nrow),
        in_specs=[cur(0), cur(ncol), prev(0), prev(ncol), nxt(0), nxt(ncol), cur(0), nxt(0),
                  vec(3, 0), vec(3, ncol), vec(1, 0), vec(1, ncol)],
        out_specs=(cur(0), cur(0), vec(8, 0), vec(8, 0)),
        compiler_params=_cparams("parallel", "arbitrary"),
    )(z, z, z, z, z, z, dact, dact, conv_w, conv_w, conv_b, conv_b)
    return dzg, dzu, dwg, dwu


OFF_QKV = POOL_W
OFF_G = POOL_W + 3 * FOX_W


def _prep_layer_weights(w, l):
    D = w["w_in"].shape[1]
    w_in = w["w_in"][l]
    off_f = OFF_G
    pad = jnp.zeros((D, LANES - N_FGATE), w_in.dtype)
    w_in_r = jnp.concatenate([w_in[:, :off_f], w_in[:, off_f + N_FGATE:], w_in[:, off_f:off_f + N_FGATE], pad], axis=1)
    return dict(w_in_r=w_in_r.astype(BF16),
                w_pool_br=w["w_pool_br"][l], w_fox_br=w["w_fox_br"][l], w_mix_out=w["w_mix_out"][l],
                w_xq=w["w_xq"][l], w_xkv=w["w_xkv"][l], w_xo=w["w_xo"][l], w_up=w["w_up"][l], w_down=w["w_down"][l])


def _row(v):
    return v.reshape(1, -1)


def _layer_fwd(x, mem, wl, p, l):
    S, D = x.shape
    n = lambda s: f"l{l}_{s}"
    sv = {"x0": x}
    h1 = _rms_fwd(x, _row(p["mix_pre_g"][l]), out_dtype=BF16, name=n("rms1"))
    w_in_r = wl["w_in_r"]
    og = OFF_G
    zu = _matmul(h1, w_in_r[:, :OFF_QKV], out_dtype=F32, name=n("mm_zu"))
    zqkv = _matmul(h1, w_in_r[:, OFF_QKV:og], out_dtype=BF16, name=n("mm_zqkv"))
    zg = _matmul(h1, w_in_r[:, og:og + 2 * D], out_dtype=F32, name=n("mm_zg"))
    zf = _matmul(h1, w_in_r[:, og + 2 * D:], out_dtype=F32, name=n("mm_zf"))
    bf = jnp.pad(p["b_forget"][l], (0, LANES - N_FGATE)).reshape(1, LANES)
    c = _fgate_fwd(zf, bf, name=n("fgate"))
    cT = c[:, :FOX_HEADS].T
    ccol = cT.reshape(FOX_HEADS, S, 1)
    crow = cT.reshape(FOX_HEADS, 1, S)
    qkv = zqkv.reshape(S, 3, FOX_HEADS, FOX_DH).transpose(1, 2, 0, 3)
    o_fox, lse = _fox_fwd(qkv, ccol, crow, name=n("fox"))
    o_flat = o_fox.transpose(1, 0, 2).reshape(S, FOX_W).astype(BF16)
    pooled, mixed = _pool_fwd(zu, p["pool_w"][l], _row(p["pool_scale"][l]), name=n("pool"))
    y_pool = _matmul(mixed, wl["w_pool_br"], out_dtype=F32, name=n("mm_ypool"))
    y_fox = _matmul(o_flat, wl["w_fox_br"], out_dtype=F32, name=n("mm_yfox"))
    merged = _merge_fwd(zg, y_pool, y_fox, name=n("merge"))
    r1 = _matmul(merged, wl["w_mix_out"], out_dtype=F32, name=n("mm_r1"))
    x1 = _add_rms(x, r1, _row(p["mix_post_g"][l]), name=n("addrms1"))
    sv.update(h1=h1, zg=zg, zf=zf, bf=bf, ccol=ccol, crow=crow, qkv=qkv, o_fox=o_fox, lse=lse, o_flat=o_flat,
              pooled=pooled, mixed=mixed, y_pool=y_pool, y_fox=y_fox, merged=merged, r1=r1, x1=x1)
    h2 = _rms_fwd(x1, _row(p["xa_pre_g"][l]), out_dtype=BF16, name=n("rms2"))
    mem_n = _rms_fwd(mem, _row(p["mem_g"][l]), out_dtype=BF16, name=n("rms_mem"))
    q2 = _matmul(h2, wl["w_xq"], out_dtype=BF16, name=n("mm_q2"))
    kv = _matmul(mem_n, wl["w_xkv"], out_dtype=BF16, name=n("mm_kv"))
    o2 = _xattn_fwd(q2, kv, name=n("xattn"))
    a2 = _matmul(o2, wl["w_xo"], out_dtype=F32, name=n("mm_a2"))
    x2 = _add_rms(x1, a2, _row(p["xa_post_g"][l]), name=n("addrms2"))
    sv.update(h2=h2, mem_n=mem_n, q2=q2, kv=kv, o2=o2, a2=a2, x2=x2)
    h3 = _rms_fwd(x2, _row(p["ffn_pre_g"][l]), out_dtype=BF16, name=n("rms3"))
    z3 = _matmul(h3, wl["w_up"], out_dtype=F32, name=n("mm_z3"))
    act = _convglu_fwd(z3, p["conv_w"][l], _row(p["conv_b"][l]), name=n("convglu"))
    d3 = _matmul(act, wl["w_down"], out_dtype=F32, name=n("mm_d3"))
    x3 = _add_rms(x2, d3, _row(p["ffn_post_g"][l]), name=n("addrms3"))
    sv.update(h3=h3, z3=z3, act=act, d3=d3)
    return x3, sv


def _layer_bwd(dx, mem, wl, p, l, sv):
    S, D = dx.shape
    n = lambda s: f"l{l}_b_{s}"
    g = {}
    red = lambda part: jnp.sum(part, axis=0)
    dd3, dg = _rms_bwd(sv["d3"], _row(p["ffn_post_g"][l]), dx, out_dtype=BF16, name=n("rms3post"))
    g["ffn_post_g"] = red(dg)
    dact = _matmul(dd3, wl["w_down"], tb=True, out_dtype=F32, name=n("mm_dact"))
    g["w_down"] = _matmul(sv["act"], dd3, ta=True, out_dtype=F32, name=n("mm_dwdown"))
    dzg3, dzu3, dwg, dwu = _convglu_bwd(sv["z3"], dact, p["conv_w"][l], _row(p["conv_b"][l]), name=n("convglu"))
    g["conv_w"] = jnp.concatenate([dwg[:3], dwu[:3]], axis=1)
    g["conv_b"] = jnp.concatenate([dwg[3], dwu[3]], axis=0)
    dz3 = jnp.concatenate([dzg3, dzu3], axis=1)
    dh3 = _matmul(dz3, wl["w_up"], tb=True, out_dtype=F32, name=n("mm_dh3"))
    g["w_up"] = _matmul(sv["h3"], dz3, ta=True, out_dtype=F32, name=n("mm_dwup"))
    dx, dg = _rms_bwd(sv["x2"], _row(p["ffn_pre_g"][l]), dh3, dx, out_dtype=F32, name=n("rms3pre"))
    g["ffn_pre_g"] = red(dg)
    da2, dg = _rms_bwd(sv["a2"], _row(p["xa_post_g"][l]), dx, out_dtype=BF16, name=n("rms2post"))
    g["xa_post_g"] = red(dg)
    do2 = _matmul(da2, wl["w_xo"], tb=True, out_dtype=BF16, name=n("mm_do2"))
    g["w_xo"] = _matmul(sv["o2"], da2, ta=True, out_dtype=F32, name=n("mm_dwxo"))
    dq2, dkv = _xattn_bwd(sv["q2"], sv["kv"], do2, name=n("xattn"))
    dh2 = _matmul(dq2, wl["w_xq"], tb=True, out_dtype=F32, name=n("mm_dh2"))
    g["w_xq"] = _matmul(sv["h2"], dq2, ta=True, out_dtype=F32, name=n("mm_dwxq"))
    dmem_n = _matmul(dkv, wl["w_xkv"], tb=True, out_dtype=F32, name=n("mm_dmemn"))
    g["w_xkv"] = _matmul(sv["mem_n"], dkv, ta=True, out_dtype=F32, name=n("mm_dwxkv"))
    _, dg = _rms_bwd(mem, _row(p["mem_g"][l]), dmem_n, out_dtype=BF16, name=n("rms_mem"))
    g["mem_g"] = red(dg)
    dx, dg = _rms_bwd(sv["x1"], _row(p["xa_pre_g"][l]), dh2, dx, out_dtype=F32, name=n("rms2pre"))
    g["xa_pre_g"] = red(dg)
    dr1, dg = _rms_bwd(sv["r1"], _row(p["mix_post_g"][l]), dx, out_dtype=BF16, name=n("rms1post"))
    g["mix_post_g"] = red(dg)
    dmerged = _matmul(dr1, wl["w_mix_out"], tb=True, out_dtype=F32, name=n("mm_dmerged"))
    g["w_mix_out"] = _matmul(sv["merged"], dr1, ta=True, out_dtype=F32, name=n("mm_dwmo"))
    dyp, dyf, dzg = _merge_bwd(sv["zg"], sv["y_pool"], sv["y_fox"], dmerged, name=n("merge"))
    dmixed = _matmul(dyp, wl["w_pool_br"], tb=True, out_dtype=F32, name=n("mm_dmixed"))
    g["w_pool_br"] = _matmul(sv["mixed"], dyp, ta=True, out_dtype=F32, name=n("mm_dwpb"))
    dofox = _matmul(dyf, wl["w_fox_br"], tb=True, out_dtype=F32, name=n("mm_dofox"))
    g["w_fox_br"] = _matmul(sv["o_flat"], dyf, ta=True, out_dtype=F32, name=n("mm_dwfb"))
    dzu, dpw, dsc = _pool_bwd(sv["pooled"], dmixed, p["pool_w"][l], _row(p["pool_scale"][l]), name=n("pool"))
    g["pool_w"] = dpw
    g["pool_scale"] = red(dsc)
    do_h = dofox.reshape(S, FOX_HEADS, FOX_DH).transpose(1, 0, 2)
    delta = _rowdot(do_h, sv["o_fox"], name=n("delta"))
    dq, dk, dv, dck, dcq = _fox_bwd(sv["qkv"], do_h.astype(BF16), sv["ccol"], sv["crow"],
                                    sv["lse"].reshape(FOX_HEADS, 1, S), delta.reshape(FOX_HEADS, 1, S), name=n("fox"))
    dc = dcq.reshape(FOX_HEADS, S) + dck.reshape(FOX_HEADS, S)
    dc_pad = jnp.pad(dc.T, ((0, 0), (0, LANES - FOX_HEADS)))
    dzf, db = _fgate_bwd(sv["zf"], sv["bf"], dc_pad, name=n("fgate"))
    g["b_forget"] = red(db)[:N_FGATE]
    dqkv = jnp.stack([dq.astype(BF16), dk, dv]).transpose(2, 0, 1, 3).reshape(S, 3 * FOX_W)
    dz_cat = jnp.concatenate([dzu, dqkv, dzg, dzf], axis=1)
    dh1 = _matmul(dz_cat, wl["w_in_r"], tb=True, out_dtype=F32, name=n("mm_dh1"))
    dw_in_r = _matmul(sv["h1"], dz_cat, ta=True, out_dtype=F32, name=n("mm_dwin"))
    og = OFF_G
    g["w_in"] = jnp.concatenate([dw_in_r[:, :og], dw_in_r[:, og + 2 * D:og + 2 * D + N_FGATE],
                                 dw_in_r[:, og:og + 2 * D]], axis=1)
    dx, dg = _rms_bwd(sv["x0"], _row(p["mix_pre_g"][l]), dh1, dx, out_dtype=F32, name=n("rms1pre"))
    g["mix_pre_g"] = red(dg)
    return dx, g


MATMUL_WEIGHTS = ("w_in", "w_pool_br", "w_fox_br", "w_mix_out", "w_xq", "w_xkv", "w_xo", "w_up", "w_down")
WEIGHT_NAMES = ("mix_pre_g", "mix_post_g", "w_in", "b_forget", "pool_w", "pool_scale", "w_pool_br", "w_fox_br",
                "w_mix_out", "xa_pre_g", "xa_post_g", "mem_g", "w_xq", "w_xkv", "w_xo", "ffn_pre_g", "ffn_post_g",
                "w_up", "conv_w", "conv_b", "w_down")


def _local_step(x, mem, loss_target, wfull, p):
    L = p["mix_pre_g"].shape[0]
    saved, wls = [], []
    h = x
    for l in range(L):
        wl = _prep_layer_weights(wfull, l)
        h, sv = _layer_fwd(h, mem, wl, p, l)
        saved.append(sv)
        wls.append(wl)
    D = x.shape[1]
    dy, sq = _loss_head(h, loss_target, name="loss_head")
    loss = 0.5 * jnp.sum(sq) / D
    grads = []
    dx = dy
    for l in reversed(range(L)):
        dx, g = _layer_bwd(dx, mem, wls[l], p, l, saved[l])
        grads.append(g)
    grads = grads[::-1]
    gfull = {k: jnp.stack([grads[l][k] for l in range(L)]) for k in WEIGHT_NAMES}
    return loss, dx, gfull


PACK_W = 512
PACK_ROW_ALIGN = 1024
N_CHIPS = 4
N_DEV = 8
SHARDED = (("w_in", 2), ("w_pool_br", 2), ("w_fox_br", 2), ("w_mix_out", 1), ("w_xq", 1), ("w_xkv", 1),
           ("w_xo", 2), ("w_up", 2), ("w_down", 1), ("conv_w", 2))
REPLICATED = ("mix_pre_g", "mix_post_g", "b_forget", "pool_w", "pool_scale", "xa_pre_g", "xa_post_g", "mem_g",
              "ffn_pre_g", "ffn_post_g", "conv_b")


def _round_up(n, m):
    return -(-n // m) * m


def _pack(arrs, rows):
    flat = jnp.concatenate([a.reshape(-1) for a in arrs])
    return jnp.pad(flat, (0, rows * PACK_W - flat.shape[0])).reshape(rows, PACK_W)


def _unpack(buf, shapes):
    flat = buf.reshape(-1)
    out, off = [], 0
    for s in shapes:
        n = math.prod(s)
        out.append(flat[off:off + n].reshape(s))
        off += n
    return out


ANY = pl.BlockSpec(memory_space=pl.ANY)


def _remote(send_sems, recv_sems, k, src, dst, to):
    return pltpu.make_async_remote_copy(src_ref=src, dst_ref=dst, send_sem=send_sems.at[k], recv_sem=recv_sems.at[k],
                                        device_id=to, device_id_type=MESH)


def _my_place():
    return lax.axis_index("x"), lax.axis_index("y"), lax.axis_index("c")


def _gather_weights(wb, cw):
    RB, W = wb.shape
    HB = RB // 2
    assert HB % 16 == 0

    def body(wb_ref, cw_ref, ob_ref, oc_ref, send_sems, recv_sems, local_sems):
        x, y, c = _my_place()
        me = 2 * x + y
        sibling = (x, y, 1 - c)
        chips = [(1 - x, y), (x, 1 - y), (1 - x, 1 - y)]
        half = pl.ds(pl.multiple_of(c * HB, 16), HB)
        other = pl.ds(pl.multiple_of((1 - c) * HB, 16), HB)
        rc = functools.partial(_remote, send_sems, recv_sems)
        mine_b = pltpu.make_async_copy(wb_ref, ob_ref.at[me], local_sems.at[0])
        mine_c = pltpu.make_async_copy(cw_ref, oc_ref.at[me], local_sems.at[1])
        mine_b.start()
        mine_c.start()
        first = []
        for k, (px, py) in enumerate(chips):
            first.append(rc(k, wb_ref.at[half], ob_ref.at[me, half], (px, py, c)))
            first.append(rc(3 + k, cw_ref, oc_ref.at[me], (px, py, c)))
        for cp in first:
            cp.start()
        passed = []
        for k, (px, py) in enumerate(chips):
            src = 2 * px + py
            rc(k, wb_ref.at[half], ob_ref.at[src, half], (px, py, c)).wait_recv()
            fwd = rc(6 + k, ob_ref.at[src, half], ob_ref.at[src, half], sibling)
            fwd.start()
            passed.append(fwd)
        for k, (px, py) in enumerate(chips):
            src = 2 * px + py
            rc(3 + k, cw_ref, oc_ref.at[src], (px, py, c)).wait_recv()
            rc(6 + k, ob_ref.at[src, other], ob_ref.at[src, other], sibling).wait_recv()
        for cp in first + passed:
            cp.wait_send()
        mine_b.wait()
        mine_c.wait()

    return pl.pallas_call(
        body, name="gather_weights",
        out_shape=(jax.ShapeDtypeStruct((N_CHIPS, RB, W), wb.dtype), jax.ShapeDtypeStruct((N_CHIPS,) + cw.shape, cw.dtype)),
        in_specs=[ANY, ANY], out_specs=(ANY, ANY),
        scratch_shapes=[pltpu.SemaphoreType.DMA((9,)), pltpu.SemaphoreType.DMA((9,)), pltpu.SemaphoreType.DMA((2,))],
    )(wb, cw)


def _exchange_halves(G, rep):
    _, R, W = G.shape
    HR = R // 2
    RR = rep.shape[0]

    def body(g_ref, rep_ref, ra_ref, rall_ref, send_sems, recv_sems, local_sem):
        x, y, c = _my_place()
        me = 4 * x + 2 * y + c
        rc = functools.partial(_remote, send_sems, recv_sems)
        other = pl.ds(pl.multiple_of((1 - c) * HR, 8), HR)

        def peer(idx):
            px = (1 - x) if (idx >> 2) & 1 else x
            py = (1 - y) if (idx >> 1) & 1 else y
            pc = (1 - c) if idx & 1 else c
            return px, py, pc

        loc = pltpu.make_async_copy(rep_ref, rall_ref.at[me], local_sem)
        loc.start()
        cps = [rc(0, g_ref.at[:, other, :], ra_ref, (x, y, 1 - c))]
        for idx in range(1, N_DEV):
            cps.append(rc(idx, rep_ref, rall_ref.at[me], peer(idx)))
        for cp in cps:
            cp.start()
        rc(0, g_ref.at[:, other, :], ra_ref, (x, y, 1 - c)).wait_recv()
        for idx in range(1, N_DEV):
            px, py, pc = peer(idx)
            rc(idx, rep_ref, rall_ref.at[4 * px + 2 * py + pc], (px, py, pc)).wait_recv()
        for cp in cps:
            cp.wait_send()
        loc.wait()

    return pl.pallas_call(
        body, name="exchange_halves",
        out_shape=(jax.ShapeDtypeStruct((N_CHIPS, HR, W), F32), jax.ShapeDtypeStruct((N_DEV, RR, W), F32)),
        in_specs=[ANY, ANY], out_specs=(ANY, ANY),
        scratch_shapes=[pltpu.SemaphoreType.DMA((N_DEV,)), pltpu.SemaphoreType.DMA((N_DEV,)), pltpu.SemaphoreType.DMA],
    )(G, rep)


def _exchange_chips(A):
    def body(a_ref, rb_ref, send_sems, recv_sems, local_sem):
        x, y, c = _my_place()
        me = 2 * x + y
        chips = [(1 - x, y), (x, 1 - y), (1 - x, 1 - y)]
        rc = functools.partial(_remote, send_sems, recv_sems)
        loc = pltpu.make_async_copy(a_ref.at[me], rb_ref.at[me], local_sem)
        loc.start()
        cps = [rc(k, a_ref.at[2 * px + py], rb_ref.at[me], (px, py, c)) for k, (px, py) in enumerate(chips)]
        for cp in cps:
            cp.start()
        for k, (px, py) in enumerate(chips):
            rc(k, a_ref.at[me], rb_ref.at[2 * px + py], (px, py, c)).wait_recv()
        for cp in cps:
            cp.wait_send()
        loc.wait()

    return pl.pallas_call(
        body, name="exchange_chips", out_shape=jax.ShapeDtypeStruct(A.shape, F32),
        in_specs=[ANY], out_specs=ANY,
        scratch_shapes=[pltpu.SemaphoreType.DMA((3,)), pltpu.SemaphoreType.DMA((3,)), pltpu.SemaphoreType.DMA],
    )(A)


def _exchange_sibling(T):
    HR, W = T.shape

    def body(t_ref, g_ref, send_sem, recv_sem, local_sem):
        x, y, c = _my_place()
        half = pl.ds(pl.multiple_of(c * HR, 8), HR)
        other = pl.ds(pl.multiple_of((1 - c) * HR, 8), HR)
        loc = pltpu.make_async_copy(t_ref, g_ref.at[half], local_sem)
        loc.start()
        cp = pltpu.make_async_remote_copy(src_ref=t_ref, dst_ref=g_ref.at[half], send_sem=send_sem, recv_sem=recv_sem,
                                          device_id=(x, y, 1 - c), device_id_type=MESH)
        cp.start()
        pltpu.make_async_remote_copy(src_ref=t_ref, dst_ref=g_ref.at[other], send_sem=send_sem, recv_sem=recv_sem,
                                     device_id=(x, y, 1 - c), device_id_type=MESH).wait_recv()
        cp.wait_send()
        loc.wait()

    return pl.pallas_call(
        body, name="exchange_sibling", out_shape=jax.ShapeDtypeStruct((2 * HR, W), F32),
        in_specs=[ANY], out_specs=ANY,
        scratch_shapes=[pltpu.SemaphoreType.DMA, pltpu.SemaphoreType.DMA, pltpu.SemaphoreType.DMA],
    )(T)


def _add_halves(G, recv, core):
    _, R, W = G.shape
    HR = R // 2
    tr = _pick(HR, 512, 8)
    nb = HR // tr

    def body(core_ref, g_ref, r_ref, o_ref):
        o_ref[...] = g_ref[...] + r_ref[...]

    return pl.pallas_call(
        body, name="add_halves", out_shape=jax.ShapeDtypeStruct((N_CHIPS, HR, W), F32),
        grid_spec=pltpu.PrefetchScalarGridSpec(
            num_scalar_prefetch=1, grid=(N_CHIPS, nb),
            in_specs=[pl.BlockSpec((None, tr, W), lambda p, i, core_ref: (p, core_ref[0] * nb + i, 0)),
                      pl.BlockSpec((None, tr, W), lambda p, i, core_ref: (p, i, 0))],
            out_specs=pl.BlockSpec((None, tr, W), lambda p, i, core_ref: (p, i, 0))),
        compiler_params=_cparams("parallel", "parallel"),
    )(core, G, recv)


def _sum_slots(a, *, name):
    n, rows, W = a.shape
    tr = _pick(rows, 512, 8)

    def body(a_ref, o_ref):
        s = a_ref[0]
        for q in range(1, n):
            s = s + a_ref[q]
        o_ref[...] = s

    return pl.pallas_call(
        body, name=name, out_shape=jax.ShapeDtypeStruct((rows, W), F32), grid=(rows // tr,),
        in_specs=[pl.BlockSpec((n, tr, W), lambda i: (0, i, 0))], out_specs=pl.BlockSpec((tr, W), lambda i: (i, 0)),
        compiler_params=_cparams("parallel"),
    )(a)


def _adamw(w, g, m, v, *, name):
    rows, W = w.shape
    tr = _pick(rows, 512, 8)

    def body(w_ref, g_ref, m_ref, v_ref, d_ref, nm_ref, nv_ref):
        gg = g_ref[...]
        nm = ADAM_B1 * m_ref[...] + (1.0 - ADAM_B1) * gg
        nv = ADAM_B2 * v_ref[...] + (1.0 - ADAM_B2) * jnp.square(gg)
        m_hat = nm / (1.0 - ADAM_B1 ** ADAM_STEP)
        v_hat = nv / (1.0 - ADAM_B2 ** ADAM_STEP)
        d_ref[...] = -ADAM_LR * (m_hat / (jnp.sqrt(v_hat) + ADAM_EPS) + ADAM_WD * w_ref[...])
        nm_ref[...] = nm
        nv_ref[...] = nv

    blk = pl.BlockSpec((tr, W), lambda i: (i, 0))
    shp = jax.ShapeDtypeStruct((rows, W), F32)
    return pl.pallas_call(
        body, name=name, out_shape=(shp, shp, shp), grid=(rows // tr,),
        in_specs=[blk, blk, blk, blk], out_specs=(blk, blk, blk),
        compiler_params=_cparams("parallel"),
    )(w, g, m, v)


INPUT_NAMES = (("x", "mem") + WEIGHT_NAMES + ("loss_target",) + tuple("m_" + n for n in WEIGHT_NAMES)
               + tuple("v_" + n for n in WEIGHT_NAMES))


def kernel(*args):
    a = dict(zip(INPUT_NAMES, args, strict=True))
    x, mem, target = a["x"][0], a["mem"][0], a["loss_target"][0]
    sh_names = [n for n, _ in SHARDED]
    sh_shapes = [a[n].shape for n in sh_names]
    n_mm = sum(math.prod(s) for s in sh_shapes[:-1])
    n_all = n_mm + math.prod(sh_shapes[-1])
    rows_b = _round_up(-(-n_mm // PACK_W), 32)
    rows = _round_up(max(-(-n_all // PACK_W), rows_b), PACK_ROW_ALIGN)
    rows_c = _round_up(-(-math.prod(sh_shapes[-1]) // PACK_W), 8)

    wpack = _pack([a[n] for n in sh_names], rows)
    cwpack = _pack([a["conv_w"]], rows_c)
    gb, gc = _gather_weights(wpack[:rows_b].astype(BF16), cwpack)
    parts = [_unpack(gb[q], sh_shapes[:-1]) for q in range(N_CHIPS)]
    wfull = {n: jnp.concatenate([parts[q][i] for q in range(N_CHIPS)], axis=ax)
             for i, (n, ax) in enumerate(SHARDED[:-1])}
    p = {n: a[n] for n in REPLICATED}
    p["conv_w"] = jnp.concatenate([_unpack(gc[q], sh_shapes[-1:])[0] for q in range(N_CHIPS)], axis=SHARDED[-1][1])

    loss, dx, gfull = _local_step(x, mem, target, wfull, p)
    loss = lax.psum(loss, ("x", "y", "c"))

    def shard(g, ax, q):
        n = g.shape[ax] // N_CHIPS
        return lax.slice_in_dim(g, q * n, (q + 1) * n, axis=ax)

    G = jnp.stack([_pack([shard(gfull[n], ax, q) for n, ax in SHARDED], rows) for q in range(N_CHIPS)])
    rep_shapes = [a[n].shape for n in REPLICATED]
    rows_r = _round_up(-(-sum(math.prod(s) for s in rep_shapes) // PACK_W), 64)
    rep = _pack([gfull[n] for n in REPLICATED], rows_r)
    core = lax.axis_index("c").astype(jnp.int32).reshape(1)
    recv, repall = _exchange_halves(G, rep)
    chip_sums = _exchange_chips(_add_halves(G, recv, core))
    gshard = _exchange_sibling(_sum_slots(chip_sums, name="sum_chips"))
    grep = _sum_slots(repall, name="sum_devices")

    mpack = _pack([a["m_" + n] for n in sh_names], rows)
    vpack = _pack([a["v_" + n] for n in sh_names], rows)
    d_s, m_s, v_s = _adamw(wpack, gshard, mpack, vpack, name="adamw_sharded")
    d_r, m_r, v_r = _adamw(_pack([a[n] for n in REPLICATED], rows_r), grep,
                           _pack([a["m_" + n] for n in REPLICATED], rows_r),
                           _pack([a["v_" + n] for n in REPLICATED], rows_r), name="adamw_replicated")

    outs = []
    for sharded_buf, rep_buf in ((gshard, grep), (d_s, d_r), (m_s, m_r), (v_s, v_r)):
        got = dict(zip(sh_names, _unpack(sharded_buf, sh_shapes)))
        got.update(zip(REPLICATED, _unpack(rep_buf, rep_shapes)))
        outs.extend(got[n] for n in WEIGHT_NAMES)
    return (loss, dx[None], *outs)
```

```python
import functools
import math

import jax
import jax.numpy as jnp
from jax import lax
from jax.experimental import pallas as pl
from jax.experimental.pallas import tpu as pltpu

F32 = jnp.float32
BF16 = jnp.bfloat16
MESH = pl.DeviceIdType.MESH

RMS_EPS = 1e-6
POOL_WINDOWS = (2, 4, 8, 16)
POOL_GROUP = 128
POOL_W = 512
FOX_HEADS = 8
FOX_DH = 64
FOX_W = 512
X_HEADS = 4
X_DH = 128
X_W = 512
N_FGATE = 8
LANES = 128
HALO = 16

ADAM_LR = 0.001
ADAM_B1 = 0.9
ADAM_B2 = 0.999
ADAM_EPS = 1e-08
ADAM_WD = 0.01
ADAM_STEP = 10

VMEM_LIMIT_BYTES = 56 * 1024 * 1024


def _cparams(*sem):
    return pltpu.CompilerParams(dimension_semantics=sem, vmem_limit_bytes=VMEM_LIMIT_BYTES)


def _pick(n, cap, align=LANES):
    if n <= cap:
        return n
    best = None
    for t in range(align, cap + 1, align):
        if n % t == 0:
            best = t
    assert best is not None, (n, cap, align)
    return best


def _sigmoid(x):
    return 1.0 / (1.0 + jnp.exp(-x))


def _matmul(a, b, *, ta=False, tb=False, out_dtype=F32, name):
    if ta:
        K, M = a.shape
    else:
        M, K = a.shape
    if tb:
        N, Kb = b.shape
    else:
        Kb, N = b.shape
    assert K == Kb, (a.shape, b.shape, ta, tb)
    tn = _pick(N, 1408)
    tm = _pick(M, 1024 if tn <= 1024 else 512)
    tk = _pick(K, 2048 if (tm * tn <= 512 * 1024) else 1024)
    if K <= 1536:
        tk = K
    nk = K // tk
    dims = (((0 if ta else 1,), (1 if tb else 0,)), ((), ()))

    def body(a_ref, b_ref, o_ref, *scratch):
        p = lax.dot_general(a_ref[...].astype(BF16), b_ref[...].astype(BF16), dims,
                            preferred_element_type=F32)
        if nk == 1:
            o_ref[...] = p.astype(out_dtype)
        else:
            acc_ref, = scratch
            k = pl.program_id(2)

            @pl.when(k == 0)
            def _():
                acc_ref[...] = p

            @pl.when(k > 0)
            def _():
                acc_ref[...] += p

            @pl.when(k == nk - 1)
            def _():
                o_ref[...] = acc_ref[...].astype(out_dtype)

    a_spec = pl.BlockSpec((tk, tm), lambda i, j, k: (k, i)) if ta else pl.BlockSpec((tm, tk), lambda i, j, k: (i, k))
    b_spec = pl.BlockSpec((tn, tk), lambda i, j, k: (j, k)) if tb else pl.BlockSpec((tk, tn), lambda i, j, k: (k, j))
    return pl.pallas_call(
        body, name=name,
        out_shape=jax.ShapeDtypeStruct((M, N), out_dtype),
        grid=(M // tm, N // tn, nk),
        in_specs=[a_spec, b_spec],
        out_specs=pl.BlockSpec((tm, tn), lambda i, j, k: (i, j)),
        scratch_shapes=[pltpu.VMEM((tm, tn), F32)] if nk > 1 else [],
        compiler_params=_cparams("parallel", "parallel", "arbitrary"),
    )(a, b)


def _row_block(S, D, cap_bytes=2 * 1024 * 1024):
    ts = max(8, min(S, cap_bytes // (4 * D)))
    return _pick(S, ts, 8)


def _rms_fwd(x, g, *, out_dtype, name):
    S, D = x.shape
    ts = _row_block(S, D)

    def body(x_ref, g_ref, o_ref):
        xf = x_ref[...]
        r = lax.rsqrt(jnp.mean(xf * xf, axis=-1, keepdims=True) + RMS_EPS)
        o_ref[...] = (xf * r * g_ref[...]).astype(out_dtype)

    return pl.pallas_call(
        body, name=name, out_shape=jax.ShapeDtypeStruct((S, D), out_dtype), grid=(S // ts,),
        in_specs=[pl.BlockSpec((ts, D), lambda i: (i, 0)), pl.BlockSpec((1, D), lambda i: (0, 0))],
        out_specs=pl.BlockSpec((ts, D), lambda i: (i, 0)),
        compiler_params=_cparams("parallel"),
    )(x, g)


def _add_rms(x, r, g, *, name):
    S, D = x.shape
    ts = _row_block(S, D)

    def body(x_ref, r_ref, g_ref, o_ref):
        rf = r_ref[...]
        s = lax.rsqrt(jnp.mean(rf * rf, axis=-1, keepdims=True) + RMS_EPS)
        o_ref[...] = x_ref[...] + rf * s * g_ref[...]

    return pl.pallas_call(
        body, name=name, out_shape=jax.ShapeDtypeStruct((S, D), F32), grid=(S // ts,),
        in_specs=[pl.BlockSpec((ts, D), lambda i: (i, 0)), pl.BlockSpec((ts, D), lambda i: (i, 0)),
                  pl.BlockSpec((1, D), lambda i: (0, 0))],
        out_specs=pl.BlockSpec((ts, D), lambda i: (i, 0)),
        compiler_params=_cparams("parallel"),
    )(x, r, g)


def _rms_bwd(x, g, dy, res=None, *, out_dtype, name):
    S, D = x.shape
    ts = _row_block(S, D)
    has_res = res is not None

    def body(*refs):
        if has_res:
            x_ref, g_ref, dy_ref, res_ref, dx_ref, dg_ref = refs
        else:
            x_ref, g_ref, dy_ref, dx_ref, dg_ref = refs
        i = pl.program_id(0)
        xf = x_ref[...]
        dyf = dy_ref[...].astype(F32)
        r = lax.rsqrt(jnp.mean(xf * xf, axis=-1, keepdims=True) + RMS_EPS)
        n = xf * r
        dn = dyf * g_ref[...]
        dx = r * (dn - n * jnp.mean(dn * n, axis=-1, keepdims=True))
        if has_res:
            dx = dx + res_ref[...]
        dx_ref[...] = dx.astype(out_dtype)
        part = jnp.sum((dyf * n).reshape(ts // 8, 8, D), axis=0)

        @pl.when(i == 0)
        def _():
            dg_ref[...] = part

        @pl.when(i > 0)
        def _():
            dg_ref[...] += part

    row = pl.BlockSpec((ts, D), lambda i: (i, 0))
    in_specs = [row, pl.BlockSpec((1, D), lambda i: (0, 0)), row] + ([row] if has_res else [])
    args = (x, g, dy) + ((res,) if has_res else ())
    dx, dg = pl.pallas_call(
        body, name=name,
        out_shape=(jax.ShapeDtypeStruct((S, D), out_dtype), jax.ShapeDtypeStruct((8, D), F32)),
        grid=(S // ts,), in_specs=in_specs,
        out_specs=(row, pl.BlockSpec((8, D), lambda i: (0, 0))),
        compiler_params=_cparams("arbitrary"),
    )(*args)
    return dx, dg


def _loss_head(y, t, *, name):
    S, D = y.shape
    ts = _row_block(S, D)

    def body(y_ref, t_ref, dy_ref, sq_ref):
        i = pl.program_id(0)
        e = y_ref[...] - t_ref[...]
        dy_ref[...] = e / D
        part = jnp.sum((e * e).reshape(ts // 8, 8, D), axis=0)

        @pl.when(i == 0)
        def _():
            sq_ref[...] = part

        @pl.when(i > 0)
        def _():
            sq_ref[...] += part

    row = pl.BlockSpec((ts, D), lambda i: (i, 0))
    return pl.pallas_call(
        body, name=name,
        out_shape=(jax.ShapeDtypeStruct((S, D), F32), jax.ShapeDtypeStruct((8, D), F32)),
        grid=(S // ts,), in_specs=[row, row],
        out_specs=(row, pl.BlockSpec((8, D), lambda i: (0, 0))),
        compiler_params=_cparams("arbitrary"),
    )(y, t)


def _window_counts(i, T, w):
    t = i * T + lax.broadcasted_iota(jnp.int32, (T, 1), 0)
    return jnp.minimum(t + 1, w).astype(F32)


def _pool_fwd(zu, pool_w, pool_scale, *, name):
    S, W = zu.shape
    T = _pick(S, 1024, 8)

    def body(u_ref, pw_ref, sc_ref, pooled_ref, mixed_ref, halo_ref):
        i = pl.program_id(0)

        @pl.when(i == 0)
        def _():
            halo_ref[...] = jnp.zeros_like(halo_ref)

        u = u_ref[...]
        ext = jnp.concatenate([halo_ref[...], u], axis=0)
        halo_ref[...] = u[T - HALO:, :]
        for g, w in enumerate(POOL_WINDOWS):
            cols = slice(g * POOL_GROUP, (g + 1) * POOL_GROUP)
            s = ext[:, cols]
            sh = 1
            while sh < w:
                s = s + pltpu.roll(s, sh, 0)
                sh *= 2
            pooled = s[HALO:, :] / _window_counts(i, T, w) - u[:, cols]
            pooled_bf = pooled.astype(BF16)
            pm = jnp.dot(pooled_bf, pw_ref[g].astype(BF16), preferred_element_type=F32)
            pooled_ref[:, cols] = pooled_bf
            mixed_ref[:, cols] = (pm * sc_ref[:, cols]).astype(BF16)

    row = pl.BlockSpec((T, W), lambda i: (i, 0))
    return pl.pallas_call(
        body, name=name,
        out_shape=(jax.ShapeDtypeStruct((S, W), BF16), jax.ShapeDtypeStruct((S, W), BF16)),
        grid=(S // T,),
        in_specs=[row, pl.BlockSpec(pool_w.shape, lambda i: (0, 0, 0)), pl.BlockSpec((1, W), lambda i: (0, 0))],
        out_specs=(row, row),
        scratch_shapes=[pltpu.VMEM((HALO, W), F32)],
        compiler_params=_cparams("arbitrary"),
    )(zu, pool_w, pool_scale)


def _pool_bwd(pooled, dmixed, pool_w, pool_scale, *, name):
    S, W = pooled.shape
    T = _pick(S, 1024, 8)
    nb = S // T

    def body(p_ref, dm_ref, pw_ref, sc_ref, dzu_ref, dpw_ref, dsc_ref, halo_ref):
        i = pl.program_id(0)
        blk = nb - 1 - i

        @pl.when(i == 0)
        def _():
            halo_ref[...] = jnp.zeros_like(halo_ref)
            dpw_ref[...] = jnp.zeros_like(dpw_ref)
            dsc_ref[...] = jnp.zeros_like(dsc_ref)

        for g, w in enumerate(POOL_WINDOWS):
            cols = slice(g * POOL_GROUP, (g + 1) * POOL_GROUP)
            p = p_ref[:, cols]
            dm = dm_ref[:, cols]
            pw = pw_ref[g].astype(BF16)
            pm = jnp.dot(p, pw, preferred_element_type=F32)
            dsc_ref[:, cols] += jnp.sum((dm * pm).reshape(T // 8, 8, POOL_GROUP), axis=0)
            dpm = (dm * sc_ref[:, cols]).astype(BF16)
            dpw_ref[g] += lax.dot_general(p, dpm, (((0,), (0,)), ((), ())), preferred_element_type=F32)
            dpooled = lax.dot_general(dpm, pw, (((1,), (1,)), ((), ())), preferred_element_type=F32)
            e = dpooled / _window_counts(blk, T, w)
            ext = jnp.concatenate([e, halo_ref[:, cols]], axis=0)
            halo_ref[:, cols] = e[:HALO, :]
            s = ext
            sh = 1
            while sh < w:
                s = s + pltpu.roll(s, T + HALO - sh, 0)
                sh *= 2
            dzu_ref[:, cols] = (s[:T, :] - dpooled).astype(BF16)

    row = pl.BlockSpec((T, W), lambda i: (nb - 1 - i, 0))
    return pl.pallas_call(
        body, name=name,
        out_shape=(jax.ShapeDtypeStruct((S, W), BF16), jax.ShapeDtypeStruct(pool_w.shape, F32),
                   jax.ShapeDtypeStruct((8, W), F32)),
        grid=(nb,),
        in_specs=[row, row, pl.BlockSpec(pool_w.shape, lambda i: (0, 0, 0)), pl.BlockSpec((1, W), lambda i: (0, 0))],
        out_specs=(row, pl.BlockSpec(pool_w.shape, lambda i: (0, 0, 0)), pl.BlockSpec((8, W), lambda i: (0, 0))),
        scratch_shapes=[pltpu.VMEM((HALO, W), F32)],
        compiler_params=_cparams("arbitrary"),
    )(pooled, dmixed, pool_w, pool_scale)


def _fgate_fwd(zf, bf, *, name):
    S, W = zf.shape
    T = _pick(S, 512, 8)

    def body(z_ref, b_ref, c_ref, aq_ref, ak_ref, carry_ref):
        i = pl.program_id(0)

        @pl.when(i == 0)
        def _():
            carry_ref[...] = jnp.zeros_like(carry_ref)

        a = z_ref[...] + b_ref[...]
        s = jnp.minimum(a, 0.0) - jnp.log(1.0 + jnp.exp(-jnp.abs(a)))
        row = lax.broadcasted_iota(jnp.int32, (T, W), 0)
        sh = 1
        while sh < T:
            s = s + jnp.where(row >= sh, pltpu.roll(s, sh, 0), 0.0)
            sh *= 2
        c = s + carry_ref[0:1, :]
        c_ref[...] = c
        carry_ref[...] = jnp.broadcast_to(c[T - 1:T, :], carry_ref.shape)
        lane = lax.broadcasted_iota(jnp.int32, (T, W), 1)
        for h in range(FOX_HEADS):
            ch = c[:, h:h + 1]
            hi = ch.astype(BF16).astype(F32)
            r1 = ch - hi
            lo = r1.astype(BF16).astype(F32)
            lo2 = (r1 - lo).astype(BF16).astype(F32)
            aq = jnp.where(lane == 0, hi, jnp.where(lane == 1, lo, jnp.where(lane == 2, lo2,
                                                                              jnp.where(lane < 6, 1.0, 0.0))))
            ak = jnp.where(lane < 3, 1.0, jnp.where(lane == 3, -hi, jnp.where(lane == 4, -lo,
                                                                               jnp.where(lane == 5, -lo2, 0.0))))
            aq_ref[h] = aq.astype(BF16)
            ak_ref[h] = ak.astype(BF16)

    aug = jax.ShapeDtypeStruct((FOX_HEADS, S, W), BF16)
    aug_spec = pl.BlockSpec((FOX_HEADS, T, W), lambda i: (0, i, 0))
    return pl.pallas_call(
        body, name=name, out_shape=(jax.ShapeDtypeStruct((S, W), F32), aug, aug), grid=(S // T,),
        in_specs=[pl.BlockSpec((T, W), lambda i: (i, 0)), pl.BlockSpec((1, W), lambda i: (0, 0))],
        out_specs=(pl.BlockSpec((T, W), lambda i: (i, 0)), aug_spec, aug_spec),
        scratch_shapes=[pltpu.VMEM((8, W), F32)],
        compiler_params=_cparams("arbitrary"),
    )(zf, bf)


def _fgate_bwd(zf, bf, dc, *, name):
    S, W = zf.shape
    T = _pick(S, 512, 8)
    nb = S // T

    def body(z_ref, b_ref, dc_ref, dz_ref, db_ref, carry_ref):
        i = pl.program_id(0)

        @pl.when(i == 0)
        def _():
            carry_ref[...] = jnp.zeros_like(carry_ref)
            db_ref[...] = jnp.zeros_like(db_ref)

        s = dc_ref[...]
        row = lax.broadcasted_iota(jnp.int32, (T, W), 0)
        sh = 1
        while sh < T:
            s = s + jnp.where(row < T - sh, pltpu.roll(s, T - sh, 0), 0.0)
            sh *= 2
        dlf = s + carry_ref[0:1, :]
        carry_ref[...] = jnp.broadcast_to(dlf[0:1, :], carry_ref.shape)
        dz = dlf * (1.0 - _sigmoid(z_ref[...] + b_ref[...]))
        dz_ref[...] = dz.astype(BF16)
        db_ref[...] += jnp.sum(dz.reshape(T // 8, 8, W), axis=0)

    row_spec = pl.BlockSpec((T, W), lambda i: (nb - 1 - i, 0))
    return pl.pallas_call(
        body, name=name,
        out_shape=(jax.ShapeDtypeStruct((S, W), BF16), jax.ShapeDtypeStruct((8, W), F32)),
        grid=(nb,),
        in_specs=[row_spec, pl.BlockSpec((1, W), lambda i: (0, 0)), row_spec],
        out_specs=(row_spec, pl.BlockSpec((8, W), lambda i: (0, 0))),
        scratch_shapes=[pltpu.VMEM((8, W), F32)],
        compiler_params=_cparams("arbitrary"),
    )(zf, bf, dc)


def _merge_fwd(zg, yp, yf, *, name):
    S, D = yp.shape
    ts = _row_block(S, D, 1024 * 1024)

    def body(zg_ref, yp_ref, yf_ref, o_ref):
        o_ref[...] = (_sigmoid(zg_ref[:, :D]) * yp_ref[...] + _sigmoid(zg_ref[:, D:]) * yf_ref[...]).astype(BF16)

    row = pl.BlockSpec((ts, D), lambda i: (i, 0))
    return pl.pallas_call(
        body, name=name, out_shape=jax.ShapeDtypeStruct((S, D), BF16), grid=(S // ts,),
        in_specs=[pl.BlockSpec((ts, 2 * D), lambda i: (i, 0)), row, row], out_specs=row,
        compiler_params=_cparams("parallel"),
    )(zg, yp, yf)


def _merge_bwd(zg, yp, yf, dmerged, *, name):
    S, D = yp.shape
    ts = _row_block(S, D, 1024 * 1024)

    def body(zg_ref, yp_ref, yf_ref, dm_ref, dyp_ref, dyf_ref, dzg_ref):
        dm = dm_ref[...]
        sp = _sigmoid(zg_ref[:, :D])
        sf = _sigmoid(zg_ref[:, D:])
        dyp_ref[...] = (dm * sp).astype(BF16)
        dyf_ref[...] = (dm * sf).astype(BF16)
        dzg_ref[:, :D] = (dm * yp_ref[...] * (sp * (1.0 - sp))).astype(BF16)
        dzg_ref[:, D:] = (dm * yf_ref[...] * (sf * (1.0 - sf))).astype(BF16)

    row = pl.BlockSpec((ts, D), lambda i: (i, 0))
    wide = pl.BlockSpec((ts, 2 * D), lambda i: (i, 0))
    return pl.pallas_call(
        body, name=name,
        out_shape=(jax.ShapeDtypeStruct((S, D), BF16), jax.ShapeDtypeStruct((S, D), BF16),
                   jax.ShapeDtypeStruct((S, 2 * D), BF16)),
        grid=(S // ts,), in_specs=[wide, row, row, row], out_specs=(row, row, wide),
        compiler_params=_cparams("parallel"),
    )(zg, yp, yf, dmerged)


NEG_BIG = -1e30


FOX_BLOCK = 1024
PAIR = LANES // FOX_DH
N_PAIRS = FOX_HEADS // PAIR


def _fox_fwd(zqkv, augq, augk, *, name):
    S = zqkv.shape[0]
    bq = _pick(S, FOX_BLOCK, 128)
    nq = S // bq
    scale = 1.0 / math.sqrt(FOX_DH)

    def body(q_ref, k_ref, v_ref, aq_ref, ak_ref, o_ref, lse_ref):
        i = pl.program_id(1)
        lane = lax.broadcasted_iota(jnp.int32, (1, LANES), 1)
        first = lane < FOX_DH
        q2 = q_ref[...] * scale
        zero = jnp.zeros_like(q2)
        qh = (jnp.concatenate([jnp.where(first, q2, zero), aq_ref[0]], axis=1),
              jnp.concatenate([jnp.where(first, zero, q2), aq_ref[1]], axis=1))

        def step(j, carry, masked):
            start = pl.multiple_of(j * bq, bq)
            kb = k_ref[pl.ds(start, bq), :]
            vb = v_ref[pl.ds(start, bq), :]
            one = jnp.ones_like(vb)
            vh = (jnp.where(first, vb, one), jnp.where(first, one, vb))
            out = []
            for h in range(PAIR):
                m, acc = carry[h]
                kh = jnp.concatenate([kb, ak_ref[h, pl.ds(start, bq), :]], axis=1)
                s = lax.dot_general(qh[h], kh, (((1,), (1,)), ((), ())), preferred_element_type=F32)
                if masked:
                    r = lax.broadcasted_iota(jnp.int32, (bq, bq), 0)
                    c = lax.broadcasted_iota(jnp.int32, (bq, bq), 1)
                    s = jnp.where(c <= r, s, NEG_BIG)
                m_new = jnp.maximum(m, jnp.max(s, axis=-1, keepdims=True))
                alpha = jnp.exp(m - m_new)
                p = jnp.exp(s - m_new).astype(BF16)
                acc = alpha * acc + jnp.dot(p, vh[h], preferred_element_type=F32)
                out.append((m_new, acc))
            return tuple(out)

        init = tuple((jnp.full((bq, 1), NEG_BIG, F32), jnp.zeros((bq, LANES), F32)) for _ in range(PAIR))
        carry = lax.fori_loop(0, i, lambda j, c: step(j, c, False), init)
        (ma, acca), (mb, accb) = step(i, carry, True)
        num = jnp.where(first, acca, accb)
        den = jnp.where(first, pltpu.roll(acca, FOX_DH, 1), pltpu.roll(accb, FOX_DH, 1))
        o_ref[...] = num / den
        lse_ref[...] = jnp.where(first, mb, ma) + jnp.log(jnp.where(first, accb, acca))

    npair = N_PAIRS
    return pl.pallas_call(
        body, name=name,
        out_shape=(jax.ShapeDtypeStruct((S, FOX_W), F32), jax.ShapeDtypeStruct((S, FOX_W), F32)),
        grid=(npair, nq),
        in_specs=[pl.BlockSpec((bq, LANES), lambda hp, i: (i, hp)),
                  pl.BlockSpec((S, LANES), lambda hp, i: (0, npair + hp)),
                  pl.BlockSpec((S, LANES), lambda hp, i: (0, 2 * npair + hp)),
                  pl.BlockSpec((PAIR, bq, LANES), lambda hp, i: (hp, i, 0)),
                  pl.BlockSpec((PAIR, S, LANES), lambda hp, i: (hp, 0, 0))],
        out_specs=(pl.BlockSpec((bq, LANES), lambda hp, i: (i, hp)),
                   pl.BlockSpec((bq, LANES), lambda hp, i: (i, hp))),
        compiler_params=_cparams("parallel", "arbitrary"),
    )(zqkv, zqkv, zqkv, augq, augk)


def _head_rowsum(a, b, *, name):
    S, W = a.shape
    ts = _pick(S, 1024, 8)

    def body(a_ref, b_ref, o_ref):
        prod = a_ref[...].astype(F32) * b_ref[...].astype(F32)
        hi = prod.astype(BF16)
        lo = (prod - hi.astype(F32)).astype(BF16)
        r = lax.broadcasted_iota(jnp.int32, (W, LANES), 0)
        c = lax.broadcasted_iota(jnp.int32, (W, LANES), 1)
        sel = jnp.where(r // FOX_DH == c, 1.0, 0.0).astype(BF16)
        o_ref[...] = (jnp.dot(hi, sel, preferred_element_type=F32) + jnp.dot(lo, sel, preferred_element_type=F32))

    return pl.pallas_call(
        body, name=name, out_shape=jax.ShapeDtypeStruct((S, LANES), F32), grid=(S // ts,),
        in_specs=[pl.BlockSpec((ts, W), lambda i: (i, 0)), pl.BlockSpec((ts, W), lambda i: (i, 0))],
        out_specs=pl.BlockSpec((ts, LANES), lambda i: (i, 0)),
        compiler_params=_cparams("parallel"),
    )(a, b)


def _fox_bwd(zqkv, do, augq, augk, lse_row, delta_row, *, name):
    S = zqkv.shape[0]
    bk = _pick(S, FOX_BLOCK, 128)
    nk = S // bk
    scale = 1.0 / math.sqrt(FOX_DH)
    npair = N_PAIRS

    def body(q_ref, k_ref, v_ref, do_ref, ak_ref, aq_ref, lse_ref, dl_ref, dq_ref, dk_ref, dv_ref, dck_ref, dcq_ref):
        j = pl.program_id(1)

        @pl.when(j == 0)
        def _():
            dq_ref[...] = jnp.zeros_like(dq_ref)
            dcq_ref[...] = jnp.zeros_like(dcq_ref)

        lane = lax.broadcasted_iota(jnp.int32, (1, LANES), 1)
        first = lane < FOX_DH
        kb = k_ref[...]
        vb = v_ref[...]
        kh = (jnp.concatenate([kb, ak_ref[0]], axis=1), jnp.concatenate([kb, ak_ref[1]], axis=1))

        def step(i, carry, masked):
            start = pl.multiple_of(i * bk, bk)
            qs = q_ref[pl.ds(start, bk), :] * scale
            dob = do_ref[pl.ds(start, bk), :]
            zero = jnp.zeros_like(qs)
            qh = (jnp.where(first, qs, zero), jnp.where(first, zero, qs))
            doh = (jnp.where(first, dob, zero), jnp.where(first, zero, dob))
            out = []
            dqc = []
            for h in range(PAIR):
                dk, dv, dc = carry[h]
                qaug = jnp.concatenate([qh[h], aq_ref[h, pl.ds(start, bk), :]], axis=1)
                lse = lse_ref[h, :, pl.ds(start, bk)]
                dl = dl_ref[h, :, pl.ds(start, bk)]
                st = lax.dot_general(kh[h], qaug, (((1,), (1,)), ((), ())), preferred_element_type=F32)
                pt = jnp.exp(st - lse)
                if masked:
                    r = lax.broadcasted_iota(jnp.int32, (bk, bk), 0)
                    c = lax.broadcasted_iota(jnp.int32, (bk, bk), 1)
                    pt = jnp.where(c >= r, pt, 0.0)
                dpt = lax.dot_general(vb, doh[h], (((1,), (1,)), ((), ())), preferred_element_type=F32)
                dst = pt * (dpt - dl)
                pt_bf = pt.astype(BF16)
                dst_bf = dst.astype(BF16)
                dv = dv + jnp.dot(pt_bf, dob, preferred_element_type=F32)
                dk = dk + jnp.dot(dst_bf, qs, preferred_element_type=F32)
                dc = dc - jnp.sum(dst, axis=-1, keepdims=True)
                dcq_ref[h, :, pl.ds(start, bk)] += jnp.sum(dst, axis=0, keepdims=True)
                dqc.append(lax.dot_general(dst_bf, kb, (((0,), (0,)), ((), ())), preferred_element_type=F32))
                out.append((dk, dv, dc))
            dq_ref[pl.ds(start, bk), :] += jnp.where(first, dqc[0], dqc[1])
            return tuple(out)

        init = tuple((jnp.zeros((bk, LANES), F32), jnp.zeros((bk, LANES), F32), jnp.zeros((bk, 1), F32))
                     for _ in range(PAIR))
        carry = step(j, init, True)
        (dka, dva, dca), (dkb, dvb, dcb) = lax.fori_loop(j + 1, nk, lambda i, c: step(i, c, False), carry)
        dk_ref[...] = jnp.where(first, dka, dkb).astype(BF16)
        dv_ref[...] = jnp.where(first, dva, dvb).astype(BF16)
        dck_ref[0] = dca
        dck_ref[1] = dcb

        @pl.when(j == nk - 1)
        def _():
            dq_ref[...] = dq_ref[...] * scale

    rowfull = pl.BlockSpec((PAIR, 1, S), lambda hp, j: (hp, 0, 0))
    return pl.pallas_call(
        body, name=name,
        out_shape=(jax.ShapeDtypeStruct((S, FOX_W), F32), jax.ShapeDtypeStruct((S, FOX_W), BF16),
                   jax.ShapeDtypeStruct((S, FOX_W), BF16), jax.ShapeDtypeStruct((FOX_HEADS, S, 1), F32),
                   jax.ShapeDtypeStruct((FOX_HEADS, 1, S), F32)),
        grid=(npair, nk),
        in_specs=[pl.BlockSpec((S, LANES), lambda hp, j: (0, hp)),
                  pl.BlockSpec((bk, LANES), lambda hp, j: (j, npair + hp)),
                  pl.BlockSpec((bk, LANES), lambda hp, j: (j, 2 * npair + hp)),
                  pl.BlockSpec((S, LANES), lambda hp, j: (0, hp)),
                  pl.BlockSpec((PAIR, bk, LANES), lambda hp, j: (hp, j, 0)),
                  pl.BlockSpec((PAIR, S, LANES), lambda hp, j: (hp, 0, 0)), rowfull, rowfull],
        out_specs=(pl.BlockSpec((S, LANES), lambda hp, j: (0, hp)),
                   pl.BlockSpec((bk, LANES), lambda hp, j: (j, hp)),
                   pl.BlockSpec((bk, LANES), lambda hp, j: (j, hp)),
                   pl.BlockSpec((PAIR, bk, 1), lambda hp, j: (hp, j, 0)), rowfull),
        compiler_params=_cparams("parallel", "arbitrary"),
    )(zqkv, zqkv, zqkv, do, augk, augq, lse_row, delta_row)


def _xattn_fwd(q, kv, *, name):
    S, W = q.shape
    M = kv.shape[0]
    tq = _pick(S, 512, 8)
    scale = 1.0 / math.sqrt(X_DH)

    def body(q_ref, kv_ref, o_ref):
        for h in range(X_HEADS):
            cols = slice(h * X_DH, (h + 1) * X_DH)
            vcols = slice(W + h * X_DH, W + (h + 1) * X_DH)
            s = lax.dot_general(q_ref[:, cols], kv_ref[:, cols], (((1,), (1,)), ((), ())),
                                preferred_element_type=F32) * scale
            e = jnp.exp(s - jnp.max(s, axis=-1, keepdims=True))
            p = e / jnp.sum(e, axis=-1, keepdims=True)
            o_ref[:, cols] = jnp.dot(p.astype(BF16), kv_ref[:, vcols], preferred_element_type=F32).astype(BF16)

    return pl.pallas_call(
        body, name=name, out_shape=jax.ShapeDtypeStruct((S, W), BF16), grid=(S // tq,),
        in_specs=[pl.BlockSpec((tq, W), lambda i: (i, 0)), pl.BlockSpec((M, 2 * W), lambda i: (0, 0))],
        out_specs=pl.BlockSpec((tq, W), lambda i: (i, 0)),
        compiler_params=_cparams("parallel"),
    )(q, kv)


def _xattn_bwd(q, kv, do, *, name):
    S, W = q.shape
    M = kv.shape[0]
    tq = _pick(S, 512, 8)
    scale = 1.0 / math.sqrt(X_DH)

    def body(q_ref, kv_ref, do_ref, dq_ref, dkv_ref):
        i = pl.program_id(0)

        @pl.when(i == 0)
        def _():
            dkv_ref[...] = jnp.zeros_like(dkv_ref)

        for h in range(X_HEADS):
            cols = slice(h * X_DH, (h + 1) * X_DH)
            vcols = slice(W + h * X_DH, W + (h + 1) * X_DH)
            qh = q_ref[:, cols]
            kh = kv_ref[:, cols]
            vh = kv_ref[:, vcols]
            doh = do_ref[:, cols]
            s = lax.dot_general(qh, kh, (((1,), (1,)), ((), ())), preferred_element_type=F32) * scale
            e = jnp.exp(s - jnp.max(s, axis=-1, keepdims=True))
            p = e / jnp.sum(e, axis=-1, keepdims=True)
            dp = lax.dot_general(doh, vh, (((1,), (1,)), ((), ())), preferred_element_type=F32)
            ds = (p * (dp - jnp.sum(p * dp, axis=-1, keepdims=True)) * scale).astype(BF16)
            dq_ref[:, cols] = jnp.dot(ds, kh, preferred_element_type=F32).astype(BF16)
            dkv_ref[:, cols] += lax.dot_general(ds, qh, (((0,), (0,)), ((), ())), preferred_element_type=F32)
            dkv_ref[:, vcols] += lax.dot_general(p.astype(BF16), doh, (((0,), (0,)), ((), ())),
                                                 preferred_element_type=F32)

    return pl.pallas_call(
        body, name=name,
        out_shape=(jax.ShapeDtypeStruct((S, W), BF16), jax.ShapeDtypeStruct((M, 2 * W), F32)),
        grid=(S // tq,),
        in_specs=[pl.BlockSpec((tq, W), lambda i: (i, 0)), pl.BlockSpec((M, 2 * W), lambda i: (0, 0)),
                  pl.BlockSpec((tq, W), lambda i: (i, 0))],
        out_specs=(pl.BlockSpec((tq, W), lambda i: (i, 0)), pl.BlockSpec((M, 2 * W), lambda i: (0, 0))),
        compiler_params=_cparams("arbitrary"),
    )(q, kv, do)


GELU_C = math.sqrt(2.0 / math.pi)
GELU_A = 0.044715
CONV_HALO = 8


def _gelu_parts(x):
    u = GELU_C * (x + GELU_A * x * x * x)
    t = jnp.tanh(u)
    g = 0.5 * x * (1.0 + t)
    dg = 0.5 * (1.0 + t) + 0.5 * x * (1.0 - t * t) * (GELU_C * (1.0 + 3.0 * GELU_A * x * x))
    return g, dg


def _conv3(ext, w_ref, b_ref):
    return (w_ref[2:3, :] * ext + w_ref[1:2, :] * pltpu.roll(ext, 1, 0)
            + w_ref[0:1, :] * pltpu.roll(ext, 2, 0) + b_ref[...])


def _convglu_fwd(z, conv_w, conv_b, *, name):
    S, F2 = z.shape
    F = F2 // 2
    tc = _pick(F, 1408)
    ncol = F // tc
    T = _pick(S, 512, 8)
    hb = T // CONV_HALO

    def body(zg_ref, zu_ref, zgp_ref, zup_ref, wg_ref, wu_ref, bg_ref, bu_ref, act_ref):
        i = pl.program_id(1)
        first = (i > 0).astype(F32)

        def conv(z_ref, zp_ref, w_ref, b_ref):
            ext = jnp.concatenate([zp_ref[...] * first, z_ref[...]], axis=0)
            return _conv3(ext, w_ref, b_ref)[CONV_HALO:, :]

        gc = conv(zg_ref, zgp_ref, wg_ref, bg_ref)
        uc = conv(zu_ref, zup_ref, wu_ref, bu_ref)
        act_ref[...] = (_gelu_parts(gc)[0] * uc).astype(BF16)

    cur = lambda off: pl.BlockSpec((T, tc), lambda j, i: (i, j + off))
    prev = lambda off: pl.BlockSpec((CONV_HALO, tc), lambda j, i: (jnp.maximum(i * hb - 1, 0), j + off))
    vec = lambda rows, off: pl.BlockSpec((rows, tc), lambda j, i: (0, j + off))
    return pl.pallas_call(
        body, name=name, out_shape=jax.ShapeDtypeStruct((S, F), BF16), grid=(ncol, S // T),
        in_specs=[cur(0), cur(ncol), prev(0), prev(ncol), vec(3, 0), vec(3, ncol), vec(1, 0), vec(1, ncol)],
        out_specs=pl.BlockSpec((T, tc), lambda j, i: (i, j)),
        compiler_params=_cparams("parallel", "parallel"),
    )(z, z, z, z, conv_w, conv_w, conv_b, conv_b)


def _convglu_bwd(z, dact, conv_w, conv_b, *, name):
    S, F2 = z.shape
    F = F2 // 2
    tc = _pick(F, 1408)
    ncol = F // tc
    T = _pick(S, 256, 8)
    nrow = S // T
    hb = T // CONV_HALO
    TE = T + CONV_HALO

    def body(zg_ref, zu_ref, zgp_ref, zup_ref, zgn_ref, zun_ref, da_ref, dan_ref,
             wg_ref, wu_ref, bg_ref, bu_ref, dzg_ref, dzu_ref, dwg_ref, dwu_ref):
        i = pl.program_id(1)
        first = (i > 0).astype(F32)
        last = (i < nrow - 1).astype(F32)

        @pl.when(i == 0)
        def _():
            dwg_ref[...] = jnp.zeros_like(dwg_ref)
            dwu_ref[...] = jnp.zeros_like(dwu_ref)

        extg = jnp.concatenate([zgp_ref[...] * first, zg_ref[...], zgn_ref[...]], axis=0)
        extu = jnp.concatenate([zup_ref[...] * first, zu_ref[...], zun_ref[...]], axis=0)
        gc = _conv3(extg, wg_ref, bg_ref)[CONV_HALO:, :]
        uc = _conv3(extu, wu_ref, bu_ref)[CONV_HALO:, :]
        da = jnp.concatenate([da_ref[...], dan_ref[...] * last], axis=0)
        gl, dgl = _gelu_parts(gc)
        dgc = da * uc * dgl
        duc = da * gl

        def back(d, ext, w_ref, dz_ref, dw_ref):
            dz = w_ref[2:3, :] * d + w_ref[1:2, :] * pltpu.roll(d, TE - 1, 0) + w_ref[0:1, :] * pltpu.roll(d, TE - 2, 0)
            dz_ref[...] = dz[:T, :].astype(BF16)
            dc = d[:T, :]
            z0 = ext[CONV_HALO:CONV_HALO + T, :]
            z1 = pltpu.roll(ext, 1, 0)[CONV_HALO:CONV_HALO + T, :]
            z2 = pltpu.roll(ext, 2, 0)[CONV_HALO:CONV_HALO + T, :]
            rows = [jnp.sum(dc * z2, axis=0, keepdims=True), jnp.sum(dc * z1, axis=0, keepdims=True),
                    jnp.sum(dc * z0, axis=0, keepdims=True), jnp.sum(dc, axis=0, keepdims=True)]
            dw_ref[0:4, :] += jnp.concatenate(rows, axis=0)

        back(dgc, extg, wg_ref, dzg_ref, dwg_ref)
        back(duc, extu, wu_ref, dzu_ref, dwu_ref)

    cur = lambda off: pl.BlockSpec((T, tc), lambda j, i: (i, j + off))
    prev = lambda off: pl.BlockSpec((CONV_HALO, tc), lambda j, i: (jnp.maximum(i * hb - 1, 0), j + off))
    nxt = lambda off: pl.BlockSpec((CONV_HALO, tc), lambda j, i: (jnp.minimum((i + 1) * hb, S // CONV_HALO - 1), j + off))
    vec = lambda rows, off: pl.BlockSpec((rows, tc), lambda j, i: (0, j + off))
    dzg, dzu, dwg, dwu = pl.pallas_call(
        body, name=name,
        out_shape=(jax.ShapeDtypeStruct((S, F), BF16), jax.ShapeDtypeStruct((S, F), BF16),
                   jax.ShapeDtypeStruct((8, F), F32), jax.ShapeDtypeStruct((8, F), F32)),
        grid=(ncol, nrow),
        in_specs=[cur(0), cur(ncol), prev(0), prev(ncol), nxt(0), nxt(ncol), cur(0), nxt(0),
                  vec(3, 0), vec(3, ncol), vec(1, 0), vec(1, ncol)],
        out_specs=(cur(0), cur(0), vec(8, 0), vec(8, 0)),
        compiler_params=_cparams("parallel", "arbitrary"),
    )(z, z, z, z, z, z, dact, dact, conv_w, conv_w, conv_b, conv_b)
    return dzg, dzu, dwg, dwu


OFF_QKV = POOL_W
OFF_G = POOL_W + 3 * FOX_W


def _prep_layer_weights(w, l):
    D = w["w_in"].shape[1]
    w_in = w["w_in"][l]
    off_f = OFF_G
    pad = jnp.zeros((D, LANES - N_FGATE), w_in.dtype)
    w_in_r = jnp.concatenate([w_in[:, :off_f], w_in[:, off_f + N_FGATE:], w_in[:, off_f:off_f + N_FGATE], pad], axis=1)
    return dict(w_in_r=w_in_r.astype(BF16),
                w_pool_br=w["w_pool_br"][l], w_fox_br=w["w_fox_br"][l], w_mix_out=w["w_mix_out"][l],
                w_xq=w["w_xq"][l], w_xkv=w["w_xkv"][l], w_xo=w["w_xo"][l], w_up=w["w_up"][l], w_down=w["w_down"][l])


def _row(v):
    return v.reshape(1, -1)


def _layer_fwd(x, mem, wl, p, l):
    S, D = x.shape
    n = lambda s: f"l{l}_{s}"
    sv = {"x0": x}
    h1 = _rms_fwd(x, _row(p["mix_pre_g"][l]), out_dtype=BF16, name=n("rms1"))
    w_in_r = wl["w_in_r"]
    og = OFF_G
    zu = _matmul(h1, w_in_r[:, :OFF_QKV], out_dtype=F32, name=n("mm_zu"))
    zqkv = _matmul(h1, w_in_r[:, OFF_QKV:og], out_dtype=BF16, name=n("mm_zqkv"))
    zg = _matmul(h1, w_in_r[:, og:og + 2 * D], out_dtype=F32, name=n("mm_zg"))
    zf = _matmul(h1, w_in_r[:, og + 2 * D:], out_dtype=F32, name=n("mm_zf"))
    bf = jnp.pad(p["b_forget"][l], (0, LANES - N_FGATE)).reshape(1, LANES)
    _, augq, augk = _fgate_fwd(zf, bf, name=n("fgate"))
    o_fox, lse2 = _fox_fwd(zqkv, augq, augk, name=n("fox"))
    pooled, mixed = _pool_fwd(zu, p["pool_w"][l], _row(p["pool_scale"][l]), name=n("pool"))
    y_pool = _matmul(mixed, wl["w_pool_br"], out_dtype=F32, name=n("mm_ypool"))
    y_fox = _matmul(o_fox, wl["w_fox_br"], out_dtype=F32, name=n("mm_yfox"))
    merged = _merge_fwd(zg, y_pool, y_fox, name=n("merge"))
    r1 = _matmul(merged, wl["w_mix_out"], out_dtype=F32, name=n("mm_r1"))
    x1 = _add_rms(x, r1, _row(p["mix_post_g"][l]), name=n("addrms1"))
    sv.update(h1=h1, zg=zg, zf=zf, bf=bf, zqkv=zqkv, augq=augq, augk=augk, o_fox=o_fox, lse2=lse2,
              pooled=pooled, mixed=mixed, y_pool=y_pool, y_fox=y_fox, merged=merged, r1=r1, x1=x1)
    h2 = _rms_fwd(x1, _row(p["xa_pre_g"][l]), out_dtype=BF16, name=n("rms2"))
    mem_n = _rms_fwd(mem, _row(p["mem_g"][l]), out_dtype=BF16, name=n("rms_mem"))
    q2 = _matmul(h2, wl["w_xq"], out_dtype=BF16, name=n("mm_q2"))
    kv = _matmul(mem_n, wl["w_xkv"], out_dtype=BF16, name=n("mm_kv"))
    o2 = _xattn_fwd(q2, kv, name=n("xattn"))
    a2 = _matmul(o2, wl["w_xo"], out_dtype=F32, name=n("mm_a2"))
    x2 = _add_rms(x1, a2, _row(p["xa_post_g"][l]), name=n("addrms2"))
    sv.update(h2=h2, mem_n=mem_n, q2=q2, kv=kv, o2=o2, a2=a2, x2=x2)
    h3 = _rms_fwd(x2, _row(p["ffn_pre_g"][l]), out_dtype=BF16, name=n("rms3"))
    z3 = _matmul(h3, wl["w_up"], out_dtype=F32, name=n("mm_z3"))
    act = _convglu_fwd(z3, p["conv_w"][l], _row(p["conv_b"][l]), name=n("convglu"))
    d3 = _matmul(act, wl["w_down"], out_dtype=F32, name=n("mm_d3"))
    x3 = _add_rms(x2, d3, _row(p["ffn_post_g"][l]), name=n("addrms3"))
    sv.update(h3=h3, z3=z3, act=act, d3=d3)
    return x3, sv


def _layer_bwd(dx, mem, wl, p, l, sv):
    S, D = dx.shape
    n = lambda s: f"l{l}_b_{s}"
    g = {}
    red = lambda part: jnp.sum(part, axis=0)
    dd3, dg = _rms_bwd(sv["d3"], _row(p["ffn_post_g"][l]), dx, out_dtype=BF16, name=n("rms3post"))
    g["ffn_post_g"] = red(dg)
    dact = _matmul(dd3, wl["w_down"], tb=True, out_dtype=F32, name=n("mm_dact"))
    g["w_down"] = _matmul(sv["act"], dd3, ta=True, out_dtype=F32, name=n("mm_dwdown"))
    dzg3, dzu3, dwg, dwu = _convglu_bwd(sv["z3"], dact, p["conv_w"][l], _row(p["conv_b"][l]), name=n("convglu"))
    g["conv_w"] = jnp.concatenate([dwg[:3], dwu[:3]], axis=1)
    g["conv_b"] = jnp.concatenate([dwg[3], dwu[3]], axis=0)
    dz3 = jnp.concatenate([dzg3, dzu3], axis=1)
    dh3 = _matmul(dz3, wl["w_up"], tb=True, out_dtype=F32, name=n("mm_dh3"))
    g["w_up"] = _matmul(sv["h3"], dz3, ta=True, out_dtype=F32, name=n("mm_dwup"))
    dx, dg = _rms_bwd(sv["x2"], _row(p["ffn_pre_g"][l]), dh3, dx, out_dtype=F32, name=n("rms3pre"))
    g["ffn_pre_g"] = red(dg)
    da2, dg = _rms_bwd(sv["a2"], _row(p["xa_post_g"][l]), dx, out_dtype=BF16, name=n("rms2post"))
    g["xa_post_g"] = red(dg)
    do2 = _matmul(da2, wl["w_xo"], tb=True, out_dtype=BF16, name=n("mm_do2"))
    g["w_xo"] = _matmul(sv["o2"], da2, ta=True, out_dtype=F32, name=n("mm_dwxo"))
    dq2, dkv = _xattn_bwd(sv["q2"], sv["kv"], do2, name=n("xattn"))
    dh2 = _matmul(dq2, wl["w_xq"], tb=True, out_dtype=F32, name=n("mm_dh2"))
    g["w_xq"] = _matmul(sv["h2"], dq2, ta=True, out_dtype=F32, name=n("mm_dwxq"))
    dmem_n = _matmul(dkv, wl["w_xkv"], tb=True, out_dtype=F32, name=n("mm_dmemn"))
    g["w_xkv"] = _matmul(sv["mem_n"], dkv, ta=True, out_dtype=F32, name=n("mm_dwxkv"))
    _, dg = _rms_bwd(mem, _row(p["mem_g"][l]), dmem_n, out_dtype=BF16, name=n("rms_mem"))
    g["mem_g"] = red(dg)
    dx, dg = _rms_bwd(sv["x1"], _row(p["xa_pre_g"][l]), dh2, dx, out_dtype=F32, name=n("rms2pre"))
    g["xa_pre_g"] = red(dg)
    dr1, dg = _rms_bwd(sv["r1"], _row(p["mix_post_g"][l]), dx, out_dtype=BF16, name=n("rms1post"))
    g["mix_post_g"] = red(dg)
    dmerged = _matmul(dr1, wl["w_mix_out"], tb=True, out_dtype=F32, name=n("mm_dmerged"))
    g["w_mix_out"] = _matmul(sv["merged"], dr1, ta=True, out_dtype=F32, name=n("mm_dwmo"))
    dyp, dyf, dzg = _merge_bwd(sv["zg"], sv["y_pool"], sv["y_fox"], dmerged, name=n("merge"))
    dmixed = _matmul(dyp, wl["w_pool_br"], tb=True, out_dtype=F32, name=n("mm_dmixed"))
    g["w_pool_br"] = _matmul(sv["mixed"], dyp, ta=True, out_dtype=F32, name=n("mm_dwpb"))
    dofox = _matmul(dyf, wl["w_fox_br"], tb=True, out_dtype=F32, name=n("mm_dofox"))
    g["w_fox_br"] = _matmul(sv["o_fox"], dyf, ta=True, out_dtype=F32, name=n("mm_dwfb"))
    dzu, dpw, dsc = _pool_bwd(sv["pooled"], dmixed, p["pool_w"][l], _row(p["pool_scale"][l]), name=n("pool"))
    g["pool_w"] = dpw
    g["pool_scale"] = red(dsc)
    delta = _head_rowsum(dofox, sv["o_fox"], name=n("delta"))
    lse = sv["lse2"].reshape(S, N_PAIRS, PAIR, FOX_DH)[:, :, ::-1, 0].reshape(S, FOX_HEADS)
    dq, dk, dv, dck, dcq = _fox_bwd(sv["zqkv"], dofox.astype(BF16), sv["augq"], sv["augk"],
                                    lse.T.reshape(FOX_HEADS, 1, S), delta[:, :FOX_HEADS].T.reshape(FOX_HEADS, 1, S),
                                    name=n("fox"))
    dc = dcq.reshape(FOX_HEADS, S) + dck.reshape(FOX_HEADS, S)
    dc_pad = jnp.pad(dc.T, ((0, 0), (0, LANES - FOX_HEADS)))
    dzf, db = _fgate_bwd(sv["zf"], sv["bf"], dc_pad, name=n("fgate"))
    g["b_forget"] = red(db)[:N_FGATE]
    dz_cat = jnp.concatenate([dzu, dq.astype(BF16), dk, dv, dzg, dzf], axis=1)
    dh1 = _matmul(dz_cat, wl["w_in_r"], tb=True, out_dtype=F32, name=n("mm_dh1"))
    dw_in_r = _matmul(sv["h1"], dz_cat, ta=True, out_dtype=F32, name=n("mm_dwin"))
    og = OFF_G
    g["w_in"] = jnp.concatenate([dw_in_r[:, :og], dw_in_r[:, og + 2 * D:og + 2 * D + N_FGATE],
                                 dw_in_r[:, og:og + 2 * D]], axis=1)
    dx, dg = _rms_bwd(sv["x0"], _row(p["mix_pre_g"][l]), dh1, dx, out_dtype=F32, name=n("rms1pre"))
    g["mix_pre_g"] = red(dg)
    return dx, g


MATMUL_WEIGHTS = ("w_in", "w_pool_br", "w_fox_br", "w_mix_out", "w_xq", "w_xkv", "w_xo", "w_up", "w_down")
WEIGHT_NAMES = ("mix_pre_g", "mix_post_g", "w_in", "b_forget", "pool_w", "pool_scale", "w_pool_br", "w_fox_br",
                "w_mix_out", "xa_pre_g", "xa_post_g", "mem_g", "w_xq", "w_xkv", "w_xo", "ffn_pre_g", "ffn_post_g",
                "w_up", "conv_w", "conv_b", "w_down")


def _local_step(x, mem, loss_target, wfull, p):
    L = p["mix_pre_g"].shape[0]
    saved, wls = [], []
    h = x
    for l in range(L):
        wl = _prep_layer_weights(wfull, l)
        h, sv = _layer_fwd(h, mem, wl, p, l)
        saved.append(sv)
        wls.append(wl)
    D = x.shape[1]
    dy, sq = _loss_head(h, loss_target, name="loss_head")
    loss = 0.5 * jnp.sum(sq) / D
    grads = []
    dx = dy
    for l in reversed(range(L)):
        dx, g = _layer_bwd(dx, mem, wls[l], p, l, saved[l])
        grads.append(g)
    grads = grads[::-1]
    gfull = {k: jnp.stack([grads[l][k] for l in range(L)]) for k in WEIGHT_NAMES}
    return loss, dx, gfull


PACK_W = 512
PACK_ROW_ALIGN = 1024
N_CHIPS = 4
N_DEV = 8
SHARDED = (("w_in", 2), ("w_pool_br", 2), ("w_fox_br", 2), ("w_mix_out", 1), ("w_xq", 1), ("w_xkv", 1),
           ("w_xo", 2), ("w_up", 2), ("w_down", 1), ("conv_w", 2))
REPLICATED = ("mix_pre_g", "mix_post_g", "b_forget", "pool_w", "pool_scale", "xa_pre_g", "xa_post_g", "mem_g",
              "ffn_pre_g", "ffn_post_g", "conv_b")


def _round_up(n, m):
    return -(-n // m) * m


def _pack(arrs, rows):
    flat = jnp.concatenate([a.reshape(-1) for a in arrs])
    return jnp.pad(flat, (0, rows * PACK_W - flat.shape[0])).reshape(rows, PACK_W)


def _unpack(buf, shapes):
    flat = buf.reshape(-1)
    out, off = [], 0
    for s in shapes:
        n = math.prod(s)
        out.append(flat[off:off + n].reshape(s))
        off += n
    return out


ANY = pl.BlockSpec(memory_space=pl.ANY)


def _remote(send_sems, recv_sems, k, src, dst, to):
    return pltpu.make_async_remote_copy(src_ref=src, dst_ref=dst, send_sem=send_sems.at[k], recv_sem=recv_sems.at[k],
                                        device_id=to, device_id_type=MESH)


def _my_place():
    return lax.axis_index("x"), lax.axis_index("y"), lax.axis_index("c")


def _place_shard(wpack, rows_b, chip):
    W = wpack.shape[1]
    tr = _pick(rows_b, 1024, 16)

    def body(chip_ref, w_ref, o_ref):
        o_ref[...] = w_ref[...].astype(BF16)

    return pl.pallas_call(
        body, name="place_shard", out_shape=jax.ShapeDtypeStruct((N_CHIPS, rows_b, W), BF16),
        grid_spec=pltpu.PrefetchScalarGridSpec(
            num_scalar_prefetch=1, grid=(rows_b // tr,),
            in_specs=[pl.BlockSpec((tr, W), lambda i, chip_ref: (i, 0))],
            out_specs=pl.BlockSpec((None, tr, W), lambda i, chip_ref: (chip_ref[0], i, 0))),
        compiler_params=_cparams("parallel"),
    )(chip, wpack)


def _gather_weights(ob, cw):
    _, RB, W = ob.shape
    HB = RB // 2
    assert HB % 16 == 0

    def body(ob_in, cw_ref, ob_ref, oc_ref, send_sems, recv_sems, local_sem):
        del ob_in
        x, y, c = _my_place()
        me = 2 * x + y
        sibling = (x, y, 1 - c)
        chips = [(1 - x, y), (x, 1 - y), (1 - x, 1 - y)]
        half = pl.ds(pl.multiple_of(c * HB, 16), HB)
        other = pl.ds(pl.multiple_of((1 - c) * HB, 16), HB)
        rc = functools.partial(_remote, send_sems, recv_sems)
        mine_c = pltpu.make_async_copy(cw_ref, oc_ref.at[me], local_sem)
        mine_c.start()
        first = []
        for k, (px, py) in enumerate(chips):
            first.append(rc(k, ob_ref.at[me, half], ob_ref.at[me, half], (px, py, c)))
            first.append(rc(3 + k, cw_ref, oc_ref.at[me], (px, py, c)))
        for cp in first:
            cp.start()
        passed = []
        for k, (px, py) in enumerate(chips):
            src = 2 * px + py
            rc(k, ob_ref.at[src, half], ob_ref.at[src, half], (px, py, c)).wait_recv()
            fwd = rc(6 + k, ob_ref.at[src, half], ob_ref.at[src, half], sibling)
            fwd.start()
            passed.append(fwd)
        for k, (px, py) in enumerate(chips):
            src = 2 * px + py
            rc(3 + k, cw_ref, oc_ref.at[src], (px, py, c)).wait_recv()
            rc(6 + k, ob_ref.at[src, other], ob_ref.at[src, other], sibling).wait_recv()
        for cp in first + passed:
            cp.wait_send()
        mine_c.wait()

    return pl.pallas_call(
        body, name="gather_weights",
        out_shape=(jax.ShapeDtypeStruct(ob.shape, ob.dtype), jax.ShapeDtypeStruct((N_CHIPS,) + cw.shape, cw.dtype)),
        in_specs=[ANY, ANY], out_specs=(ANY, ANY), input_output_aliases={0: 0},
        scratch_shapes=[pltpu.SemaphoreType.DMA((9,)), pltpu.SemaphoreType.DMA((9,)), pltpu.SemaphoreType.DMA],
    )(ob, cw)


def _exchange_halves(G, rep):
    _, R, W = G.shape
    HR = R // 2
    RR = rep.shape[0]

    def body(g_ref, rep_ref, ra_ref, rall_ref, send_sems, recv_sems, local_sem):
        x, y, c = _my_place()
        me = 4 * x + 2 * y + c
        rc = functools.partial(_remote, send_sems, recv_sems)
        other = pl.ds(pl.multiple_of((1 - c) * HR, 8), HR)

        def peer(idx):
            px = (1 - x) if (idx >> 2) & 1 else x
            py = (1 - y) if (idx >> 1) & 1 else y
            pc = (1 - c) if idx & 1 else c
            return px, py, pc

        loc = pltpu.make_async_copy(rep_ref, rall_ref.at[me], local_sem)
        loc.start()
        cps = [rc(0, g_ref.at[:, other, :], ra_ref, (x, y, 1 - c))]
        for idx in range(1, N_DEV):
            cps.append(rc(idx, rep_ref, rall_ref.at[me], peer(idx)))
        for cp in cps:
            cp.start()
        rc(0, g_ref.at[:, other, :], ra_ref, (x, y, 1 - c)).wait_recv()
        for idx in range(1, N_DEV):
            px, py, pc = peer(idx)
            rc(idx, rep_ref, rall_ref.at[4 * px + 2 * py + pc], (px, py, pc)).wait_recv()
        for cp in cps:
            cp.wait_send()
        loc.wait()

    return pl.pallas_call(
        body, name="exchange_halves",
        out_shape=(jax.ShapeDtypeStruct((N_CHIPS, HR, W), F32), jax.ShapeDtypeStruct((N_DEV, RR, W), F32)),
        in_specs=[ANY, ANY], out_specs=(ANY, ANY),
        scratch_shapes=[pltpu.SemaphoreType.DMA((N_DEV,)), pltpu.SemaphoreType.DMA((N_DEV,)), pltpu.SemaphoreType.DMA],
    )(G, rep)


def _exchange_chips(A):
    _, HR, W = A.shape

    def body(a_ref, rb_ref, send_sems, recv_sems):
        x, y, c = _my_place()
        me = 2 * x + y
        chips = [(1 - x, y), (x, 1 - y), (1 - x, 1 - y)]
        rc = functools.partial(_remote, send_sems, recv_sems)
        cps = [rc(k, a_ref.at[2 * px + py], rb_ref.at[k], (px, py, c)) for k, (px, py) in enumerate(chips)]
        for cp in cps:
            cp.start()
        for k, (px, py) in enumerate(chips):
            rc(k, a_ref.at[me], rb_ref.at[k], (px, py, c)).wait_recv()
        for cp in cps:
            cp.wait_send()

    return pl.pallas_call(
        body, name="exchange_chips", out_shape=jax.ShapeDtypeStruct((N_CHIPS - 1, HR, W), F32),
        in_specs=[ANY], out_specs=ANY,
        scratch_shapes=[pltpu.SemaphoreType.DMA((3,)), pltpu.SemaphoreType.DMA((3,))],
    )(A)


def _exchange_sibling(gbuf):
    R, W = gbuf.shape
    HR = R // 2

    def body(g_in, g_ref, send_sem, recv_sem):
        del g_in
        x, y, c = _my_place()
        half = pl.ds(pl.multiple_of(c * HR, 8), HR)
        other = pl.ds(pl.multiple_of((1 - c) * HR, 8), HR)
        cp = pltpu.make_async_remote_copy(src_ref=g_ref.at[half], dst_ref=g_ref.at[half], send_sem=send_sem,
                                          recv_sem=recv_sem, device_id=(x, y, 1 - c), device_id_type=MESH)
        cp.start()
        pltpu.make_async_remote_copy(src_ref=g_ref.at[other], dst_ref=g_ref.at[other], send_sem=send_sem,
                                     recv_sem=recv_sem, device_id=(x, y, 1 - c), device_id_type=MESH).wait_recv()
        cp.wait_send()

    return pl.pallas_call(
        body, name="exchange_sibling", out_shape=jax.ShapeDtypeStruct((R, W), F32),
        in_specs=[ANY], out_specs=ANY, input_output_aliases={0: 0},
        scratch_shapes=[pltpu.SemaphoreType.DMA, pltpu.SemaphoreType.DMA],
    )(gbuf)


def _add_halves(G, recv, core):
    _, R, W = G.shape
    HR = R // 2
    tr = _pick(HR, 512, 8)
    nb = HR // tr

    def body(core_ref, g_ref, r_ref, o_ref):
        o_ref[...] = g_ref[...] + r_ref[...]

    return pl.pallas_call(
        body, name="add_halves", out_shape=jax.ShapeDtypeStruct((N_CHIPS, HR, W), F32),
        grid_spec=pltpu.PrefetchScalarGridSpec(
            num_scalar_prefetch=1, grid=(N_CHIPS, nb),
            in_specs=[pl.BlockSpec((None, tr, W), lambda p, i, core_ref: (p, core_ref[0] * nb + i, 0)),
                      pl.BlockSpec((None, tr, W), lambda p, i, core_ref: (p, i, 0))],
            out_specs=pl.BlockSpec((None, tr, W), lambda p, i, core_ref: (p, i, 0))),
        compiler_params=_cparams("parallel", "parallel"),
    )(core, G, recv)


def _sum_chips(A, rb, place):
    _, HR, W = A.shape
    tr = _pick(HR, 512, 8)
    nb = HR // tr

    def body(place_ref, a_ref, r_ref, o_ref):
        o_ref[...] = ((a_ref[...] + r_ref[0]) + r_ref[1]) + r_ref[2]

    return pl.pallas_call(
        body, name="sum_chips", out_shape=jax.ShapeDtypeStruct((2 * HR, W), F32),
        grid_spec=pltpu.PrefetchScalarGridSpec(
            num_scalar_prefetch=1, grid=(nb,),
            in_specs=[pl.BlockSpec((None, tr, W), lambda i, place_ref: (place_ref[0], i, 0)),
                      pl.BlockSpec((N_CHIPS - 1, tr, W), lambda i, place_ref: (0, i, 0))],
            out_specs=pl.BlockSpec((tr, W), lambda i, place_ref: (place_ref[1] * nb + i, 0))),
        compiler_params=_cparams("parallel"),
    )(place, A, rb)


def _sum_slots(a, *, name):
    n, rows, W = a.shape
    tr = _pick(rows, 512, 8)

    def body(a_ref, o_ref):
        s = a_ref[0]
        for q in range(1, n):
            s = s + a_ref[q]
        o_ref[...] = s

    return pl.pallas_call(
        body, name=name, out_shape=jax.ShapeDtypeStruct((rows, W), F32), grid=(rows // tr,),
        in_specs=[pl.BlockSpec((n, tr, W), lambda i: (0, i, 0))], out_specs=pl.BlockSpec((tr, W), lambda i: (i, 0)),
        compiler_params=_cparams("parallel"),
    )(a)


def _adamw(w, g, m, v, *, name):
    rows, W = w.shape
    tr = _pick(rows, 512, 8)

    def body(w_ref, g_ref, m_ref, v_ref, d_ref, nm_ref, nv_ref):
        gg = g_ref[...]
        nm = ADAM_B1 * m_ref[...] + (1.0 - ADAM_B1) * gg
        nv = ADAM_B2 * v_ref[...] + (1.0 - ADAM_B2) * jnp.square(gg)
        m_hat = nm / (1.0 - ADAM_B1 ** ADAM_STEP)
        v_hat = nv / (1.0 - ADAM_B2 ** ADAM_STEP)
        d_ref[...] = -ADAM_LR * (m_hat / (jnp.sqrt(v_hat) + ADAM_EPS) + ADAM_WD * w_ref[...])
        nm_ref[...] = nm
        nv_ref[...] = nv

    blk = pl.BlockSpec((tr, W), lambda i: (i, 0))
    shp = jax.ShapeDtypeStruct((rows, W), F32)
    return pl.pallas_call(
        body, name=name, out_shape=(shp, shp, shp), grid=(rows // tr,),
        in_specs=[blk, blk, blk, blk], out_specs=(blk, blk, blk),
        compiler_params=_cparams("parallel"),
    )(w, g, m, v)


INPUT_NAMES = (("x", "mem") + WEIGHT_NAMES + ("loss_target",) + tuple("m_" + n for n in WEIGHT_NAMES)
               + tuple("v_" + n for n in WEIGHT_NAMES))


def kernel(*args):
    a = dict(zip(INPUT_NAMES, args, strict=True))
    x, mem, target = a["x"][0], a["mem"][0], a["loss_target"][0]
    sh_names = [n for n, _ in SHARDED]
    sh_shapes = [a[n].shape for n in sh_names]
    n_mm = sum(math.prod(s) for s in sh_shapes[:-1])
    n_all = n_mm + math.prod(sh_shapes[-1])
    rows_b = _round_up(-(-n_mm // PACK_W), PACK_ROW_ALIGN)
    rows = _round_up(max(-(-n_all // PACK_W), rows_b), PACK_ROW_ALIGN)
    rows_c = _round_up(-(-math.prod(sh_shapes[-1]) // PACK_W), 8)
    core = lax.axis_index("c").astype(jnp.int32)
    chip = (2 * lax.axis_index("x") + lax.axis_index("y")).astype(jnp.int32)

    wpack = _pack([a[n] for n in sh_names], rows)
    cwpack = _pack([a["conv_w"]], rows_c)
    gb, gc = _gather_weights(_place_shard(wpack, rows_b, chip.reshape(1)), cwpack)
    parts = [_unpack(gb[q], sh_shapes[:-1]) for q in range(N_CHIPS)]
    wfull = {n: jnp.concatenate([parts[q][i] for q in range(N_CHIPS)], axis=ax)
             for i, (n, ax) in enumerate(SHARDED[:-1])}
    p = {n: a[n] for n in REPLICATED}
    p["conv_w"] = jnp.concatenate([_unpack(gc[q], sh_shapes[-1:])[0] for q in range(N_CHIPS)], axis=SHARDED[-1][1])

    loss, dx, gfull = _local_step(x, mem, target, wfull, p)
    loss = lax.psum(loss, ("x", "y", "c"))

    def shard(g, ax, q):
        n = g.shape[ax] // N_CHIPS
        return lax.slice_in_dim(g, q * n, (q + 1) * n, axis=ax)

    G = jnp.stack([_pack([shard(gfull[n], ax, q) for n, ax in SHARDED], rows) for q in range(N_CHIPS)])
    rep_shapes = [a[n].shape for n in REPLICATED]
    rows_r = _round_up(-(-sum(math.prod(s) for s in rep_shapes) // PACK_W), 64)
    rep = _pack([gfull[n] for n in REPLICATED], rows_r)
    recv, repall = _exchange_halves(G, rep)
    A = _add_halves(G, recv, core.reshape(1))
    gshard = _exchange_sibling(_sum_chips(A, _exchange_chips(A), jnp.stack([chip, core])))
    grep = _sum_slots(repall, name="sum_devices")

    mpack = _pack([a["m_" + n] for n in sh_names], rows)
    vpack = _pack([a["v_" + n] for n in sh_names], rows)
    d_s, m_s, v_s = _adamw(wpack, gshard, mpack, vpack, name="adamw_sharded")
    d_r, m_r, v_r = _adamw(_pack([a[n] for n in REPLICATED], rows_r), grep,
                           _pack([a["m_" + n] for n in REPLICATED], rows_r),
                           _pack([a["v_" + n] for n in REPLICATED], rows_r), name="adamw_replicated")

    outs = []
    for sharded_buf, rep_buf in ((gshard, grep), (d_s, d_r), (m_s, m_r), (v_s, v_r)):
        got = dict(zip(sh_names, _unpack(sharded_buf, sh_shapes)))
        got.update(zip(REPLICATED, _unpack(rep_buf, rep_shapes)))
        outs.extend(got[n] for n in WEIGHT_NAMES)
    return (loss, dx[None], *outs)
```

```python
import functools
import math

import jax
import jax.numpy as jnp
from jax import lax
from jax.experimental import pallas as pl
from jax.experimental.pallas import tpu as pltpu

F32 = jnp.float32
BF16 = jnp.bfloat16
MESH = pl.DeviceIdType.MESH

RMS_EPS = 1e-6
POOL_WINDOWS = (2, 4, 8, 16)
POOL_GROUP = 128
POOL_W = 512
FOX_HEADS = 8
FOX_DH = 64
FOX_W = 512
X_HEADS = 4
X_DH = 128
X_W = 512
N_FGATE = 8
LANES = 128
HALO = 16

ADAM_LR = 0.001
ADAM_B1 = 0.9
ADAM_B2 = 0.999
ADAM_EPS = 1e-08
ADAM_WD = 0.01
ADAM_STEP = 10

VMEM_LIMIT_BYTES = 56 * 1024 * 1024


def _cparams(*sem):
    return pltpu.CompilerParams(dimension_semantics=sem, vmem_limit_bytes=VMEM_LIMIT_BYTES)


def _pick(n, cap, align=LANES):
    if n <= cap:
        return n
    best = None
    for t in range(align, cap + 1, align):
        if n % t == 0:
            best = t
    assert best is not None, (n, cap, align)
    return best


def _sigmoid(x):
    return 1.0 / (1.0 + jnp.exp(-x))


class _ChipMajor:
    def __init__(self, arr, layer, by_rows):
        self.arr, self.layer, self.by_rows = arr, layer, by_rows
        n, _, a, b = arr.shape
        self.n_chips, self.per_chip = n, (a if by_rows else b)
        self.shape = (n * a, b) if by_rows else (a, n * b)


def _matmul(a, b, *, ta=False, tb=False, out_dtype=F32, out_chips=None, name):
    view = b if isinstance(b, _ChipMajor) else None
    if ta:
        K, M = a.shape
    else:
        M, K = a.shape
    if tb:
        N, Kb = b.shape
    else:
        Kb, N = b.shape
    assert K == Kb, (a.shape, b.shape, ta, tb)
    tn = _pick(N, 1408)
    if out_chips:
        tn = _pick(N // out_chips, 1408)
    if view is not None and not view.by_rows and not tb:
        tn = _pick(view.per_chip, 1408)
    if view is not None and view.by_rows and tb:
        tn = N
    tm = _pick(M, 1024 if tn <= 1024 else 512)
    tk = _pick(K, 2048 if (tm * tn <= 512 * 1024) else 1024)
    if K <= 1536:
        tk = K
    if view is not None and view.by_rows and not tb:
        tk = K
    if view is not None and not view.by_rows and tb:
        tk = _pick(view.per_chip, 2048 if (tm * tn <= 512 * 1024) else 1024)
    nk = K // tk
    dims = (((0 if ta else 1,), (1 if tb else 0,)), ((), ()))

    def body(a_ref, b_ref, o_ref, *scratch):
        bt = b_ref[...]
        if view is not None and view.by_rows:
            bt = bt.reshape(bt.shape[0] * bt.shape[1], bt.shape[2])
        p = lax.dot_general(a_ref[...].astype(BF16), bt.astype(BF16), dims, preferred_element_type=F32)
        if nk == 1:
            o_ref[...] = p.astype(out_dtype)
        else:
            acc_ref, = scratch
            k = pl.program_id(2)

            @pl.when(k == 0)
            def _():
                acc_ref[...] = p

            @pl.when(k > 0)
            def _():
                acc_ref[...] += p

            @pl.when(k == nk - 1)
            def _():
                o_ref[...] = acc_ref[...].astype(out_dtype)

    a_spec = pl.BlockSpec((tk, tm), lambda i, j, k: (k, i)) if ta else pl.BlockSpec((tm, tk), lambda i, j, k: (i, k))
    b_tile = (tn, tk) if tb else (tk, tn)
    b_rc = (lambda i, j, k: (j, k)) if tb else (lambda i, j, k: (k, j))
    if view is None:
        b_arr = b
        b_spec = pl.BlockSpec(b_tile, b_rc)
    elif view.by_rows:
        b_arr = view.arr
        assert b_tile[0] == view.shape[0]
        b_spec = pl.BlockSpec((view.n_chips, None, view.arr.shape[2], b_tile[1]),
                              lambda i, j, k: (0, view.layer, 0, b_rc(i, j, k)[1]))
    else:
        b_arr = view.arr
        per = view.per_chip // b_tile[1]
        b_spec = pl.BlockSpec((None, None) + b_tile,
                              lambda i, j, k: (b_rc(i, j, k)[1] // per, view.layer, b_rc(i, j, k)[0],
                                               b_rc(i, j, k)[1] % per))
    if out_chips:
        per_o = (N // out_chips) // tn
        out_shape = jax.ShapeDtypeStruct((out_chips, M, N // out_chips), out_dtype)
        out_spec = pl.BlockSpec((None, tm, tn), lambda i, j, k: (j // per_o, i, j % per_o))
    else:
        out_shape = jax.ShapeDtypeStruct((M, N), out_dtype)
        out_spec = pl.BlockSpec((tm, tn), lambda i, j, k: (i, j))
    return pl.pallas_call(
        body, name=name, out_shape=out_shape,
        grid=(M // tm, N // tn, nk),
        in_specs=[a_spec, b_spec], out_specs=out_spec,
        scratch_shapes=[pltpu.VMEM((tm, tn), F32)] if nk > 1 else [],
        compiler_params=_cparams("parallel", "parallel", "arbitrary"),
    )(a, b_arr)


def _row_block(S, D, cap_bytes=2 * 1024 * 1024):
    ts = max(8, min(S, cap_bytes // (4 * D)))
    return _pick(S, ts, 8)


def _rms_fwd(x, g, *, out_dtype, name):
    S, D = x.shape
    ts = _row_block(S, D)

    def body(x_ref, g_ref, o_ref):
        xf = x_ref[...]
        r = lax.rsqrt(jnp.mean(xf * xf, axis=-1, keepdims=True) + RMS_EPS)
        o_ref[...] = (xf * r * g_ref[...]).astype(out_dtype)

    return pl.pallas_call(
        body, name=name, out_shape=jax.ShapeDtypeStruct((S, D), out_dtype), grid=(S // ts,),
        in_specs=[pl.BlockSpec((ts, D), lambda i: (i, 0)), pl.BlockSpec((1, D), lambda i: (0, 0))],
        out_specs=pl.BlockSpec((ts, D), lambda i: (i, 0)),
        compiler_params=_cparams("parallel"),
    )(x, g)


def _add_rms(x, r, g, *, name):
    S, D = x.shape
    ts = _row_block(S, D)

    def body(x_ref, r_ref, g_ref, o_ref):
        rf = r_ref[...]
        s = lax.rsqrt(jnp.mean(rf * rf, axis=-1, keepdims=True) + RMS_EPS)
        o_ref[...] = x_ref[...] + rf * s * g_ref[...]

    return pl.pallas_call(
        body, name=name, out_shape=jax.ShapeDtypeStruct((S, D), F32), grid=(S // ts,),
        in_specs=[pl.BlockSpec((ts, D), lambda i: (i, 0)), pl.BlockSpec((ts, D), lambda i: (i, 0)),
                  pl.BlockSpec((1, D), lambda i: (0, 0))],
        out_specs=pl.BlockSpec((ts, D), lambda i: (i, 0)),
        compiler_params=_cparams("parallel"),
    )(x, r, g)


def _rms_bwd(x, g, dy, res=None, *, out_dtype, name):
    S, D = x.shape
    ts = _row_block(S, D)
    has_res = res is not None

    def body(*refs):
        if has_res:
            x_ref, g_ref, dy_ref, res_ref, dx_ref, dg_ref = refs
        else:
            x_ref, g_ref, dy_ref, dx_ref, dg_ref = refs
        i = pl.program_id(0)
        xf = x_ref[...]
        dyf = dy_ref[...].astype(F32)
        r = lax.rsqrt(jnp.mean(xf * xf, axis=-1, keepdims=True) + RMS_EPS)
        n = xf * r
        dn = dyf * g_ref[...]
        dx = r * (dn - n * jnp.mean(dn * n, axis=-1, keepdims=True))
        if has_res:
            dx = dx + res_ref[...]
        dx_ref[...] = dx.astype(out_dtype)
        part = jnp.sum((dyf * n).reshape(ts // 8, 8, D), axis=0)

        @pl.when(i == 0)
        def _():
            dg_ref[...] = part

        @pl.when(i > 0)
        def _():
            dg_ref[...] += part

    row = pl.BlockSpec((ts, D), lambda i: (i, 0))
    in_specs = [row, pl.BlockSpec((1, D), lambda i: (0, 0)), row] + ([row] if has_res else [])
    args = (x, g, dy) + ((res,) if has_res else ())
    dx, dg = pl.pallas_call(
        body, name=name,
        out_shape=(jax.ShapeDtypeStruct((S, D), out_dtype), jax.ShapeDtypeStruct((8, D), F32)),
        grid=(S // ts,), in_specs=in_specs,
        out_specs=(row, pl.BlockSpec((8, D), lambda i: (0, 0))),
        compiler_params=_cparams("arbitrary"),
    )(*args)
    return dx, dg


def _loss_head(y, t, *, name):
    S, D = y.shape
    ts = _row_block(S, D)

    def body(y_ref, t_ref, dy_ref, sq_ref):
        i = pl.program_id(0)
        e = y_ref[...] - t_ref[...]
        dy_ref[...] = e / D
        part = jnp.sum((e * e).reshape(ts // 8, 8, D), axis=0)

        @pl.when(i == 0)
        def _():
            sq_ref[...] = part

        @pl.when(i > 0)
        def _():
            sq_ref[...] += part

    row = pl.BlockSpec((ts, D), lambda i: (i, 0))
    return pl.pallas_call(
        body, name=name,
        out_shape=(jax.ShapeDtypeStruct((S, D), F32), jax.ShapeDtypeStruct((8, D), F32)),
        grid=(S // ts,), in_specs=[row, row],
        out_specs=(row, pl.BlockSpec((8, D), lambda i: (0, 0))),
        compiler_params=_cparams("arbitrary"),
    )(y, t)


def _window_counts(i, T, w):
    t = i * T + lax.broadcasted_iota(jnp.int32, (T, 1), 0)
    return jnp.minimum(t + 1, w).astype(F32)


def _pool_fwd(zu, pool_w, pool_scale, *, name):
    S, W = zu.shape
    T = _pick(S, 1024, 8)

    def body(u_ref, pw_ref, sc_ref, pooled_ref, mixed_ref, halo_ref):
        i = pl.program_id(0)

        @pl.when(i == 0)
        def _():
            halo_ref[...] = jnp.zeros_like(halo_ref)

        u = u_ref[...]
        ext = jnp.concatenate([halo_ref[...], u], axis=0)
        halo_ref[...] = u[T - HALO:, :]
        for g, w in enumerate(POOL_WINDOWS):
            cols = slice(g * POOL_GROUP, (g + 1) * POOL_GROUP)
            s = ext[:, cols]
            sh = 1
            while sh < w:
                s = s + pltpu.roll(s, sh, 0)
                sh *= 2
            pooled = s[HALO:, :] / _window_counts(i, T, w) - u[:, cols]
            pooled_bf = pooled.astype(BF16)
            pm = jnp.dot(pooled_bf, pw_ref[g].astype(BF16), preferred_element_type=F32)
            pooled_ref[:, cols] = pooled_bf
            mixed_ref[:, cols] = (pm * sc_ref[:, cols]).astype(BF16)

    row = pl.BlockSpec((T, W), lambda i: (i, 0))
    return pl.pallas_call(
        body, name=name,
        out_shape=(jax.ShapeDtypeStruct((S, W), BF16), jax.ShapeDtypeStruct((S, W), BF16)),
        grid=(S // T,),
        in_specs=[row, pl.BlockSpec(pool_w.shape, lambda i: (0, 0, 0)), pl.BlockSpec((1, W), lambda i: (0, 0))],
        out_specs=(row, row),
        scratch_shapes=[pltpu.VMEM((HALO, W), F32)],
        compiler_params=_cparams("arbitrary"),
    )(zu, pool_w, pool_scale)


def _pool_bwd(pooled, dmixed, pool_w, pool_scale, *, name):
    S, W = pooled.shape
    T = _pick(S, 1024, 8)
    nb = S // T

    def body(p_ref, dm_ref, pw_ref, sc_ref, dzu_ref, dpw_ref, dsc_ref, halo_ref):
        i = pl.program_id(0)
        blk = nb - 1 - i

        @pl.when(i == 0)
        def _():
            halo_ref[...] = jnp.zeros_like(halo_ref)
            dpw_ref[...] = jnp.zeros_like(dpw_ref)
            dsc_ref[...] = jnp.zeros_like(dsc_ref)

        for g, w in enumerate(POOL_WINDOWS):
            cols = slice(g * POOL_GROUP, (g + 1) * POOL_GROUP)
            p = p_ref[:, cols]
            dm = dm_ref[:, cols]
            pw = pw_ref[g].astype(BF16)
            pm = jnp.dot(p, pw, preferred_element_type=F32)
            dsc_ref[:, cols] += jnp.sum((dm * pm).reshape(T // 8, 8, POOL_GROUP), axis=0)
            dpm = (dm * sc_ref[:, cols]).astype(BF16)
            dpw_ref[g] += lax.dot_general(p, dpm, (((0,), (0,)), ((), ())), preferred_element_type=F32)
            dpooled = lax.dot_general(dpm, pw, (((1,), (1,)), ((), ())), preferred_element_type=F32)
            e = dpooled / _window_counts(blk, T, w)
            ext = jnp.concatenate([e, halo_ref[:, cols]], axis=0)
            halo_ref[:, cols] = e[:HALO, :]
            s = ext
            sh = 1
            while sh < w:
                s = s + pltpu.roll(s, T + HALO - sh, 0)
                sh *= 2
            dzu_ref[:, cols] = (s[:T, :] - dpooled).astype(BF16)

    row = pl.BlockSpec((T, W), lambda i: (nb - 1 - i, 0))
    return pl.pallas_call(
        body, name=name,
        out_shape=(jax.ShapeDtypeStruct((S, W), BF16), jax.ShapeDtypeStruct(pool_w.shape, F32),
                   jax.ShapeDtypeStruct((8, W), F32)),
        grid=(nb,),
        in_specs=[row, row, pl.BlockSpec(pool_w.shape, lambda i: (0, 0, 0)), pl.BlockSpec((1, W), lambda i: (0, 0))],
        out_specs=(row, pl.BlockSpec(pool_w.shape, lambda i: (0, 0, 0)), pl.BlockSpec((8, W), lambda i: (0, 0))),
        scratch_shapes=[pltpu.VMEM((HALO, W), F32)],
        compiler_params=_cparams("arbitrary"),
    )(pooled, dmixed, pool_w, pool_scale)


def _fgate_fwd(zf, bf, *, name):
    S, W = zf.shape
    T = _pick(S, 512, 8)

    def body(z_ref, b_ref, c_ref, aq_ref, ak_ref, carry_ref):
        i = pl.program_id(0)

        @pl.when(i == 0)
        def _():
            carry_ref[...] = jnp.zeros_like(carry_ref)

        a = z_ref[...] + b_ref[...]
        s = jnp.minimum(a, 0.0) - jnp.log(1.0 + jnp.exp(-jnp.abs(a)))
        row = lax.broadcasted_iota(jnp.int32, (T, W), 0)
        sh = 1
        while sh < T:
            s = s + jnp.where(row >= sh, pltpu.roll(s, sh, 0), 0.0)
            sh *= 2
        c = s + carry_ref[0:1, :]
        c_ref[...] = c
        carry_ref[...] = jnp.broadcast_to(c[T - 1:T, :], carry_ref.shape)
        lane = lax.broadcasted_iota(jnp.int32, (T, W), 1)
        for h in range(FOX_HEADS):
            ch = c[:, h:h + 1]
            hi = ch.astype(BF16).astype(F32)
            r1 = ch - hi
            lo = r1.astype(BF16).astype(F32)
            lo2 = (r1 - lo).astype(BF16).astype(F32)
            aq = jnp.where(lane == 0, hi, jnp.where(lane == 1, lo, jnp.where(lane == 2, lo2,
                                                                              jnp.where(lane < 6, 1.0, 0.0))))
            ak = jnp.where(lane < 3, 1.0, jnp.where(lane == 3, -hi, jnp.where(lane == 4, -lo,
                                                                               jnp.where(lane == 5, -lo2, 0.0))))
            aq_ref[h] = aq.astype(BF16)
            ak_ref[h] = ak.astype(BF16)

    aug = jax.ShapeDtypeStruct((FOX_HEADS, S, W), BF16)
    aug_spec = pl.BlockSpec((FOX_HEADS, T, W), lambda i: (0, i, 0))
    return pl.pallas_call(
        body, name=name, out_shape=(jax.ShapeDtypeStruct((S, W), F32), aug, aug), grid=(S // T,),
        in_specs=[pl.BlockSpec((T, W), lambda i: (i, 0)), pl.BlockSpec((1, W), lambda i: (0, 0))],
        out_specs=(pl.BlockSpec((T, W), lambda i: (i, 0)), aug_spec, aug_spec),
        scratch_shapes=[pltpu.VMEM((8, W), F32)],
        compiler_params=_cparams("arbitrary"),
    )(zf, bf)


def _fgate_bwd(zf, bf, dc, *, name):
    S, W = zf.shape
    T = _pick(S, 512, 8)
    nb = S // T

    def body(z_ref, b_ref, dc_ref, dz_ref, db_ref, carry_ref):
        i = pl.program_id(0)

        @pl.when(i == 0)
        def _():
            carry_ref[...] = jnp.zeros_like(carry_ref)
            db_ref[...] = jnp.zeros_like(db_ref)

        s = dc_ref[...]
        row = lax.broadcasted_iota(jnp.int32, (T, W), 0)
        sh = 1
        while sh < T:
            s = s + jnp.where(row < T - sh, pltpu.roll(s, T - sh, 0), 0.0)
            sh *= 2
        dlf = s + carry_ref[0:1, :]
        carry_ref[...] = jnp.broadcast_to(dlf[0:1, :], carry_ref.shape)
        dz = dlf * (1.0 - _sigmoid(z_ref[...] + b_ref[...]))
        dz_ref[...] = dz.astype(BF16)
        db_ref[...] += jnp.sum(dz.reshape(T // 8, 8, W), axis=0)

    row_spec = pl.BlockSpec((T, W), lambda i: (nb - 1 - i, 0))
    return pl.pallas_call(
        body, name=name,
        out_shape=(jax.ShapeDtypeStruct((S, W), BF16), jax.ShapeDtypeStruct((8, W), F32)),
        grid=(nb,),
        in_specs=[row_spec, pl.BlockSpec((1, W), lambda i: (0, 0)), row_spec],
        out_specs=(row_spec, pl.BlockSpec((8, W), lambda i: (0, 0))),
        scratch_shapes=[pltpu.VMEM((8, W), F32)],
        compiler_params=_cparams("arbitrary"),
    )(zf, bf, dc)


def _merge_fwd(zg, yp, yf, *, name):
    S, D = yp.shape
    ts = _row_block(S, D, 1024 * 1024)

    def body(zg_ref, yp_ref, yf_ref, o_ref):
        o_ref[...] = (_sigmoid(zg_ref[:, :D]) * yp_ref[...] + _sigmoid(zg_ref[:, D:]) * yf_ref[...]).astype(BF16)

    row = pl.BlockSpec((ts, D), lambda i: (i, 0))
    return pl.pallas_call(
        body, name=name, out_shape=jax.ShapeDtypeStruct((S, D), BF16), grid=(S // ts,),
        in_specs=[pl.BlockSpec((ts, 2 * D), lambda i: (i, 0)), row, row], out_specs=row,
        compiler_params=_cparams("parallel"),
    )(zg, yp, yf)


def _merge_bwd(zg, yp, yf, dmerged, *, name):
    S, D = yp.shape
    ts = _row_block(S, D, 1024 * 1024)

    def body(zg_ref, yp_ref, yf_ref, dm_ref, dyp_ref, dyf_ref, dzg_ref):
        dm = dm_ref[...]
        sp = _sigmoid(zg_ref[:, :D])
        sf = _sigmoid(zg_ref[:, D:])
        dyp_ref[...] = (dm * sp).astype(BF16)
        dyf_ref[...] = (dm * sf).astype(BF16)
        dzg_ref[:, :D] = (dm * yp_ref[...] * (sp * (1.0 - sp))).astype(BF16)
        dzg_ref[:, D:] = (dm * yf_ref[...] * (sf * (1.0 - sf))).astype(BF16)

    row = pl.BlockSpec((ts, D), lambda i: (i, 0))
    wide = pl.BlockSpec((ts, 2 * D), lambda i: (i, 0))
    return pl.pallas_call(
        body, name=name,
        out_shape=(jax.ShapeDtypeStruct((S, D), BF16), jax.ShapeDtypeStruct((S, D), BF16),
                   jax.ShapeDtypeStruct((S, 2 * D), BF16)),
        grid=(S // ts,), in_specs=[wide, row, row, row], out_specs=(row, row, wide),
        compiler_params=_cparams("parallel"),
    )(zg, yp, yf, dmerged)


NEG_BIG = -1e30


FOX_BLOCK = 1024
PAIR = LANES // FOX_DH
N_PAIRS = FOX_HEADS // PAIR


def _fox_fwd(zqkv, augq, augk, *, name):
    S = zqkv.shape[0]
    bq = _pick(S, FOX_BLOCK, 128)
    nq = S // bq
    scale = 1.0 / math.sqrt(FOX_DH)

    def body(q_ref, k_ref, v_ref, aq_ref, ak_ref, o_ref, lse_ref):
        i = pl.program_id(1)
        lane = lax.broadcasted_iota(jnp.int32, (1, LANES), 1)
        first = lane < FOX_DH
        q2 = q_ref[...] * scale
        zero = jnp.zeros_like(q2)
        qh = (jnp.concatenate([jnp.where(first, q2, zero), aq_ref[0]], axis=1),
              jnp.concatenate([jnp.where(first, zero, q2), aq_ref[1]], axis=1))

        def step(j, carry, masked):
            start = pl.multiple_of(j * bq, bq)
            kb = k_ref[pl.ds(start, bq), :]
            vb = v_ref[pl.ds(start, bq), :]
            one = jnp.ones_like(vb)
            vh = (jnp.where(first, vb, one), jnp.where(first, one, vb))
            out = []
            for h in range(PAIR):
                m, acc = carry[h]
                kh = jnp.concatenate([kb, ak_ref[h, pl.ds(start, bq), :]], axis=1)
                s = lax.dot_general(qh[h], kh, (((1,), (1,)), ((), ())), preferred_element_type=F32)
                if masked:
                    r = lax.broadcasted_iota(jnp.int32, (bq, bq), 0)
                    c = lax.broadcasted_iota(jnp.int32, (bq, bq), 1)
                    s = jnp.where(c <= r, s, NEG_BIG)
                m_new = jnp.maximum(m, jnp.max(s, axis=-1, keepdims=True))
                alpha = jnp.exp(m - m_new)
                p = jnp.exp(s - m_new).astype(BF16)
                acc = alpha * acc + jnp.dot(p, vh[h], preferred_element_type=F32)
                out.append((m_new, acc))
            return tuple(out)

        init = tuple((jnp.full((bq, 1), NEG_BIG, F32), jnp.zeros((bq, LANES), F32)) for _ in range(PAIR))
        carry = lax.fori_loop(0, i, lambda j, c: step(j, c, False), init)
        (ma, acca), (mb, accb) = step(i, carry, True)
        num = jnp.where(first, acca, accb)
        den = jnp.where(first, pltpu.roll(acca, FOX_DH, 1), pltpu.roll(accb, FOX_DH, 1))
        o_ref[...] = num / den
        lse_ref[...] = jnp.where(first, mb, ma) + jnp.log(jnp.where(first, accb, acca))

    npair = N_PAIRS
    return pl.pallas_call(
        body, name=name,
        out_shape=(jax.ShapeDtypeStruct((S, FOX_W), F32), jax.ShapeDtypeStruct((S, FOX_W), F32)),
        grid=(npair, nq),
        in_specs=[pl.BlockSpec((bq, LANES), lambda hp, i: (i, hp)),
                  pl.BlockSpec((S, LANES), lambda hp, i: (0, npair + hp)),
                  pl.BlockSpec((S, LANES), lambda hp, i: (0, 2 * npair + hp)),
                  pl.BlockSpec((PAIR, bq, LANES), lambda hp, i: (hp, i, 0)),
                  pl.BlockSpec((PAIR, S, LANES), lambda hp, i: (hp, 0, 0))],
        out_specs=(pl.BlockSpec((bq, LANES), lambda hp, i: (i, hp)),
                   pl.BlockSpec((bq, LANES), lambda hp, i: (i, hp))),
        compiler_params=_cparams("parallel", "arbitrary"),
    )(zqkv, zqkv, zqkv, augq, augk)


def _head_rowsum(a, b, *, name):
    S, W = a.shape
    ts = _pick(S, 1024, 8)

    def body(a_ref, b_ref, o_ref):
        prod = a_ref[...].astype(F32) * b_ref[...].astype(F32)
        hi = prod.astype(BF16)
        lo = (prod - hi.astype(F32)).astype(BF16)
        r = lax.broadcasted_iota(jnp.int32, (W, LANES), 0)
        c = lax.broadcasted_iota(jnp.int32, (W, LANES), 1)
        sel = jnp.where(r // FOX_DH == c, 1.0, 0.0).astype(BF16)
        o_ref[...] = (jnp.dot(hi, sel, preferred_element_type=F32) + jnp.dot(lo, sel, preferred_element_type=F32))

    return pl.pallas_call(
        body, name=name, out_shape=jax.ShapeDtypeStruct((S, LANES), F32), grid=(S // ts,),
        in_specs=[pl.BlockSpec((ts, W), lambda i: (i, 0)), pl.BlockSpec((ts, W), lambda i: (i, 0))],
        out_specs=pl.BlockSpec((ts, LANES), lambda i: (i, 0)),
        compiler_params=_cparams("parallel"),
    )(a, b)


def _fox_bwd(zqkv, do, augq, augk, lse_row, delta_row, *, name):
    S = zqkv.shape[0]
    bk = _pick(S, FOX_BLOCK, 128)
    nk = S // bk
    scale = 1.0 / math.sqrt(FOX_DH)
    npair = N_PAIRS

    def body(q_ref, k_ref, v_ref, do_ref, ak_ref, aq_ref, lse_ref, dl_ref, dq_ref, dk_ref, dv_ref, dck_ref, dcq_ref):
        j = pl.program_id(1)

        @pl.when(j == 0)
        def _():
            dq_ref[...] = jnp.zeros_like(dq_ref)
            dcq_ref[...] = jnp.zeros_like(dcq_ref)

        lane = lax.broadcasted_iota(jnp.int32, (1, LANES), 1)
        first = lane < FOX_DH
        kb = k_ref[...]
        vb = v_ref[...]
        kh = (jnp.concatenate([kb, ak_ref[0]], axis=1), jnp.concatenate([kb, ak_ref[1]], axis=1))

        def step(i, carry, masked):
            start = pl.multiple_of(i * bk, bk)
            qs = q_ref[pl.ds(start, bk), :] * scale
            dob = do_ref[pl.ds(start, bk), :]
            zero = jnp.zeros_like(qs)
            qh = (jnp.where(first, qs, zero), jnp.where(first, zero, qs))
            doh = (jnp.where(first, dob, zero), jnp.where(first, zero, dob))
            out = []
            dqc = []
            for h in range(PAIR):
                dk, dv, dc = carry[h]
                qaug = jnp.concatenate([qh[h], aq_ref[h, pl.ds(start, bk), :]], axis=1)
                lse = lse_ref[h, :, pl.ds(start, bk)]
                dl = dl_ref[h, :, pl.ds(start, bk)]
                st = lax.dot_general(kh[h], qaug, (((1,), (1,)), ((), ())), preferred_element_type=F32)
                pt = jnp.exp(st - lse)
                if masked:
                    r = lax.broadcasted_iota(jnp.int32, (bk, bk), 0)
                    c = lax.broadcasted_iota(jnp.int32, (bk, bk), 1)
                    pt = jnp.where(c >= r, pt, 0.0)
                dpt = lax.dot_general(vb, doh[h], (((1,), (1,)), ((), ())), preferred_element_type=F32)
                dst = pt * (dpt - dl)
                pt_bf = pt.astype(BF16)
                dst_bf = dst.astype(BF16)
                dv = dv + jnp.dot(pt_bf, dob, preferred_element_type=F32)
                dk = dk + jnp.dot(dst_bf, qs, preferred_element_type=F32)
                dc = dc - jnp.sum(dst, axis=-1, keepdims=True)
                dcq_ref[h, :, pl.ds(start, bk)] += jnp.sum(dst, axis=0, keepdims=True)
                dqc.append(lax.dot_general(dst_bf, kb, (((0,), (0,)), ((), ())), preferred_element_type=F32))
                out.append((dk, dv, dc))
            dq_ref[pl.ds(start, bk), :] += jnp.where(first, dqc[0], dqc[1])
            return tuple(out)

        init = tuple((jnp.zeros((bk, LANES), F32), jnp.zeros((bk, LANES), F32), jnp.zeros((bk, 1), F32))
                     for _ in range(PAIR))
        carry = step(j, init, True)
        (dka, dva, dca), (dkb, dvb, dcb) = lax.fori_loop(j + 1, nk, lambda i, c: step(i, c, False), carry)
        dk_ref[...] = jnp.where(first, dka, dkb).astype(BF16)
        dv_ref[...] = jnp.where(first, dva, dvb).astype(BF16)
        dck_ref[0] = dca
        dck_ref[1] = dcb

        @pl.when(j == nk - 1)
        def _():
            dq_ref[...] = dq_ref[...] * scale

    rowfull = pl.BlockSpec((PAIR, 1, S), lambda hp, j: (hp, 0, 0))
    return pl.pallas_call(
        body, name=name,
        out_shape=(jax.ShapeDtypeStruct((S, FOX_W), F32), jax.ShapeDtypeStruct((S, FOX_W), BF16),
                   jax.ShapeDtypeStruct((S, FOX_W), BF16), jax.ShapeDtypeStruct((FOX_HEADS, S, 1), F32),
                   jax.ShapeDtypeStruct((FOX_HEADS, 1, S), F32)),
        grid=(npair, nk),
        in_specs=[pl.BlockSpec((S, LANES), lambda hp, j: (0, hp)),
                  pl.BlockSpec((bk, LANES), lambda hp, j: (j, npair + hp)),
                  pl.BlockSpec((bk, LANES), lambda hp, j: (j, 2 * npair + hp)),
                  pl.BlockSpec((S, LANES), lambda hp, j: (0, hp)),
                  pl.BlockSpec((PAIR, bk, LANES), lambda hp, j: (hp, j, 0)),
                  pl.BlockSpec((PAIR, S, LANES), lambda hp, j: (hp, 0, 0)), rowfull, rowfull],
        out_specs=(pl.BlockSpec((S, LANES), lambda hp, j: (0, hp)),
                   pl.BlockSpec((bk, LANES), lambda hp, j: (j, hp)),
                   pl.BlockSpec((bk, LANES), lambda hp, j: (j, hp)),
                   pl.BlockSpec((PAIR, bk, 1), lambda hp, j: (hp, j, 0)), rowfull),
        compiler_params=_cparams("parallel", "arbitrary"),
    )(zqkv, zqkv, zqkv, do, augk, augq, lse_row, delta_row)


def _xattn_fwd(q, kv, *, name):
    S, W = q.shape
    M = kv.shape[0]
    tq = _pick(S, 512, 8)
    scale = 1.0 / math.sqrt(X_DH)

    def body(q_ref, kv_ref, o_ref):
        for h in range(X_HEADS):
            cols = slice(h * X_DH, (h + 1) * X_DH)
            vcols = slice(W + h * X_DH, W + (h + 1) * X_DH)
            s = lax.dot_general(q_ref[:, cols], kv_ref[:, cols], (((1,), (1,)), ((), ())),
                                preferred_element_type=F32) * scale
            e = jnp.exp(s - jnp.max(s, axis=-1, keepdims=True))
            p = e / jnp.sum(e, axis=-1, keepdims=True)
            o_ref[:, cols] = jnp.dot(p.astype(BF16), kv_ref[:, vcols], preferred_element_type=F32).astype(BF16)

    return pl.pallas_call(
        body, name=name, out_shape=jax.ShapeDtypeStruct((S, W), BF16), grid=(S // tq,),
        in_specs=[pl.BlockSpec((tq, W), lambda i: (i, 0)), pl.BlockSpec((M, 2 * W), lambda i: (0, 0))],
        out_specs=pl.BlockSpec((tq, W), lambda i: (i, 0)),
        compiler_params=_cparams("parallel"),
    )(q, kv)


def _xattn_bwd(q, kv, do, *, name):
    S, W = q.shape
    M = kv.shape[0]
    tq = _pick(S, 512, 8)
    scale = 1.0 / math.sqrt(X_DH)

    def body(q_ref, kv_ref, do_ref, dq_ref, dkv_ref):
        i = pl.program_id(0)

        @pl.when(i == 0)
        def _():
            dkv_ref[...] = jnp.zeros_like(dkv_ref)

        for h in range(X_HEADS):
            cols = slice(h * X_DH, (h + 1) * X_DH)
            vcols = slice(W + h * X_DH, W + (h + 1) * X_DH)
            qh = q_ref[:, cols]
            kh = kv_ref[:, cols]
            vh = kv_ref[:, vcols]
            doh = do_ref[:, cols]
            s = lax.dot_general(qh, kh, (((1,), (1,)), ((), ())), preferred_element_type=F32) * scale
            e = jnp.exp(s - jnp.max(s, axis=-1, keepdims=True))
            p = e / jnp.sum(e, axis=-1, keepdims=True)
            dp = lax.dot_general(doh, vh, (((1,), (1,)), ((), ())), preferred_element_type=F32)
            ds = (p * (dp - jnp.sum(p * dp, axis=-1, keepdims=True)) * scale).astype(BF16)
            dq_ref[:, cols] = jnp.dot(ds, kh, preferred_element_type=F32).astype(BF16)
            dkv_ref[:, cols] += lax.dot_general(ds, qh, (((0,), (0,)), ((), ())), preferred_element_type=F32)
            dkv_ref[:, vcols] += lax.dot_general(p.astype(BF16), doh, (((0,), (0,)), ((), ())),
                                                 preferred_element_type=F32)

    return pl.pallas_call(
        body, name=name,
        out_shape=(jax.ShapeDtypeStruct((S, W), BF16), jax.ShapeDtypeStruct((M, 2 * W), F32)),
        grid=(S // tq,),
        in_specs=[pl.BlockSpec((tq, W), lambda i: (i, 0)), pl.BlockSpec((M, 2 * W), lambda i: (0, 0)),
                  pl.BlockSpec((tq, W), lambda i: (i, 0))],
        out_specs=(pl.BlockSpec((tq, W), lambda i: (i, 0)), pl.BlockSpec((M, 2 * W), lambda i: (0, 0))),
        compiler_params=_cparams("arbitrary"),
    )(q, kv, do)


GELU_C = math.sqrt(2.0 / math.pi)
GELU_A = 0.044715
CONV_HALO = 8


def _gelu_parts(x):
    u = GELU_C * (x + GELU_A * x * x * x)
    t = jnp.tanh(u)
    g = 0.5 * x * (1.0 + t)
    dg = 0.5 * (1.0 + t) + 0.5 * x * (1.0 - t * t) * (GELU_C * (1.0 + 3.0 * GELU_A * x * x))
    return g, dg


def _conv3(ext, w_ref, b_ref):
    return (w_ref[2:3, :] * ext + w_ref[1:2, :] * pltpu.roll(ext, 1, 0)
            + w_ref[0:1, :] * pltpu.roll(ext, 2, 0) + b_ref[...])


def _convglu_fwd(z, conv_w, conv_b, *, name):
    S, F2 = z.shape
    F = F2 // 2
    tc = _pick(F, 1408)
    ncol = F // tc
    T = _pick(S, 512, 8)
    hb = T // CONV_HALO

    def body(zg_ref, zu_ref, zgp_ref, zup_ref, wg_ref, wu_ref, bg_ref, bu_ref, act_ref):
        i = pl.program_id(1)
        first = (i > 0).astype(F32)

        def conv(z_ref, zp_ref, w_ref, b_ref):
            ext = jnp.concatenate([zp_ref[...] * first, z_ref[...]], axis=0)
            return _conv3(ext, w_ref, b_ref)[CONV_HALO:, :]

        gc = conv(zg_ref, zgp_ref, wg_ref, bg_ref)
        uc = conv(zu_ref, zup_ref, wu_ref, bu_ref)
        act_ref[...] = (_gelu_parts(gc)[0] * uc).astype(BF16)

    cur = lambda off: pl.BlockSpec((T, tc), lambda j, i: (i, j + off))
    prev = lambda off: pl.BlockSpec((CONV_HALO, tc), lambda j, i: (jnp.maximum(i * hb - 1, 0), j + off))
    vec = lambda rows, off: pl.BlockSpec((rows, tc), lambda j, i: (0, j + off))
    return pl.pallas_call(
        body, name=name, out_shape=jax.ShapeDtypeStruct((S, F), BF16), grid=(ncol, S // T),
        in_specs=[cur(0), cur(ncol), prev(0), prev(ncol), vec(3, 0), vec(3, ncol), vec(1, 0), vec(1, ncol)],
        out_specs=pl.BlockSpec((T, tc), lambda j, i: (i, j)),
        compiler_params=_cparams("parallel", "parallel"),
    )(z, z, z, z, conv_w, conv_w, conv_b, conv_b)


def _convglu_bwd(z, dact, conv_w, conv_b, *, name):
    S, F2 = z.shape
    F = F2 // 2
    tc = _pick(F, 1408)
    ncol = F // tc
    T = _pick(S, 256, 8)
    nrow = S // T
    hb = T // CONV_HALO
    TE = T + CONV_HALO

    def body(zg_ref, zu_ref, zgp_ref, zup_ref, zgn_ref, zun_ref, da_ref, dan_ref,
             wg_ref, wu_ref, bg_ref, bu_ref, dzg_ref, dzu_ref, dwg_ref, dwu_ref):
        i = pl.program_id(1)
        first = (i > 0).astype(F32)
        last = (i < nrow - 1).astype(F32)

        @pl.when(i == 0)
        def _():
            dwg_ref[...] = jnp.zeros_like(dwg_ref)
            dwu_ref[...] = jnp.zeros_like(dwu_ref)

        extg = jnp.concatenate([zgp_ref[...] * first, zg_ref[...], zgn_ref[...]], axis=0)
        extu = jnp.concatenate([zup_ref[...] * first, zu_ref[...], zun_ref[...]], axis=0)
        gc = _conv3(extg, wg_ref, bg_ref)[CONV_HALO:, :]
        uc = _conv3(extu, wu_ref, bu_ref)[CONV_HALO:, :]
        da = jnp.concatenate([da_ref[...], dan_ref[...] * last], axis=0)
        gl, dgl = _gelu_parts(gc)
        dgc = da * uc * dgl
        duc = da * gl

        def back(d, ext, w_ref, dz_ref, dw_ref):
            dz = w_ref[2:3, :] * d + w_ref[1:2, :] * pltpu.roll(d, TE - 1, 0) + w_ref[0:1, :] * pltpu.roll(d, TE - 2, 0)
            dz_ref[...] = dz[:T, :].astype(BF16)
            dc = d[:T, :]
            z0 = ext[CONV_HALO:CONV_HALO + T, :]
            z1 = pltpu.roll(ext, 1, 0)[CONV_HALO:CONV_HALO + T, :]
            z2 = pltpu.roll(ext, 2, 0)[CONV_HALO:CONV_HALO + T, :]
            rows = [jnp.sum(dc * z2, axis=0, keepdims=True), jnp.sum(dc * z1, axis=0, keepdims=True),
                    jnp.sum(dc * z0, axis=0, keepdims=True), jnp.sum(dc, axis=0, keepdims=True)]
            dw_ref[0:4, :] += jnp.concatenate(rows, axis=0)

        back(dgc, extg, wg_ref, dzg_ref, dwg_ref)
        back(duc, extu, wu_ref, dzu_ref, dwu_ref)

    cur = lambda off: pl.BlockSpec((T, tc), lambda j, i: (i, j + off))
    prev = lambda off: pl.BlockSpec((CONV_HALO, tc), lambda j, i: (jnp.maximum(i * hb - 1, 0), j + off))
    nxt = lambda off: pl.BlockSpec((CONV_HALO, tc), lambda j, i: (jnp.minimum((i + 1) * hb, S // CONV_HALO - 1), j + off))
    vec = lambda rows, off: pl.BlockSpec((rows, tc), lambda j, i: (0, j + off))
    dzg, dzu, dwg, dwu = pl.pallas_call(
        body, name=name,
        out_shape=(jax.ShapeDtypeStruct((S, F), BF16), jax.ShapeDtypeStruct((S, F), BF16),
                   jax.ShapeDtypeStruct((8, F), F32), jax.ShapeDtypeStruct((8, F), F32)),
        grid=(ncol, nrow),
        in_specs=[cur(0), cur(ncol), prev(0), prev(ncol), nxt(0), nxt(ncol), cur(0), nxt(0),
                  vec(3, 0), vec(3, ncol), vec(1, 0), vec(1, ncol)],
        out_specs=(cur(0), cur(0), vec(8, 0), vec(8, 0)),
        compiler_params=_cparams("parallel", "arbitrary"),
    )(z, z, z, z, z, z, dact, dact, conv_w, conv_w, conv_b, conv_b)
    return dzg, dzu, dwg, dwu


OFF_QKV = POOL_W
OFF_G = POOL_W + 3 * FOX_W


SHARD_BY_ROWS = {"w_in": False, "w_pool_br": False, "w_fox_br": False, "w_mix_out": True, "w_xq": True,
                 "w_xkv": True, "w_xo": False, "w_up": False, "w_down": True, "conv_w": False}


def _prep_layer_weights(w, l):
    n, _, D, b = w["w_in"].shape
    w_in = w["w_in"][:, l].transpose(1, 0, 2).reshape(D, n * b)
    off_f = OFF_G
    pad = jnp.zeros((D, LANES - N_FGATE), w_in.dtype)
    w_in_r = jnp.concatenate([w_in[:, :off_f], w_in[:, off_f + N_FGATE:], w_in[:, off_f:off_f + N_FGATE], pad], axis=1)
    wl = {k: _ChipMajor(w[k], l, SHARD_BY_ROWS[k]) for k in MATMUL_WEIGHTS if k != "w_in"}
    wl["w_in_r"] = w_in_r
    return wl


def _row(v):
    return v.reshape(1, -1)


def _layer_fwd(x, mem, wl, p, l):
    S, D = x.shape
    n = lambda s: f"l{l}_{s}"
    sv = {"x0": x}
    h1 = _rms_fwd(x, _row(p["mix_pre_g"][l]), out_dtype=BF16, name=n("rms1"))
    w_in_r = wl["w_in_r"]
    og = OFF_G
    zu = _matmul(h1, w_in_r[:, :OFF_QKV], out_dtype=F32, name=n("mm_zu"))
    zqkv = _matmul(h1, w_in_r[:, OFF_QKV:og], out_dtype=BF16, name=n("mm_zqkv"))
    zg = _matmul(h1, w_in_r[:, og:og + 2 * D], out_dtype=F32, name=n("mm_zg"))
    zf = _matmul(h1, w_in_r[:, og + 2 * D:], out_dtype=F32, name=n("mm_zf"))
    bf = jnp.pad(p["b_forget"][l], (0, LANES - N_FGATE)).reshape(1, LANES)
    _, augq, augk = _fgate_fwd(zf, bf, name=n("fgate"))
    o_fox, lse2 = _fox_fwd(zqkv, augq, augk, name=n("fox"))
    pooled, mixed = _pool_fwd(zu, p["pool_w"][l], _row(p["pool_scale"][l]), name=n("pool"))
    y_pool = _matmul(mixed, wl["w_pool_br"], out_dtype=F32, name=n("mm_ypool"))
    y_fox = _matmul(o_fox, wl["w_fox_br"], out_dtype=F32, name=n("mm_yfox"))
    merged = _merge_fwd(zg, y_pool, y_fox, name=n("merge"))
    r1 = _matmul(merged, wl["w_mix_out"], out_dtype=F32, name=n("mm_r1"))
    x1 = _add_rms(x, r1, _row(p["mix_post_g"][l]), name=n("addrms1"))
    sv.update(h1=h1, zg=zg, zf=zf, bf=bf, zqkv=zqkv, augq=augq, augk=augk, o_fox=o_fox, lse2=lse2,
              pooled=pooled, mixed=mixed, y_pool=y_pool, y_fox=y_fox, merged=merged, r1=r1, x1=x1)
    h2 = _rms_fwd(x1, _row(p["xa_pre_g"][l]), out_dtype=BF16, name=n("rms2"))
    mem_n = _rms_fwd(mem, _row(p["mem_g"][l]), out_dtype=BF16, name=n("rms_mem"))
    q2 = _matmul(h2, wl["w_xq"], out_dtype=BF16, name=n("mm_q2"))
    kv = _matmul(mem_n, wl["w_xkv"], out_dtype=BF16, name=n("mm_kv"))
    o2 = _xattn_fwd(q2, kv, name=n("xattn"))
    a2 = _matmul(o2, wl["w_xo"], out_dtype=F32, name=n("mm_a2"))
    x2 = _add_rms(x1, a2, _row(p["xa_post_g"][l]), name=n("addrms2"))
    sv.update(h2=h2, mem_n=mem_n, q2=q2, kv=kv, o2=o2, a2=a2, x2=x2)
    h3 = _rms_fwd(x2, _row(p["ffn_pre_g"][l]), out_dtype=BF16, name=n("rms3"))
    z3 = _matmul(h3, wl["w_up"], out_dtype=F32, name=n("mm_z3"))
    act = _convglu_fwd(z3, p["conv_w"][l], _row(p["conv_b"][l]), name=n("convglu"))
    d3 = _matmul(act, wl["w_down"], out_dtype=F32, name=n("mm_d3"))
    x3 = _add_rms(x2, d3, _row(p["ffn_post_g"][l]), name=n("addrms3"))
    sv.update(h3=h3, z3=z3, act=act, d3=d3)
    return x3, sv


def _layer_bwd(dx, mem, wl, p, l, sv):
    S, D = dx.shape
    n = lambda s: f"l{l}_b_{s}"
    g = {}
    red = lambda part: jnp.sum(part, axis=0)
    dd3, dg = _rms_bwd(sv["d3"], _row(p["ffn_post_g"][l]), dx, out_dtype=BF16, name=n("rms3post"))
    g["ffn_post_g"] = red(dg)
    dact = _matmul(dd3, wl["w_down"], tb=True, out_dtype=F32, name=n("mm_dact"))
    g["w_down"] = _matmul(sv["act"], dd3, ta=True, out_dtype=F32, name=n("mm_dwdown"))
    dzg3, dzu3, dwg, dwu = _convglu_bwd(sv["z3"], dact, p["conv_w"][l], _row(p["conv_b"][l]), name=n("convglu"))
    g["conv_w"] = jnp.concatenate([dwg[:3], dwu[:3]], axis=1)
    g["conv_b"] = jnp.concatenate([dwg[3], dwu[3]], axis=0)
    dz3 = jnp.concatenate([dzg3, dzu3], axis=1)
    dh3 = _matmul(dz3, wl["w_up"], tb=True, out_dtype=F32, name=n("mm_dh3"))
    g["w_up"] = _matmul(sv["h3"], dz3, ta=True, out_dtype=F32, out_chips=N_CHIPS, name=n("mm_dwup"))
    dx, dg = _rms_bwd(sv["x2"], _row(p["ffn_pre_g"][l]), dh3, dx, out_dtype=F32, name=n("rms3pre"))
    g["ffn_pre_g"] = red(dg)
    da2, dg = _rms_bwd(sv["a2"], _row(p["xa_post_g"][l]), dx, out_dtype=BF16, name=n("rms2post"))
    g["xa_post_g"] = red(dg)
    do2 = _matmul(da2, wl["w_xo"], tb=True, out_dtype=BF16, name=n("mm_do2"))
    g["w_xo"] = _matmul(sv["o2"], da2, ta=True, out_dtype=F32, out_chips=N_CHIPS, name=n("mm_dwxo"))
    dq2, dkv = _xattn_bwd(sv["q2"], sv["kv"], do2, name=n("xattn"))
    dh2 = _matmul(dq2, wl["w_xq"], tb=True, out_dtype=F32, name=n("mm_dh2"))
    g["w_xq"] = _matmul(sv["h2"], dq2, ta=True, out_dtype=F32, name=n("mm_dwxq"))
    dmem_n = _matmul(dkv, wl["w_xkv"], tb=True, out_dtype=F32, name=n("mm_dmemn"))
    g["w_xkv"] = _matmul(sv["mem_n"], dkv, ta=True, out_dtype=F32, name=n("mm_dwxkv"))
    _, dg = _rms_bwd(mem, _row(p["mem_g"][l]), dmem_n, out_dtype=BF16, name=n("rms_mem"))
    g["mem_g"] = red(dg)
    dx, dg = _rms_bwd(sv["x1"], _row(p["xa_pre_g"][l]), dh2, dx, out_dtype=F32, name=n("rms2pre"))
    g["xa_pre_g"] = red(dg)
    dr1, dg = _rms_bwd(sv["r1"], _row(p["mix_post_g"][l]), dx, out_dtype=BF16, name=n("rms1post"))
    g["mix_post_g"] = red(dg)
    dmerged = _matmul(dr1, wl["w_mix_out"], tb=True, out_dtype=F32, name=n("mm_dmerged"))
    g["w_mix_out"] = _matmul(sv["merged"], dr1, ta=True, out_dtype=F32, name=n("mm_dwmo"))
    dyp, dyf, dzg = _merge_bwd(sv["zg"], sv["y_pool"], sv["y_fox"], dmerged, name=n("merge"))
    dmixed = _matmul(dyp, wl["w_pool_br"], tb=True, out_dtype=F32, name=n("mm_dmixed"))
    g["w_pool_br"] = _matmul(sv["mixed"], dyp, ta=True, out_dtype=F32, out_chips=N_CHIPS, name=n("mm_dwpb"))
    dofox = _matmul(dyf, wl["w_fox_br"], tb=True, out_dtype=F32, name=n("mm_dofox"))
    g["w_fox_br"] = _matmul(sv["o_fox"], dyf, ta=True, out_dtype=F32, out_chips=N_CHIPS, name=n("mm_dwfb"))
    dzu, dpw, dsc = _pool_bwd(sv["pooled"], dmixed, p["pool_w"][l], _row(p["pool_scale"][l]), name=n("pool"))
    g["pool_w"] = dpw
    g["pool_scale"] = red(dsc)
    delta = _head_rowsum(dofox, sv["o_fox"], name=n("delta"))
    lse = sv["lse2"].reshape(S, N_PAIRS, PAIR, FOX_DH)[:, :, ::-1, 0].reshape(S, FOX_HEADS)
    dq, dk, dv, dck, dcq = _fox_bwd(sv["zqkv"], dofox.astype(BF16), sv["augq"], sv["augk"],
                                    lse.T.reshape(FOX_HEADS, 1, S), delta[:, :FOX_HEADS].T.reshape(FOX_HEADS, 1, S),
                                    name=n("fox"))
    dc = dcq.reshape(FOX_HEADS, S) + dck.reshape(FOX_HEADS, S)
    dc_pad = jnp.pad(dc.T, ((0, 0), (0, LANES - FOX_HEADS)))
    dzf, db = _fgate_bwd(sv["zf"], sv["bf"], dc_pad, name=n("fgate"))
    g["b_forget"] = red(db)[:N_FGATE]
    dz_cat = jnp.concatenate([dzu, dq.astype(BF16), dk, dv, dzg, dzf], axis=1)
    dh1 = _matmul(dz_cat, wl["w_in_r"], tb=True, out_dtype=F32, name=n("mm_dh1"))
    dw_in_r = _matmul(sv["h1"], dz_cat, ta=True, out_dtype=F32, name=n("mm_dwin"))
    og = OFF_G
    g["w_in"] = jnp.concatenate([dw_in_r[:, :og], dw_in_r[:, og + 2 * D:og + 2 * D + N_FGATE],
                                 dw_in_r[:, og:og + 2 * D]], axis=1)
    dx, dg = _rms_bwd(sv["x0"], _row(p["mix_pre_g"][l]), dh1, dx, out_dtype=F32, name=n("rms1pre"))
    g["mix_pre_g"] = red(dg)
    return dx, g


MATMUL_WEIGHTS = ("w_in", "w_pool_br", "w_fox_br", "w_mix_out", "w_xq", "w_xkv", "w_xo", "w_up", "w_down")
WEIGHT_NAMES = ("mix_pre_g", "mix_post_g", "w_in", "b_forget", "pool_w", "pool_scale", "w_pool_br", "w_fox_br",
                "w_mix_out", "xa_pre_g", "xa_post_g", "mem_g", "w_xq", "w_xkv", "w_xo", "ffn_pre_g", "ffn_post_g",
                "w_up", "conv_w", "conv_b", "w_down")


def _local_step(x, mem, loss_target, wfull, p):
    L = p["mix_pre_g"].shape[0]
    saved, wls = [], []
    h = x
    for l in range(L):
        wl = _prep_layer_weights(wfull, l)
        h, sv = _layer_fwd(h, mem, wl, p, l)
        saved.append(sv)
        wls.append(wl)
    D = x.shape[1]
    dy, sq = _loss_head(h, loss_target, name="loss_head")
    loss = 0.5 * jnp.sum(sq) / D
    grads = []
    dx = dy
    for l in reversed(range(L)):
        dx, g = _layer_bwd(dx, mem, wls[l], p, l, saved[l])
        grads.append(g)
    grads = grads[::-1]

    def chip_major(k, g):
        if g.ndim == 3:
            return g
        if SHARD_BY_ROWS[k]:
            return g.reshape(N_CHIPS, g.shape[0] // N_CHIPS, g.shape[1])
        return g.reshape(g.shape[0], N_CHIPS, g.shape[1] // N_CHIPS).transpose(1, 0, 2)

    gfull = {k: jnp.stack([grads[l][k] for l in range(L)]) for k in REPLICATED}
    gfull.update({k: jnp.stack([chip_major(k, grads[l][k]) for l in range(L)], axis=1) for k in SHARD_BY_ROWS})
    return loss, dx, gfull


PACK_W = 512
PACK_ROW_ALIGN = 1024
N_CHIPS = 4
N_DEV = 8
SHARDED = (("w_in", 2), ("w_pool_br", 2), ("w_fox_br", 2), ("w_mix_out", 1), ("w_xq", 1), ("w_xkv", 1),
           ("w_xo", 2), ("w_up", 2), ("w_down", 1), ("conv_w", 2))
REPLICATED = ("mix_pre_g", "mix_post_g", "b_forget", "pool_w", "pool_scale", "xa_pre_g", "xa_post_g", "mem_g",
              "ffn_pre_g", "ffn_post_g", "conv_b")


def _round_up(n, m):
    return -(-n // m) * m


def _pack(arrs, rows):
    flat = jnp.concatenate([a.reshape(-1) for a in arrs])
    return jnp.pad(flat, (0, rows * PACK_W - flat.shape[0])).reshape(rows, PACK_W)


def _unpack(buf, shapes):
    flat = buf.reshape(-1)
    out, off = [], 0
    for s in shapes:
        n = math.prod(s)
        out.append(flat[off:off + n].reshape(s))
        off += n
    return out


ANY = pl.BlockSpec(memory_space=pl.ANY)


def _remote(send_sems, recv_sems, k, src, dst, to):
    return pltpu.make_async_remote_copy(src_ref=src, dst_ref=dst, send_sem=send_sems.at[k], recv_sem=recv_sems.at[k],
                                        device_id=to, device_id_type=MESH)


def _my_place():
    return lax.axis_index("x"), lax.axis_index("y"), lax.axis_index("c")


def _rows_per_block(a, b, cap_bytes=1024 * 1024):
    if a % 8:
        return a
    return _pick(a, max(8, cap_bytes // (4 * b) // 8 * 8), 8)


def _place_shard(w, chip, dtype, *, name):
    L, a, b = w.shape
    ta = _rows_per_block(a, b)

    def body(chip_ref, w_ref, o_ref):
        o_ref[...] = w_ref[...].astype(dtype)

    return pl.pallas_call(
        body, name=name, out_shape=jax.ShapeDtypeStruct((N_CHIPS, L, a, b), dtype),
        grid_spec=pltpu.PrefetchScalarGridSpec(
            num_scalar_prefetch=1, grid=(L, a // ta),
            in_specs=[pl.BlockSpec((None, ta, b), lambda l, i, chip_ref: (l, i, 0))],
            out_specs=pl.BlockSpec((None, None, ta, b), lambda l, i, chip_ref: (chip_ref[0], l, i, 0))),
        compiler_params=_cparams("parallel", "parallel"),
    )(chip, w)


def _layer_halves(L, c):
    assert L % 2 == 0
    return pl.ds(c * (L // 2), L // 2), pl.ds((1 - c) * (L // 2), L // 2)


def _gather_weights(bufs):
    n = len(bufs)
    L = bufs[0].shape[1]

    def body(*refs):
        outs, (send_sems, recv_sems) = refs[n:2 * n], refs[2 * n:]
        x, y, c = _my_place()
        me = 2 * x + y
        sibling = (x, y, 1 - c)
        chips = [(1 - x, y), (x, 1 - y), (1 - x, 1 - y)]
        half, other = _layer_halves(L, c)
        rc = functools.partial(_remote, send_sems, recv_sems)
        first = [rc(6 * w + k, o.at[me, half], o.at[me, half], (px, py, c))
                 for w, o in enumerate(outs) for k, (px, py) in enumerate(chips)]
        for cp in first:
            cp.start()
        passed = []
        for k, (px, py) in enumerate(chips):
            src = 2 * px + py
            for w, o in enumerate(outs):
                rc(6 * w + k, o.at[src, half], o.at[src, half], (px, py, c)).wait_recv()
                fwd = rc(6 * w + 3 + k, o.at[src, half], o.at[src, half], sibling)
                fwd.start()
                passed.append(fwd)
        for k, (px, py) in enumerate(chips):
            src = 2 * px + py
            for w, o in enumerate(outs):
                rc(6 * w + 3 + k, o.at[src, other], o.at[src, other], sibling).wait_recv()
        for cp in first + passed:
            cp.wait_send()

    return pl.pallas_call(
        body, name="gather_weights",
        out_shape=tuple(jax.ShapeDtypeStruct(b.shape, b.dtype) for b in bufs),
        in_specs=[ANY] * n, out_specs=tuple([ANY] * n), input_output_aliases={i: i for i in range(n)},
        scratch_shapes=[pltpu.SemaphoreType.DMA((6 * n,)), pltpu.SemaphoreType.DMA((6 * n,))],
    )(*bufs)


def _exchange_halves(Gs, rep):
    n = len(Gs)
    L = Gs[0].shape[1]
    RR, W = rep.shape

    def body(*refs):
        g_refs, rep_ref, ra_refs, rall_ref = refs[:n], refs[n], refs[n + 1:2 * n + 1], refs[2 * n + 1]
        send_sems, recv_sems, local_sem = refs[2 * n + 2:]
        x, y, c = _my_place()
        me = 4 * x + 2 * y + c
        rc = functools.partial(_remote, send_sems, recv_sems)
        _, other = _layer_halves(L, c)

        def peer(idx):
            px = (1 - x) if (idx >> 2) & 1 else x
            py = (1 - y) if (idx >> 1) & 1 else y
            pc = (1 - c) if idx & 1 else c
            return px, py, pc

        loc = pltpu.make_async_copy(rep_ref, rall_ref.at[me], local_sem)
        loc.start()
        cps = [rc(N_DEV + w, g.at[:, other], ra, (x, y, 1 - c)) for w, (g, ra) in enumerate(zip(g_refs, ra_refs))]
        for idx in range(1, N_DEV):
            cps.append(rc(idx, rep_ref, rall_ref.at[me], peer(idx)))
        for cp in cps:
            cp.start()
        for w, (g, ra) in enumerate(zip(g_refs, ra_refs)):
            rc(N_DEV + w, g.at[:, other], ra, (x, y, 1 - c)).wait_recv()
        for idx in range(1, N_DEV):
            px, py, pc = peer(idx)
            rc(idx, rep_ref, rall_ref.at[4 * px + 2 * py + pc], (px, py, pc)).wait_recv()
        for cp in cps:
            cp.wait_send()
        loc.wait()

    halves = tuple(jax.ShapeDtypeStruct((g.shape[0], L // 2) + g.shape[2:], F32) for g in Gs)
    out = pl.pallas_call(
        body, name="exchange_halves",
        out_shape=halves + (jax.ShapeDtypeStruct((N_DEV, RR, W), F32),),
        in_specs=[ANY] * (n + 1), out_specs=tuple([ANY] * (n + 1)),
        scratch_shapes=[pltpu.SemaphoreType.DMA((N_DEV + n,)), pltpu.SemaphoreType.DMA((N_DEV + n,)),
                        pltpu.SemaphoreType.DMA],
    )(*Gs, rep)
    return out[:n], out[n]


def _exchange_chips(As):
    n = len(As)

    def body(*refs):
        a_refs, rb_refs, (send_sems, recv_sems) = refs[:n], refs[n:2 * n], refs[2 * n:]
        x, y, c = _my_place()
        me = 2 * x + y
        chips = [(1 - x, y), (x, 1 - y), (1 - x, 1 - y)]
        rc = functools.partial(_remote, send_sems, recv_sems)
        cps = [rc(3 * w + k, a.at[2 * px + py], rb.at[k], (px, py, c))
               for w, (a, rb) in enumerate(zip(a_refs, rb_refs)) for k, (px, py) in enumerate(chips)]
        for cp in cps:
            cp.start()
        for w, (a, rb) in enumerate(zip(a_refs, rb_refs)):
            for k, (px, py) in enumerate(chips):
                rc(3 * w + k, a.at[me], rb.at[k], (px, py, c)).wait_recv()
        for cp in cps:
            cp.wait_send()

    return pl.pallas_call(
        body, name="exchange_chips",
        out_shape=tuple(jax.ShapeDtypeStruct((N_CHIPS - 1,) + a.shape[1:], F32) for a in As),
        in_specs=[ANY] * n, out_specs=tuple([ANY] * n),
        scratch_shapes=[pltpu.SemaphoreType.DMA((3 * n,)), pltpu.SemaphoreType.DMA((3 * n,))],
    )(*As)


def _exchange_sibling(gs):
    n = len(gs)
    L = gs[0].shape[0]

    def body(*refs):
        g_refs, (send_sems, recv_sems) = refs[n:2 * n], refs[2 * n:]
        x, y, c = _my_place()
        half, other = _layer_halves(L, c)
        rc = functools.partial(_remote, send_sems, recv_sems)
        cps = [rc(w, g.at[half], g.at[half], (x, y, 1 - c)) for w, g in enumerate(g_refs)]
        for cp in cps:
            cp.start()
        for w, g in enumerate(g_refs):
            rc(w, g.at[other], g.at[other], (x, y, 1 - c)).wait_recv()
        for cp in cps:
            cp.wait_send()

    return pl.pallas_call(
        body, name="exchange_sibling", out_shape=tuple(jax.ShapeDtypeStruct(g.shape, F32) for g in gs),
        in_specs=[ANY] * n, out_specs=tuple([ANY] * n), input_output_aliases={i: i for i in range(n)},
        scratch_shapes=[pltpu.SemaphoreType.DMA((n,)), pltpu.SemaphoreType.DMA((n,))],
    )(*gs)


def _add_halves(G, recv, core, *, name):
    n, L, a, b = G.shape
    Lh = L // 2
    ta = _rows_per_block(a, b)

    def body(core_ref, g_ref, r_ref, o_ref):
        o_ref[...] = g_ref[...] + r_ref[...]

    blk = pl.BlockSpec((None, None, ta, b), lambda p, l, i, core_ref: (p, l, i, 0))
    return pl.pallas_call(
        body, name=name, out_shape=jax.ShapeDtypeStruct((n, Lh, a, b), F32),
        grid_spec=pltpu.PrefetchScalarGridSpec(
            num_scalar_prefetch=1, grid=(n, Lh, a // ta),
            in_specs=[pl.BlockSpec((None, None, ta, b), lambda p, l, i, core_ref: (p, core_ref[0] * Lh + l, i, 0)), blk],
            out_specs=blk),
        compiler_params=_cparams("parallel", "parallel", "parallel"),
    )(core, G, recv)


def _sum_chips(A, rb, place, *, name):
    _, Lh, a, b = A.shape
    ta = _rows_per_block(a, b, 512 * 1024)

    def body(place_ref, a_ref, r_ref, o_ref):
        o_ref[...] = ((a_ref[...] + r_ref[0]) + r_ref[1]) + r_ref[2]

    return pl.pallas_call(
        body, name=name, out_shape=jax.ShapeDtypeStruct((2 * Lh, a, b), F32),
        grid_spec=pltpu.PrefetchScalarGridSpec(
            num_scalar_prefetch=1, grid=(Lh, a // ta),
            in_specs=[pl.BlockSpec((None, None, ta, b), lambda l, i, place_ref: (place_ref[0], l, i, 0)),
                      pl.BlockSpec((N_CHIPS - 1, None, ta, b), lambda l, i, place_ref: (0, l, i, 0))],
            out_specs=pl.BlockSpec((None, ta, b), lambda l, i, place_ref: (place_ref[1] * Lh + l, i, 0))),
        compiler_params=_cparams("parallel", "parallel"),
    )(place, A, rb)


def _sum_slots(a, *, name):
    n, rows, W = a.shape
    tr = _pick(rows, 512, 8)

    def body(a_ref, o_ref):
        s = a_ref[0]
        for q in range(1, n):
            s = s + a_ref[q]
        o_ref[...] = s

    return pl.pallas_call(
        body, name=name, out_shape=jax.ShapeDtypeStruct((rows, W), F32), grid=(rows // tr,),
        in_specs=[pl.BlockSpec((n, tr, W), lambda i: (0, i, 0))], out_specs=pl.BlockSpec((tr, W), lambda i: (i, 0)),
        compiler_params=_cparams("parallel"),
    )(a)


def _adamw(w, g, m, v, *, name):
    L, a, b = w.shape
    ta = _rows_per_block(a, b, 512 * 1024)

    def body(w_ref, g_ref, m_ref, v_ref, d_ref, nm_ref, nv_ref):
        gg = g_ref[...]
        nm = ADAM_B1 * m_ref[...] + (1.0 - ADAM_B1) * gg
        nv = ADAM_B2 * v_ref[...] + (1.0 - ADAM_B2) * jnp.square(gg)
        m_hat = nm / (1.0 - ADAM_B1 ** ADAM_STEP)
        v_hat = nv / (1.0 - ADAM_B2 ** ADAM_STEP)
        d_ref[...] = -ADAM_LR * (m_hat / (jnp.sqrt(v_hat) + ADAM_EPS) + ADAM_WD * w_ref[...])
        nm_ref[...] = nm
        nv_ref[...] = nv

    blk = pl.BlockSpec((None, ta, b), lambda l, i: (l, i, 0))
    shp = jax.ShapeDtypeStruct((L, a, b), F32)
    return pl.pallas_call(
        body, name=name, out_shape=(shp, shp, shp), grid=(L, a // ta),
        in_specs=[blk, blk, blk, blk], out_specs=(blk, blk, blk),
        compiler_params=_cparams("parallel", "parallel"),
    )(w, g, m, v)


INPUT_NAMES = (("x", "mem") + WEIGHT_NAMES + ("loss_target",) + tuple("m_" + n for n in WEIGHT_NAMES)
               + tuple("v_" + n for n in WEIGHT_NAMES))


def kernel(*args):
    a = dict(zip(INPUT_NAMES, args, strict=True))
    x, mem, target = a["x"][0], a["mem"][0], a["loss_target"][0]
    sh_names = list(SHARD_BY_ROWS)
    core = lax.axis_index("c").astype(jnp.int32)
    chip = (2 * lax.axis_index("x") + lax.axis_index("y")).astype(jnp.int32)
    place = jnp.stack([chip, core])

    placed = [_place_shard(a[n], chip.reshape(1), F32 if n == "conv_w" else BF16, name="place_" + n) for n in sh_names]
    wfull = dict(zip(sh_names, _gather_weights(placed)))
    p = {n: a[n] for n in REPLICATED}
    cw = wfull.pop("conv_w")
    p["conv_w"] = cw.transpose(1, 2, 0, 3).reshape(cw.shape[1], cw.shape[2], N_CHIPS * cw.shape[3])

    loss, dx, gfull = _local_step(x, mem, target, wfull, p)
    loss = lax.psum(loss, ("x", "y", "c"))

    Gs = [gfull[n] for n in sh_names]
    rep_shapes = [a[n].shape for n in REPLICATED]
    rows_r = _round_up(-(-sum(math.prod(s) for s in rep_shapes) // PACK_W), 64)
    rep = _pack([gfull[n] for n in REPLICATED], rows_r)
    recvs, repall = _exchange_halves(Gs, rep)
    As = [_add_halves(g, r, core.reshape(1), name="add_halves_" + n) for n, g, r in zip(sh_names, Gs, recvs)]
    rbs = _exchange_chips(As)
    gsh = _exchange_sibling([_sum_chips(A, rb, place, name="sum_chips_" + n) for n, A, rb in zip(sh_names, As, rbs)])
    grep = _sum_slots(repall, name="sum_devices")

    got = {"g": dict(zip(sh_names, gsh)), "d": {}, "m": {}, "v": {}}
    for n, g in zip(sh_names, gsh):
        got["d"][n], got["m"][n], got["v"][n] = _adamw(a[n], g, a["m_" + n], a["v_" + n], name="adamw_" + n)
    packed = [_pack([a[pre + n] for n in REPLICATED], rows_r)[None] for pre in ("", "m_", "v_")]
    d_r, m_r, v_r = _adamw(packed[0], grep[None], packed[1], packed[2], name="adamw_replicated")
    for key, buf in (("g", grep), ("d", d_r[0]), ("m", m_r[0]), ("v", v_r[0])):
        got[key].update(zip(REPLICATED, _unpack(buf, rep_shapes)))
    outs = [got[key][n] for key in ("g", "d", "m", "v") for n in WEIGHT_NAMES]
    return (loss, dx[None], *outs)
```

```python
import functools
import math

import jax
import jax.numpy as jnp
from jax import lax
from jax.experimental import pallas as pl
from jax.experimental.pallas import tpu as pltpu

F32 = jnp.float32
BF16 = jnp.bfloat16
MESH = pl.DeviceIdType.MESH

RMS_EPS = 1e-6
POOL_WINDOWS = (2, 4, 8, 16)
POOL_GROUP = 128
POOL_W = 512
FOX_HEADS = 8
FOX_DH = 64
FOX_W = 512
X_HEADS = 4
X_DH = 128
X_W = 512
N_FGATE = 8
LANES = 128
HALO = 16

ADAM_LR = 0.001
ADAM_B1 = 0.9
ADAM_B2 = 0.999
ADAM_EPS = 1e-08
ADAM_WD = 0.01
ADAM_STEP = 10

VMEM_LIMIT_BYTES = 56 * 1024 * 1024
MATMUL_VMEM_BUDGET = 44 * 1024 * 1024


def _cparams(*sem):
    return pltpu.CompilerParams(dimension_semantics=sem, vmem_limit_bytes=VMEM_LIMIT_BYTES)


def _pick(n, cap, align=LANES):
    if n <= cap:
        return n
    best = None
    for t in range(align, cap + 1, align):
        if n % t == 0:
            best = t
    assert best is not None, (n, cap, align)
    return best


def _sigmoid(x):
    return 1.0 / (1.0 + jnp.exp(-x))


class _ChipMajor:
    def __init__(self, arr, layer, by_rows):
        self.arr, self.layer, self.by_rows = arr, layer, by_rows
        n, _, a, b = arr.shape
        self.n_chips, self.per_chip = n, (a if by_rows else b)
        self.shape = (n * a, b) if by_rows else (a, n * b)


SPAN_CHIPS_BELOW = 512


def _matmul(a, b, *, ta=False, tb=False, out_dtype=F32, out_chips=None, name):
    view = b if isinstance(b, _ChipMajor) else None
    if ta:
        K, M = a.shape
    else:
        M, K = a.shape
    if tb:
        N, Kb = b.shape
    else:
        Kb, N = b.shape
    assert K == Kb, (a.shape, b.shape, ta, tb)
    by_rows = view is not None and view.by_rows
    by_cols = view is not None and not view.by_rows
    b_itemsize = (view.arr if view is not None else b).dtype.itemsize
    span_b = by_cols and view.per_chip < SPAN_CHIPS_BELOW
    span_o = bool(out_chips) and N // out_chips < SPAN_CHIPS_BELOW
    if by_rows and tb:
        tn = N
    elif by_cols and not tb:
        tn = N if span_b else _pick(view.per_chip, 1408)
    elif out_chips:
        tn = N if span_o else _pick(N // out_chips, 1408)
    else:
        tn = _pick(N, 1408)
    tm = _pick(M, 1024 if tn <= 1024 else 512)
    if by_rows and not tb:
        tk = K
    elif by_cols and tb:
        tk = K if span_b else (view.per_chip if view.per_chip <= 2048 else _pick(view.per_chip, 1024))
    else:
        ab, bb, ob = a.dtype.itemsize, b_itemsize, jnp.dtype(out_dtype).itemsize
        for cap in (K, 2048, 1024, 512, 128):
            tk = _pick(K, cap)
            need = (2 * tk * (tm * ab + tn * bb) + (2 * ob + 4 + (4 if tk < K else 0)) * tm * tn)
            if need <= MATMUL_VMEM_BUDGET:
                break
    nk = K // tk
    dims = (((0 if ta else 1,), (1 if tb else 0,)), ((), ()))

    def body(a_ref, b_ref, o_ref, *scratch):
        bt = b_ref[...]
        if by_rows:
            bt = bt.reshape(bt.shape[0] * bt.shape[1], bt.shape[2])
        elif span_b:
            bt = jnp.concatenate([bt[q] for q in range(view.n_chips)], axis=1)
        p = lax.dot_general(a_ref[...].astype(BF16), bt.astype(BF16), dims, preferred_element_type=F32)

        def store(val):
            if span_o:
                w = N // out_chips
                for q in range(out_chips):
                    o_ref[q] = val[:, q * w:(q + 1) * w].astype(out_dtype)
            else:
                o_ref[...] = val.astype(out_dtype)

        if nk == 1:
            store(p)
        else:
            acc_ref, = scratch
            k = pl.program_id(2)

            @pl.when(k == 0)
            def _():
                acc_ref[...] = p

            @pl.when(k > 0)
            def _():
                acc_ref[...] += p

            @pl.when(k == nk - 1)
            def _():
                store(acc_ref[...])

    a_spec = pl.BlockSpec((tk, tm), lambda i, j, k: (k, i)) if ta else pl.BlockSpec((tm, tk), lambda i, j, k: (i, k))
    b_tile = (tn, tk) if tb else (tk, tn)
    b_rc = (lambda i, j, k: (j, k)) if tb else (lambda i, j, k: (k, j))
    if view is None:
        b_arr = b
        b_spec = pl.BlockSpec(b_tile, b_rc)
    elif by_rows:
        b_arr = view.arr
        assert b_tile[0] == view.shape[0]
        b_spec = pl.BlockSpec((view.n_chips, None, view.arr.shape[2], b_tile[1]),
                              lambda i, j, k: (0, view.layer, 0, b_rc(i, j, k)[1]))
    elif span_b:
        b_arr = view.arr
        assert b_tile[1] == view.shape[1]
        b_spec = pl.BlockSpec((view.n_chips, None, b_tile[0], view.per_chip),
                              lambda i, j, k: (0, view.layer, b_rc(i, j, k)[0], 0))
    else:
        b_arr = view.arr
        per = view.per_chip // b_tile[1]
        b_spec = pl.BlockSpec((None, None) + b_tile,
                              lambda i, j, k: (b_rc(i, j, k)[1] // per, view.layer, b_rc(i, j, k)[0],
                                               b_rc(i, j, k)[1] % per))
    if span_o:
        out_shape = jax.ShapeDtypeStruct((out_chips, M, N // out_chips), out_dtype)
        out_spec = pl.BlockSpec((out_chips, tm, N // out_chips), lambda i, j, k: (0, i, 0))
    elif out_chips:
        per_o = (N // out_chips) // tn
        out_shape = jax.ShapeDtypeStruct((out_chips, M, N // out_chips), out_dtype)
        out_spec = pl.BlockSpec((None, tm, tn), lambda i, j, k: (j // per_o, i, j % per_o))
    else:
        out_shape = jax.ShapeDtypeStruct((M, N), out_dtype)
        out_spec = pl.BlockSpec((tm, tn), lambda i, j, k: (i, j))
    grid = (M // tm, N // tn, nk)
    specs = [a_spec, b_spec, out_spec]
    a_bytes, b_bytes = M * K * a.dtype.itemsize, K * N * b_itemsize
    if nk == 1 and b_bytes + a_bytes * grid[1] < a_bytes + b_bytes * grid[0]:
        grid = (grid[1], grid[0], nk)
        specs = [pl.BlockSpec(s.block_shape, functools.partial(lambda f, j, i, k: f(i, j, k), s.index_map))
                 for s in specs]
    return pl.pallas_call(
        body, name=name, out_shape=out_shape, grid=grid,
        in_specs=specs[:2], out_specs=specs[2],
        scratch_shapes=[pltpu.VMEM((tm, tn), F32)] if nk > 1 else [],
        compiler_params=_cparams("parallel", "parallel", "arbitrary"),
    )(a, b_arr)


def _row_block(S, D, cap_bytes=2 * 1024 * 1024):
    ts = max(8, min(S, cap_bytes // (4 * D)))
    return _pick(S, ts, 8)


def _rms_fwd(x, g, *, out_dtype, name):
    S, D = x.shape
    ts = _row_block(S, D)

    def body(x_ref, g_ref, o_ref):
        xf = x_ref[...]
        r = lax.rsqrt(jnp.mean(xf * xf, axis=-1, keepdims=True) + RMS_EPS)
        o_ref[...] = (xf * r * g_ref[...]).astype(out_dtype)

    return pl.pallas_call(
        body, name=name, out_shape=jax.ShapeDtypeStruct((S, D), out_dtype), grid=(S // ts,),
        in_specs=[pl.BlockSpec((ts, D), lambda i: (i, 0)), pl.BlockSpec((1, D), lambda i: (0, 0))],
        out_specs=pl.BlockSpec((ts, D), lambda i: (i, 0)),
        compiler_params=_cparams("parallel"),
    )(x, g)


def _add_rms(x, r, g, *, name):
    S, D = x.shape
    ts = _row_block(S, D)

    def body(x_ref, r_ref, g_ref, o_ref):
        rf = r_ref[...]
        s = lax.rsqrt(jnp.mean(rf * rf, axis=-1, keepdims=True) + RMS_EPS)
        o_ref[...] = x_ref[...] + rf * s * g_ref[...]

    return pl.pallas_call(
        body, name=name, out_shape=jax.ShapeDtypeStruct((S, D), F32), grid=(S // ts,),
        in_specs=[pl.BlockSpec((ts, D), lambda i: (i, 0)), pl.BlockSpec((ts, D), lambda i: (i, 0)),
                  pl.BlockSpec((1, D), lambda i: (0, 0))],
        out_specs=pl.BlockSpec((ts, D), lambda i: (i, 0)),
        compiler_params=_cparams("parallel"),
    )(x, r, g)


def _rms_bwd(x, g, dy, res=None, *, out_dtype, name):
    S, D = x.shape
    ts = _row_block(S, D)
    has_res = res is not None

    def body(*refs):
        if has_res:
            x_ref, g_ref, dy_ref, res_ref, dx_ref, dg_ref = refs
        else:
            x_ref, g_ref, dy_ref, dx_ref, dg_ref = refs
        i = pl.program_id(0)
        xf = x_ref[...]
        dyf = dy_ref[...].astype(F32)
        r = lax.rsqrt(jnp.mean(xf * xf, axis=-1, keepdims=True) + RMS_EPS)
        n = xf * r
        dn = dyf * g_ref[...]
        dx = r * (dn - n * jnp.mean(dn * n, axis=-1, keepdims=True))
        if has_res:
            dx = dx + res_ref[...]
        dx_ref[...] = dx.astype(out_dtype)
        part = jnp.sum((dyf * n).reshape(ts // 8, 8, D), axis=0)

        @pl.when(i == 0)
        def _():
            dg_ref[...] = part

        @pl.when(i > 0)
        def _():
            dg_ref[...] += part

    row = pl.BlockSpec((ts, D), lambda i: (i, 0))
    in_specs = [row, pl.BlockSpec((1, D), lambda i: (0, 0)), row] + ([row] if has_res else [])
    args = (x, g, dy) + ((res,) if has_res else ())
    dx, dg = pl.pallas_call(
        body, name=name,
        out_shape=(jax.ShapeDtypeStruct((S, D), out_dtype), jax.ShapeDtypeStruct((8, D), F32)),
        grid=(S // ts,), in_specs=in_specs,
        out_specs=(row, pl.BlockSpec((8, D), lambda i: (0, 0))),
        compiler_params=_cparams("arbitrary"),
    )(*args)
    return dx, dg


def _loss_head(y, t, *, name):
    S, D = y.shape
    ts = _row_block(S, D)

    def body(y_ref, t_ref, dy_ref, sq_ref):
        i = pl.program_id(0)
        e = y_ref[...] - t_ref[...]
        dy_ref[...] = e / D
        part = jnp.sum((e * e).reshape(ts // 8, 8, D), axis=0)

        @pl.when(i == 0)
        def _():
            sq_ref[...] = part

        @pl.when(i > 0)
        def _():
            sq_ref[...] += part

    row = pl.BlockSpec((ts, D), lambda i: (i, 0))
    return pl.pallas_call(
        body, name=name,
        out_shape=(jax.ShapeDtypeStruct((S, D), F32), jax.ShapeDtypeStruct((8, D), F32)),
        grid=(S // ts,), in_specs=[row, row],
        out_specs=(row, pl.BlockSpec((8, D), lambda i: (0, 0))),
        compiler_params=_cparams("arbitrary"),
    )(y, t)


def _window_counts(i, T, w):
    t = i * T + lax.broadcasted_iota(jnp.int32, (T, 1), 0)
    return jnp.minimum(t + 1, w).astype(F32)


def _pool_fwd(zu, pool_w, pool_scale, *, name):
    S, W = zu.shape
    T = _pick(S, 1024, 8)

    def body(u_ref, pw_ref, sc_ref, pooled_ref, mixed_ref, halo_ref):
        i = pl.program_id(0)

        @pl.when(i == 0)
        def _():
            halo_ref[...] = jnp.zeros_like(halo_ref)

        u = u_ref[...]
        ext = jnp.concatenate([halo_ref[...], u], axis=0)
        halo_ref[...] = u[T - HALO:, :]
        for g, w in enumerate(POOL_WINDOWS):
            cols = slice(g * POOL_GROUP, (g + 1) * POOL_GROUP)
            s = ext[:, cols]
            sh = 1
            while sh < w:
                s = s + pltpu.roll(s, sh, 0)
                sh *= 2
            pooled = s[HALO:, :] / _window_counts(i, T, w) - u[:, cols]
            pooled_bf = pooled.astype(BF16)
            pm = jnp.dot(pooled_bf, pw_ref[g].astype(BF16), preferred_element_type=F32)
            pooled_ref[:, cols] = pooled_bf
            mixed_ref[:, cols] = (pm * sc_ref[:, cols]).astype(BF16)

    row = pl.BlockSpec((T, W), lambda i: (i, 0))
    return pl.pallas_call(
        body, name=name,
        out_shape=(jax.ShapeDtypeStruct((S, W), BF16), jax.ShapeDtypeStruct((S, W), BF16)),
        grid=(S // T,),
        in_specs=[row, pl.BlockSpec(pool_w.shape, lambda i: (0, 0, 0)), pl.BlockSpec((1, W), lambda i: (0, 0))],
        out_specs=(row, row),
        scratch_shapes=[pltpu.VMEM((HALO, W), F32)],
        compiler_params=_cparams("arbitrary"),
    )(zu, pool_w, pool_scale)


def _pool_bwd(pooled, dmixed, pool_w, pool_scale, *, name):
    S, W = pooled.shape
    T = _pick(S, 1024, 8)
    nb = S // T

    def body(p_ref, dm_ref, pw_ref, sc_ref, dzu_ref, dpw_ref, dsc_ref, halo_ref):
        i = pl.program_id(0)
        blk = nb - 1 - i

        @pl.when(i == 0)
        def _():
            halo_ref[...] = jnp.zeros_like(halo_ref)
            dpw_ref[...] = jnp.zeros_like(dpw_ref)
            dsc_ref[...] = jnp.zeros_like(dsc_ref)

        for g, w in enumerate(POOL_WINDOWS):
            cols = slice(g * POOL_GROUP, (g + 1) * POOL_GROUP)
            p = p_ref[:, cols]
            dm = dm_ref[:, cols]
            pw = pw_ref[g].astype(BF16)
            pm = jnp.dot(p, pw, preferred_element_type=F32)
            dsc_ref[:, cols] += jnp.sum((dm * pm).reshape(T // 8, 8, POOL_GROUP), axis=0)
            dpm = (dm * sc_ref[:, cols]).astype(BF16)
            dpw_ref[g] += lax.dot_general(p, dpm, (((0,), (0,)), ((), ())), preferred_element_type=F32)
            dpooled = lax.dot_general(dpm, pw, (((1,), (1,)), ((), ())), preferred_element_type=F32)
            e = dpooled / _window_counts(blk, T, w)
            ext = jnp.concatenate([e, halo_ref[:, cols]], axis=0)
            halo_ref[:, cols] = e[:HALO, :]
            s = ext
            sh = 1
            while sh < w:
                s = s + pltpu.roll(s, T + HALO - sh, 0)
                sh *= 2
            dzu_ref[:, cols] = (s[:T, :] - dpooled).astype(BF16)

    row = pl.BlockSpec((T, W), lambda i: (nb - 1 - i, 0))
    return pl.pallas_call(
        body, name=name,
        out_shape=(jax.ShapeDtypeStruct((S, W), BF16), jax.ShapeDtypeStruct(pool_w.shape, F32),
                   jax.ShapeDtypeStruct((8, W), F32)),
        grid=(nb,),
        in_specs=[row, row, pl.BlockSpec(pool_w.shape, lambda i: (0, 0, 0)), pl.BlockSpec((1, W), lambda i: (0, 0))],
        out_specs=(row, pl.BlockSpec(pool_w.shape, lambda i: (0, 0, 0)), pl.BlockSpec((8, W), lambda i: (0, 0))),
        scratch_shapes=[pltpu.VMEM((HALO, W), F32)],
        compiler_params=_cparams("arbitrary"),
    )(pooled, dmixed, pool_w, pool_scale)


def _fgate_fwd(zf, bf, *, name):
    S, W = zf.shape
    T = _pick(S, 512, 8)

    def body(z_ref, b_ref, c_ref, aq_ref, ak_ref, carry_ref):
        i = pl.program_id(0)

        @pl.when(i == 0)
        def _():
            carry_ref[...] = jnp.zeros_like(carry_ref)

        a = z_ref[...] + b_ref[...]
        s = jnp.minimum(a, 0.0) - jnp.log(1.0 + jnp.exp(-jnp.abs(a)))
        row = lax.broadcasted_iota(jnp.int32, (T, W), 0)
        sh = 1
        while sh < T:
            s = s + jnp.where(row >= sh, pltpu.roll(s, sh, 0), 0.0)
            sh *= 2
        c = s + carry_ref[0:1, :]
        c_ref[...] = c
        carry_ref[...] = jnp.broadcast_to(c[T - 1:T, :], carry_ref.shape)
        lane = lax.broadcasted_iota(jnp.int32, (T, W), 1)
        for h in range(FOX_HEADS):
            ch = c[:, h:h + 1]
            hi = ch.astype(BF16).astype(F32)
            r1 = ch - hi
            lo = r1.astype(BF16).astype(F32)
            lo2 = (r1 - lo).astype(BF16).astype(F32)
            aq = jnp.where(lane == 0, hi, jnp.where(lane == 1, lo, jnp.where(lane == 2, lo2,
                                                                              jnp.where(lane < 6, 1.0, 0.0))))
            ak = jnp.where(lane < 3, 1.0, jnp.where(lane == 3, -hi, jnp.where(lane == 4, -lo,
                                                                               jnp.where(lane == 5, -lo2, 0.0))))
            aq_ref[h] = aq.astype(BF16)
            ak_ref[h] = ak.astype(BF16)

    aug = jax.ShapeDtypeStruct((FOX_HEADS, S, W), BF16)
    aug_spec = pl.BlockSpec((FOX_HEADS, T, W), lambda i: (0, i, 0))
    return pl.pallas_call(
        body, name=name, out_shape=(jax.ShapeDtypeStruct((S, W), F32), aug, aug), grid=(S // T,),
        in_specs=[pl.BlockSpec((T, W), lambda i: (i, 0)), pl.BlockSpec((1, W), lambda i: (0, 0))],
        out_specs=(pl.BlockSpec((T, W), lambda i: (i, 0)), aug_spec, aug_spec),
        scratch_shapes=[pltpu.VMEM((8, W), F32)],
        compiler_params=_cparams("arbitrary"),
    )(zf, bf)


def _fgate_bwd(zf, bf, dc, *, name):
    S, W = zf.shape
    T = _pick(S, 512, 8)
    nb = S // T

    def body(z_ref, b_ref, dc_ref, dz_ref, db_ref, carry_ref):
        i = pl.program_id(0)

        @pl.when(i == 0)
        def _():
            carry_ref[...] = jnp.zeros_like(carry_ref)
            db_ref[...] = jnp.zeros_like(db_ref)

        s = dc_ref[...]
        row = lax.broadcasted_iota(jnp.int32, (T, W), 0)
        sh = 1
        while sh < T:
            s = s + jnp.where(row < T - sh, pltpu.roll(s, T - sh, 0), 0.0)
            sh *= 2
        dlf = s + carry_ref[0:1, :]
        carry_ref[...] = jnp.broadcast_to(dlf[0:1, :], carry_ref.shape)
        dz = dlf * (1.0 - _sigmoid(z_ref[...] + b_ref[...]))
        dz_ref[...] = dz.astype(BF16)
        db_ref[...] += jnp.sum(dz.reshape(T // 8, 8, W), axis=0)

    row_spec = pl.BlockSpec((T, W), lambda i: (nb - 1 - i, 0))
    return pl.pallas_call(
        body, name=name,
        out_shape=(jax.ShapeDtypeStruct((S, W), BF16), jax.ShapeDtypeStruct((8, W), F32)),
        grid=(nb,),
        in_specs=[row_spec, pl.BlockSpec((1, W), lambda i: (0, 0)), row_spec],
        out_specs=(row_spec, pl.BlockSpec((8, W), lambda i: (0, 0))),
        scratch_shapes=[pltpu.VMEM((8, W), F32)],
        compiler_params=_cparams("arbitrary"),
    )(zf, bf, dc)


def _merge_fwd(zg, yp, yf, *, name):
    S, D = yp.shape
    ts = _row_block(S, D, 1024 * 1024)

    def body(zg_ref, yp_ref, yf_ref, o_ref):
        o_ref[...] = (_sigmoid(zg_ref[:, :D]) * yp_ref[...] + _sigmoid(zg_ref[:, D:]) * yf_ref[...]).astype(BF16)

    row = pl.BlockSpec((ts, D), lambda i: (i, 0))
    return pl.pallas_call(
        body, name=name, out_shape=jax.ShapeDtypeStruct((S, D), BF16), grid=(S // ts,),
        in_specs=[pl.BlockSpec((ts, 2 * D), lambda i: (i, 0)), row, row], out_specs=row,
        compiler_params=_cparams("parallel"),
    )(zg, yp, yf)


def _merge_bwd(zg, yp, yf, dmerged, *, name):
    S, D = yp.shape
    ts = _row_block(S, D, 1024 * 1024)

    def body(zg_ref, yp_ref, yf_ref, dm_ref, dyp_ref, dyf_ref, dzg_ref):
        dm = dm_ref[...]
        sp = _sigmoid(zg_ref[:, :D])
        sf = _sigmoid(zg_ref[:, D:])
        dyp_ref[...] = (dm * sp).astype(BF16)
        dyf_ref[...] = (dm * sf).astype(BF16)
        dzg_ref[:, :D] = (dm * yp_ref[...] * (sp * (1.0 - sp))).astype(BF16)
        dzg_ref[:, D:] = (dm * yf_ref[...] * (sf * (1.0 - sf))).astype(BF16)

    row = pl.BlockSpec((ts, D), lambda i: (i, 0))
    wide = pl.BlockSpec((ts, 2 * D), lambda i: (i, 0))
    return pl.pallas_call(
        body, name=name,
        out_shape=(jax.ShapeDtypeStruct((S, D), BF16), jax.ShapeDtypeStruct((S, D), BF16),
                   jax.ShapeDtypeStruct((S, 2 * D), BF16)),
        grid=(S // ts,), in_specs=[wide, row, row, row], out_specs=(row, row, wide),
        compiler_params=_cparams("parallel"),
    )(zg, yp, yf, dmerged)


NEG_BIG = -1e30


FOX_BLOCK = 1024
PAIR = LANES // FOX_DH
N_PAIRS = FOX_HEADS // PAIR


def _fox_fwd(zqkv, augq, augk, *, name):
    S = zqkv.shape[0]
    bq = _pick(S, FOX_BLOCK, 128)
    nq = S // bq
    scale = 1.0 / math.sqrt(FOX_DH)

    def body(q_ref, k_ref, v_ref, aq_ref, ak_ref, o_ref, lse_ref):
        i = pl.program_id(1)
        lane = lax.broadcasted_iota(jnp.int32, (1, LANES), 1)
        first = lane < FOX_DH
        q2 = q_ref[...] * scale
        zero = jnp.zeros_like(q2)
        qh = (jnp.concatenate([jnp.where(first, q2, zero), aq_ref[0]], axis=1),
              jnp.concatenate([jnp.where(first, zero, q2), aq_ref[1]], axis=1))

        def step(j, carry, masked):
            start = pl.multiple_of(j * bq, bq)
            kb = k_ref[pl.ds(start, bq), :]
            vb = v_ref[pl.ds(start, bq), :]
            one = jnp.ones_like(vb)
            vh = (jnp.where(first, vb, one), jnp.where(first, one, vb))
            out = []
            for h in range(PAIR):
                m, acc = carry[h]
                kh = jnp.concatenate([kb, ak_ref[h, pl.ds(start, bq), :]], axis=1)
                s = lax.dot_general(qh[h], kh, (((1,), (1,)), ((), ())), preferred_element_type=F32)
                if masked:
                    r = lax.broadcasted_iota(jnp.int32, (bq, bq), 0)
                    c = lax.broadcasted_iota(jnp.int32, (bq, bq), 1)
                    s = jnp.where(c <= r, s, NEG_BIG)
                m_new = jnp.maximum(m, jnp.max(s, axis=-1, keepdims=True))
                alpha = jnp.exp(m - m_new)
                p = jnp.exp(s - m_new).astype(BF16)
                acc = alpha * acc + jnp.dot(p, vh[h], preferred_element_type=F32)
                out.append((m_new, acc))
            return tuple(out)

        init = tuple((jnp.full((bq, 1), NEG_BIG, F32), jnp.zeros((bq, LANES), F32)) for _ in range(PAIR))
        carry = lax.fori_loop(0, i, lambda j, c: step(j, c, False), init)
        (ma, acca), (mb, accb) = step(i, carry, True)
        num = jnp.where(first, acca, accb)
        den = jnp.where(first, pltpu.roll(acca, FOX_DH, 1), pltpu.roll(accb, FOX_DH, 1))
        o_ref[...] = num / den
        lse_ref[...] = jnp.where(first, mb, ma) + jnp.log(jnp.where(first, accb, acca))

    npair = N_PAIRS
    return pl.pallas_call(
        body, name=name,
        out_shape=(jax.ShapeDtypeStruct((S, FOX_W), F32), jax.ShapeDtypeStruct((S, FOX_W), F32)),
        grid=(npair, nq),
        in_specs=[pl.BlockSpec((bq, LANES), lambda hp, i: (i, hp)),
                  pl.BlockSpec((S, LANES), lambda hp, i: (0, npair + hp)),
                  pl.BlockSpec((S, LANES), lambda hp, i: (0, 2 * npair + hp)),
                  pl.BlockSpec((PAIR, bq, LANES), lambda hp, i: (hp, i, 0)),
                  pl.BlockSpec((PAIR, S, LANES), lambda hp, i: (hp, 0, 0))],
        out_specs=(pl.BlockSpec((bq, LANES), lambda hp, i: (i, hp)),
                   pl.BlockSpec((bq, LANES), lambda hp, i: (i, hp))),
        compiler_params=_cparams("parallel", "arbitrary"),
    )(zqkv, zqkv, zqkv, augq, augk)


def _head_rowsum(a, b, *, name):
    S, W = a.shape
    ts = _pick(S, 1024, 8)

    def body(a_ref, b_ref, o_ref):
        prod = a_ref[...].astype(F32) * b_ref[...].astype(F32)
        hi = prod.astype(BF16)
        lo = (prod - hi.astype(F32)).astype(BF16)
        r = lax.broadcasted_iota(jnp.int32, (W, LANES), 0)
        c = lax.broadcasted_iota(jnp.int32, (W, LANES), 1)
        sel = jnp.where(r // FOX_DH == c, 1.0, 0.0).astype(BF16)
        o_ref[...] = (jnp.dot(hi, sel, preferred_element_type=F32) + jnp.dot(lo, sel, preferred_element_type=F32))

    return pl.pallas_call(
        body, name=name, out_shape=jax.ShapeDtypeStruct((S, LANES), F32), grid=(S // ts,),
        in_specs=[pl.BlockSpec((ts, W), lambda i: (i, 0)), pl.BlockSpec((ts, W), lambda i: (i, 0))],
        out_specs=pl.BlockSpec((ts, LANES), lambda i: (i, 0)),
        compiler_params=_cparams("parallel"),
    )(a, b)


def _fox_bwd(zqkv, do, augq, augk, lse_row, delta_row, *, name):
    S = zqkv.shape[0]
    bk = _pick(S, FOX_BLOCK, 128)
    nk = S // bk
    scale = 1.0 / math.sqrt(FOX_DH)
    npair = N_PAIRS

    def body(q_ref, k_ref, v_ref, do_ref, ak_ref, aq_ref, lse_ref, dl_ref, dq_ref, dk_ref, dv_ref, dck_ref, dcq_ref):
        j = pl.program_id(1)

        @pl.when(j == 0)
        def _():
            dq_ref[...] = jnp.zeros_like(dq_ref)
            dcq_ref[...] = jnp.zeros_like(dcq_ref)

        lane = lax.broadcasted_iota(jnp.int32, (1, LANES), 1)
        first = lane < FOX_DH
        kb = k_ref[...]
        vb = v_ref[...]
        kh = (jnp.concatenate([kb, ak_ref[0]], axis=1), jnp.concatenate([kb, ak_ref[1]], axis=1))

        def step(i, carry, masked):
            start = pl.multiple_of(i * bk, bk)
            qs = q_ref[pl.ds(start, bk), :] * scale
            dob = do_ref[pl.ds(start, bk), :]
            zero = jnp.zeros_like(qs)
            qh = (jnp.where(first, qs, zero), jnp.where(first, zero, qs))
            doh = (jnp.where(first, dob, zero), jnp.where(first, zero, dob))
            out = []
            dqc = []
            for h in range(PAIR):
                dk, dv, dc = carry[h]
                qaug = jnp.concatenate([qh[h], aq_ref[h, pl.ds(start, bk), :]], axis=1)
                lse = lse_ref[h, :, pl.ds(start, bk)]
                dl = dl_ref[h, :, pl.ds(start, bk)]
                st = lax.dot_general(kh[h], qaug, (((1,), (1,)), ((), ())), preferred_element_type=F32)
                pt = jnp.exp(st - lse)
                if masked:
                    r = lax.broadcasted_iota(jnp.int32, (bk, bk), 0)
                    c = lax.broadcasted_iota(jnp.int32, (bk, bk), 1)
                    pt = jnp.where(c >= r, pt, 0.0)
                dpt = lax.dot_general(vb, doh[h], (((1,), (1,)), ((), ())), preferred_element_type=F32)
                dst = pt * (dpt - dl)
                pt_bf = pt.astype(BF16)
                dst_bf = dst.astype(BF16)
                dv = dv + jnp.dot(pt_bf, dob, preferred_element_type=F32)
                dk = dk + jnp.dot(dst_bf, qs, preferred_element_type=F32)
                dc = dc - jnp.sum(dst, axis=-1, keepdims=True)
                dcq_ref[h, :, pl.ds(start, bk)] += jnp.sum(dst, axis=0, keepdims=True)
                dqc.append(lax.dot_general(dst_bf, kb, (((0,), (0,)), ((), ())), preferred_element_type=F32))
                out.append((dk, dv, dc))
            dq_ref[pl.ds(start, bk), :] += jnp.where(first, dqc[0], dqc[1])
            return tuple(out)

        init = tuple((jnp.zeros((bk, LANES), F32), jnp.zeros((bk, LANES), F32), jnp.zeros((bk, 1), F32))
                     for _ in range(PAIR))
        carry = step(j, init, True)
        (dka, dva, dca), (dkb, dvb, dcb) = lax.fori_loop(j + 1, nk, lambda i, c: step(i, c, False), carry)
        dk_ref[...] = jnp.where(first, dka, dkb).astype(BF16)
        dv_ref[...] = jnp.where(first, dva, dvb).astype(BF16)
        dck_ref[0] = dca
        dck_ref[1] = dcb

        @pl.when(j == nk - 1)
        def _():
            dq_ref[...] = dq_ref[...] * scale

    rowfull = pl.BlockSpec((PAIR, 1, S), lambda hp, j: (hp, 0, 0))
    return pl.pallas_call(
        body, name=name,
        out_shape=(jax.ShapeDtypeStruct((S, FOX_W), F32), jax.ShapeDtypeStruct((S, FOX_W), BF16),
                   jax.ShapeDtypeStruct((S, FOX_W), BF16), jax.ShapeDtypeStruct((FOX_HEADS, S, 1), F32),
                   jax.ShapeDtypeStruct((FOX_HEADS, 1, S), F32)),
        grid=(npair, nk),
        in_specs=[pl.BlockSpec((S, LANES), lambda hp, j: (0, hp)),
                  pl.BlockSpec((bk, LANES), lambda hp, j: (j, npair + hp)),
                  pl.BlockSpec((bk, LANES), lambda hp, j: (j, 2 * npair + hp)),
                  pl.BlockSpec((S, LANES), lambda hp, j: (0, hp)),
                  pl.BlockSpec((PAIR, bk, LANES), lambda hp, j: (hp, j, 0)),
                  pl.BlockSpec((PAIR, S, LANES), lambda hp, j: (hp, 0, 0)), rowfull, rowfull],
        out_specs=(pl.BlockSpec((S, LANES), lambda hp, j: (0, hp)),
                   pl.BlockSpec((bk, LANES), lambda hp, j: (j, hp)),
                   pl.BlockSpec((bk, LANES), lambda hp, j: (j, hp)),
                   pl.BlockSpec((PAIR, bk, 1), lambda hp, j: (hp, j, 0)), rowfull),
        compiler_params=_cparams("parallel", "arbitrary"),
    )(zqkv, zqkv, zqkv, do, augk, augq, lse_row, delta_row)


def _xattn_fwd(q, kv, *, name):
    S, W = q.shape
    M = kv.shape[0]
    tq = _pick(S, 512, 8)
    scale = 1.0 / math.sqrt(X_DH)

    def body(q_ref, kv_ref, o_ref):
        for h in range(X_HEADS):
            cols = slice(h * X_DH, (h + 1) * X_DH)
            vcols = slice(W + h * X_DH, W + (h + 1) * X_DH)
            s = lax.dot_general(q_ref[:, cols], kv_ref[:, cols], (((1,), (1,)), ((), ())),
                                preferred_element_type=F32) * scale
            e = jnp.exp(s - jnp.max(s, axis=-1, keepdims=True))
            p = e / jnp.sum(e, axis=-1, keepdims=True)
            o_ref[:, cols] = jnp.dot(p.astype(BF16), kv_ref[:, vcols], preferred_element_type=F32).astype(BF16)

    return pl.pallas_call(
        body, name=name, out_shape=jax.ShapeDtypeStruct((S, W), BF16), grid=(S // tq,),
        in_specs=[pl.BlockSpec((tq, W), lambda i: (i, 0)), pl.BlockSpec((M, 2 * W), lambda i: (0, 0))],
        out_specs=pl.BlockSpec((tq, W), lambda i: (i, 0)),
        compiler_params=_cparams("parallel"),
    )(q, kv)


def _xattn_bwd(q, kv, do, *, name):
    S, W = q.shape
    M = kv.shape[0]
    tq = _pick(S, 512, 8)
    scale = 1.0 / math.sqrt(X_DH)

    def body(q_ref, kv_ref, do_ref, dq_ref, dkv_ref):
        i = pl.program_id(0)

        @pl.when(i == 0)
        def _():
            dkv_ref[...] = jnp.zeros_like(dkv_ref)

        for h in range(X_HEADS):
            cols = slice(h * X_DH, (h + 1) * X_DH)
            vcols = slice(W + h * X_DH, W + (h + 1) * X_DH)
            qh = q_ref[:, cols]
            kh = kv_ref[:, cols]
            vh = kv_ref[:, vcols]
            doh = do_ref[:, cols]
            s = lax.dot_general(qh, kh, (((1,), (1,)), ((), ())), preferred_element_type=F32) * scale
            e = jnp.exp(s - jnp.max(s, axis=-1, keepdims=True))
            p = e / jnp.sum(e, axis=-1, keepdims=True)
            dp = lax.dot_general(doh, vh, (((1,), (1,)), ((), ())), preferred_element_type=F32)
            ds = (p * (dp - jnp.sum(p * dp, axis=-1, keepdims=True)) * scale).astype(BF16)
            dq_ref[:, cols] = jnp.dot(ds, kh, preferred_element_type=F32).astype(BF16)
            dkv_ref[:, cols] += lax.dot_general(ds, qh, (((0,), (0,)), ((), ())), preferred_element_type=F32)
            dkv_ref[:, vcols] += lax.dot_general(p.astype(BF16), doh, (((0,), (0,)), ((), ())),
                                                 preferred_element_type=F32)

    return pl.pallas_call(
        body, name=name,
        out_shape=(jax.ShapeDtypeStruct((S, W), BF16), jax.ShapeDtypeStruct((M, 2 * W), F32)),
        grid=(S // tq,),
        in_specs=[pl.BlockSpec((tq, W), lambda i: (i, 0)), pl.BlockSpec((M, 2 * W), lambda i: (0, 0)),
                  pl.BlockSpec((tq, W), lambda i: (i, 0))],
        out_specs=(pl.BlockSpec((tq, W), lambda i: (i, 0)), pl.BlockSpec((M, 2 * W), lambda i: (0, 0))),
        compiler_params=_cparams("arbitrary"),
    )(q, kv, do)


GELU_C = math.sqrt(2.0 / math.pi)
GELU_A = 0.044715
CONV_HALO = 8


def _gelu_parts(x):
    u = GELU_C * (x + GELU_A * x * x * x)
    t = jnp.tanh(u)
    g = 0.5 * x * (1.0 + t)
    dg = 0.5 * (1.0 + t) + 0.5 * x * (1.0 - t * t) * (GELU_C * (1.0 + 3.0 * GELU_A * x * x))
    return g, dg


def _conv3(ext, w_ref, b_ref):
    return (w_ref[2:3, :] * ext + w_ref[1:2, :] * pltpu.roll(ext, 1, 0)
            + w_ref[0:1, :] * pltpu.roll(ext, 2, 0) + b_ref[...])


def _convglu_fwd(z, conv_w, conv_b, *, name):
    S, F2 = z.shape
    F = F2 // 2
    tc = _pick(F, 1408)
    ncol = F // tc
    T = _pick(S, 512, 8)
    hb = T // CONV_HALO

    def body(zg_ref, zu_ref, zgp_ref, zup_ref, wg_ref, wu_ref, bg_ref, bu_ref, act_ref):
        i = pl.program_id(1)
        first = (i > 0).astype(F32)

        def conv(z_ref, zp_ref, w_ref, b_ref):
            ext = jnp.concatenate([zp_ref[...] * first, z_ref[...]], axis=0)
            return _conv3(ext, w_ref, b_ref)[CONV_HALO:, :]

        gc = conv(zg_ref, zgp_ref, wg_ref, bg_ref)
        uc = conv(zu_ref, zup_ref, wu_ref, bu_ref)
        act_ref[...] = (_gelu_parts(gc)[0] * uc).astype(BF16)

    cur = lambda off: pl.BlockSpec((T, tc), lambda j, i: (i, j + off))
    prev = lambda off: pl.BlockSpec((CONV_HALO, tc), lambda j, i: (jnp.maximum(i * hb - 1, 0), j + off))
    vec = lambda rows, off: pl.BlockSpec((rows, tc), lambda j, i: (0, j + off))
    return pl.pallas_call(
        body, name=name, out_shape=jax.ShapeDtypeStruct((S, F), BF16), grid=(ncol, S // T),
        in_specs=[cur(0), cur(ncol), prev(0), prev(ncol), vec(3, 0), vec(3, ncol), vec(1, 0), vec(1, ncol)],
        out_specs=pl.BlockSpec((T, tc), lambda j, i: (i, j)),
        compiler_params=_cparams("parallel", "parallel"),
    )(z, z, z, z, conv_w, conv_w, conv_b, conv_b)


def _convglu_bwd(z, dact, conv_w, conv_b, *, name):
    S, F2 = z.shape
    F = F2 // 2
    tc = _pick(F, 1408)
    ncol = F // tc
    T = _pick(S, 256, 8)
    nrow = S // T
    hb = T // CONV_HALO
    TE = T + CONV_HALO

    def body(zg_ref, zu_ref, zgp_ref, zup_ref, zgn_ref, zun_ref, da_ref, dan_ref,
             wg_ref, wu_ref, bg_ref, bu_ref, dzg_ref, dzu_ref, dwg_ref, dwu_ref):
        i = pl.program_id(1)
        first = (i > 0).astype(F32)
        last = (i < nrow - 1).astype(F32)

        @pl.when(i == 0)
        def _():
            dwg_ref[...] = jnp.zeros_like(dwg_ref)
            dwu_ref[...] = jnp.zeros_like(dwu_ref)

        extg = jnp.concatenate([zgp_ref[...] * first, zg_ref[...], zgn_ref[...]], axis=0)
        extu = jnp.concatenate([zup_ref[...] * first, zu_ref[...], zun_ref[...]], axis=0)
        gc = _conv3(extg, wg_ref, bg_ref)[CONV_HALO:, :]
        uc = _conv3(extu, wu_ref, bu_ref)[CONV_HALO:, :]
        da = jnp.concatenate([da_ref[...], dan_ref[...] * last], axis=0)
        gl, dgl = _gelu_parts(gc)
        dgc = da * uc * dgl
        duc = da * gl

        def back(d, ext, w_ref, dz_ref, dw_ref):
            dz = w_ref[2:3, :] * d + w_ref[1:2, :] * pltpu.roll(d, TE - 1, 0) + w_ref[0:1, :] * pltpu.roll(d, TE - 2, 0)
            dz_ref[...] = dz[:T, :].astype(BF16)
            dc = d[:T, :]
            z0 = ext[CONV_HALO:CONV_HALO + T, :]
            z1 = pltpu.roll(ext, 1, 0)[CONV_HALO:CONV_HALO + T, :]
            z2 = pltpu.roll(ext, 2, 0)[CONV_HALO:CONV_HALO + T, :]
            rows = [jnp.sum(dc * z2, axis=0, keepdims=True), jnp.sum(dc * z1, axis=0, keepdims=True),
                    jnp.sum(dc * z0, axis=0, keepdims=True), jnp.sum(dc, axis=0, keepdims=True)]
            dw_ref[0:4, :] += jnp.concatenate(rows, axis=0)

        back(dgc, extg, wg_ref, dzg_ref, dwg_ref)
        back(duc, extu, wu_ref, dzu_ref, dwu_ref)

    cur = lambda off: pl.BlockSpec((T, tc), lambda j, i: (i, j + off))
    prev = lambda off: pl.BlockSpec((CONV_HALO, tc), lambda j, i: (jnp.maximum(i * hb - 1, 0), j + off))
    nxt = lambda off: pl.BlockSpec((CONV_HALO, tc), lambda j, i: (jnp.minimum((i + 1) * hb, S // CONV_HALO - 1), j + off))
    vec = lambda rows, off: pl.BlockSpec((rows, tc), lambda j, i: (0, j + off))
    dzg, dzu, dwg, dwu = pl.pallas_call(
        body, name=name,
        out_shape=(jax.ShapeDtypeStruct((S, F), BF16), jax.ShapeDtypeStruct((S, F), BF16),
                   jax.ShapeDtypeStruct((8, F), F32), jax.ShapeDtypeStruct((8, F), F32)),
        grid=(ncol, nrow),
        in_specs=[cur(0), cur(ncol), prev(0), prev(ncol), nxt(0), nxt(ncol), cur(0), nxt(0),
                  vec(3, 0), vec(3, ncol), vec(1, 0), vec(1, ncol)],
        out_specs=(cur(0), cur(0), vec(8, 0), vec(8, 0)),
        compiler_params=_cparams("parallel", "arbitrary"),
    )(z, z, z, z, z, z, dact, dact, conv_w, conv_w, conv_b, conv_b)
    return dzg, dzu, dwg, dwu


OFF_QKV = POOL_W
OFF_G = POOL_W + 3 * FOX_W


SHARD_BY_ROWS = {"w_in": False, "w_pool_br": False, "w_fox_br": False, "w_mix_out": True, "w_xq": True,
                 "w_xkv": True, "w_xo": False, "w_up": False, "w_down": True, "conv_w": False}


def _prep_layer_weights(w, l):
    n, _, D, b = w["w_in"].shape
    w_in = w["w_in"][:, l].transpose(1, 0, 2).reshape(D, n * b)
    off_f = OFF_G
    pad = jnp.zeros((D, LANES - N_FGATE), w_in.dtype)
    w_in_r = jnp.concatenate([w_in[:, :off_f], w_in[:, off_f + N_FGATE:], w_in[:, off_f:off_f + N_FGATE], pad], axis=1)
    wl = {k: _ChipMajor(w[k], l, SHARD_BY_ROWS[k]) for k in MATMUL_WEIGHTS if k != "w_in"}
    wl["w_in_r"] = w_in_r
    return wl


def _row(v):
    return v.reshape(1, -1)


def _layer_fwd(x, mem, wl, p, l):
    S, D = x.shape
    n = lambda s: f"l{l}_{s}"
    sv = {"x0": x}
    h1 = _rms_fwd(x, _row(p["mix_pre_g"][l]), out_dtype=BF16, name=n("rms1"))
    w_in_r = wl["w_in_r"]
    og = OFF_G
    zu = _matmul(h1, w_in_r[:, :OFF_QKV], out_dtype=F32, name=n("mm_zu"))
    zqkv = _matmul(h1, w_in_r[:, OFF_QKV:og], out_dtype=BF16, name=n("mm_zqkv"))
    zg = _matmul(h1, w_in_r[:, og:og + 2 * D], out_dtype=F32, name=n("mm_zg"))
    zf = _matmul(h1, w_in_r[:, og + 2 * D:], out_dtype=F32, name=n("mm_zf"))
    bf = jnp.pad(p["b_forget"][l], (0, LANES - N_FGATE)).reshape(1, LANES)
    _, augq, augk = _fgate_fwd(zf, bf, name=n("fgate"))
    o_fox, lse2 = _fox_fwd(zqkv, augq, augk, name=n("fox"))
    pooled, mixed = _pool_fwd(zu, p["pool_w"][l], _row(p["pool_scale"][l]), name=n("pool"))
    y_pool = _matmul(mixed, wl["w_pool_br"], out_dtype=F32, name=n("mm_ypool"))
    y_fox = _matmul(o_fox, wl["w_fox_br"], out_dtype=F32, name=n("mm_yfox"))
    merged = _merge_fwd(zg, y_pool, y_fox, name=n("merge"))
    r1 = _matmul(merged, wl["w_mix_out"], out_dtype=F32, name=n("mm_r1"))
    x1 = _add_rms(x, r1, _row(p["mix_post_g"][l]), name=n("addrms1"))
    sv.update(h1=h1, zg=zg, zf=zf, bf=bf, zqkv=zqkv, augq=augq, augk=augk, o_fox=o_fox, lse2=lse2,
              pooled=pooled, mixed=mixed, y_pool=y_pool, y_fox=y_fox, merged=merged, r1=r1, x1=x1)
    h2 = _rms_fwd(x1, _row(p["xa_pre_g"][l]), out_dtype=BF16, name=n("rms2"))
    mem_n = _rms_fwd(mem, _row(p["mem_g"][l]), out_dtype=BF16, name=n("rms_mem"))
    q2 = _matmul(h2, wl["w_xq"], out_dtype=BF16, name=n("mm_q2"))
    kv = _matmul(mem_n, wl["w_xkv"], out_dtype=BF16, name=n("mm_kv"))
    o2 = _xattn_fwd(q2, kv, name=n("xattn"))
    a2 = _matmul(o2, wl["w_xo"], out_dtype=F32, name=n("mm_a2"))
    x2 = _add_rms(x1, a2, _row(p["xa_post_g"][l]), name=n("addrms2"))
    sv.update(h2=h2, mem_n=mem_n, q2=q2, kv=kv, o2=o2, a2=a2, x2=x2)
    h3 = _rms_fwd(x2, _row(p["ffn_pre_g"][l]), out_dtype=BF16, name=n("rms3"))
    z3 = _matmul(h3, wl["w_up"], out_dtype=F32, name=n("mm_z3"))
    act = _convglu_fwd(z3, p["conv_w"][l], _row(p["conv_b"][l]), name=n("convglu"))
    d3 = _matmul(act, wl["w_down"], out_dtype=F32, name=n("mm_d3"))
    x3 = _add_rms(x2, d3, _row(p["ffn_post_g"][l]), name=n("addrms3"))
    sv.update(h3=h3, z3=z3, act=act, d3=d3)
    return x3, sv


def _layer_bwd(dx, mem, wl, p, l, sv):
    S, D = dx.shape
    n = lambda s: f"l{l}_b_{s}"
    g = {}
    red = lambda part: jnp.sum(part, axis=0)
    dd3, dg = _rms_bwd(sv["d3"], _row(p["ffn_post_g"][l]), dx, out_dtype=BF16, name=n("rms3post"))
    g["ffn_post_g"] = red(dg)
    dact = _matmul(dd3, wl["w_down"], tb=True, out_dtype=F32, name=n("mm_dact"))
    g["w_down"] = _matmul(sv["act"], dd3, ta=True, out_dtype=F32, name=n("mm_dwdown"))
    dzg3, dzu3, dwg, dwu = _convglu_bwd(sv["z3"], dact, p["conv_w"][l], _row(p["conv_b"][l]), name=n("convglu"))
    g["conv_w"] = jnp.concatenate([dwg[:3], dwu[:3]], axis=1)
    g["conv_b"] = jnp.concatenate([dwg[3], dwu[3]], axis=0)
    dz3 = jnp.concatenate([dzg3, dzu3], axis=1)
    dh3 = _matmul(dz3, wl["w_up"], tb=True, out_dtype=F32, name=n("mm_dh3"))
    g["w_up"] = _matmul(sv["h3"], dz3, ta=True, out_dtype=F32, out_chips=N_CHIPS, name=n("mm_dwup"))
    dx, dg = _rms_bwd(sv["x2"], _row(p["ffn_pre_g"][l]), dh3, dx, out_dtype=F32, name=n("rms3pre"))
    g["ffn_pre_g"] = red(dg)
    da2, dg = _rms_bwd(sv["a2"], _row(p["xa_post_g"][l]), dx, out_dtype=BF16, name=n("rms2post"))
    g["xa_post_g"] = red(dg)
    do2 = _matmul(da2, wl["w_xo"], tb=True, out_dtype=BF16, name=n("mm_do2"))
    g["w_xo"] = _matmul(sv["o2"], da2, ta=True, out_dtype=F32, out_chips=N_CHIPS, name=n("mm_dwxo"))
    dq2, dkv = _xattn_bwd(sv["q2"], sv["kv"], do2, name=n("xattn"))
    dh2 = _matmul(dq2, wl["w_xq"], tb=True, out_dtype=F32, name=n("mm_dh2"))
    g["w_xq"] = _matmul(sv["h2"], dq2, ta=True, out_dtype=F32, name=n("mm_dwxq"))
    dmem_n = _matmul(dkv, wl["w_xkv"], tb=True, out_dtype=F32, name=n("mm_dmemn"))
    g["w_xkv"] = _matmul(sv["mem_n"], dkv, ta=True, out_dtype=F32, name=n("mm_dwxkv"))
    _, dg = _rms_bwd(mem, _row(p["mem_g"][l]), dmem_n, out_dtype=BF16, name=n("rms_mem"))
    g["mem_g"] = red(dg)
    dx, dg = _rms_bwd(sv["x1"], _row(p["xa_pre_g"][l]), dh2, dx, out_dtype=F32, name=n("rms2pre"))
    g["xa_pre_g"] = red(dg)
    dr1, dg = _rms_bwd(sv["r1"], _row(p["mix_post_g"][l]), dx, out_dtype=BF16, name=n("rms1post"))
    g["mix_post_g"] = red(dg)
    dmerged = _matmul(dr1, wl["w_mix_out"], tb=True, out_dtype=F32, name=n("mm_dmerged"))
    g["w_mix_out"] = _matmul(sv["merged"], dr1, ta=True, out_dtype=F32, name=n("mm_dwmo"))
    dyp, dyf, dzg = _merge_bwd(sv["zg"], sv["y_pool"], sv["y_fox"], dmerged, name=n("merge"))
    dmixed = _matmul(dyp, wl["w_pool_br"], tb=True, out_dtype=F32, name=n("mm_dmixed"))
    g["w_pool_br"] = _matmul(sv["mixed"], dyp, ta=True, out_dtype=F32, out_chips=N_CHIPS, name=n("mm_dwpb"))
    dofox = _matmul(dyf, wl["w_fox_br"], tb=True, out_dtype=F32, name=n("mm_dofox"))
    g["w_fox_br"] = _matmul(sv["o_fox"], dyf, ta=True, out_dtype=F32, out_chips=N_CHIPS, name=n("mm_dwfb"))
    dzu, dpw, dsc = _pool_bwd(sv["pooled"], dmixed, p["pool_w"][l], _row(p["pool_scale"][l]), name=n("pool"))
    g["pool_w"] = dpw
    g["pool_scale"] = red(dsc)
    delta = _head_rowsum(dofox, sv["o_fox"], name=n("delta"))
    lse = sv["lse2"].reshape(S, N_PAIRS, PAIR, FOX_DH)[:, :, ::-1, 0].reshape(S, FOX_HEADS)
    dq, dk, dv, dck, dcq = _fox_bwd(sv["zqkv"], dofox.astype(BF16), sv["augq"], sv["augk"],
                                    lse.T.reshape(FOX_HEADS, 1, S), delta[:, :FOX_HEADS].T.reshape(FOX_HEADS, 1, S),
                                    name=n("fox"))
    dc = dcq.reshape(FOX_HEADS, S) + dck.reshape(FOX_HEADS, S)
    dc_pad = jnp.pad(dc.T, ((0, 0), (0, LANES - FOX_HEADS)))
    dzf, db = _fgate_bwd(sv["zf"], sv["bf"], dc_pad, name=n("fgate"))
    g["b_forget"] = red(db)[:N_FGATE]
    dz_cat = jnp.concatenate([dzu, dq.astype(BF16), dk, dv, dzg, dzf], axis=1)
    dh1 = _matmul(dz_cat, wl["w_in_r"], tb=True, out_dtype=F32, name=n("mm_dh1"))
    dw_in_r = _matmul(sv["h1"], dz_cat, ta=True, out_dtype=F32, name=n("mm_dwin"))
    og = OFF_G
    g["w_in"] = jnp.concatenate([dw_in_r[:, :og], dw_in_r[:, og + 2 * D:og + 2 * D + N_FGATE],
                                 dw_in_r[:, og:og + 2 * D]], axis=1)
    dx, dg = _rms_bwd(sv["x0"], _row(p["mix_pre_g"][l]), dh1, dx, out_dtype=F32, name=n("rms1pre"))
    g["mix_pre_g"] = red(dg)
    return dx, g


MATMUL_WEIGHTS = ("w_in", "w_pool_br", "w_fox_br", "w_mix_out", "w_xq", "w_xkv", "w_xo", "w_up", "w_down")
WEIGHT_NAMES = ("mix_pre_g", "mix_post_g", "w_in", "b_forget", "pool_w", "pool_scale", "w_pool_br", "w_fox_br",
                "w_mix_out", "xa_pre_g", "xa_post_g", "mem_g", "w_xq", "w_xkv", "w_xo", "ffn_pre_g", "ffn_post_g",
                "w_up", "conv_w", "conv_b", "w_down")


def _local_step(x, mem, loss_target, wfull, p):
    L = p["mix_pre_g"].shape[0]
    saved, wls = [], []
    h = x
    for l in range(L):
        wl = _prep_layer_weights(wfull, l)
        h, sv = _layer_fwd(h, mem, wl, p, l)
        saved.append(sv)
        wls.append(wl)
    D = x.shape[1]
    dy, sq = _loss_head(h, loss_target, name="loss_head")
    loss = 0.5 * jnp.sum(sq) / D
    grads = []
    dx = dy
    for l in reversed(range(L)):
        dx, g = _layer_bwd(dx, mem, wls[l], p, l, saved[l])
        grads.append(g)
    grads = grads[::-1]

    def chip_major(k, g):
        if g.ndim == 3:
            return g
        if SHARD_BY_ROWS[k]:
            return g.reshape(N_CHIPS, g.shape[0] // N_CHIPS, g.shape[1])
        return g.reshape(g.shape[0], N_CHIPS, g.shape[1] // N_CHIPS).transpose(1, 0, 2)

    gfull = {k: jnp.stack([grads[l][k] for l in range(L)]) for k in REPLICATED}
    gfull.update({k: jnp.stack([chip_major(k, grads[l][k]) for l in range(L)], axis=1) for k in SHARD_BY_ROWS})
    return loss, dx, gfull


PACK_W = 512
PACK_ROW_ALIGN = 1024
N_CHIPS = 4
N_DEV = 8
SHARDED = (("w_in", 2), ("w_pool_br", 2), ("w_fox_br", 2), ("w_mix_out", 1), ("w_xq", 1), ("w_xkv", 1),
           ("w_xo", 2), ("w_up", 2), ("w_down", 1), ("conv_w", 2))
REPLICATED = ("mix_pre_g", "mix_post_g", "b_forget", "pool_w", "pool_scale", "xa_pre_g", "xa_post_g", "mem_g",
              "ffn_pre_g", "ffn_post_g", "conv_b")


def _round_up(n, m):
    return -(-n // m) * m


def _pack(arrs, rows):
    flat = jnp.concatenate([a.reshape(-1) for a in arrs])
    return jnp.pad(flat, (0, rows * PACK_W - flat.shape[0])).reshape(rows, PACK_W)


def _unpack(buf, shapes):
    flat = buf.reshape(-1)
    out, off = [], 0
    for s in shapes:
        n = math.prod(s)
        out.append(flat[off:off + n].reshape(s))
        off += n
    return out


ANY = pl.BlockSpec(memory_space=pl.ANY)


def _remote(send_sems, recv_sems, k, src, dst, to):
    return pltpu.make_async_remote_copy(src_ref=src, dst_ref=dst, send_sem=send_sems.at[k], recv_sem=recv_sems.at[k],
                                        device_id=to, device_id_type=MESH)


def _my_place():
    return lax.axis_index("x"), lax.axis_index("y"), lax.axis_index("c")


def _rows_per_block(a, b, cap_bytes=1024 * 1024):
    if a % 8:
        return a
    return _pick(a, max(8, cap_bytes // (4 * b) // 8 * 8), 8)


def _place_shard(w, chip, dtype, *, name):
    L, a, b = w.shape
    ta = _rows_per_block(a, b)

    def body(chip_ref, w_ref, o_ref):
        o_ref[...] = w_ref[...].astype(dtype)

    return pl.pallas_call(
        body, name=name, out_shape=jax.ShapeDtypeStruct((N_CHIPS, L, a, b), dtype),
        grid_spec=pltpu.PrefetchScalarGridSpec(
            num_scalar_prefetch=1, grid=(L, a // ta),
            in_specs=[pl.BlockSpec((None, ta, b), lambda l, i, chip_ref: (l, i, 0))],
            out_specs=pl.BlockSpec((None, None, ta, b), lambda l, i, chip_ref: (chip_ref[0], l, i, 0))),
        compiler_params=_cparams("parallel", "parallel"),
    )(chip, w)


def _layer_halves(L, c):
    assert L % 2 == 0
    return pl.ds(c * (L // 2), L // 2), pl.ds((1 - c) * (L // 2), L // 2)


def _gather_weights(bufs):
    n = len(bufs)
    L = bufs[0].shape[1]

    def body(*refs):
        outs, (send_sems, recv_sems) = refs[n:2 * n], refs[2 * n:]
        x, y, c = _my_place()
        me = 2 * x + y
        sibling = (x, y, 1 - c)
        chips = [(1 - x, y), (x, 1 - y), (1 - x, 1 - y)]
        half, other = _layer_halves(L, c)
        rc = functools.partial(_remote, send_sems, recv_sems)
        first = [rc(6 * w + k, o.at[me, half], o.at[me, half], (px, py, c))
                 for w, o in enumerate(outs) for k, (px, py) in enumerate(chips)]
        for cp in first:
            cp.start()
        passed = []
        for k, (px, py) in enumerate(chips):
            src = 2 * px + py
            for w, o in enumerate(outs):
                rc(6 * w + k, o.at[src, half], o.at[src, half], (px, py, c)).wait_recv()
                fwd = rc(6 * w + 3 + k, o.at[src, half], o.at[src, half], sibling)
                fwd.start()
                passed.append(fwd)
        for k, (px, py) in enumerate(chips):
            src = 2 * px + py
            for w, o in enumerate(outs):
                rc(6 * w + 3 + k, o.at[src, other], o.at[src, other], sibling).wait_recv()
        for cp in first + passed:
            cp.wait_send()

    return pl.pallas_call(
        body, name="gather_weights",
        out_shape=tuple(jax.ShapeDtypeStruct(b.shape, b.dtype) for b in bufs),
        in_specs=[ANY] * n, out_specs=tuple([ANY] * n), input_output_aliases={i: i for i in range(n)},
        scratch_shapes=[pltpu.SemaphoreType.DMA((6 * n,)), pltpu.SemaphoreType.DMA((6 * n,))],
    )(*bufs)


def _exchange_halves(Gs, rep):
    n = len(Gs)
    L = Gs[0].shape[1]
    RR, W = rep.shape

    def body(*refs):
        g_refs, rep_ref, ra_refs, rall_ref = refs[:n], refs[n], refs[n + 1:2 * n + 1], refs[2 * n + 1]
        send_sems, recv_sems, local_sem = refs[2 * n + 2:]
        x, y, c = _my_place()
        me = 4 * x + 2 * y + c
        rc = functools.partial(_remote, send_sems, recv_sems)
        _, other = _layer_halves(L, c)

        def peer(idx):
            px = (1 - x) if (idx >> 2) & 1 else x
            py = (1 - y) if (idx >> 1) & 1 else y
            pc = (1 - c) if idx & 1 else c
            return px, py, pc

        loc = pltpu.make_async_copy(rep_ref, rall_ref.at[me], local_sem)
        loc.start()
        cps = [rc(N_DEV + w, g.at[:, other], ra, (x, y, 1 - c)) for w, (g, ra) in enumerate(zip(g_refs, ra_refs))]
        for idx in range(1, N_DEV):
            cps.append(rc(idx, rep_ref, rall_ref.at[me], peer(idx)))
        for cp in cps:
            cp.start()
        for w, (g, ra) in enumerate(zip(g_refs, ra_refs)):
            rc(N_DEV + w, g.at[:, other], ra, (x, y, 1 - c)).wait_recv()
        for idx in range(1, N_DEV):
            px, py, pc = peer(idx)
            rc(idx, rep_ref, rall_ref.at[4 * px + 2 * py + pc], (px, py, pc)).wait_recv()
        for cp in cps:
            cp.wait_send()
        loc.wait()

    halves = tuple(jax.ShapeDtypeStruct((g.shape[0], L // 2) + g.shape[2:], F32) for g in Gs)
    out = pl.pallas_call(
        body, name="exchange_halves",
        out_shape=halves + (jax.ShapeDtypeStruct((N_DEV, RR, W), F32),),
        in_specs=[ANY] * (n + 1), out_specs=tuple([ANY] * (n + 1)),
        scratch_shapes=[pltpu.SemaphoreType.DMA((N_DEV + n,)), pltpu.SemaphoreType.DMA((N_DEV + n,)),
                        pltpu.SemaphoreType.DMA],
    )(*Gs, rep)
    return out[:n], out[n]


def _exchange_chips(As):
    n = len(As)

    def body(*refs):
        a_refs, rb_refs, (send_sems, recv_sems) = refs[:n], refs[n:2 * n], refs[2 * n:]
        x, y, c = _my_place()
        me = 2 * x + y
        chips = [(1 - x, y), (x, 1 - y), (1 - x, 1 - y)]
        rc = functools.partial(_remote, send_sems, recv_sems)
        cps = [rc(3 * w + k, a.at[2 * px + py], rb.at[k], (px, py, c))
               for w, (a, rb) in enumerate(zip(a_refs, rb_refs)) for k, (px, py) in enumerate(chips)]
        for cp in cps:
            cp.start()
        for w, (a, rb) in enumerate(zip(a_refs, rb_refs)):
            for k, (px, py) in enumerate(chips):
                rc(3 * w + k, a.at[me], rb.at[k], (px, py, c)).wait_recv()
        for cp in cps:
            cp.wait_send()

    return pl.pallas_call(
        body, name="exchange_chips",
        out_shape=tuple(jax.ShapeDtypeStruct((N_CHIPS - 1,) + a.shape[1:], a.dtype) for a in As),
        in_specs=[ANY] * n, out_specs=tuple([ANY] * n),
        scratch_shapes=[pltpu.SemaphoreType.DMA((3 * n,)), pltpu.SemaphoreType.DMA((3 * n,))],
    )(*As)


def _exchange_sibling(gs):
    n = len(gs)
    L = gs[0].shape[0]

    def body(*refs):
        g_refs, (send_sems, recv_sems) = refs[n:2 * n], refs[2 * n:]
        x, y, c = _my_place()
        half, other = _layer_halves(L, c)
        rc = functools.partial(_remote, send_sems, recv_sems)
        cps = [rc(w, g.at[half], g.at[half], (x, y, 1 - c)) for w, g in enumerate(g_refs)]
        for cp in cps:
            cp.start()
        for w, g in enumerate(g_refs):
            rc(w, g.at[other], g.at[other], (x, y, 1 - c)).wait_recv()
        for cp in cps:
            cp.wait_send()

    return pl.pallas_call(
        body, name="exchange_sibling", out_shape=tuple(jax.ShapeDtypeStruct(g.shape, F32) for g in gs),
        in_specs=[ANY] * n, out_specs=tuple([ANY] * n), input_output_aliases={i: i for i in range(n)},
        scratch_shapes=[pltpu.SemaphoreType.DMA((n,)), pltpu.SemaphoreType.DMA((n,))],
    )(*gs)


def _add_halves(G, recv, core, *, name):
    n, L, a, b = G.shape
    Lh = L // 2
    ta = _rows_per_block(a, b)

    def body(core_ref, g_ref, r_ref, o_ref, o16_ref):
        s = g_ref[...] + r_ref[...]
        o_ref[...] = s
        o16_ref[...] = s.astype(BF16)

    blk = pl.BlockSpec((None, None, ta, b), lambda p, l, i, core_ref: (p, l, i, 0))
    return pl.pallas_call(
        body, name=name,
        out_shape=(jax.ShapeDtypeStruct((n, Lh, a, b), F32), jax.ShapeDtypeStruct((n, Lh, a, b), BF16)),
        grid_spec=pltpu.PrefetchScalarGridSpec(
            num_scalar_prefetch=1, grid=(n, Lh, a // ta),
            in_specs=[pl.BlockSpec((None, None, ta, b), lambda p, l, i, core_ref: (p, core_ref[0] * Lh + l, i, 0)), blk],
            out_specs=(blk, blk)),
        compiler_params=_cparams("parallel", "parallel", "parallel"),
    )(core, G, recv)


def _sum_chips(A, rb, place, *, name):
    _, Lh, a, b = A.shape
    ta = _rows_per_block(a, b, 512 * 1024)

    def body(place_ref, a_ref, r_ref, o_ref):
        o_ref[...] = ((a_ref[...] + r_ref[0].astype(F32)) + r_ref[1].astype(F32)) + r_ref[2].astype(F32)

    return pl.pallas_call(
        body, name=name, out_shape=jax.ShapeDtypeStruct((2 * Lh, a, b), F32),
        grid_spec=pltpu.PrefetchScalarGridSpec(
            num_scalar_prefetch=1, grid=(Lh, a // ta),
            in_specs=[pl.BlockSpec((None, None, ta, b), lambda l, i, place_ref: (place_ref[0], l, i, 0)),
                      pl.BlockSpec((N_CHIPS - 1, None, ta, b), lambda l, i, place_ref: (0, l, i, 0))],
            out_specs=pl.BlockSpec((None, ta, b), lambda l, i, place_ref: (place_ref[1] * Lh + l, i, 0))),
        compiler_params=_cparams("parallel", "parallel"),
    )(place, A, rb)


def _sum_slots(a, *, name):
    n, rows, W = a.shape
    tr = _pick(rows, 512, 8)

    def body(a_ref, o_ref):
        s = a_ref[0]
        for q in range(1, n):
            s = s + a_ref[q]
        o_ref[...] = s

    return pl.pallas_call(
        body, name=name, out_shape=jax.ShapeDtypeStruct((rows, W), F32), grid=(rows // tr,),
        in_specs=[pl.BlockSpec((n, tr, W), lambda i: (0, i, 0))], out_specs=pl.BlockSpec((tr, W), lambda i: (i, 0)),
        compiler_params=_cparams("parallel"),
    )(a)


def _adamw(w, g, m, v, *, name):
    L, a, b = w.shape
    ta = _rows_per_block(a, b, 512 * 1024)

    def body(w_ref, g_ref, m_ref, v_ref, d_ref, nm_ref, nv_ref):
        gg = g_ref[...]
        nm = ADAM_B1 * m_ref[...] + (1.0 - ADAM_B1) * gg
        nv = ADAM_B2 * v_ref[...] + (1.0 - ADAM_B2) * jnp.square(gg)
        m_hat = nm / (1.0 - ADAM_B1 ** ADAM_STEP)
        v_hat = nv / (1.0 - ADAM_B2 ** ADAM_STEP)
        d_ref[...] = -ADAM_LR * (m_hat / (jnp.sqrt(v_hat) + ADAM_EPS) + ADAM_WD * w_ref[...])
        nm_ref[...] = nm
        nv_ref[...] = nv

    blk = pl.BlockSpec((None, ta, b), lambda l, i: (l, i, 0))
    shp = jax.ShapeDtypeStruct((L, a, b), F32)
    return pl.pallas_call(
        body, name=name, out_shape=(shp, shp, shp), grid=(L, a // ta),
        in_specs=[blk, blk, blk, blk], out_specs=(blk, blk, blk),
        compiler_params=_cparams("parallel", "parallel"),
    )(w, g, m, v)


INPUT_NAMES = (("x", "mem") + WEIGHT_NAMES + ("loss_target",) + tuple("m_" + n for n in WEIGHT_NAMES)
               + tuple("v_" + n for n in WEIGHT_NAMES))


def kernel(*args):
    a = dict(zip(INPUT_NAMES, args, strict=True))
    x, mem, target = a["x"][0], a["mem"][0], a["loss_target"][0]
    sh_names = list(SHARD_BY_ROWS)
    core = lax.axis_index("c").astype(jnp.int32)
    chip = (2 * lax.axis_index("x") + lax.axis_index("y")).astype(jnp.int32)
    place = jnp.stack([chip, core])

    placed = [_place_shard(a[n], chip.reshape(1), F32 if n == "conv_w" else BF16, name="place_" + n) for n in sh_names]
    wfull = dict(zip(sh_names, _gather_weights(placed)))
    p = {n: a[n] for n in REPLICATED}
    cw = wfull.pop("conv_w")
    p["conv_w"] = cw.transpose(1, 2, 0, 3).reshape(cw.shape[1], cw.shape[2], N_CHIPS * cw.shape[3])

    loss, dx, gfull = _local_step(x, mem, target, wfull, p)
    loss = lax.psum(loss, ("x", "y", "c"))

    Gs = [gfull[n] for n in sh_names]
    rep_shapes = [a[n].shape for n in REPLICATED]
    rows_r = _round_up(-(-sum(math.prod(s) for s in rep_shapes) // PACK_W), 64)
    rep = _pack([gfull[n] for n in REPLICATED], rows_r)
    recvs, repall = _exchange_halves(Gs, rep)
    As = [_add_halves(g, r, core.reshape(1), name="add_halves_" + n) for n, g, r in zip(sh_names, Gs, recvs)]
    rbs = _exchange_chips([a16 for _, a16 in As])
    gsh = _exchange_sibling([_sum_chips(A, rb, place, name="sum_chips_" + n)
                             for n, (A, _), rb in zip(sh_names, As, rbs)])
    grep = _sum_slots(repall, name="sum_devices")

    got = {"g": dict(zip(sh_names, gsh)), "d": {}, "m": {}, "v": {}}
    for n, g in zip(sh_names, gsh):
        got["d"][n], got["m"][n], got["v"][n] = _adamw(a[n], g, a["m_" + n], a["v_" + n], name="adamw_" + n)
    packed = [_pack([a[pre + n] for n in REPLICATED], rows_r)[None] for pre in ("", "m_", "v_")]
    d_r, m_r, v_r = _adamw(packed[0], grep[None], packed[1], packed[2], name="adamw_replicated")
    for key, buf in (("g", grep), ("d", d_r[0]), ("m", m_r[0]), ("v", v_r[0])):
        got[key].update(zip(REPLICATED, _unpack(buf, rep_shapes)))
    outs = [got[key][n] for key in ("g", "d", "m", "v") for n in WEIGHT_NAMES]
    return (loss, dx[None], *outs)
```

```python
import functools
import math

import jax
import jax.numpy as jnp
from jax import lax
from jax.experimental import pallas as pl
from jax.experimental.pallas import tpu as pltpu

F32 = jnp.float32
BF16 = jnp.bfloat16
MESH = pl.DeviceIdType.MESH

RMS_EPS = 1e-6
POOL_WINDOWS = (2, 4, 8, 16)
POOL_GROUP = 128
POOL_W = 512
FOX_HEADS = 8
FOX_DH = 64
FOX_W = 512
X_HEADS = 4
X_DH = 128
X_W = 512
N_FGATE = 8
LANES = 128
HALO = 16

ADAM_LR = 0.001
ADAM_B1 = 0.9
ADAM_B2 = 0.999
ADAM_EPS = 1e-08
ADAM_WD = 0.01
ADAM_STEP = 10

VMEM_LIMIT_BYTES = 56 * 1024 * 1024
MATMUL_VMEM_BUDGET = 44 * 1024 * 1024


def _cparams(*sem):
    return pltpu.CompilerParams(dimension_semantics=sem, vmem_limit_bytes=VMEM_LIMIT_BYTES)


def _pick(n, cap, align=LANES):
    if n <= cap:
        return n
    best = None
    for t in range(align, cap + 1, align):
        if n % t == 0:
            best = t
    assert best is not None, (n, cap, align)
    return best


def _sigmoid(x):
    return 1.0 / (1.0 + jnp.exp(-x))


class _ChipMajor:
    def __init__(self, arr, layer, by_rows):
        self.arr, self.layer, self.by_rows = arr, layer, by_rows
        n, _, a, b = arr.shape
        self.n_chips, self.per_chip = n, (a if by_rows else b)
        self.shape = (n * a, b) if by_rows else (a, n * b)


SPAN_CHIPS_BELOW = 512


def _matmul(a, b, *, ta=False, tb=False, out_dtype=F32, out_chips=None, out_rows=None, out_layer=None, name):
    view = b if isinstance(b, _ChipMajor) else None
    if ta:
        K, M = a.shape
    else:
        M, K = a.shape
    if tb:
        N, Kb = b.shape
    else:
        Kb, N = b.shape
    assert K == Kb, (a.shape, b.shape, ta, tb)
    by_rows = view is not None and view.by_rows
    by_cols = view is not None and not view.by_rows
    b_itemsize = (view.arr if view is not None else b).dtype.itemsize
    span_b = by_cols and view.per_chip < SPAN_CHIPS_BELOW
    span_o = bool(out_chips) and N // out_chips < SPAN_CHIPS_BELOW
    if by_rows and tb:
        tn = N
    elif by_cols and not tb:
        tn = N if span_b else _pick(view.per_chip, 1408)
    elif out_chips:
        tn = N if span_o else _pick(N // out_chips, 1408)
    elif out_rows:
        tn = _pick(N, 512)
    else:
        tn = _pick(N, 1408)
    tm = M if out_rows else _pick(M, 1024 if tn <= 1024 else 512)
    if by_rows and not tb:
        tk = K
    elif by_cols and tb:
        tk = K if span_b else (view.per_chip if view.per_chip <= 2048 else _pick(view.per_chip, 1024))
    else:
        ab, bb, ob = a.dtype.itemsize, b_itemsize, jnp.dtype(out_dtype).itemsize
        for cap in (K, 2048, 1024, 512, 128):
            tk = _pick(K, cap)
            need = (2 * tk * (tm * ab + tn * bb) + (2 * ob + 4 + (4 if tk < K else 0)) * tm * tn)
            if need <= MATMUL_VMEM_BUDGET:
                break
    nk = K // tk
    dims = (((0 if ta else 1,), (1 if tb else 0,)), ((), ()))

    aliased = out_layer is not None and out_layer[0] is not None

    def body(a_ref, b_ref, *rest):
        o_ref, scratch = (rest[1], rest[2:]) if aliased else (rest[0], rest[1:])
        bt = b_ref[...]
        if by_rows:
            bt = bt.reshape(bt.shape[0] * bt.shape[1], bt.shape[2])
        elif span_b:
            bt = jnp.concatenate([bt[q] for q in range(view.n_chips)], axis=1)
        p = lax.dot_general(a_ref[...].astype(BF16), bt.astype(BF16), dims, preferred_element_type=F32)

        def store(val):
            if span_o:
                w = N // out_chips
                for q in range(out_chips):
                    o_ref[q] = val[:, q * w:(q + 1) * w].astype(out_dtype)
            elif out_rows:
                h = M // out_rows
                for q in range(out_rows):
                    o_ref[q] = val[q * h:(q + 1) * h, :].astype(out_dtype)
            else:
                o_ref[...] = val.astype(out_dtype)

        if nk == 1:
            store(p)
        else:
            acc_ref, = scratch
            k = pl.program_id(2)

            @pl.when(k == 0)
            def _():
                acc_ref[...] = p

            @pl.when(k > 0)
            def _():
                acc_ref[...] += p

            @pl.when(k == nk - 1)
            def _():
                store(acc_ref[...])

    a_spec = pl.BlockSpec((tk, tm), lambda i, j, k: (k, i)) if ta else pl.BlockSpec((tm, tk), lambda i, j, k: (i, k))
    b_tile = (tn, tk) if tb else (tk, tn)
    b_rc = (lambda i, j, k: (j, k)) if tb else (lambda i, j, k: (k, j))
    if view is None:
        b_arr = b
        b_spec = pl.BlockSpec(b_tile, b_rc)
    elif by_rows:
        b_arr = view.arr
        assert b_tile[0] == view.shape[0]
        b_spec = pl.BlockSpec((view.n_chips, None, view.arr.shape[2], b_tile[1]),
                              lambda i, j, k: (0, view.layer, 0, b_rc(i, j, k)[1]))
    elif span_b:
        b_arr = view.arr
        assert b_tile[1] == view.shape[1]
        b_spec = pl.BlockSpec((view.n_chips, None, b_tile[0], view.per_chip),
                              lambda i, j, k: (0, view.layer, b_rc(i, j, k)[0], 0))
    else:
        b_arr = view.arr
        per = view.per_chip // b_tile[1]
        b_spec = pl.BlockSpec((None, None) + b_tile,
                              lambda i, j, k: (b_rc(i, j, k)[1] // per, view.layer, b_rc(i, j, k)[0],
                                               b_rc(i, j, k)[1] % per))
    if span_o:
        o_full, o_blk = (out_chips, M, N // out_chips), (out_chips, tm, N // out_chips)
        o_idx = lambda i, j, k: (0, i, 0)
    elif out_chips:
        per_o = (N // out_chips) // tn
        o_full, o_blk = (out_chips, M, N // out_chips), (None, tm, tn)
        o_idx = lambda i, j, k: (j // per_o, i, j % per_o)
    elif out_rows:
        o_full, o_blk = (out_rows, M // out_rows, N), (out_rows, M // out_rows, tn)
        o_idx = lambda i, j, k: (0, 0, j)
    else:
        o_full, o_blk = (M, N), (tm, tn)
        o_idx = lambda i, j, k: (i, j)
    if out_layer is not None:
        _, layer, n_layers = out_layer
        o_full, o_blk = o_full[:1] + (n_layers,) + o_full[1:], o_blk[:1] + (None,) + o_blk[1:]
        o_idx = functools.partial(lambda f, i, j, k: (f(i, j, k)[0], layer) + f(i, j, k)[1:], o_idx)
    out_shape = jax.ShapeDtypeStruct(o_full, out_dtype)
    out_spec = pl.BlockSpec(o_blk, o_idx)
    grid = (M // tm, N // tn, nk)
    specs = [a_spec, b_spec, out_spec]
    a_bytes, b_bytes = M * K * a.dtype.itemsize, K * N * b_itemsize
    if nk == 1 and b_bytes + a_bytes * grid[1] < a_bytes + b_bytes * grid[0]:
        grid = (grid[1], grid[0], nk)
        specs = [pl.BlockSpec(s.block_shape, functools.partial(lambda f, j, i, k: f(i, j, k), s.index_map))
                 for s in specs]
    extra = (out_layer[0],) if aliased else ()
    return pl.pallas_call(
        body, name=name, out_shape=out_shape, grid=grid,
        in_specs=specs[:2] + [pl.BlockSpec(memory_space=pl.ANY)] * len(extra), out_specs=specs[2],
        input_output_aliases={2: 0} if aliased else {},
        scratch_shapes=[pltpu.VMEM((tm, tn), F32)] if nk > 1 else [],
        compiler_params=_cparams("parallel", "parallel", "arbitrary"),
    )(a, b_arr, *extra)


def _row_block(S, D, cap_bytes=2 * 1024 * 1024):
    ts = max(8, min(S, cap_bytes // (4 * D)))
    return _pick(S, ts, 8)


def _rms_fwd(x, g, *, out_dtype, name):
    S, D = x.shape
    ts = _row_block(S, D)

    def body(x_ref, g_ref, o_ref):
        xf = x_ref[...]
        r = lax.rsqrt(jnp.mean(xf * xf, axis=-1, keepdims=True) + RMS_EPS)
        o_ref[...] = (xf * r * g_ref[...]).astype(out_dtype)

    return pl.pallas_call(
        body, name=name, out_shape=jax.ShapeDtypeStruct((S, D), out_dtype), grid=(S // ts,),
        in_specs=[pl.BlockSpec((ts, D), lambda i: (i, 0)), pl.BlockSpec((1, D), lambda i: (0, 0))],
        out_specs=pl.BlockSpec((ts, D), lambda i: (i, 0)),
        compiler_params=_cparams("parallel"),
    )(x, g)


def _add_rms(x, r, g, *, name):
    S, D = x.shape
    ts = _row_block(S, D)

    def body(x_ref, r_ref, g_ref, o_ref):
        rf = r_ref[...]
        s = lax.rsqrt(jnp.mean(rf * rf, axis=-1, keepdims=True) + RMS_EPS)
        o_ref[...] = x_ref[...] + rf * s * g_ref[...]

    return pl.pallas_call(
        body, name=name, out_shape=jax.ShapeDtypeStruct((S, D), F32), grid=(S // ts,),
        in_specs=[pl.BlockSpec((ts, D), lambda i: (i, 0)), pl.BlockSpec((ts, D), lambda i: (i, 0)),
                  pl.BlockSpec((1, D), lambda i: (0, 0))],
        out_specs=pl.BlockSpec((ts, D), lambda i: (i, 0)),
        compiler_params=_cparams("parallel"),
    )(x, r, g)


def _rms_bwd(x, g, dy, res=None, *, out_dtype, name):
    S, D = x.shape
    ts = _row_block(S, D)
    has_res = res is not None

    def body(*refs):
        if has_res:
            x_ref, g_ref, dy_ref, res_ref, dx_ref, dg_ref = refs
        else:
            x_ref, g_ref, dy_ref, dx_ref, dg_ref = refs
        i = pl.program_id(0)
        xf = x_ref[...]
        dyf = dy_ref[...].astype(F32)
        r = lax.rsqrt(jnp.mean(xf * xf, axis=-1, keepdims=True) + RMS_EPS)
        n = xf * r
        dn = dyf * g_ref[...]
        dx = r * (dn - n * jnp.mean(dn * n, axis=-1, keepdims=True))
        if has_res:
            dx = dx + res_ref[...]
        dx_ref[...] = dx.astype(out_dtype)
        part = jnp.sum((dyf * n).reshape(ts // 8, 8, D), axis=0)

        @pl.when(i == 0)
        def _():
            dg_ref[...] = part

        @pl.when(i > 0)
        def _():
            dg_ref[...] += part

    row = pl.BlockSpec((ts, D), lambda i: (i, 0))
    in_specs = [row, pl.BlockSpec((1, D), lambda i: (0, 0)), row] + ([row] if has_res else [])
    args = (x, g, dy) + ((res,) if has_res else ())
    dx, dg = pl.pallas_call(
        body, name=name,
        out_shape=(jax.ShapeDtypeStruct((S, D), out_dtype), jax.ShapeDtypeStruct((8, D), F32)),
        grid=(S // ts,), in_specs=in_specs,
        out_specs=(row, pl.BlockSpec((8, D), lambda i: (0, 0))),
        compiler_params=_cparams("arbitrary"),
    )(*args)
    return dx, dg


def _loss_head(y, t, *, name):
    S, D = y.shape
    ts = _row_block(S, D)

    def body(y_ref, t_ref, dy_ref, sq_ref):
        i = pl.program_id(0)
        e = y_ref[...] - t_ref[...]
        dy_ref[...] = e / D
        part = jnp.sum((e * e).reshape(ts // 8, 8, D), axis=0)

        @pl.when(i == 0)
        def _():
            sq_ref[...] = part

        @pl.when(i > 0)
        def _():
            sq_ref[...] += part

    row = pl.BlockSpec((ts, D), lambda i: (i, 0))
    return pl.pallas_call(
        body, name=name,
        out_shape=(jax.ShapeDtypeStruct((S, D), F32), jax.ShapeDtypeStruct((8, D), F32)),
        grid=(S // ts,), in_specs=[row, row],
        out_specs=(row, pl.BlockSpec((8, D), lambda i: (0, 0))),
        compiler_params=_cparams("arbitrary"),
    )(y, t)


def _window_counts(i, T, w):
    t = i * T + lax.broadcasted_iota(jnp.int32, (T, 1), 0)
    return jnp.minimum(t + 1, w).astype(F32)


def _pool_fwd(zu, pool_w, pool_scale, *, name):
    S, W = zu.shape
    T = _pick(S, 1024, 8)

    def body(u_ref, pw_ref, sc_ref, pooled_ref, mixed_ref, halo_ref):
        i = pl.program_id(0)

        @pl.when(i == 0)
        def _():
            halo_ref[...] = jnp.zeros_like(halo_ref)

        u = u_ref[...]
        ext = jnp.concatenate([halo_ref[...], u], axis=0)
        halo_ref[...] = u[T - HALO:, :]
        for g, w in enumerate(POOL_WINDOWS):
            cols = slice(g * POOL_GROUP, (g + 1) * POOL_GROUP)
            s = ext[:, cols]
            sh = 1
            while sh < w:
                s = s + pltpu.roll(s, sh, 0)
                sh *= 2
            pooled = s[HALO:, :] / _window_counts(i, T, w) - u[:, cols]
            pooled_bf = pooled.astype(BF16)
            pm = jnp.dot(pooled_bf, pw_ref[g].astype(BF16), preferred_element_type=F32)
            pooled_ref[:, cols] = pooled_bf
            mixed_ref[:, cols] = (pm * sc_ref[:, cols]).astype(BF16)

    row = pl.BlockSpec((T, W), lambda i: (i, 0))
    return pl.pallas_call(
        body, name=name,
        out_shape=(jax.ShapeDtypeStruct((S, W), BF16), jax.ShapeDtypeStruct((S, W), BF16)),
        grid=(S // T,),
        in_specs=[row, pl.BlockSpec(pool_w.shape, lambda i: (0, 0, 0)), pl.BlockSpec((1, W), lambda i: (0, 0))],
        out_specs=(row, row),
        scratch_shapes=[pltpu.VMEM((HALO, W), F32)],
        compiler_params=_cparams("arbitrary"),
    )(zu, pool_w, pool_scale)


def _pool_bwd(pooled, dmixed, pool_w, pool_scale, *, name):
    S, W = pooled.shape
    T = _pick(S, 1024, 8)
    nb = S // T

    def body(p_ref, dm_ref, pw_ref, sc_ref, dzu_ref, dpw_ref, dsc_ref, halo_ref):
        i = pl.program_id(0)
        blk = nb - 1 - i

        @pl.when(i == 0)
        def _():
            halo_ref[...] = jnp.zeros_like(halo_ref)
            dpw_ref[...] = jnp.zeros_like(dpw_ref)
            dsc_ref[...] = jnp.zeros_like(dsc_ref)

        for g, w in enumerate(POOL_WINDOWS):
            cols = slice(g * POOL_GROUP, (g + 1) * POOL_GROUP)
            p = p_ref[:, cols]
            dm = dm_ref[:, cols]
            pw = pw_ref[g].astype(BF16)
            pm = jnp.dot(p, pw, preferred_element_type=F32)
            dsc_ref[:, cols] += jnp.sum((dm * pm).reshape(T // 8, 8, POOL_GROUP), axis=0)
            dpm = (dm * sc_ref[:, cols]).astype(BF16)
            dpw_ref[g] += lax.dot_general(p, dpm, (((0,), (0,)), ((), ())), preferred_element_type=F32)
            dpooled = lax.dot_general(dpm, pw, (((1,), (1,)), ((), ())), preferred_element_type=F32)
            e = dpooled / _window_counts(blk, T, w)
            ext = jnp.concatenate([e, halo_ref[:, cols]], axis=0)
            halo_ref[:, cols] = e[:HALO, :]
            s = ext
            sh = 1
            while sh < w:
                s = s + pltpu.roll(s, T + HALO - sh, 0)
                sh *= 2
            dzu_ref[:, cols] = (s[:T, :] - dpooled).astype(BF16)

    row = pl.BlockSpec((T, W), lambda i: (nb - 1 - i, 0))
    return pl.pallas_call(
        body, name=name,
        out_shape=(jax.ShapeDtypeStruct((S, W), BF16), jax.ShapeDtypeStruct(pool_w.shape, F32),
                   jax.ShapeDtypeStruct((8, W), F32)),
        grid=(nb,),
        in_specs=[row, row, pl.BlockSpec(pool_w.shape, lambda i: (0, 0, 0)), pl.BlockSpec((1, W), lambda i: (0, 0))],
        out_specs=(row, pl.BlockSpec(pool_w.shape, lambda i: (0, 0, 0)), pl.BlockSpec((8, W), lambda i: (0, 0))),
        scratch_shapes=[pltpu.VMEM((HALO, W), F32)],
        compiler_params=_cparams("arbitrary"),
    )(pooled, dmixed, pool_w, pool_scale)


def _fgate_fwd(zf, bf, *, name):
    S, W = zf.shape
    T = _pick(S, 512, 8)

    def body(z_ref, b_ref, c_ref, aq_ref, ak_ref, carry_ref):
        i = pl.program_id(0)

        @pl.when(i == 0)
        def _():
            carry_ref[...] = jnp.zeros_like(carry_ref)

        a = z_ref[...] + b_ref[...]
        s = jnp.minimum(a, 0.0) - jnp.log(1.0 + jnp.exp(-jnp.abs(a)))
        row = lax.broadcasted_iota(jnp.int32, (T, W), 0)
        sh = 1
        while sh < T:
            s = s + jnp.where(row >= sh, pltpu.roll(s, sh, 0), 0.0)
            sh *= 2
        c = s + carry_ref[0:1, :]
        c_ref[...] = c
        carry_ref[...] = jnp.broadcast_to(c[T - 1:T, :], carry_ref.shape)
        lane = lax.broadcasted_iota(jnp.int32, (T, W), 1)
        for h in range(FOX_HEADS):
            ch = c[:, h:h + 1]
            hi = ch.astype(BF16).astype(F32)
            r1 = ch - hi
            lo = r1.astype(BF16).astype(F32)
            lo2 = (r1 - lo).astype(BF16).astype(F32)
            aq = jnp.where(lane == 0, hi, jnp.where(lane == 1, lo, jnp.where(lane == 2, lo2,
                                                                              jnp.where(lane < 6, 1.0, 0.0))))
            ak = jnp.where(lane < 3, 1.0, jnp.where(lane == 3, -hi, jnp.where(lane == 4, -lo,
                                                                               jnp.where(lane == 5, -lo2, 0.0))))
            aq_ref[h] = aq.astype(BF16)
            ak_ref[h] = ak.astype(BF16)

    aug = jax.ShapeDtypeStruct((FOX_HEADS, S, W), BF16)
    aug_spec = pl.BlockSpec((FOX_HEADS, T, W), lambda i: (0, i, 0))
    return pl.pallas_call(
        body, name=name, out_shape=(jax.ShapeDtypeStruct((S, W), F32), aug, aug), grid=(S // T,),
        in_specs=[pl.BlockSpec((T, W), lambda i: (i, 0)), pl.BlockSpec((1, W), lambda i: (0, 0))],
        out_specs=(pl.BlockSpec((T, W), lambda i: (i, 0)), aug_spec, aug_spec),
        scratch_shapes=[pltpu.VMEM((8, W), F32)],
        compiler_params=_cparams("arbitrary"),
    )(zf, bf)


def _fgate_bwd(zf, bf, dc, *, name):
    S, W = zf.shape
    T = _pick(S, 512, 8)
    nb = S // T

    def body(z_ref, b_ref, dc_ref, dz_ref, db_ref, carry_ref):
        i = pl.program_id(0)

        @pl.when(i == 0)
        def _():
            carry_ref[...] = jnp.zeros_like(carry_ref)
            db_ref[...] = jnp.zeros_like(db_ref)

        s = dc_ref[...]
        row = lax.broadcasted_iota(jnp.int32, (T, W), 0)
        sh = 1
        while sh < T:
            s = s + jnp.where(row < T - sh, pltpu.roll(s, T - sh, 0), 0.0)
            sh *= 2
        dlf = s + carry_ref[0:1, :]
        carry_ref[...] = jnp.broadcast_to(dlf[0:1, :], carry_ref.shape)
        dz = dlf * (1.0 - _sigmoid(z_ref[...] + b_ref[...]))
        dz_ref[...] = dz.astype(BF16)
        db_ref[...] += jnp.sum(dz.reshape(T // 8, 8, W), axis=0)

    row_spec = pl.BlockSpec((T, W), lambda i: (nb - 1 - i, 0))
    return pl.pallas_call(
        body, name=name,
        out_shape=(jax.ShapeDtypeStruct((S, W), BF16), jax.ShapeDtypeStruct((8, W), F32)),
        grid=(nb,),
        in_specs=[row_spec, pl.BlockSpec((1, W), lambda i: (0, 0)), row_spec],
        out_specs=(row_spec, pl.BlockSpec((8, W), lambda i: (0, 0))),
        scratch_shapes=[pltpu.VMEM((8, W), F32)],
        compiler_params=_cparams("arbitrary"),
    )(zf, bf, dc)


def _merge_fwd(zg, yp, yf, *, name):
    S, D = yp.shape
    ts = _row_block(S, D, 1024 * 1024)

    def body(zg_ref, yp_ref, yf_ref, o_ref):
        o_ref[...] = (_sigmoid(zg_ref[:, :D]) * yp_ref[...] + _sigmoid(zg_ref[:, D:]) * yf_ref[...]).astype(BF16)

    row = pl.BlockSpec((ts, D), lambda i: (i, 0))
    return pl.pallas_call(
        body, name=name, out_shape=jax.ShapeDtypeStruct((S, D), BF16), grid=(S // ts,),
        in_specs=[pl.BlockSpec((ts, 2 * D), lambda i: (i, 0)), row, row], out_specs=row,
        compiler_params=_cparams("parallel"),
    )(zg, yp, yf)


def _merge_bwd(zg, yp, yf, dmerged, *, name):
    S, D = yp.shape
    ts = _row_block(S, D, 1024 * 1024)

    def body(zg_ref, yp_ref, yf_ref, dm_ref, dyp_ref, dyf_ref, dzg_ref):
        dm = dm_ref[...]
        sp = _sigmoid(zg_ref[:, :D])
        sf = _sigmoid(zg_ref[:, D:])
        dyp_ref[...] = (dm * sp).astype(BF16)
        dyf_ref[...] = (dm * sf).astype(BF16)
        dzg_ref[:, :D] = (dm * yp_ref[...] * (sp * (1.0 - sp))).astype(BF16)
        dzg_ref[:, D:] = (dm * yf_ref[...] * (sf * (1.0 - sf))).astype(BF16)

    row = pl.BlockSpec((ts, D), lambda i: (i, 0))
    wide = pl.BlockSpec((ts, 2 * D), lambda i: (i, 0))
    return pl.pallas_call(
        body, name=name,
        out_shape=(jax.ShapeDtypeStruct((S, D), BF16), jax.ShapeDtypeStruct((S, D), BF16),
                   jax.ShapeDtypeStruct((S, 2 * D), BF16)),
        grid=(S // ts,), in_specs=[wide, row, row, row], out_specs=(row, row, wide),
        compiler_params=_cparams("parallel"),
    )(zg, yp, yf, dmerged)


NEG_BIG = -1e30


FOX_BLOCK = 1024
PAIR = LANES // FOX_DH
N_PAIRS = FOX_HEADS // PAIR


def _fox_fwd(zqkv, augq, augk, *, name):
    S = zqkv.shape[0]
    bq = _pick(S, FOX_BLOCK, 128)
    nq = S // bq
    scale = 1.0 / math.sqrt(FOX_DH)

    def body(q_ref, k_ref, v_ref, aq_ref, ak_ref, o_ref, lse_ref):
        i = pl.program_id(1)
        lane = lax.broadcasted_iota(jnp.int32, (1, LANES), 1)
        first = lane < FOX_DH
        q2 = q_ref[...] * scale
        zero = jnp.zeros_like(q2)
        qh = (jnp.concatenate([jnp.where(first, q2, zero), aq_ref[0]], axis=1),
              jnp.concatenate([jnp.where(first, zero, q2), aq_ref[1]], axis=1))

        def step(j, carry, masked):
            start = pl.multiple_of(j * bq, bq)
            kb = k_ref[pl.ds(start, bq), :]
            vb = v_ref[pl.ds(start, bq), :]
            one = jnp.ones_like(vb)
            vh = (jnp.where(first, vb, one), jnp.where(first, one, vb))
            out = []
            for h in range(PAIR):
                m, acc = carry[h]
                kh = jnp.concatenate([kb, ak_ref[h, pl.ds(start, bq), :]], axis=1)
                s = lax.dot_general(qh[h], kh, (((1,), (1,)), ((), ())), preferred_element_type=F32)
                if masked:
                    r = lax.broadcasted_iota(jnp.int32, (bq, bq), 0)
                    c = lax.broadcasted_iota(jnp.int32, (bq, bq), 1)
                    s = jnp.where(c <= r, s, NEG_BIG)
                m_new = jnp.maximum(m, jnp.max(s, axis=-1, keepdims=True))
                alpha = jnp.exp(m - m_new)
                p = jnp.exp(s - m_new).astype(BF16)
                acc = alpha * acc + jnp.dot(p, vh[h], preferred_element_type=F32)
                out.append((m_new, acc))
            return tuple(out)

        init = tuple((jnp.full((bq, 1), NEG_BIG, F32), jnp.zeros((bq, LANES), F32)) for _ in range(PAIR))
        carry = lax.fori_loop(0, i, lambda j, c: step(j, c, False), init)
        (ma, acca), (mb, accb) = step(i, carry, True)
        num = jnp.where(first, acca, accb)
        den = jnp.where(first, pltpu.roll(acca, FOX_DH, 1), pltpu.roll(accb, FOX_DH, 1))
        o_ref[...] = num / den
        lse_ref[...] = jnp.where(first, mb, ma) + jnp.log(jnp.where(first, accb, acca))

    npair = N_PAIRS
    return pl.pallas_call(
        body, name=name,
        out_shape=(jax.ShapeDtypeStruct((S, FOX_W), F32), jax.ShapeDtypeStruct((S, FOX_W), F32)),
        grid=(npair, nq),
        in_specs=[pl.BlockSpec((bq, LANES), lambda hp, i: (i, hp)),
                  pl.BlockSpec((S, LANES), lambda hp, i: (0, npair + hp)),
                  pl.BlockSpec((S, LANES), lambda hp, i: (0, 2 * npair + hp)),
                  pl.BlockSpec((PAIR, bq, LANES), lambda hp, i: (hp, i, 0)),
                  pl.BlockSpec((PAIR, S, LANES), lambda hp, i: (hp, 0, 0))],
        out_specs=(pl.BlockSpec((bq, LANES), lambda hp, i: (i, hp)),
                   pl.BlockSpec((bq, LANES), lambda hp, i: (i, hp))),
        compiler_params=_cparams("parallel", "arbitrary"),
    )(zqkv, zqkv, zqkv, augq, augk)


def _head_rowsum(a, b, *, name):
    S, W = a.shape
    ts = _pick(S, 1024, 8)

    def body(a_ref, b_ref, o_ref):
        prod = a_ref[...].astype(F32) * b_ref[...].astype(F32)
        hi = prod.astype(BF16)
        lo = (prod - hi.astype(F32)).astype(BF16)
        r = lax.broadcasted_iota(jnp.int32, (W, LANES), 0)
        c = lax.broadcasted_iota(jnp.int32, (W, LANES), 1)
        sel = jnp.where(r // FOX_DH == c, 1.0, 0.0).astype(BF16)
        o_ref[...] = (jnp.dot(hi, sel, preferred_element_type=F32) + jnp.dot(lo, sel, preferred_element_type=F32))

    return pl.pallas_call(
        body, name=name, out_shape=jax.ShapeDtypeStruct((S, LANES), F32), grid=(S // ts,),
        in_specs=[pl.BlockSpec((ts, W), lambda i: (i, 0)), pl.BlockSpec((ts, W), lambda i: (i, 0))],
        out_specs=pl.BlockSpec((ts, LANES), lambda i: (i, 0)),
        compiler_params=_cparams("parallel"),
    )(a, b)


def _fox_bwd(zqkv, do, augq, augk, lse_row, delta_row, *, name):
    S = zqkv.shape[0]
    bk = _pick(S, FOX_BLOCK, 128)
    nk = S // bk
    scale = 1.0 / math.sqrt(FOX_DH)
    npair = N_PAIRS

    def body(q_ref, k_ref, v_ref, do_ref, ak_ref, aq_ref, lse_ref, dl_ref, dq_ref, dk_ref, dv_ref, dck_ref, dcq_ref):
        j = pl.program_id(1)

        @pl.when(j == 0)
        def _():
            dq_ref[...] = jnp.zeros_like(dq_ref)
            dcq_ref[...] = jnp.zeros_like(dcq_ref)

        lane = lax.broadcasted_iota(jnp.int32, (1, LANES), 1)
        first = lane < FOX_DH
        kb = k_ref[...]
        vb = v_ref[...]
        kh = (jnp.concatenate([kb, ak_ref[0]], axis=1), jnp.concatenate([kb, ak_ref[1]], axis=1))

        def step(i, carry, masked):
            start = pl.multiple_of(i * bk, bk)
            qs = q_ref[pl.ds(start, bk), :] * scale
            dob = do_ref[pl.ds(start, bk), :]
            zero = jnp.zeros_like(qs)
            qh = (jnp.where(first, qs, zero), jnp.where(first, zero, qs))
            doh = (jnp.where(first, dob, zero), jnp.where(first, zero, dob))
            out = []
            dqc = []
            for h in range(PAIR):
                dk, dv, dc = carry[h]
                qaug = jnp.concatenate([qh[h], aq_ref[h, pl.ds(start, bk), :]], axis=1)
                lse = lse_ref[h, :, pl.ds(start, bk)]
                dl = dl_ref[h, :, pl.ds(start, bk)]
                st = lax.dot_general(kh[h], qaug, (((1,), (1,)), ((), ())), preferred_element_type=F32)
                pt = jnp.exp(st - lse)
                if masked:
                    r = lax.broadcasted_iota(jnp.int32, (bk, bk), 0)
                    c = lax.broadcasted_iota(jnp.int32, (bk, bk), 1)
                    pt = jnp.where(c >= r, pt, 0.0)
                dpt = lax.dot_general(vb, doh[h], (((1,), (1,)), ((), ())), preferred_element_type=F32)
                dst = pt * (dpt - dl)
                pt_bf = pt.astype(BF16)
                dst_bf = dst.astype(BF16)
                dv = dv + jnp.dot(pt_bf, dob, preferred_element_type=F32)
                dk = dk + jnp.dot(dst_bf, qs, preferred_element_type=F32)
                dc = dc - jnp.sum(dst, axis=-1, keepdims=True)
                dcq_ref[h, :, pl.ds(start, bk)] += jnp.sum(dst, axis=0, keepdims=True)
                dqc.append(lax.dot_general(dst_bf, kb, (((0,), (0,)), ((), ())), preferred_element_type=F32))
                out.append((dk, dv, dc))
            dq_ref[pl.ds(start, bk), :] += jnp.where(first, dqc[0], dqc[1])
            return tuple(out)

        init = tuple((jnp.zeros((bk, LANES), F32), jnp.zeros((bk, LANES), F32), jnp.zeros((bk, 1), F32))
                     for _ in range(PAIR))
        carry = step(j, init, True)
        (dka, dva, dca), (dkb, dvb, dcb) = lax.fori_loop(j + 1, nk, lambda i, c: step(i, c, False), carry)
        dk_ref[...] = jnp.where(first, dka, dkb).astype(BF16)
        dv_ref[...] = jnp.where(first, dva, dvb).astype(BF16)
        dck_ref[0] = dca
        dck_ref[1] = dcb

        @pl.when(j == nk - 1)
        def _():
            dq_ref[...] = dq_ref[...] * scale

    rowfull = pl.BlockSpec((PAIR, 1, S), lambda hp, j: (hp, 0, 0))
    return pl.pallas_call(
        body, name=name,
        out_shape=(jax.ShapeDtypeStruct((S, FOX_W), F32), jax.ShapeDtypeStruct((S, FOX_W), BF16),
                   jax.ShapeDtypeStruct((S, FOX_W), BF16), jax.ShapeDtypeStruct((FOX_HEADS, S, 1), F32),
                   jax.ShapeDtypeStruct((FOX_HEADS, 1, S), F32)),
        grid=(npair, nk),
        in_specs=[pl.BlockSpec((S, LANES), lambda hp, j: (0, hp)),
                  pl.BlockSpec((bk, LANES), lambda hp, j: (j, npair + hp)),
                  pl.BlockSpec((bk, LANES), lambda hp, j: (j, 2 * npair + hp)),
                  pl.BlockSpec((S, LANES), lambda hp, j: (0, hp)),
                  pl.BlockSpec((PAIR, bk, LANES), lambda hp, j: (hp, j, 0)),
                  pl.BlockSpec((PAIR, S, LANES), lambda hp, j: (hp, 0, 0)), rowfull, rowfull],
        out_specs=(pl.BlockSpec((S, LANES), lambda hp, j: (0, hp)),
                   pl.BlockSpec((bk, LANES), lambda hp, j: (j, hp)),
                   pl.BlockSpec((bk, LANES), lambda hp, j: (j, hp)),
                   pl.BlockSpec((PAIR, bk, 1), lambda hp, j: (hp, j, 0)), rowfull),
        compiler_params=_cparams("parallel", "arbitrary"),
    )(zqkv, zqkv, zqkv, do, augk, augq, lse_row, delta_row)


def _xattn_fwd(q, kv, *, name):
    S, W = q.shape
    M = kv.shape[0]
    tq = _pick(S, 512, 8)
    scale = 1.0 / math.sqrt(X_DH)

    def body(q_ref, kv_ref, o_ref):
        for h in range(X_HEADS):
            cols = slice(h * X_DH, (h + 1) * X_DH)
            vcols = slice(W + h * X_DH, W + (h + 1) * X_DH)
            s = lax.dot_general(q_ref[:, cols], kv_ref[:, cols], (((1,), (1,)), ((), ())),
                                preferred_element_type=F32) * scale
            e = jnp.exp(s - jnp.max(s, axis=-1, keepdims=True))
            p = e / jnp.sum(e, axis=-1, keepdims=True)
            o_ref[:, cols] = jnp.dot(p.astype(BF16), kv_ref[:, vcols], preferred_element_type=F32).astype(BF16)

    return pl.pallas_call(
        body, name=name, out_shape=jax.ShapeDtypeStruct((S, W), BF16), grid=(S // tq,),
        in_specs=[pl.BlockSpec((tq, W), lambda i: (i, 0)), pl.BlockSpec((M, 2 * W), lambda i: (0, 0))],
        out_specs=pl.BlockSpec((tq, W), lambda i: (i, 0)),
        compiler_params=_cparams("parallel"),
    )(q, kv)


def _xattn_bwd(q, kv, do, *, name):
    S, W = q.shape
    M = kv.shape[0]
    tq = _pick(S, 512, 8)
    scale = 1.0 / math.sqrt(X_DH)

    def body(q_ref, kv_ref, do_ref, dq_ref, dkv_ref):
        i = pl.program_id(0)

        @pl.when(i == 0)
        def _():
            dkv_ref[...] = jnp.zeros_like(dkv_ref)

        for h in range(X_HEADS):
            cols = slice(h * X_DH, (h + 1) * X_DH)
            vcols = slice(W + h * X_DH, W + (h + 1) * X_DH)
            qh = q_ref[:, cols]
            kh = kv_ref[:, cols]
            vh = kv_ref[:, vcols]
            doh = do_ref[:, cols]
            s = lax.dot_general(qh, kh, (((1,), (1,)), ((), ())), preferred_element_type=F32) * scale
            e = jnp.exp(s - jnp.max(s, axis=-1, keepdims=True))
            p = e / jnp.sum(e, axis=-1, keepdims=True)
            dp = lax.dot_general(doh, vh, (((1,), (1,)), ((), ())), preferred_element_type=F32)
            ds = (p * (dp - jnp.sum(p * dp, axis=-1, keepdims=True)) * scale).astype(BF16)
            dq_ref[:, cols] = jnp.dot(ds, kh, preferred_element_type=F32).astype(BF16)
            dkv_ref[:, cols] += lax.dot_general(ds, qh, (((0,), (0,)), ((), ())), preferred_element_type=F32)
            dkv_ref[:, vcols] += lax.dot_general(p.astype(BF16), doh, (((0,), (0,)), ((), ())),
                                                 preferred_element_type=F32)

    return pl.pallas_call(
        body, name=name,
        out_shape=(jax.ShapeDtypeStruct((S, W), BF16), jax.ShapeDtypeStruct((M, 2 * W), F32)),
        grid=(S // tq,),
        in_specs=[pl.BlockSpec((tq, W), lambda i: (i, 0)), pl.BlockSpec((M, 2 * W), lambda i: (0, 0)),
                  pl.BlockSpec((tq, W), lambda i: (i, 0))],
        out_specs=(pl.BlockSpec((tq, W), lambda i: (i, 0)), pl.BlockSpec((M, 2 * W), lambda i: (0, 0))),
        compiler_params=_cparams("arbitrary"),
    )(q, kv, do)


GELU_C = math.sqrt(2.0 / math.pi)
GELU_A = 0.044715
CONV_HALO = 16


def _gelu_parts(x):
    u = GELU_C * (x + GELU_A * x * x * x)
    t = jnp.tanh(u)
    g = 0.5 * x * (1.0 + t)
    dg = 0.5 * (1.0 + t) + 0.5 * x * (1.0 - t * t) * (GELU_C * (1.0 + 3.0 * GELU_A * x * x))
    return g, dg


def _conv3(ext, w_ref, b_ref):
    return (w_ref[2:3, :] * ext + w_ref[1:2, :] * pltpu.roll(ext, 1, 0)
            + w_ref[0:1, :] * pltpu.roll(ext, 2, 0) + b_ref[...])


def _convglu_fwd(z, conv_w, conv_b, *, name):
    S, F2 = z.shape
    F = F2 // 2
    tc = _pick(F, 1408)
    ncol = F // tc
    T = _pick(S, 512, 8)
    hb = T // CONV_HALO

    def body(zg_ref, zu_ref, zgp_ref, zup_ref, wg_ref, wu_ref, bg_ref, bu_ref, act_ref):
        i = pl.program_id(1)
        first = (i > 0).astype(F32)

        def conv(z_ref, zp_ref, w_ref, b_ref):
            ext = jnp.concatenate([zp_ref[...].astype(F32) * first, z_ref[...].astype(F32)], axis=0)
            return _conv3(ext, w_ref, b_ref)[CONV_HALO:, :]

        gc = conv(zg_ref, zgp_ref, wg_ref, bg_ref)
        uc = conv(zu_ref, zup_ref, wu_ref, bu_ref)
        act_ref[...] = (_gelu_parts(gc)[0] * uc).astype(BF16)

    cur = lambda off: pl.BlockSpec((T, tc), lambda j, i: (i, j + off))
    prev = lambda off: pl.BlockSpec((CONV_HALO, tc), lambda j, i: (jnp.maximum(i * hb - 1, 0), j + off))
    vec = lambda rows, off: pl.BlockSpec((rows, tc), lambda j, i: (0, j + off))
    return pl.pallas_call(
        body, name=name, out_shape=jax.ShapeDtypeStruct((S, F), BF16), grid=(ncol, S // T),
        in_specs=[cur(0), cur(ncol), prev(0), prev(ncol), vec(3, 0), vec(3, ncol), vec(1, 0), vec(1, ncol)],
        out_specs=pl.BlockSpec((T, tc), lambda j, i: (i, j)),
        compiler_params=_cparams("parallel", "parallel"),
    )(z, z, z, z, conv_w, conv_w, conv_b, conv_b)


def _convglu_bwd(z, dact, conv_w, conv_b, *, name):
    S, F2 = z.shape
    F = F2 // 2
    tc = _pick(F, 1408)
    ncol = F // tc
    T = _pick(S, 256, 8)
    nrow = S // T
    hb = T // CONV_HALO
    TE = T + CONV_HALO

    def body(zg_ref, zu_ref, zgp_ref, zup_ref, zgn_ref, zun_ref, da_ref, dan_ref,
             wg_ref, wu_ref, bg_ref, bu_ref, dz_ref, dw_ref):
        i, j = pl.program_id(0), pl.program_id(1)
        first = (i > 0).astype(F32)
        last = (i < nrow - 1).astype(F32)

        @pl.when((i == 0) & (j == 0))
        def _():
            dw_ref[...] = jnp.zeros_like(dw_ref)

        f32 = lambda r: r[...].astype(F32)
        extg = jnp.concatenate([f32(zgp_ref) * first, f32(zg_ref), f32(zgn_ref)], axis=0)
        extu = jnp.concatenate([f32(zup_ref) * first, f32(zu_ref), f32(zun_ref)], axis=0)
        gc = _conv3(extg, wg_ref, bg_ref)[CONV_HALO:, :]
        uc = _conv3(extu, wu_ref, bu_ref)[CONV_HALO:, :]
        da = jnp.concatenate([f32(da_ref), f32(dan_ref) * last], axis=0)
        gl, dgl = _gelu_parts(gc)

        def back(d, ext, w_ref):
            dz = w_ref[2:3, :] * d + w_ref[1:2, :] * pltpu.roll(d, TE - 1, 0) + w_ref[0:1, :] * pltpu.roll(d, TE - 2, 0)
            dc = d[:T, :]
            z0 = ext[CONV_HALO:CONV_HALO + T, :]
            z1 = pltpu.roll(ext, 1, 0)[CONV_HALO:CONV_HALO + T, :]
            z2 = pltpu.roll(ext, 2, 0)[CONV_HALO:CONV_HALO + T, :]
            rows = [jnp.sum(dc * z2, axis=0, keepdims=True), jnp.sum(dc * z1, axis=0, keepdims=True),
                    jnp.sum(dc * z0, axis=0, keepdims=True), jnp.sum(dc, axis=0, keepdims=True)]
            return dz[:T, :].astype(BF16), jnp.concatenate(rows, axis=0)

        dzg, rg = back(da * uc * dgl, extg, wg_ref)
        dzu, ru = back(da * gl, extu, wu_ref)
        for jj in range(ncol):

            @pl.when(j == jj)
            def _():
                cg = slice(jj * tc, (jj + 1) * tc)
                cu = slice(F + jj * tc, F + (jj + 1) * tc)
                dz_ref[:, cg] = dzg
                dz_ref[:, cu] = dzu
                dw_ref[0:4, cg] += rg
                dw_ref[0:4, cu] += ru

    cur = lambda off: pl.BlockSpec((T, tc), lambda i, j: (i, j + off))
    prev = lambda off: pl.BlockSpec((CONV_HALO, tc), lambda i, j: (jnp.maximum(i * hb - 1, 0), j + off))
    nxt = lambda off: pl.BlockSpec((CONV_HALO, tc), lambda i, j: (jnp.minimum((i + 1) * hb, S // CONV_HALO - 1), j + off))
    vec = lambda rows, off: pl.BlockSpec((rows, tc), lambda i, j: (0, j + off))
    return pl.pallas_call(
        body, name=name,
        out_shape=(jax.ShapeDtypeStruct((S, F2), BF16), jax.ShapeDtypeStruct((8, F2), F32)),
        grid=(nrow, ncol),
        in_specs=[cur(0), cur(ncol), prev(0), prev(ncol), nxt(0), nxt(ncol), cur(0), nxt(0),
                  vec(3, 0), vec(3, ncol), vec(1, 0), vec(1, ncol)],
        out_specs=(pl.BlockSpec((T, F2), lambda i, j: (i, 0)), pl.BlockSpec((8, F2), lambda i, j: (0, 0))),
        compiler_params=_cparams("arbitrary", "arbitrary"),
    )(z, z, z, z, z, z, dact, dact, conv_w, conv_w, conv_b, conv_b)


OFF_QKV = POOL_W
OFF_G = POOL_W + 3 * FOX_W


SHARD_BY_ROWS = {"w_in": False, "w_pool_br": False, "w_fox_br": False, "w_mix_out": True, "w_xq": True,
                 "w_xkv": True, "w_xo": False, "w_up": False, "w_down": True, "conv_w": False}


def _prep_layer_weights(w, l):
    n, _, D, b = w["w_in"].shape
    w_in = w["w_in"][:, l].transpose(1, 0, 2).reshape(D, n * b)
    off_f = OFF_G
    pad = jnp.zeros((D, LANES - N_FGATE), w_in.dtype)
    w_in_r = jnp.concatenate([w_in[:, :off_f], w_in[:, off_f + N_FGATE:], w_in[:, off_f:off_f + N_FGATE], pad], axis=1)
    wl = {k: _ChipMajor(w[k], l, SHARD_BY_ROWS[k]) for k in MATMUL_WEIGHTS if k != "w_in"}
    wl["w_in_r"] = w_in_r
    return wl


def _row(v):
    return v.reshape(1, -1)


def _layer_fwd(x, mem, wl, p, l):
    S, D = x.shape
    n = lambda s: f"l{l}_{s}"
    sv = {"x0": x}
    h1 = _rms_fwd(x, _row(p["mix_pre_g"][l]), out_dtype=BF16, name=n("rms1"))
    w_in_r = wl["w_in_r"]
    og = OFF_G
    zu = _matmul(h1, w_in_r[:, :OFF_QKV], out_dtype=F32, name=n("mm_zu"))
    zqkv = _matmul(h1, w_in_r[:, OFF_QKV:og], out_dtype=BF16, name=n("mm_zqkv"))
    zg = _matmul(h1, w_in_r[:, og:og + 2 * D], out_dtype=F32, name=n("mm_zg"))
    zf = _matmul(h1, w_in_r[:, og + 2 * D:], out_dtype=F32, name=n("mm_zf"))
    bf = jnp.pad(p["b_forget"][l], (0, LANES - N_FGATE)).reshape(1, LANES)
    _, augq, augk = _fgate_fwd(zf, bf, name=n("fgate"))
    o_fox, lse2 = _fox_fwd(zqkv, augq, augk, name=n("fox"))
    pooled, mixed = _pool_fwd(zu, p["pool_w"][l], _row(p["pool_scale"][l]), name=n("pool"))
    y_pool = _matmul(mixed, wl["w_pool_br"], out_dtype=BF16, name=n("mm_ypool"))
    y_fox = _matmul(o_fox, wl["w_fox_br"], out_dtype=BF16, name=n("mm_yfox"))
    merged = _merge_fwd(zg, y_pool, y_fox, name=n("merge"))
    r1 = _matmul(merged, wl["w_mix_out"], out_dtype=F32, name=n("mm_r1"))
    x1 = _add_rms(x, r1, _row(p["mix_post_g"][l]), name=n("addrms1"))
    sv.update(h1=h1, zg=zg, zf=zf, bf=bf, zqkv=zqkv, augq=augq, augk=augk, o_fox=o_fox, lse2=lse2,
              pooled=pooled, mixed=mixed, y_pool=y_pool, y_fox=y_fox, merged=merged, r1=r1, x1=x1)
    h2 = _rms_fwd(x1, _row(p["xa_pre_g"][l]), out_dtype=BF16, name=n("rms2"))
    mem_n = _rms_fwd(mem, _row(p["mem_g"][l]), out_dtype=BF16, name=n("rms_mem"))
    q2 = _matmul(h2, wl["w_xq"], out_dtype=BF16, name=n("mm_q2"))
    kv = _matmul(mem_n, wl["w_xkv"], out_dtype=BF16, name=n("mm_kv"))
    o2 = _xattn_fwd(q2, kv, name=n("xattn"))
    a2 = _matmul(o2, wl["w_xo"], out_dtype=F32, name=n("mm_a2"))
    x2 = _add_rms(x1, a2, _row(p["xa_post_g"][l]), name=n("addrms2"))
    sv.update(h2=h2, mem_n=mem_n, q2=q2, kv=kv, o2=o2, a2=a2, x2=x2)
    h3 = _rms_fwd(x2, _row(p["ffn_pre_g"][l]), out_dtype=BF16, name=n("rms3"))
    z3 = _matmul(h3, wl["w_up"], out_dtype=BF16, name=n("mm_z3"))
    act = _convglu_fwd(z3, p["conv_w"][l], _row(p["conv_b"][l]), name=n("convglu"))
    d3 = _matmul(act, wl["w_down"], out_dtype=F32, name=n("mm_d3"))
    x3 = _add_rms(x2, d3, _row(p["ffn_post_g"][l]), name=n("addrms3"))
    sv.update(h3=h3, z3=z3, act=act, d3=d3)
    return x3, sv


def _layer_bwd(dx, mem, wl, p, l, sv, gbuf):
    S, D = dx.shape
    n_layers = p["mix_pre_g"].shape[0]
    n = lambda s: f"l{l}_b_{s}"
    g = {}
    red = lambda part: jnp.sum(part, axis=0)

    def wgrad(k, lhs, rhs, nm):
        how = dict(out_rows=N_CHIPS) if SHARD_BY_ROWS[k] else dict(out_chips=N_CHIPS)
        return _matmul(lhs, rhs, ta=True, out_dtype=F32, out_layer=(gbuf.get(k), l, n_layers), name=n(nm), **how)

    dd3, dg = _rms_bwd(sv["d3"], _row(p["ffn_post_g"][l]), dx, out_dtype=BF16, name=n("rms3post"))
    g["ffn_post_g"] = red(dg)
    dact = _matmul(dd3, wl["w_down"], tb=True, out_dtype=BF16, name=n("mm_dact"))
    g["w_down"] = wgrad("w_down", sv["act"], dd3, "mm_dwdown")
    dz3, dconv = _convglu_bwd(sv["z3"], dact, p["conv_w"][l], _row(p["conv_b"][l]), name=n("convglu"))
    g["conv_w"] = dconv[:3]
    g["conv_b"] = dconv[3]
    dh3 = _matmul(dz3, wl["w_up"], tb=True, out_dtype=F32, name=n("mm_dh3"))
    g["w_up"] = wgrad("w_up", sv["h3"], dz3, "mm_dwup")
    dx, dg = _rms_bwd(sv["x2"], _row(p["ffn_pre_g"][l]), dh3, dx, out_dtype=F32, name=n("rms3pre"))
    g["ffn_pre_g"] = red(dg)
    da2, dg = _rms_bwd(sv["a2"], _row(p["xa_post_g"][l]), dx, out_dtype=BF16, name=n("rms2post"))
    g["xa_post_g"] = red(dg)
    do2 = _matmul(da2, wl["w_xo"], tb=True, out_dtype=BF16, name=n("mm_do2"))
    g["w_xo"] = wgrad("w_xo", sv["o2"], da2, "mm_dwxo")
    dq2, dkv = _xattn_bwd(sv["q2"], sv["kv"], do2, name=n("xattn"))
    dh2 = _matmul(dq2, wl["w_xq"], tb=True, out_dtype=F32, name=n("mm_dh2"))
    g["w_xq"] = wgrad("w_xq", sv["h2"], dq2, "mm_dwxq")
    dmem_n = _matmul(dkv, wl["w_xkv"], tb=True, out_dtype=F32, name=n("mm_dmemn"))
    g["w_xkv"] = wgrad("w_xkv", sv["mem_n"], dkv, "mm_dwxkv")
    _, dg = _rms_bwd(mem, _row(p["mem_g"][l]), dmem_n, out_dtype=BF16, name=n("rms_mem"))
    g["mem_g"] = red(dg)
    dx, dg = _rms_bwd(sv["x1"], _row(p["xa_pre_g"][l]), dh2, dx, out_dtype=F32, name=n("rms2pre"))
    g["xa_pre_g"] = red(dg)
    dr1, dg = _rms_bwd(sv["r1"], _row(p["mix_post_g"][l]), dx, out_dtype=BF16, name=n("rms1post"))
    g["mix_post_g"] = red(dg)
    dmerged = _matmul(dr1, wl["w_mix_out"], tb=True, out_dtype=F32, name=n("mm_dmerged"))
    g["w_mix_out"] = wgrad("w_mix_out", sv["merged"], dr1, "mm_dwmo")
    dyp, dyf, dzg = _merge_bwd(sv["zg"], sv["y_pool"], sv["y_fox"], dmerged, name=n("merge"))
    dmixed = _matmul(dyp, wl["w_pool_br"], tb=True, out_dtype=F32, name=n("mm_dmixed"))
    g["w_pool_br"] = wgrad("w_pool_br", sv["mixed"], dyp, "mm_dwpb")
    dofox = _matmul(dyf, wl["w_fox_br"], tb=True, out_dtype=F32, name=n("mm_dofox"))
    g["w_fox_br"] = wgrad("w_fox_br", sv["o_fox"], dyf, "mm_dwfb")
    dzu, dpw, dsc = _pool_bwd(sv["pooled"], dmixed, p["pool_w"][l], _row(p["pool_scale"][l]), name=n("pool"))
    g["pool_w"] = dpw
    g["pool_scale"] = red(dsc)
    delta = _head_rowsum(dofox, sv["o_fox"], name=n("delta"))
    lse = sv["lse2"].reshape(S, N_PAIRS, PAIR, FOX_DH)[:, :, ::-1, 0].reshape(S, FOX_HEADS)
    dq, dk, dv, dck, dcq = _fox_bwd(sv["zqkv"], dofox.astype(BF16), sv["augq"], sv["augk"],
                                    lse.T.reshape(FOX_HEADS, 1, S), delta[:, :FOX_HEADS].T.reshape(FOX_HEADS, 1, S),
                                    name=n("fox"))
    dc = dcq.reshape(FOX_HEADS, S) + dck.reshape(FOX_HEADS, S)
    dc_pad = jnp.pad(dc.T, ((0, 0), (0, LANES - FOX_HEADS)))
    dzf, db = _fgate_bwd(sv["zf"], sv["bf"], dc_pad, name=n("fgate"))
    g["b_forget"] = red(db)[:N_FGATE]
    dz_cat = jnp.concatenate([dzu, dq.astype(BF16), dk, dv, dzg, dzf], axis=1)
    dh1 = _matmul(dz_cat, wl["w_in_r"], tb=True, out_dtype=F32, name=n("mm_dh1"))
    dw_in_r = _matmul(sv["h1"], dz_cat, ta=True, out_dtype=F32, name=n("mm_dwin"))
    og = OFF_G
    g["w_in"] = jnp.concatenate([dw_in_r[:, :og], dw_in_r[:, og + 2 * D:og + 2 * D + N_FGATE],
                                 dw_in_r[:, og:og + 2 * D]], axis=1)
    dx, dg = _rms_bwd(sv["x0"], _row(p["mix_pre_g"][l]), dh1, dx, out_dtype=F32, name=n("rms1pre"))
    g["mix_pre_g"] = red(dg)
    return dx, g


MATMUL_WEIGHTS = ("w_in", "w_pool_br", "w_fox_br", "w_mix_out", "w_xq", "w_xkv", "w_xo", "w_up", "w_down")
WEIGHT_NAMES = ("mix_pre_g", "mix_post_g", "w_in", "b_forget", "pool_w", "pool_scale", "w_pool_br", "w_fox_br",
                "w_mix_out", "xa_pre_g", "xa_post_g", "mem_g", "w_xq", "w_xkv", "w_xo", "ffn_pre_g", "ffn_post_g",
                "w_up", "conv_w", "conv_b", "w_down")


def _local_step(x, mem, loss_target, wfull, p):
    L = p["mix_pre_g"].shape[0]
    saved, wls = [], []
    h = x
    for l in range(L):
        wl = _prep_layer_weights(wfull, l)
        h, sv = _layer_fwd(h, mem, wl, p, l)
        saved.append(sv)
        wls.append(wl)
    D = x.shape[1]
    dy, sq = _loss_head(h, loss_target, name="loss_head")
    loss = 0.5 * jnp.sum(sq) / D
    grads = []
    gbuf = {}
    dx = dy
    for l in reversed(range(L)):
        dx, g = _layer_bwd(dx, mem, wls[l], p, l, saved[l], gbuf)
        gbuf = {k: g[k] for k in MATMUL_WEIGHTS if k != "w_in"}
        grads.append(g)
    grads = grads[::-1]

    def chip_major(g):
        return g.reshape(g.shape[0], N_CHIPS, g.shape[1] // N_CHIPS).transpose(1, 0, 2)

    gfull = {k: jnp.stack([grads[l][k] for l in range(L)]) for k in REPLICATED}
    gfull.update({k: jnp.stack([chip_major(grads[l][k]) for l in range(L)], axis=1) for k in ("w_in", "conv_w")})
    gfull.update(gbuf)
    return loss, dx, gfull


PACK_W = 512
PACK_ROW_ALIGN = 1024
N_CHIPS = 4
N_DEV = 8
SHARDED = (("w_in", 2), ("w_pool_br", 2), ("w_fox_br", 2), ("w_mix_out", 1), ("w_xq", 1), ("w_xkv", 1),
           ("w_xo", 2), ("w_up", 2), ("w_down", 1), ("conv_w", 2))
REPLICATED = ("mix_pre_g", "mix_post_g", "b_forget", "pool_w", "pool_scale", "xa_pre_g", "xa_post_g", "mem_g",
              "ffn_pre_g", "ffn_post_g", "conv_b")


def _round_up(n, m):
    return -(-n // m) * m


def _pack(arrs, rows):
    flat = jnp.concatenate([a.reshape(-1) for a in arrs])
    return jnp.pad(flat, (0, rows * PACK_W - flat.shape[0])).reshape(rows, PACK_W)


def _unpack(buf, shapes):
    flat = buf.reshape(-1)
    out, off = [], 0
    for s in shapes:
        n = math.prod(s)
        out.append(flat[off:off + n].reshape(s))
        off += n
    return out


ANY = pl.BlockSpec(memory_space=pl.ANY)


def _remote(send_sems, recv_sems, k, src, dst, to):
    return pltpu.make_async_remote_copy(src_ref=src, dst_ref=dst, send_sem=send_sems.at[k], recv_sem=recv_sems.at[k],
                                        device_id=to, device_id_type=MESH)


def _my_place():
    return lax.axis_index("x"), lax.axis_index("y"), lax.axis_index("c")


def _rows_per_block(a, b, cap_bytes=1024 * 1024):
    if a % 8:
        return a
    return _pick(a, max(8, cap_bytes // (4 * b) // 8 * 8), 8)


def _place_shard(w, chip, dtype, *, name):
    L, a, b = w.shape
    ta = _rows_per_block(a, b)

    def body(chip_ref, w_ref, o_ref):
        o_ref[...] = w_ref[...].astype(dtype)

    return pl.pallas_call(
        body, name=name, out_shape=jax.ShapeDtypeStruct((N_CHIPS, L, a, b), dtype),
        grid_spec=pltpu.PrefetchScalarGridSpec(
            num_scalar_prefetch=1, grid=(L, a // ta),
            in_specs=[pl.BlockSpec((None, ta, b), lambda l, i, chip_ref: (l, i, 0))],
            out_specs=pl.BlockSpec((None, None, ta, b), lambda l, i, chip_ref: (chip_ref[0], l, i, 0))),
        compiler_params=_cparams("parallel", "parallel"),
    )(chip, w)


def _layer_halves(L, c):
    assert L % 2 == 0
    return pl.ds(c * (L // 2), L // 2), pl.ds((1 - c) * (L // 2), L // 2)


def _gather_weights(bufs):
    n = len(bufs)
    L = bufs[0].shape[1]

    def body(*refs):
        outs, (send_sems, recv_sems) = refs[n:2 * n], refs[2 * n:]
        x, y, c = _my_place()
        me = 2 * x + y
        sibling = (x, y, 1 - c)
        chips = [(1 - x, y), (x, 1 - y), (1 - x, 1 - y)]
        half, other = _layer_halves(L, c)
        rc = functools.partial(_remote, send_sems, recv_sems)
        first = [rc(6 * w + k, o.at[me, half], o.at[me, half], (px, py, c))
                 for w, o in enumerate(outs) for k, (px, py) in enumerate(chips)]
        for cp in first:
            cp.start()
        passed = []
        for k, (px, py) in enumerate(chips):
            src = 2 * px + py
            for w, o in enumerate(outs):
                rc(6 * w + k, o.at[src, half], o.at[src, half], (px, py, c)).wait_recv()
                fwd = rc(6 * w + 3 + k, o.at[src, half], o.at[src, half], sibling)
                fwd.start()
                passed.append(fwd)
        for k, (px, py) in enumerate(chips):
            src = 2 * px + py
            for w, o in enumerate(outs):
                rc(6 * w + 3 + k, o.at[src, other], o.at[src, other], sibling).wait_recv()
        for cp in first + passed:
            cp.wait_send()

    return pl.pallas_call(
        body, name="gather_weights",
        out_shape=tuple(jax.ShapeDtypeStruct(b.shape, b.dtype) for b in bufs),
        in_specs=[ANY] * n, out_specs=tuple([ANY] * n), input_output_aliases={i: i for i in range(n)},
        scratch_shapes=[pltpu.SemaphoreType.DMA((6 * n,)), pltpu.SemaphoreType.DMA((6 * n,))],
    )(*bufs)


def _exchange_halves(Gs, rep):
    n = len(Gs)
    L = Gs[0].shape[1]
    RR, W = rep.shape

    def body(*refs):
        g_refs, rep_ref, ra_refs, rall_ref = refs[:n], refs[n], refs[n + 1:2 * n + 1], refs[2 * n + 1]
        send_sems, recv_sems, local_sem = refs[2 * n + 2:]
        x, y, c = _my_place()
        me = 4 * x + 2 * y + c
        rc = functools.partial(_remote, send_sems, recv_sems)
        _, other = _layer_halves(L, c)

        def peer(idx):
            px = (1 - x) if (idx >> 2) & 1 else x
            py = (1 - y) if (idx >> 1) & 1 else y
            pc = (1 - c) if idx & 1 else c
            return px, py, pc

        loc = pltpu.make_async_copy(rep_ref, rall_ref.at[me], local_sem)
        loc.start()
        cps = [rc(N_DEV + w, g.at[:, other], ra, (x, y, 1 - c)) for w, (g, ra) in enumerate(zip(g_refs, ra_refs))]
        for idx in range(1, N_DEV):
            cps.append(rc(idx, rep_ref, rall_ref.at[me], peer(idx)))
        for cp in cps:
            cp.start()
        for w, (g, ra) in enumerate(zip(g_refs, ra_refs)):
            rc(N_DEV + w, g.at[:, other], ra, (x, y, 1 - c)).wait_recv()
        for idx in range(1, N_DEV):
            px, py, pc = peer(idx)
            rc(idx, rep_ref, rall_ref.at[4 * px + 2 * py + pc], (px, py, pc)).wait_recv()
        for cp in cps:
            cp.wait_send()
        loc.wait()

    halves = tuple(jax.ShapeDtypeStruct((g.shape[0], L // 2) + g.shape[2:], F32) for g in Gs)
    out = pl.pallas_call(
        body, name="exchange_halves",
        out_shape=halves + (jax.ShapeDtypeStruct((N_DEV, RR, W), F32),),
        in_specs=[ANY] * (n + 1), out_specs=tuple([ANY] * (n + 1)),
        scratch_shapes=[pltpu.SemaphoreType.DMA((N_DEV + n,)), pltpu.SemaphoreType.DMA((N_DEV + n,)),
                        pltpu.SemaphoreType.DMA],
    )(*Gs, rep)
    return out[:n], out[n]


def _exchange_chips(As):
    n = len(As)

    def body(*refs):
        a_refs, rb_refs, (send_sems, recv_sems) = refs[:n], refs[n:2 * n], refs[2 * n:]
        x, y, c = _my_place()
        me = 2 * x + y
        chips = [(1 - x, y), (x, 1 - y), (1 - x, 1 - y)]
        rc = functools.partial(_remote, send_sems, recv_sems)
        cps = [rc(3 * w + k, a.at[2 * px + py], rb.at[k], (px, py, c))
               for w, (a, rb) in enumerate(zip(a_refs, rb_refs)) for k, (px, py) in enumerate(chips)]
        for cp in cps:
            cp.start()
        for w, (a, rb) in enumerate(zip(a_refs, rb_refs)):
            for k, (px, py) in enumerate(chips):
                rc(3 * w + k, a.at[me], rb.at[k], (px, py, c)).wait_recv()
        for cp in cps:
            cp.wait_send()

    return pl.pallas_call(
        body, name="exchange_chips",
        out_shape=tuple(jax.ShapeDtypeStruct((N_CHIPS - 1,) + a.shape[1:], a.dtype) for a in As),
        in_specs=[ANY] * n, out_specs=tuple([ANY] * n),
        scratch_shapes=[pltpu.SemaphoreType.DMA((3 * n,)), pltpu.SemaphoreType.DMA((3 * n,))],
    )(*As)


def _exchange_sibling(gs):
    n = len(gs)
    L = gs[0].shape[0]

    def body(*refs):
        g_refs, (send_sems, recv_sems) = refs[n:2 * n], refs[2 * n:]
        x, y, c = _my_place()
        half, other = _layer_halves(L, c)
        rc = functools.partial(_remote, send_sems, recv_sems)
        cps = [rc(w, g.at[half], g.at[half], (x, y, 1 - c)) for w, g in enumerate(g_refs)]
        for cp in cps:
            cp.start()
        for w, g in enumerate(g_refs):
            rc(w, g.at[other], g.at[other], (x, y, 1 - c)).wait_recv()
        for cp in cps:
            cp.wait_send()

    return pl.pallas_call(
        body, name="exchange_sibling", out_shape=tuple(jax.ShapeDtypeStruct(g.shape, F32) for g in gs),
        in_specs=[ANY] * n, out_specs=tuple([ANY] * n), input_output_aliases={i: i for i in range(n)},
        scratch_shapes=[pltpu.SemaphoreType.DMA((n,)), pltpu.SemaphoreType.DMA((n,))],
    )(*gs)


def _add_halves(G, recv, core, *, name):
    n, L, a, b = G.shape
    Lh = L // 2
    ta = _rows_per_block(a, b)

    def body(core_ref, g_ref, r_ref, o_ref, o16_ref):
        s = g_ref[...] + r_ref[...]
        o_ref[...] = s
        o16_ref[...] = s.astype(BF16)

    blk = pl.BlockSpec((None, None, ta, b), lambda p, l, i, core_ref: (p, l, i, 0))
    return pl.pallas_call(
        body, name=name,
        out_shape=(jax.ShapeDtypeStruct((n, Lh, a, b), F32), jax.ShapeDtypeStruct((n, Lh, a, b), BF16)),
        grid_spec=pltpu.PrefetchScalarGridSpec(
            num_scalar_prefetch=1, grid=(n, Lh, a // ta),
            in_specs=[pl.BlockSpec((None, None, ta, b), lambda p, l, i, core_ref: (p, core_ref[0] * Lh + l, i, 0)), blk],
            out_specs=(blk, blk)),
        compiler_params=_cparams("parallel", "parallel", "parallel"),
    )(core, G, recv)


def _sum_chips(A, rb, place, *, name):
    _, Lh, a, b = A.shape
    ta = _rows_per_block(a, b, 512 * 1024)

    def body(place_ref, a_ref, r_ref, o_ref):
        o_ref[...] = ((a_ref[...] + r_ref[0].astype(F32)) + r_ref[1].astype(F32)) + r_ref[2].astype(F32)

    return pl.pallas_call(
        body, name=name, out_shape=jax.ShapeDtypeStruct((2 * Lh, a, b), F32),
        grid_spec=pltpu.PrefetchScalarGridSpec(
            num_scalar_prefetch=1, grid=(Lh, a // ta),
            in_specs=[pl.BlockSpec((None, None, ta, b), lambda l, i, place_ref: (place_ref[0], l, i, 0)),
                      pl.BlockSpec((N_CHIPS - 1, None, ta, b), lambda l, i, place_ref: (0, l, i, 0))],
            out_specs=pl.BlockSpec((None, ta, b), lambda l, i, place_ref: (place_ref[1] * Lh + l, i, 0))),
        compiler_params=_cparams("parallel", "parallel"),
    )(place, A, rb)


def _sum_slots(a, *, name):
    n, rows, W = a.shape
    tr = _pick(rows, 512, 8)

    def body(a_ref, o_ref):
        s = a_ref[0]
        for q in range(1, n):
            s = s + a_ref[q]
        o_ref[...] = s

    return pl.pallas_call(
        body, name=name, out_shape=jax.ShapeDtypeStruct((rows, W), F32), grid=(rows // tr,),
        in_specs=[pl.BlockSpec((n, tr, W), lambda i: (0, i, 0))], out_specs=pl.BlockSpec((tr, W), lambda i: (i, 0)),
        compiler_params=_cparams("parallel"),
    )(a)


def _adamw(w, g, m, v, *, name):
    L, a, b = w.shape
    ta = _rows_per_block(a, b, 512 * 1024)

    def body(w_ref, g_ref, m_ref, v_ref, d_ref, nm_ref, nv_ref):
        gg = g_ref[...]
        nm = ADAM_B1 * m_ref[...] + (1.0 - ADAM_B1) * gg
        nv = ADAM_B2 * v_ref[...] + (1.0 - ADAM_B2) * jnp.square(gg)
        m_hat = nm / (1.0 - ADAM_B1 ** ADAM_STEP)
        v_hat = nv / (1.0 - ADAM_B2 ** ADAM_STEP)
        d_ref[...] = -ADAM_LR * (m_hat / (jnp.sqrt(v_hat) + ADAM_EPS) + ADAM_WD * w_ref[...])
        nm_ref[...] = nm
        nv_ref[...] = nv

    blk = pl.BlockSpec((None, ta, b), lambda l, i: (l, i, 0))
    shp = jax.ShapeDtypeStruct((L, a, b), F32)
    return pl.pallas_call(
        body, name=name, out_shape=(shp, shp, shp), grid=(L, a // ta),
        in_specs=[blk, blk, blk, blk], out_specs=(blk, blk, blk),
        compiler_params=_cparams("parallel", "parallel"),
    )(w, g, m, v)


INPUT_NAMES = (("x", "mem") + WEIGHT_NAMES + ("loss_target",) + tuple("m_" + n for n in WEIGHT_NAMES)
               + tuple("v_" + n for n in WEIGHT_NAMES))


def kernel(*args):
    a = dict(zip(INPUT_NAMES, args, strict=True))
    x, mem, target = a["x"][0], a["mem"][0], a["loss_target"][0]
    sh_names = list(SHARD_BY_ROWS)
    core = lax.axis_index("c").astype(jnp.int32)
    chip = (2 * lax.axis_index("x") + lax.axis_index("y")).astype(jnp.int32)
    place = jnp.stack([chip, core])

    placed = [_place_shard(a[n], chip.reshape(1), F32 if n == "conv_w" else BF16, name="place_" + n) for n in sh_names]
    wfull = dict(zip(sh_names, _gather_weights(placed)))
    p = {n: a[n] for n in REPLICATED}
    cw = wfull.pop("conv_w")
    p["conv_w"] = cw.transpose(1, 2, 0, 3).reshape(cw.shape[1], cw.shape[2], N_CHIPS * cw.shape[3])

    loss, dx, gfull = _local_step(x, mem, target, wfull, p)
    loss = lax.psum(loss, ("x", "y", "c"))

    Gs = [gfull[n] for n in sh_names]
    rep_shapes = [a[n].shape for n in REPLICATED]
    rows_r = _round_up(-(-sum(math.prod(s) for s in rep_shapes) // PACK_W), 64)
    rep = _pack([gfull[n] for n in REPLICATED], rows_r)
    recvs, repall = _exchange_halves(Gs, rep)
    As = [_add_halves(g, r, core.reshape(1), name="add_halves_" + n) for n, g, r in zip(sh_names, Gs, recvs)]
    rbs = _exchange_chips([a16 for _, a16 in As])
    gsh = _exchange_sibling([_sum_chips(A, rb, place, name="sum_chips_" + n)
                             for n, (A, _), rb in zip(sh_names, As, rbs)])
    grep = _sum_slots(repall, name="sum_devices")

    got = {"g": dict(zip(sh_names, gsh)), "d": {}, "m": {}, "v": {}}
    for n, g in zip(sh_names, gsh):
        got["d"][n], got["m"][n], got["v"][n] = _adamw(a[n], g, a["m_" + n], a["v_" + n], name="adamw_" + n)
    packed = [_pack([a[pre + n] for n in REPLICATED], rows_r)[None] for pre in ("", "m_", "v_")]
    d_r, m_r, v_r = _adamw(packed[0], grep[None], packed[1], packed[2], name="adamw_replicated")
    for key, buf in (("g", grep), ("d", d_r[0]), ("m", m_r[0]), ("v", v_r[0])):
        got[key].update(zip(REPLICATED, _unpack(buf, rep_shapes)))
    outs = [got[key][n] for key in ("g", "d", "m", "v") for n in WEIGHT_NAMES]
    return (loss, dx[None], *outs)
```

```python
import functools
import math

import jax
import jax.numpy as jnp
from jax import lax
from jax.experimental import pallas as pl
from jax.experimental.pallas import tpu as pltpu

F32 = jnp.float32
BF16 = jnp.bfloat16
MESH = pl.DeviceIdType.MESH

RMS_EPS = 1e-6
POOL_WINDOWS = (2, 4, 8, 16)
POOL_GROUP = 128
POOL_W = 512
FOX_HEADS = 8
FOX_DH = 64
FOX_W = 512
X_HEADS = 4
X_DH = 128
X_W = 512
N_FGATE = 8
LANES = 128
HALO = 16

ADAM_LR = 0.001
ADAM_B1 = 0.9
ADAM_B2 = 0.999
ADAM_EPS = 1e-08
ADAM_WD = 0.01
ADAM_STEP = 10

VMEM_LIMIT_BYTES = 56 * 1024 * 1024
MATMUL_VMEM_BUDGET = 44 * 1024 * 1024


def _cparams(*sem):
    return pltpu.CompilerParams(dimension_semantics=sem, vmem_limit_bytes=VMEM_LIMIT_BYTES)


def _pick(n, cap, align=LANES):
    if n <= cap:
        return n
    best = None
    for t in range(align, cap + 1, align):
        if n % t == 0:
            best = t
    assert best is not None, (n, cap, align)
    return best


def _sigmoid(x):
    return 1.0 / (1.0 + jnp.exp(-x))


class _ChipMajor:
    def __init__(self, arr, layer, by_rows):
        self.arr, self.layer, self.by_rows = arr, layer, by_rows
        n, _, a, b = arr.shape
        self.n_chips, self.per_chip = n, (a if by_rows else b)
        self.shape = (n * a, b) if by_rows else (a, n * b)


SPAN_CHIPS_BELOW = 512


def _matmul(a, b, *, ta=False, tb=False, out_dtype=F32, out_chips=None, out_rows=None, out_layer=None, name):
    view = b if isinstance(b, _ChipMajor) else None
    if ta:
        K, M = a.shape
    else:
        M, K = a.shape
    if tb:
        N, Kb = b.shape
    else:
        Kb, N = b.shape
    assert K == Kb, (a.shape, b.shape, ta, tb)
    by_rows = view is not None and view.by_rows
    by_cols = view is not None and not view.by_rows
    b_itemsize = (view.arr if view is not None else b).dtype.itemsize
    span_b = by_cols and view.per_chip < SPAN_CHIPS_BELOW
    span_o = bool(out_chips) and N // out_chips < SPAN_CHIPS_BELOW
    if by_rows and tb:
        tn = N
    elif by_cols and not tb:
        tn = N if span_b else _pick(view.per_chip, 1408)
    elif out_chips:
        tn = N if span_o else _pick(N // out_chips, 1408)
    elif out_rows:
        tn = _pick(N, 512)
    else:
        tn = _pick(N, 1408)
    tm = M if out_rows else _pick(M, 1024 if tn <= 1024 else 512)
    if by_rows and not tb:
        tk = K
    elif by_cols and tb:
        tk = K if span_b else (view.per_chip if view.per_chip <= 2048 else _pick(view.per_chip, 1024))
    else:
        ab, bb, ob = a.dtype.itemsize, b_itemsize, jnp.dtype(out_dtype).itemsize
        for cap in (K, 2048, 1024, 512, 128):
            tk = _pick(K, cap)
            need = (2 * tk * (tm * ab + tn * bb) + (2 * ob + 4 + (4 if tk < K else 0)) * tm * tn)
            if need <= MATMUL_VMEM_BUDGET:
                break
    nk = K // tk
    dims = (((0 if ta else 1,), (1 if tb else 0,)), ((), ()))

    aliased = out_layer is not None and out_layer[0] is not None

    def body(a_ref, b_ref, *rest):
        o_ref, scratch = (rest[1], rest[2:]) if aliased else (rest[0], rest[1:])
        bt = b_ref[...]
        if by_rows:
            bt = bt.reshape(bt.shape[0] * bt.shape[1], bt.shape[2])
        elif span_b:
            bt = jnp.concatenate([bt[q] for q in range(view.n_chips)], axis=1)
        p = lax.dot_general(a_ref[...].astype(BF16), bt.astype(BF16), dims, preferred_element_type=F32)

        def store(val):
            if span_o:
                w = N // out_chips
                for q in range(out_chips):
                    o_ref[q] = val[:, q * w:(q + 1) * w].astype(out_dtype)
            elif out_rows:
                h = M // out_rows
                for q in range(out_rows):
                    o_ref[q] = val[q * h:(q + 1) * h, :].astype(out_dtype)
            else:
                o_ref[...] = val.astype(out_dtype)

        if nk == 1:
            store(p)
        else:
            acc_ref, = scratch
            k = pl.program_id(2)

            @pl.when(k == 0)
            def _():
                acc_ref[...] = p

            @pl.when(k > 0)
            def _():
                acc_ref[...] += p

            @pl.when(k == nk - 1)
            def _():
                store(acc_ref[...])

    a_spec = pl.BlockSpec((tk, tm), lambda i, j, k: (k, i)) if ta else pl.BlockSpec((tm, tk), lambda i, j, k: (i, k))
    b_tile = (tn, tk) if tb else (tk, tn)
    b_rc = (lambda i, j, k: (j, k)) if tb else (lambda i, j, k: (k, j))
    if view is None:
        b_arr = b
        b_spec = pl.BlockSpec(b_tile, b_rc)
    elif by_rows:
        b_arr = view.arr
        assert b_tile[0] == view.shape[0]
        b_spec = pl.BlockSpec((view.n_chips, None, view.arr.shape[2], b_tile[1]),
                              lambda i, j, k: (0, view.layer, 0, b_rc(i, j, k)[1]))
    elif span_b:
        b_arr = view.arr
        assert b_tile[1] == view.shape[1]
        b_spec = pl.BlockSpec((view.n_chips, None, b_tile[0], view.per_chip),
                              lambda i, j, k: (0, view.layer, b_rc(i, j, k)[0], 0))
    else:
        b_arr = view.arr
        per = view.per_chip // b_tile[1]
        b_spec = pl.BlockSpec((None, None) + b_tile,
                              lambda i, j, k: (b_rc(i, j, k)[1] // per, view.layer, b_rc(i, j, k)[0],
                                               b_rc(i, j, k)[1] % per))
    if span_o:
        o_full, o_blk = (out_chips, M, N // out_chips), (out_chips, tm, N // out_chips)
        o_idx = lambda i, j, k: (0, i, 0)
    elif out_chips:
        per_o = (N // out_chips) // tn
        o_full, o_blk = (out_chips, M, N // out_chips), (None, tm, tn)
        o_idx = lambda i, j, k: (j // per_o, i, j % per_o)
    elif out_rows:
        o_full, o_blk = (out_rows, M // out_rows, N), (out_rows, M // out_rows, tn)
        o_idx = lambda i, j, k: (0, 0, j)
    else:
        o_full, o_blk = (M, N), (tm, tn)
        o_idx = lambda i, j, k: (i, j)
    if out_layer is not None:
        _, layer, n_layers = out_layer
        o_full, o_blk = o_full[:1] + (n_layers,) + o_full[1:], o_blk[:1] + (None,) + o_blk[1:]
        o_idx = functools.partial(lambda f, i, j, k: (f(i, j, k)[0], layer) + f(i, j, k)[1:], o_idx)
    out_shape = jax.ShapeDtypeStruct(o_full, out_dtype)
    out_spec = pl.BlockSpec(o_blk, o_idx)
    grid = (M // tm, N // tn, nk)
    specs = [a_spec, b_spec, out_spec]
    a_bytes, b_bytes = M * K * a.dtype.itemsize, K * N * b_itemsize
    if nk == 1 and b_bytes + a_bytes * grid[1] < a_bytes + b_bytes * grid[0]:
        grid = (grid[1], grid[0], nk)
        specs = [pl.BlockSpec(s.block_shape, functools.partial(lambda f, j, i, k: f(i, j, k), s.index_map))
                 for s in specs]
    extra = (out_layer[0],) if aliased else ()
    return pl.pallas_call(
        body, name=name, out_shape=out_shape, grid=grid,
        in_specs=specs[:2] + [pl.BlockSpec(memory_space=pl.ANY)] * len(extra), out_specs=specs[2],
        input_output_aliases={2: 0} if aliased else {},
        scratch_shapes=[pltpu.VMEM((tm, tn), F32)] if nk > 1 else [],
        compiler_params=_cparams("parallel", "parallel", "arbitrary"),
    )(a, b_arr, *extra)


def _row_block(S, D, cap_bytes=2 * 1024 * 1024):
    ts = max(8, min(S, cap_bytes // (4 * D)))
    return _pick(S, ts, 8)


def _rms_fwd(x, g, *, out_dtype, name):
    S, D = x.shape
    ts = _row_block(S, D)

    def body(x_ref, g_ref, o_ref):
        xf = x_ref[...]
        r = lax.rsqrt(jnp.mean(xf * xf, axis=-1, keepdims=True) + RMS_EPS)
        o_ref[...] = (xf * r * g_ref[...]).astype(out_dtype)

    return pl.pallas_call(
        body, name=name, out_shape=jax.ShapeDtypeStruct((S, D), out_dtype), grid=(S // ts,),
        in_specs=[pl.BlockSpec((ts, D), lambda i: (i, 0)), pl.BlockSpec((1, D), lambda i: (0, 0))],
        out_specs=pl.BlockSpec((ts, D), lambda i: (i, 0)),
        compiler_params=_cparams("parallel"),
    )(x, g)


def _add_rms(x, r, g, *, name):
    S, D = x.shape
    ts = _row_block(S, D)

    def body(x_ref, r_ref, g_ref, o_ref):
        rf = r_ref[...]
        s = lax.rsqrt(jnp.mean(rf * rf, axis=-1, keepdims=True) + RMS_EPS)
        o_ref[...] = x_ref[...] + rf * s * g_ref[...]

    return pl.pallas_call(
        body, name=name, out_shape=jax.ShapeDtypeStruct((S, D), F32), grid=(S // ts,),
        in_specs=[pl.BlockSpec((ts, D), lambda i: (i, 0)), pl.BlockSpec((ts, D), lambda i: (i, 0)),
                  pl.BlockSpec((1, D), lambda i: (0, 0))],
        out_specs=pl.BlockSpec((ts, D), lambda i: (i, 0)),
        compiler_params=_cparams("parallel"),
    )(x, r, g)


def _rms_bwd(x, g, dy, res=None, *, out_dtype, name):
    S, D = x.shape
    ts = _row_block(S, D)
    has_res = res is not None

    def body(*refs):
        if has_res:
            x_ref, g_ref, dy_ref, res_ref, dx_ref, dg_ref = refs
        else:
            x_ref, g_ref, dy_ref, dx_ref, dg_ref = refs
        i = pl.program_id(0)
        xf = x_ref[...]
        dyf = dy_ref[...].astype(F32)
        r = lax.rsqrt(jnp.mean(xf * xf, axis=-1, keepdims=True) + RMS_EPS)
        n = xf * r
        dn = dyf * g_ref[...]
        dx = r * (dn - n * jnp.mean(dn * n, axis=-1, keepdims=True))
        if has_res:
            dx = dx + res_ref[...]
        dx_ref[...] = dx.astype(out_dtype)
        part = jnp.sum((dyf * n).reshape(ts // 8, 8, D), axis=0)

        @pl.when(i == 0)
        def _():
            dg_ref[...] = part

        @pl.when(i > 0)
        def _():
            dg_ref[...] += part

    row = pl.BlockSpec((ts, D), lambda i: (i, 0))
    in_specs = [row, pl.BlockSpec((1, D), lambda i: (0, 0)), row] + ([row] if has_res else [])
    args = (x, g, dy) + ((res,) if has_res else ())
    dx, dg = pl.pallas_call(
        body, name=name,
        out_shape=(jax.ShapeDtypeStruct((S, D), out_dtype), jax.ShapeDtypeStruct((8, D), F32)),
        grid=(S // ts,), in_specs=in_specs,
        out_specs=(row, pl.BlockSpec((8, D), lambda i: (0, 0))),
        compiler_params=_cparams("arbitrary"),
    )(*args)
    return dx, dg


def _loss_head(y, t, *, name):
    S, D = y.shape
    ts = _row_block(S, D)

    def body(y_ref, t_ref, dy_ref, sq_ref):
        i = pl.program_id(0)
        e = y_ref[...] - t_ref[...]
        dy_ref[...] = e / D
        part = jnp.sum((e * e).reshape(ts // 8, 8, D), axis=0)

        @pl.when(i == 0)
        def _():
            sq_ref[...] = part

        @pl.when(i > 0)
        def _():
            sq_ref[...] += part

    row = pl.BlockSpec((ts, D), lambda i: (i, 0))
    return pl.pallas_call(
        body, name=name,
        out_shape=(jax.ShapeDtypeStruct((S, D), F32), jax.ShapeDtypeStruct((8, D), F32)),
        grid=(S // ts,), in_specs=[row, row],
        out_specs=(row, pl.BlockSpec((8, D), lambda i: (0, 0))),
        compiler_params=_cparams("arbitrary"),
    )(y, t)


def _window_counts(i, T, w):
    t = i * T + lax.broadcasted_iota(jnp.int32, (T, 1), 0)
    return jnp.minimum(t + 1, w).astype(F32)


def _pool_fwd(zu, pool_w, pool_scale, *, name):
    S, W = zu.shape
    T = _pick(S, 1024, 8)

    def body(u_ref, pw_ref, sc_ref, pooled_ref, mixed_ref, halo_ref):
        i = pl.program_id(0)

        @pl.when(i == 0)
        def _():
            halo_ref[...] = jnp.zeros_like(halo_ref)

        u = u_ref[...]
        ext = jnp.concatenate([halo_ref[...], u], axis=0)
        halo_ref[...] = u[T - HALO:, :]
        for g, w in enumerate(POOL_WINDOWS):
            cols = slice(g * POOL_GROUP, (g + 1) * POOL_GROUP)
            s = ext[:, cols]
            sh = 1
            while sh < w:
                s = s + pltpu.roll(s, sh, 0)
                sh *= 2
            pooled = s[HALO:, :] / _window_counts(i, T, w) - u[:, cols]
            pooled_bf = pooled.astype(BF16)
            pm = jnp.dot(pooled_bf, pw_ref[g].astype(BF16), preferred_element_type=F32)
            pooled_ref[:, cols] = pooled_bf
            mixed_ref[:, cols] = (pm * sc_ref[:, cols]).astype(BF16)

    row = pl.BlockSpec((T, W), lambda i: (i, 0))
    return pl.pallas_call(
        body, name=name,
        out_shape=(jax.ShapeDtypeStruct((S, W), BF16), jax.ShapeDtypeStruct((S, W), BF16)),
        grid=(S // T,),
        in_specs=[row, pl.BlockSpec(pool_w.shape, lambda i: (0, 0, 0)), pl.BlockSpec((1, W), lambda i: (0, 0))],
        out_specs=(row, row),
        scratch_shapes=[pltpu.VMEM((HALO, W), F32)],
        compiler_params=_cparams("arbitrary"),
    )(zu, pool_w, pool_scale)


def _pool_bwd(pooled, dmixed, pool_w, pool_scale, *, name):
    S, W = pooled.shape
    T = _pick(S, 1024, 8)
    nb = S // T

    def body(p_ref, dm_ref, pw_ref, sc_ref, dzu_ref, dpw_ref, dsc_ref, halo_ref):
        i = pl.program_id(0)
        blk = nb - 1 - i

        @pl.when(i == 0)
        def _():
            halo_ref[...] = jnp.zeros_like(halo_ref)
            dpw_ref[...] = jnp.zeros_like(dpw_ref)
            dsc_ref[...] = jnp.zeros_like(dsc_ref)

        for g, w in enumerate(POOL_WINDOWS):
            cols = slice(g * POOL_GROUP, (g + 1) * POOL_GROUP)
            p = p_ref[:, cols]
            dm = dm_ref[:, cols]
            pw = pw_ref[g].astype(BF16)
            pm = jnp.dot(p, pw, preferred_element_type=F32)
            dsc_ref[:, cols] += jnp.sum((dm * pm).reshape(T // 8, 8, POOL_GROUP), axis=0)
            dpm = (dm * sc_ref[:, cols]).astype(BF16)
            dpw_ref[g] += lax.dot_general(p, dpm, (((0,), (0,)), ((), ())), preferred_element_type=F32)
            dpooled = lax.dot_general(dpm, pw, (((1,), (1,)), ((), ())), preferred_element_type=F32)
            e = dpooled / _window_counts(blk, T, w)
            ext = jnp.concatenate([e, halo_ref[:, cols]], axis=0)
            halo_ref[:, cols] = e[:HALO, :]
            s = ext
            sh = 1
            while sh < w:
                s = s + pltpu.roll(s, T + HALO - sh, 0)
                sh *= 2
            dzu_ref[:, cols] = (s[:T, :] - dpooled).astype(BF16)

    row = pl.BlockSpec((T, W), lambda i: (nb - 1 - i, 0))
    return pl.pallas_call(
        body, name=name,
        out_shape=(jax.ShapeDtypeStruct((S, W), BF16), jax.ShapeDtypeStruct(pool_w.shape, F32),
                   jax.ShapeDtypeStruct((8, W), F32)),
        grid=(nb,),
        in_specs=[row, row, pl.BlockSpec(pool_w.shape, lambda i: (0, 0, 0)), pl.BlockSpec((1, W), lambda i: (0, 0))],
        out_specs=(row, pl.BlockSpec(pool_w.shape, lambda i: (0, 0, 0)), pl.BlockSpec((8, W), lambda i: (0, 0))),
        scratch_shapes=[pltpu.VMEM((HALO, W), F32)],
        compiler_params=_cparams("arbitrary"),
    )(pooled, dmixed, pool_w, pool_scale)


def _fgate_fwd(zf, bf, *, name):
    S, W = zf.shape
    T = _pick(S, 512, 8)

    def body(z_ref, b_ref, c_ref, aq_ref, ak_ref, carry_ref):
        i = pl.program_id(0)

        @pl.when(i == 0)
        def _():
            carry_ref[...] = jnp.zeros_like(carry_ref)

        a = z_ref[...] + b_ref[...]
        s = jnp.minimum(a, 0.0) - jnp.log(1.0 + jnp.exp(-jnp.abs(a)))
        row = lax.broadcasted_iota(jnp.int32, (T, W), 0)
        sh = 1
        while sh < T:
            s = s + jnp.where(row >= sh, pltpu.roll(s, sh, 0), 0.0)
            sh *= 2
        c = s + carry_ref[0:1, :]
        c_ref[...] = c
        carry_ref[...] = jnp.broadcast_to(c[T - 1:T, :], carry_ref.shape)
        lane = lax.broadcasted_iota(jnp.int32, (T, W), 1)
        for h in range(FOX_HEADS):
            ch = c[:, h:h + 1]
            hi = ch.astype(BF16).astype(F32)
            r1 = ch - hi
            lo = r1.astype(BF16).astype(F32)
            lo2 = (r1 - lo).astype(BF16).astype(F32)
            aq = jnp.where(lane == 0, hi, jnp.where(lane == 1, lo, jnp.where(lane == 2, lo2,
                                                                              jnp.where(lane < 6, 1.0, 0.0))))
            ak = jnp.where(lane < 3, 1.0, jnp.where(lane == 3, -hi, jnp.where(lane == 4, -lo,
                                                                               jnp.where(lane == 5, -lo2, 0.0))))
            aq_ref[h] = aq.astype(BF16)
            ak_ref[h] = ak.astype(BF16)

    aug = jax.ShapeDtypeStruct((FOX_HEADS, S, W), BF16)
    aug_spec = pl.BlockSpec((FOX_HEADS, T, W), lambda i: (0, i, 0))
    return pl.pallas_call(
        body, name=name, out_shape=(jax.ShapeDtypeStruct((S, W), F32), aug, aug), grid=(S // T,),
        in_specs=[pl.BlockSpec((T, W), lambda i: (i, 0)), pl.BlockSpec((1, W), lambda i: (0, 0))],
        out_specs=(pl.BlockSpec((T, W), lambda i: (i, 0)), aug_spec, aug_spec),
        scratch_shapes=[pltpu.VMEM((8, W), F32)],
        compiler_params=_cparams("arbitrary"),
    )(zf, bf)


def _fgate_bwd(zf, bf, dc, *, name):
    S, W = zf.shape
    T = _pick(S, 512, 8)
    nb = S // T

    def body(z_ref, b_ref, dc_ref, dz_ref, db_ref, carry_ref):
        i = pl.program_id(0)

        @pl.when(i == 0)
        def _():
            carry_ref[...] = jnp.zeros_like(carry_ref)
            db_ref[...] = jnp.zeros_like(db_ref)

        s = dc_ref[...]
        row = lax.broadcasted_iota(jnp.int32, (T, W), 0)
        sh = 1
        while sh < T:
            s = s + jnp.where(row < T - sh, pltpu.roll(s, T - sh, 0), 0.0)
            sh *= 2
        dlf = s + carry_ref[0:1, :]
        carry_ref[...] = jnp.broadcast_to(dlf[0:1, :], carry_ref.shape)
        dz = dlf * (1.0 - _sigmoid(z_ref[...] + b_ref[...]))
        dz_ref[...] = dz.astype(BF16)
        db_ref[...] += jnp.sum(dz.reshape(T // 8, 8, W), axis=0)

    row_spec = pl.BlockSpec((T, W), lambda i: (nb - 1 - i, 0))
    return pl.pallas_call(
        body, name=name,
        out_shape=(jax.ShapeDtypeStruct((S, W), BF16), jax.ShapeDtypeStruct((8, W), F32)),
        grid=(nb,),
        in_specs=[row_spec, pl.BlockSpec((1, W), lambda i: (0, 0)), row_spec],
        out_specs=(row_spec, pl.BlockSpec((8, W), lambda i: (0, 0))),
        scratch_shapes=[pltpu.VMEM((8, W), F32)],
        compiler_params=_cparams("arbitrary"),
    )(zf, bf, dc)


def _merge_fwd(zg, yp, yf, *, name):
    S, D = yp.shape
    ts = _row_block(S, D, 1024 * 1024)

    def body(zg_ref, yp_ref, yf_ref, o_ref):
        o_ref[...] = (_sigmoid(zg_ref[:, :D]) * yp_ref[...] + _sigmoid(zg_ref[:, D:]) * yf_ref[...]).astype(BF16)

    row = pl.BlockSpec((ts, D), lambda i: (i, 0))
    return pl.pallas_call(
        body, name=name, out_shape=jax.ShapeDtypeStruct((S, D), BF16), grid=(S // ts,),
        in_specs=[pl.BlockSpec((ts, 2 * D), lambda i: (i, 0)), row, row], out_specs=row,
        compiler_params=_cparams("parallel"),
    )(zg, yp, yf)


def _merge_bwd(zg, yp, yf, dmerged, *, name):
    S, D = yp.shape
    ts = _row_block(S, D, 1024 * 1024)

    def body(zg_ref, yp_ref, yf_ref, dm_ref, dyp_ref, dyf_ref, dzg_ref):
        dm = dm_ref[...]
        sp = _sigmoid(zg_ref[:, :D])
        sf = _sigmoid(zg_ref[:, D:])
        dyp_ref[...] = (dm * sp).astype(BF16)
        dyf_ref[...] = (dm * sf).astype(BF16)
        dzg_ref[:, :D] = (dm * yp_ref[...] * (sp * (1.0 - sp))).astype(BF16)
        dzg_ref[:, D:] = (dm * yf_ref[...] * (sf * (1.0 - sf))).astype(BF16)

    row = pl.BlockSpec((ts, D), lambda i: (i, 0))
    wide = pl.BlockSpec((ts, 2 * D), lambda i: (i, 0))
    return pl.pallas_call(
        body, name=name,
        out_shape=(jax.ShapeDtypeStruct((S, D), BF16), jax.ShapeDtypeStruct((S, D), BF16),
                   jax.ShapeDtypeStruct((S, 2 * D), BF16)),
        grid=(S // ts,), in_specs=[wide, row, row, row], out_specs=(row, row, wide),
        compiler_params=_cparams("parallel"),
    )(zg, yp, yf, dmerged)


NEG_BIG = -1e30


FOX_BLOCK = 1024
PAIR = LANES // FOX_DH
N_PAIRS = FOX_HEADS // PAIR


def _fox_fwd(zqkv, augq, augk, *, name):
    S = zqkv.shape[0]
    bq = _pick(S, FOX_BLOCK, 128)
    nq = S // bq
    scale = 1.0 / math.sqrt(FOX_DH)

    def body(q_ref, k_ref, v_ref, aq_ref, ak_ref, o_ref, lse_ref):
        i = pl.program_id(1)
        lane = lax.broadcasted_iota(jnp.int32, (1, LANES), 1)
        first = lane < FOX_DH
        q2 = q_ref[...] * scale
        zero = jnp.zeros_like(q2)
        qh = (jnp.concatenate([jnp.where(first, q2, zero), aq_ref[0]], axis=1),
              jnp.concatenate([jnp.where(first, zero, q2), aq_ref[1]], axis=1))

        def step(j, carry, masked):
            start = pl.multiple_of(j * bq, bq)
            kb = k_ref[pl.ds(start, bq), :]
            vb = v_ref[pl.ds(start, bq), :]
            one = jnp.ones_like(vb)
            vh = (jnp.where(first, vb, one), jnp.where(first, one, vb))
            out = []
            for h in range(PAIR):
                m, acc = carry[h]
                kh = jnp.concatenate([kb, ak_ref[h, pl.ds(start, bq), :]], axis=1)
                s = lax.dot_general(qh[h], kh, (((1,), (1,)), ((), ())), preferred_element_type=F32)
                if masked:
                    r = lax.broadcasted_iota(jnp.int32, (bq, bq), 0)
                    c = lax.broadcasted_iota(jnp.int32, (bq, bq), 1)
                    s = jnp.where(c <= r, s, NEG_BIG)
                m_new = jnp.maximum(m, jnp.max(s, axis=-1, keepdims=True))
                alpha = jnp.exp(m - m_new)
                p = jnp.exp(s - m_new).astype(BF16)
                acc = alpha * acc + jnp.dot(p, vh[h], preferred_element_type=F32)
                out.append((m_new, acc))
            return tuple(out)

        init = tuple((jnp.full((bq, 1), NEG_BIG, F32), jnp.zeros((bq, LANES), F32)) for _ in range(PAIR))
        carry = lax.fori_loop(0, i, lambda j, c: step(j, c, False), init)
        (ma, acca), (mb, accb) = step(i, carry, True)
        num = jnp.where(first, acca, accb)
        den = jnp.where(first, pltpu.roll(acca, FOX_DH, 1), pltpu.roll(accb, FOX_DH, 1))
        o_ref[...] = num / den
        lse_ref[...] = jnp.where(first, mb, ma) + jnp.log(jnp.where(first, accb, acca))

    npair = N_PAIRS
    return pl.pallas_call(
        body, name=name,
        out_shape=(jax.ShapeDtypeStruct((S, FOX_W), F32), jax.ShapeDtypeStruct((S, FOX_W), F32)),
        grid=(npair, nq),
        in_specs=[pl.BlockSpec((bq, LANES), lambda hp, i: (i, hp)),
                  pl.BlockSpec((S, LANES), lambda hp, i: (0, npair + hp)),
                  pl.BlockSpec((S, LANES), lambda hp, i: (0, 2 * npair + hp)),
                  pl.BlockSpec((PAIR, bq, LANES), lambda hp, i: (hp, i, 0)),
                  pl.BlockSpec((PAIR, S, LANES), lambda hp, i: (hp, 0, 0))],
        out_specs=(pl.BlockSpec((bq, LANES), lambda hp, i: (i, hp)),
                   pl.BlockSpec((bq, LANES), lambda hp, i: (i, hp))),
        compiler_params=_cparams("parallel", "arbitrary"),
    )(zqkv, zqkv, zqkv, augq, augk)


def _head_rowsum(a, b, *, name):
    S, W = a.shape
    ts = _pick(S, 1024, 8)

    def body(a_ref, b_ref, o_ref):
        prod = a_ref[...].astype(F32) * b_ref[...].astype(F32)
        hi = prod.astype(BF16)
        lo = (prod - hi.astype(F32)).astype(BF16)
        r = lax.broadcasted_iota(jnp.int32, (W, LANES), 0)
        c = lax.broadcasted_iota(jnp.int32, (W, LANES), 1)
        sel = jnp.where(r // FOX_DH == c, 1.0, 0.0).astype(BF16)
        o_ref[...] = (jnp.dot(hi, sel, preferred_element_type=F32) + jnp.dot(lo, sel, preferred_element_type=F32))

    return pl.pallas_call(
        body, name=name, out_shape=jax.ShapeDtypeStruct((S, LANES), F32), grid=(S // ts,),
        in_specs=[pl.BlockSpec((ts, W), lambda i: (i, 0)), pl.BlockSpec((ts, W), lambda i: (i, 0))],
        out_specs=pl.BlockSpec((ts, LANES), lambda i: (i, 0)),
        compiler_params=_cparams("parallel"),
    )(a, b)


def _fox_bwd(zqkv, do, augq, augk, lse_row, delta_row, *, name):
    S = zqkv.shape[0]
    bk = _pick(S, FOX_BLOCK, 128)
    nk = S // bk
    scale = 1.0 / math.sqrt(FOX_DH)
    npair = N_PAIRS

    def body(q_ref, k_ref, v_ref, do_ref, ak_ref, aq_ref, lse_ref, dl_ref, dq_ref, dk_ref, dv_ref, dck_ref, dcq_ref):
        j = pl.program_id(1)

        @pl.when(j == 0)
        def _():
            dq_ref[...] = jnp.zeros_like(dq_ref)
            dcq_ref[...] = jnp.zeros_like(dcq_ref)

        lane = lax.broadcasted_iota(jnp.int32, (1, LANES), 1)
        first = lane < FOX_DH
        kb = k_ref[...]
        vb = v_ref[...]
        kh = (jnp.concatenate([kb, ak_ref[0]], axis=1), jnp.concatenate([kb, ak_ref[1]], axis=1))

        def step(i, carry, masked):
            start = pl.multiple_of(i * bk, bk)
            qs = q_ref[pl.ds(start, bk), :] * scale
            dob = do_ref[pl.ds(start, bk), :]
            zero = jnp.zeros_like(qs)
            qh = (jnp.where(first, qs, zero), jnp.where(first, zero, qs))
            doh = (jnp.where(first, dob, zero), jnp.where(first, zero, dob))
            out = []
            dqc = []
            for h in range(PAIR):
                dk, dv, dc = carry[h]
                qaug = jnp.concatenate([qh[h], aq_ref[h, pl.ds(start, bk), :]], axis=1)
                lse = lse_ref[h, :, pl.ds(start, bk)]
                dl = dl_ref[h, :, pl.ds(start, bk)]
                st = lax.dot_general(kh[h], qaug, (((1,), (1,)), ((), ())), preferred_element_type=F32)
                pt = jnp.exp(st - lse)
                if masked:
                    r = lax.broadcasted_iota(jnp.int32, (bk, bk), 0)
                    c = lax.broadcasted_iota(jnp.int32, (bk, bk), 1)
                    pt = jnp.where(c >= r, pt, 0.0)
                dpt = lax.dot_general(vb, doh[h], (((1,), (1,)), ((), ())), preferred_element_type=F32)
                dst = pt * (dpt - dl)
                pt_bf = pt.astype(BF16)
                dst_bf = dst.astype(BF16)
                dv = dv + jnp.dot(pt_bf, dob, preferred_element_type=F32)
                dk = dk + jnp.dot(dst_bf, qs, preferred_element_type=F32)
                dc = dc - jnp.sum(dst, axis=-1, keepdims=True)
                dcq_ref[h, :, pl.ds(start, bk)] += jnp.sum(dst, axis=0, keepdims=True)
                dqc.append(lax.dot_general(dst_bf, kb, (((0,), (0,)), ((), ())), preferred_element_type=F32))
                out.append((dk, dv, dc))
            dq_ref[pl.ds(start, bk), :] += jnp.where(first, dqc[0], dqc[1])
            return tuple(out)

        init = tuple((jnp.zeros((bk, LANES), F32), jnp.zeros((bk, LANES), F32), jnp.zeros((bk, 1), F32))
                     for _ in range(PAIR))
        carry = step(j, init, True)
        (dka, dva, dca), (dkb, dvb, dcb) = lax.fori_loop(j + 1, nk, lambda i, c: step(i, c, False), carry)
        dk_ref[...] = jnp.where(first, dka, dkb).astype(BF16)
        dv_ref[...] = jnp.where(first, dva, dvb).astype(BF16)
        dck_ref[0] = dca
        dck_ref[1] = dcb

        @pl.when(j == nk - 1)
        def _():
            dq_ref[...] = dq_ref[...] * scale

    rowfull = pl.BlockSpec((PAIR, 1, S), lambda hp, j: (hp, 0, 0))
    return pl.pallas_call(
        body, name=name,
        out_shape=(jax.ShapeDtypeStruct((S, FOX_W), F32), jax.ShapeDtypeStruct((S, FOX_W), BF16),
                   jax.ShapeDtypeStruct((S, FOX_W), BF16), jax.ShapeDtypeStruct((FOX_HEADS, S, 1), F32),
                   jax.ShapeDtypeStruct((FOX_HEADS, 1, S), F32)),
        grid=(npair, nk),
        in_specs=[pl.BlockSpec((S, LANES), lambda hp, j: (0, hp)),
                  pl.BlockSpec((bk, LANES), lambda hp, j: (j, npair + hp)),
                  pl.BlockSpec((bk, LANES), lambda hp, j: (j, 2 * npair + hp)),
                  pl.BlockSpec((S, LANES), lambda hp, j: (0, hp)),
                  pl.BlockSpec((PAIR, bk, LANES), lambda hp, j: (hp, j, 0)),
                  pl.BlockSpec((PAIR, S, LANES), lambda hp, j: (hp, 0, 0)), rowfull, rowfull],
        out_specs=(pl.BlockSpec((S, LANES), lambda hp, j: (0, hp)),
                   pl.BlockSpec((bk, LANES), lambda hp, j: (j, hp)),
                   pl.BlockSpec((bk, LANES), lambda hp, j: (j, hp)),
                   pl.BlockSpec((PAIR, bk, 1), lambda hp, j: (hp, j, 0)), rowfull),
        compiler_params=_cparams("parallel", "arbitrary"),
    )(zqkv, zqkv, zqkv, do, augk, augq, lse_row, delta_row)


def _xattn_fwd(q, kv, *, name):
    S, W = q.shape
    M = kv.shape[0]
    tq = _pick(S, 512, 8)
    scale = 1.0 / math.sqrt(X_DH)

    def body(q_ref, kv_ref, o_ref):
        for h in range(X_HEADS):
            cols = slice(h * X_DH, (h + 1) * X_DH)
            vcols = slice(W + h * X_DH, W + (h + 1) * X_DH)
            s = lax.dot_general(q_ref[:, cols], kv_ref[:, cols], (((1,), (1,)), ((), ())),
                                preferred_element_type=F32) * scale
            e = jnp.exp(s - jnp.max(s, axis=-1, keepdims=True))
            p = e / jnp.sum(e, axis=-1, keepdims=True)
            o_ref[:, cols] = jnp.dot(p.astype(BF16), kv_ref[:, vcols], preferred_element_type=F32).astype(BF16)

    return pl.pallas_call(
        body, name=name, out_shape=jax.ShapeDtypeStruct((S, W), BF16), grid=(S // tq,),
        in_specs=[pl.BlockSpec((tq, W), lambda i: (i, 0)), pl.BlockSpec((M, 2 * W), lambda i: (0, 0))],
        out_specs=pl.BlockSpec((tq, W), lambda i: (i, 0)),
        compiler_params=_cparams("parallel"),
    )(q, kv)


def _xattn_bwd(q, kv, do, *, name):
    S, W = q.shape
    M = kv.shape[0]
    tq = _pick(S, 512, 8)
    scale = 1.0 / math.sqrt(X_DH)

    def body(q_ref, kv_ref, do_ref, dq_ref, dkv_ref):
        i = pl.program_id(0)

        @pl.when(i == 0)
        def _():
            dkv_ref[...] = jnp.zeros_like(dkv_ref)

        for h in range(X_HEADS):
            cols = slice(h * X_DH, (h + 1) * X_DH)
            vcols = slice(W + h * X_DH, W + (h + 1) * X_DH)
            qh = q_ref[:, cols]
            kh = kv_ref[:, cols]
            vh = kv_ref[:, vcols]
            doh = do_ref[:, cols]
            s = lax.dot_general(qh, kh, (((1,), (1,)), ((), ())), preferred_element_type=F32) * scale
            e = jnp.exp(s - jnp.max(s, axis=-1, keepdims=True))
            p = e / jnp.sum(e, axis=-1, keepdims=True)
            dp = lax.dot_general(doh, vh, (((1,), (1,)), ((), ())), preferred_element_type=F32)
            ds = (p * (dp - jnp.sum(p * dp, axis=-1, keepdims=True)) * scale).astype(BF16)
            dq_ref[:, cols] = jnp.dot(ds, kh, preferred_element_type=F32).astype(BF16)
            dkv_ref[:, cols] += lax.dot_general(ds, qh, (((0,), (0,)), ((), ())), preferred_element_type=F32)
            dkv_ref[:, vcols] += lax.dot_general(p.astype(BF16), doh, (((0,), (0,)), ((), ())),
                                                 preferred_element_type=F32)

    return pl.pallas_call(
        body, name=name,
        out_shape=(jax.ShapeDtypeStruct((S, W), BF16), jax.ShapeDtypeStruct((M, 2 * W), F32)),
        grid=(S // tq,),
        in_specs=[pl.BlockSpec((tq, W), lambda i: (i, 0)), pl.BlockSpec((M, 2 * W), lambda i: (0, 0)),
                  pl.BlockSpec((tq, W), lambda i: (i, 0))],
        out_specs=(pl.BlockSpec((tq, W), lambda i: (i, 0)), pl.BlockSpec((M, 2 * W), lambda i: (0, 0))),
        compiler_params=_cparams("arbitrary"),
    )(q, kv, do)


GELU_C = math.sqrt(2.0 / math.pi)
GELU_A = 0.044715
CONV_HALO = 16


def _gelu_parts(x):
    u = GELU_C * (x + GELU_A * x * x * x)
    t = jnp.tanh(u)
    g = 0.5 * x * (1.0 + t)
    dg = 0.5 * (1.0 + t) + 0.5 * x * (1.0 - t * t) * (GELU_C * (1.0 + 3.0 * GELU_A * x * x))
    return g, dg


def _conv3(ext, w_ref, b_ref):
    return (w_ref[2:3, :] * ext + w_ref[1:2, :] * pltpu.roll(ext, 1, 0)
            + w_ref[0:1, :] * pltpu.roll(ext, 2, 0) + b_ref[...])


def _convglu_fwd(z, conv_w, conv_b, *, name):
    S, F2 = z.shape
    F = F2 // 2
    tc = _pick(F, 1408)
    ncol = F // tc
    T = _pick(S, 512, 8)
    hb = T // CONV_HALO

    def body(zg_ref, zu_ref, zgp_ref, zup_ref, wg_ref, wu_ref, bg_ref, bu_ref, act_ref):
        i = pl.program_id(1)
        first = (i > 0).astype(F32)

        def conv(z_ref, zp_ref, w_ref, b_ref):
            ext = jnp.concatenate([zp_ref[...].astype(F32) * first, z_ref[...].astype(F32)], axis=0)
            return _conv3(ext, w_ref, b_ref)[CONV_HALO:, :]

        gc = conv(zg_ref, zgp_ref, wg_ref, bg_ref)
        uc = conv(zu_ref, zup_ref, wu_ref, bu_ref)
        act_ref[...] = (_gelu_parts(gc)[0] * uc).astype(BF16)

    cur = lambda off: pl.BlockSpec((T, tc), lambda j, i: (i, j + off))
    prev = lambda off: pl.BlockSpec((CONV_HALO, tc), lambda j, i: (jnp.maximum(i * hb - 1, 0), j + off))
    vec = lambda rows, off: pl.BlockSpec((rows, tc), lambda j, i: (0, j + off))
    return pl.pallas_call(
        body, name=name, out_shape=jax.ShapeDtypeStruct((S, F), BF16), grid=(ncol, S // T),
        in_specs=[cur(0), cur(ncol), prev(0), prev(ncol), vec(3, 0), vec(3, ncol), vec(1, 0), vec(1, ncol)],
        out_specs=pl.BlockSpec((T, tc), lambda j, i: (i, j)),
        compiler_params=_cparams("parallel", "parallel"),
    )(z, z, z, z, conv_w, conv_w, conv_b, conv_b)


def _convglu_bwd(z, dact, conv_w, conv_b, *, name):
    S, F2 = z.shape
    F = F2 // 2
    tc = _pick(F, 1408)
    ncol = F // tc
    T = _pick(S, 256, 8)
    nrow = S // T
    hb = T // CONV_HALO
    TE = T + CONV_HALO

    def body(z_ref, zp_ref, zn_ref, da_ref, dan_ref, w_ref, b_ref, dz_ref, dw_ref):
        i = pl.program_id(0)
        first = (i > 0).astype(F32)
        last = (i < nrow - 1).astype(F32)

        @pl.when(i == 0)
        def _():
            dw_ref[...] = jnp.zeros_like(dw_ref)

        def ext_of(cols):
            return jnp.concatenate([zp_ref[:, cols].astype(F32) * first, z_ref[:, cols].astype(F32),
                                    zn_ref[:, cols].astype(F32)], axis=0)

        def back(d, ext, w, cols):
            dz = w[2:3, :] * d + w[1:2, :] * pltpu.roll(d, TE - 1, 0) + w[0:1, :] * pltpu.roll(d, TE - 2, 0)
            dz_ref[:, cols] = dz[:T, :].astype(BF16)
            dc = d[:T, :]
            z0 = ext[CONV_HALO:CONV_HALO + T, :]
            z1 = pltpu.roll(ext, 1, 0)[CONV_HALO:CONV_HALO + T, :]
            z2 = pltpu.roll(ext, 2, 0)[CONV_HALO:CONV_HALO + T, :]
            rows = [jnp.sum(dc * z2, axis=0, keepdims=True), jnp.sum(dc * z1, axis=0, keepdims=True),
                    jnp.sum(dc * z0, axis=0, keepdims=True), jnp.sum(dc, axis=0, keepdims=True)]
            dw_ref[0:4, cols] += jnp.concatenate(rows, axis=0)

        for jj in range(ncol):
            cg = slice(jj * tc, (jj + 1) * tc)
            cu = slice(F + jj * tc, F + (jj + 1) * tc)
            extg, extu = ext_of(cg), ext_of(cu)
            wg, wu = w_ref[:, cg], w_ref[:, cu]
            gc = _conv3(extg, wg, b_ref[:, cg])[CONV_HALO:, :]
            uc = _conv3(extu, wu, b_ref[:, cu])[CONV_HALO:, :]
            da = jnp.concatenate([da_ref[:, cg].astype(F32), dan_ref[:, cg].astype(F32) * last], axis=0)
            gl, dgl = _gelu_parts(gc)
            back(da * uc * dgl, extg, wg, cg)
            back(da * gl, extu, wu, cu)

    halo_rows = S // CONV_HALO
    return pl.pallas_call(
        body, name=name,
        out_shape=(jax.ShapeDtypeStruct((S, F2), BF16), jax.ShapeDtypeStruct((8, F2), F32)),
        grid=(nrow,),
        in_specs=[pl.BlockSpec((T, F2), lambda i: (i, 0)),
                  pl.BlockSpec((CONV_HALO, F2), lambda i: (jnp.maximum(i * hb - 1, 0), 0)),
                  pl.BlockSpec((CONV_HALO, F2), lambda i: (jnp.minimum((i + 1) * hb, halo_rows - 1), 0)),
                  pl.BlockSpec((T, F), lambda i: (i, 0)),
                  pl.BlockSpec((CONV_HALO, F), lambda i: (jnp.minimum((i + 1) * hb, halo_rows - 1), 0)),
                  pl.BlockSpec((3, F2), lambda i: (0, 0)), pl.BlockSpec((1, F2), lambda i: (0, 0))],
        out_specs=(pl.BlockSpec((T, F2), lambda i: (i, 0)), pl.BlockSpec((8, F2), lambda i: (0, 0))),
        compiler_params=_cparams("arbitrary"),
    )(z, z, z, dact, dact, conv_w, conv_b)


OFF_QKV = POOL_W
OFF_G = POOL_W + 3 * FOX_W


SHARD_BY_ROWS = {"w_in": False, "w_pool_br": False, "w_fox_br": False, "w_mix_out": True, "w_xq": True,
                 "w_xkv": True, "w_xo": False, "w_up": False, "w_down": True, "conv_w": False}


def _prep_layer_weights(w, l):
    n, _, D, b = w["w_in"].shape
    w_in = w["w_in"][:, l].transpose(1, 0, 2).reshape(D, n * b)
    off_f = OFF_G
    pad = jnp.zeros((D, LANES - N_FGATE), w_in.dtype)
    w_in_r = jnp.concatenate([w_in[:, :off_f], w_in[:, off_f + N_FGATE:], w_in[:, off_f:off_f + N_FGATE], pad], axis=1)
    wl = {k: _ChipMajor(w[k], l, SHARD_BY_ROWS[k]) for k in MATMUL_WEIGHTS if k != "w_in"}
    wl["w_in_r"] = w_in_r
    return wl


def _row(v):
    return v.reshape(1, -1)


def _layer_fwd(x, mem, wl, p, l):
    S, D = x.shape
    n = lambda s: f"l{l}_{s}"
    sv = {"x0": x}
    h1 = _rms_fwd(x, _row(p["mix_pre_g"][l]), out_dtype=BF16, name=n("rms1"))
    w_in_r = wl["w_in_r"]
    og = OFF_G
    zu = _matmul(h1, w_in_r[:, :OFF_QKV], out_dtype=F32, name=n("mm_zu"))
    zqkv = _matmul(h1, w_in_r[:, OFF_QKV:og], out_dtype=BF16, name=n("mm_zqkv"))
    zg = _matmul(h1, w_in_r[:, og:og + 2 * D], out_dtype=F32, name=n("mm_zg"))
    zf = _matmul(h1, w_in_r[:, og + 2 * D:], out_dtype=F32, name=n("mm_zf"))
    bf = jnp.pad(p["b_forget"][l], (0, LANES - N_FGATE)).reshape(1, LANES)
    _, augq, augk = _fgate_fwd(zf, bf, name=n("fgate"))
    o_fox, lse2 = _fox_fwd(zqkv, augq, augk, name=n("fox"))
    pooled, mixed = _pool_fwd(zu, p["pool_w"][l], _row(p["pool_scale"][l]), name=n("pool"))
    y_pool = _matmul(mixed, wl["w_pool_br"], out_dtype=BF16, name=n("mm_ypool"))
    y_fox = _matmul(o_fox, wl["w_fox_br"], out_dtype=BF16, name=n("mm_yfox"))
    merged = _merge_fwd(zg, y_pool, y_fox, name=n("merge"))
    r1 = _matmul(merged, wl["w_mix_out"], out_dtype=F32, name=n("mm_r1"))
    x1 = _add_rms(x, r1, _row(p["mix_post_g"][l]), name=n("addrms1"))
    sv.update(h1=h1, zg=zg, zf=zf, bf=bf, zqkv=zqkv, augq=augq, augk=augk, o_fox=o_fox, lse2=lse2,
              pooled=pooled, mixed=mixed, y_pool=y_pool, y_fox=y_fox, merged=merged, r1=r1, x1=x1)
    h2 = _rms_fwd(x1, _row(p["xa_pre_g"][l]), out_dtype=BF16, name=n("rms2"))
    mem_n = _rms_fwd(mem, _row(p["mem_g"][l]), out_dtype=BF16, name=n("rms_mem"))
    q2 = _matmul(h2, wl["w_xq"], out_dtype=BF16, name=n("mm_q2"))
    kv = _matmul(mem_n, wl["w_xkv"], out_dtype=BF16, name=n("mm_kv"))
    o2 = _xattn_fwd(q2, kv, name=n("xattn"))
    a2 = _matmul(o2, wl["w_xo"], out_dtype=F32, name=n("mm_a2"))
    x2 = _add_rms(x1, a2, _row(p["xa_post_g"][l]), name=n("addrms2"))
    sv.update(h2=h2, mem_n=mem_n, q2=q2, kv=kv, o2=o2, a2=a2, x2=x2)
    h3 = _rms_fwd(x2, _row(p["ffn_pre_g"][l]), out_dtype=BF16, name=n("rms3"))
    z3 = _matmul(h3, wl["w_up"], out_dtype=BF16, name=n("mm_z3"))
    act = _convglu_fwd(z3, p["conv_w"][l], _row(p["conv_b"][l]), name=n("convglu"))
    d3 = _matmul(act, wl["w_down"], out_dtype=F32, name=n("mm_d3"))
    x3 = _add_rms(x2, d3, _row(p["ffn_post_g"][l]), name=n("addrms3"))
    sv.update(h3=h3, z3=z3, act=act, d3=d3)
    return x3, sv


def _layer_bwd(dx, mem, wl, p, l, sv, gbuf):
    S, D = dx.shape
    n_layers = p["mix_pre_g"].shape[0]
    n = lambda s: f"l{l}_b_{s}"
    g = {}
    red = lambda part: jnp.sum(part, axis=0)

    def wgrad(k, lhs, rhs, nm):
        how = dict(out_rows=N_CHIPS) if SHARD_BY_ROWS[k] else dict(out_chips=N_CHIPS)
        return _matmul(lhs, rhs, ta=True, out_dtype=F32, out_layer=(gbuf.get(k), l, n_layers), name=n(nm), **how)

    dd3, dg = _rms_bwd(sv["d3"], _row(p["ffn_post_g"][l]), dx, out_dtype=BF16, name=n("rms3post"))
    g["ffn_post_g"] = red(dg)
    dact = _matmul(dd3, wl["w_down"], tb=True, out_dtype=BF16, name=n("mm_dact"))
    g["w_down"] = wgrad("w_down", sv["act"], dd3, "mm_dwdown")
    dz3, dconv = _convglu_bwd(sv["z3"], dact, p["conv_w"][l], _row(p["conv_b"][l]), name=n("convglu"))
    g["conv_w"] = dconv[:3]
    g["conv_b"] = dconv[3]
    dh3 = _matmul(dz3, wl["w_up"], tb=True, out_dtype=F32, name=n("mm_dh3"))
    g["w_up"] = wgrad("w_up", sv["h3"], dz3, "mm_dwup")
    dx, dg = _rms_bwd(sv["x2"], _row(p["ffn_pre_g"][l]), dh3, dx, out_dtype=F32, name=n("rms3pre"))
    g["ffn_pre_g"] = red(dg)
    da2, dg = _rms_bwd(sv["a2"], _row(p["xa_post_g"][l]), dx, out_dtype=BF16, name=n("rms2post"))
    g["xa_post_g"] = red(dg)
    do2 = _matmul(da2, wl["w_xo"], tb=True, out_dtype=BF16, name=n("mm_do2"))
    g["w_xo"] = wgrad("w_xo", sv["o2"], da2, "mm_dwxo")
    dq2, dkv = _xattn_bwd(sv["q2"], sv["kv"], do2, name=n("xattn"))
    dh2 = _matmul(dq2, wl["w_xq"], tb=True, out_dtype=F32, name=n("mm_dh2"))
    g["w_xq"] = wgrad("w_xq", sv["h2"], dq2, "mm_dwxq")
    dmem_n = _matmul(dkv, wl["w_xkv"], tb=True, out_dtype=F32, name=n("mm_dmemn"))
    g["w_xkv"] = wgrad("w_xkv", sv["mem_n"], dkv, "mm_dwxkv")
    _, dg = _rms_bwd(mem, _row(p["mem_g"][l]), dmem_n, out_dtype=BF16, name=n("rms_mem"))
    g["mem_g"] = red(dg)
    dx, dg = _rms_bwd(sv["x1"], _row(p["xa_pre_g"][l]), dh2, dx, out_dtype=F32, name=n("rms2pre"))
    g["xa_pre_g"] = red(dg)
    dr1, dg = _rms_bwd(sv["r1"], _row(p["mix_post_g"][l]), dx, out_dtype=BF16, name=n("rms1post"))
    g["mix_post_g"] = red(dg)
    dmerged = _matmul(dr1, wl["w_mix_out"], tb=True, out_dtype=F32, name=n("mm_dmerged"))
    g["w_mix_out"] = wgrad("w_mix_out", sv["merged"], dr1, "mm_dwmo")
    dyp, dyf, dzg = _merge_bwd(sv["zg"], sv["y_pool"], sv["y_fox"], dmerged, name=n("merge"))
    dmixed = _matmul(dyp, wl["w_pool_br"], tb=True, out_dtype=F32, name=n("mm_dmixed"))
    g["w_pool_br"] = wgrad("w_pool_br", sv["mixed"], dyp, "mm_dwpb")
    dofox = _matmul(dyf, wl["w_fox_br"], tb=True, out_dtype=F32, name=n("mm_dofox"))
    g["w_fox_br"] = wgrad("w_fox_br", sv["o_fox"], dyf, "mm_dwfb")
    dzu, dpw, dsc = _pool_bwd(sv["pooled"], dmixed, p["pool_w"][l], _row(p["pool_scale"][l]), name=n("pool"))
    g["pool_w"] = dpw
    g["pool_scale"] = red(dsc)
    delta = _head_rowsum(dofox, sv["o_fox"], name=n("delta"))
    lse = sv["lse2"].reshape(S, N_PAIRS, PAIR, FOX_DH)[:, :, ::-1, 0].reshape(S, FOX_HEADS)
    dq, dk, dv, dck, dcq = _fox_bwd(sv["zqkv"], dofox.astype(BF16), sv["augq"], sv["augk"],
                                    lse.T.reshape(FOX_HEADS, 1, S), delta[:, :FOX_HEADS].T.reshape(FOX_HEADS, 1, S),
                                    name=n("fox"))
    dc = dcq.reshape(FOX_HEADS, S) + dck.reshape(FOX_HEADS, S)
    dc_pad = jnp.pad(dc.T, ((0, 0), (0, LANES - FOX_HEADS)))
    dzf, db = _fgate_bwd(sv["zf"], sv["bf"], dc_pad, name=n("fgate"))
    g["b_forget"] = red(db)[:N_FGATE]
    dz_cat = jnp.concatenate([dzu, dq.astype(BF16), dk, dv, dzg, dzf], axis=1)
    dh1 = _matmul(dz_cat, wl["w_in_r"], tb=True, out_dtype=F32, name=n("mm_dh1"))
    dw_in_r = _matmul(sv["h1"], dz_cat, ta=True, out_dtype=F32, name=n("mm_dwin"))
    og = OFF_G
    g["w_in"] = jnp.concatenate([dw_in_r[:, :og], dw_in_r[:, og + 2 * D:og + 2 * D + N_FGATE],
                                 dw_in_r[:, og:og + 2 * D]], axis=1)
    dx, dg = _rms_bwd(sv["x0"], _row(p["mix_pre_g"][l]), dh1, dx, out_dtype=F32, name=n("rms1pre"))
    g["mix_pre_g"] = red(dg)
    return dx, g


MATMUL_WEIGHTS = ("w_in", "w_pool_br", "w_fox_br", "w_mix_out", "w_xq", "w_xkv", "w_xo", "w_up", "w_down")
WEIGHT_NAMES = ("mix_pre_g", "mix_post_g", "w_in", "b_forget", "pool_w", "pool_scale", "w_pool_br", "w_fox_br",
                "w_mix_out", "xa_pre_g", "xa_post_g", "mem_g", "w_xq", "w_xkv", "w_xo", "ffn_pre_g", "ffn_post_g",
                "w_up", "conv_w", "conv_b", "w_down")


def _local_step(x, mem, loss_target, wfull, p):
    L = p["mix_pre_g"].shape[0]
    saved, wls = [], []
    h = x
    for l in range(L):
        wl = _prep_layer_weights(wfull, l)
        h, sv = _layer_fwd(h, mem, wl, p, l)
        saved.append(sv)
        wls.append(wl)
    D = x.shape[1]
    dy, sq = _loss_head(h, loss_target, name="loss_head")
    loss = 0.5 * jnp.sum(sq) / D
    grads = []
    gbuf = {}
    dx = dy
    for l in reversed(range(L)):
        dx, g = _layer_bwd(dx, mem, wls[l], p, l, saved[l], gbuf)
        gbuf = {k: g[k] for k in MATMUL_WEIGHTS if k != "w_in"}
        grads.append(g)
    grads = grads[::-1]

    def chip_major(g):
        return g.reshape(g.shape[0], N_CHIPS, g.shape[1] // N_CHIPS).transpose(1, 0, 2)

    gfull = {k: jnp.stack([grads[l][k] for l in range(L)]) for k in REPLICATED}
    gfull.update({k: jnp.stack([chip_major(grads[l][k]) for l in range(L)], axis=1) for k in ("w_in", "conv_w")})
    gfull.update(gbuf)
    return loss, dx, gfull


PACK_W = 512
PACK_ROW_ALIGN = 1024
N_CHIPS = 4
N_DEV = 8
SHARDED = (("w_in", 2), ("w_pool_br", 2), ("w_fox_br", 2), ("w_mix_out", 1), ("w_xq", 1), ("w_xkv", 1),
           ("w_xo", 2), ("w_up", 2), ("w_down", 1), ("conv_w", 2))
REPLICATED = ("mix_pre_g", "mix_post_g", "b_forget", "pool_w", "pool_scale", "xa_pre_g", "xa_post_g", "mem_g",
              "ffn_pre_g", "ffn_post_g", "conv_b")


def _round_up(n, m):
    return -(-n // m) * m


def _pack(arrs, rows):
    flat = jnp.concatenate([a.reshape(-1) for a in arrs])
    return jnp.pad(flat, (0, rows * PACK_W - flat.shape[0])).reshape(rows, PACK_W)


def _unpack(buf, shapes):
    flat = buf.reshape(-1)
    out, off = [], 0
    for s in shapes:
        n = math.prod(s)
        out.append(flat[off:off + n].reshape(s))
        off += n
    return out


ANY = pl.BlockSpec(memory_space=pl.ANY)


def _remote(send_sems, recv_sems, k, src, dst, to):
    return pltpu.make_async_remote_copy(src_ref=src, dst_ref=dst, send_sem=send_sems.at[k], recv_sem=recv_sems.at[k],
                                        device_id=to, device_id_type=MESH)


def _my_place():
    return lax.axis_index("x"), lax.axis_index("y"), lax.axis_index("c")


def _rows_per_block(a, b, cap_bytes=1024 * 1024):
    if a % 8:
        return a
    return _pick(a, max(8, cap_bytes // (4 * b) // 8 * 8), 8)


def _place_shard(w, chip, dtype, *, name):
    L, a, b = w.shape
    ta = _rows_per_block(a, b)

    def body(chip_ref, w_ref, o_ref):
        o_ref[...] = w_ref[...].astype(dtype)

    return pl.pallas_call(
        body, name=name, out_shape=jax.ShapeDtypeStruct((N_CHIPS, L, a, b), dtype),
        grid_spec=pltpu.PrefetchScalarGridSpec(
            num_scalar_prefetch=1, grid=(L, a // ta),
            in_specs=[pl.BlockSpec((None, ta, b), lambda l, i, chip_ref: (l, i, 0))],
            out_specs=pl.BlockSpec((None, None, ta, b), lambda l, i, chip_ref: (chip_ref[0], l, i, 0))),
        compiler_params=_cparams("parallel", "parallel"),
    )(chip, w)


def _layer_halves(L, c):
    assert L % 2 == 0
    return pl.ds(c * (L // 2), L // 2), pl.ds((1 - c) * (L // 2), L // 2)


def _gather_weights(bufs):
    n = len(bufs)
    L = bufs[0].shape[1]

    def body(*refs):
        outs, (send_sems, recv_sems) = refs[n:2 * n], refs[2 * n:]
        x, y, c = _my_place()
        me = 2 * x + y
        sibling = (x, y, 1 - c)
        chips = [(1 - x, y), (x, 1 - y), (1 - x, 1 - y)]
        half, other = _layer_halves(L, c)
        rc = functools.partial(_remote, send_sems, recv_sems)
        first = [rc(6 * w + k, o.at[me, half], o.at[me, half], (px, py, c))
                 for w, o in enumerate(outs) for k, (px, py) in enumerate(chips)]
        for cp in first:
            cp.start()
        passed = []
        for k, (px, py) in enumerate(chips):
            src = 2 * px + py
            for w, o in enumerate(outs):
                rc(6 * w + k, o.at[src, half], o.at[src, half], (px, py, c)).wait_recv()
                fwd = rc(6 * w + 3 + k, o.at[src, half], o.at[src, half], sibling)
                fwd.start()
                passed.append(fwd)
        for k, (px, py) in enumerate(chips):
            src = 2 * px + py
            for w, o in enumerate(outs):
                rc(6 * w + 3 + k, o.at[src, other], o.at[src, other], sibling).wait_recv()
        for cp in first + passed:
            cp.wait_send()

    return pl.pallas_call(
        body, name="gather_weights",
        out_shape=tuple(jax.ShapeDtypeStruct(b.shape, b.dtype) for b in bufs),
        in_specs=[ANY] * n, out_specs=tuple([ANY] * n), input_output_aliases={i: i for i in range(n)},
        scratch_shapes=[pltpu.SemaphoreType.DMA((6 * n,)), pltpu.SemaphoreType.DMA((6 * n,))],
    )(*bufs)


def _exchange_halves(Gs, rep):
    n = len(Gs)
    L = Gs[0].shape[1]
    RR, W = rep.shape

    def body(*refs):
        g_refs, rep_ref, ra_refs, rall_ref = refs[:n], refs[n], refs[n + 1:2 * n + 1], refs[2 * n + 1]
        send_sems, recv_sems, local_sem = refs[2 * n + 2:]
        x, y, c = _my_place()
        me = 4 * x + 2 * y + c
        rc = functools.partial(_remote, send_sems, recv_sems)
        _, other = _layer_halves(L, c)

        def peer(idx):
            px = (1 - x) if (idx >> 2) & 1 else x
            py = (1 - y) if (idx >> 1) & 1 else y
            pc = (1 - c) if idx & 1 else c
            return px, py, pc

        loc = pltpu.make_async_copy(rep_ref, rall_ref.at[me], local_sem)
        loc.start()
        cps = [rc(N_DEV + w, g.at[:, other], ra, (x, y, 1 - c)) for w, (g, ra) in enumerate(zip(g_refs, ra_refs))]
        for idx in range(1, N_DEV):
            cps.append(rc(idx, rep_ref, rall_ref.at[me], peer(idx)))
        for cp in cps:
            cp.start()
        for w, (g, ra) in enumerate(zip(g_refs, ra_refs)):
            rc(N_DEV + w, g.at[:, other], ra, (x, y, 1 - c)).wait_recv()
        for idx in range(1, N_DEV):
            px, py, pc = peer(idx)
            rc(idx, rep_ref, rall_ref.at[4 * px + 2 * py + pc], (px, py, pc)).wait_recv()
        for cp in cps:
            cp.wait_send()
        loc.wait()

    halves = tuple(jax.ShapeDtypeStruct((g.shape[0], L // 2) + g.shape[2:], F32) for g in Gs)
    out = pl.pallas_call(
        body, name="exchange_halves",
        out_shape=halves + (jax.ShapeDtypeStruct((N_DEV, RR, W), F32),),
        in_specs=[ANY] * (n + 1), out_specs=tuple([ANY] * (n + 1)),
        scratch_shapes=[pltpu.SemaphoreType.DMA((N_DEV + n,)), pltpu.SemaphoreType.DMA((N_DEV + n,)),
                        pltpu.SemaphoreType.DMA],
    )(*Gs, rep)
    return out[:n], out[n]


def _exchange_chips(As):
    n = len(As)

    def body(*refs):
        a_refs, rb_refs, (send_sems, recv_sems) = refs[:n], refs[n:2 * n], refs[2 * n:]
        x, y, c = _my_place()
        me = 2 * x + y
        chips = [(1 - x, y), (x, 1 - y), (1 - x, 1 - y)]
        rc = functools.partial(_remote, send_sems, recv_sems)
        cps = [rc(3 * w + k, a.at[2 * px + py], rb.at[k], (px, py, c))
               for w, (a, rb) in enumerate(zip(a_refs, rb_refs)) for k, (px, py) in enumerate(chips)]
        for cp in cps:
            cp.start()
        for w, (a, rb) in enumerate(zip(a_refs, rb_refs)):
            for k, (px, py) in enumerate(chips):
                rc(3 * w + k, a.at[me], rb.at[k], (px, py, c)).wait_recv()
        for cp in cps:
            cp.wait_send()

    return pl.pallas_call(
        body, name="exchange_chips",
        out_shape=tuple(jax.ShapeDtypeStruct((N_CHIPS - 1,) + a.shape[1:], a.dtype) for a in As),
        in_specs=[ANY] * n, out_specs=tuple([ANY] * n),
        scratch_shapes=[pltpu.SemaphoreType.DMA((3 * n,)), pltpu.SemaphoreType.DMA((3 * n,))],
    )(*As)


def _exchange_sibling(gs):
    n = len(gs)
    L = gs[0].shape[0]

    def body(*refs):
        g_refs, (send_sems, recv_sems) = refs[n:2 * n], refs[2 * n:]
        x, y, c = _my_place()
        half, other = _layer_halves(L, c)
        rc = functools.partial(_remote, send_sems, recv_sems)
        cps = [rc(w, g.at[half], g.at[half], (x, y, 1 - c)) for w, g in enumerate(g_refs)]
        for cp in cps:
            cp.start()
        for w, g in enumerate(g_refs):
            rc(w, g.at[other], g.at[other], (x, y, 1 - c)).wait_recv()
        for cp in cps:
            cp.wait_send()

    return pl.pallas_call(
        body, name="exchange_sibling", out_shape=tuple(jax.ShapeDtypeStruct(g.shape, F32) for g in gs),
        in_specs=[ANY] * n, out_specs=tuple([ANY] * n), input_output_aliases={i: i for i in range(n)},
        scratch_shapes=[pltpu.SemaphoreType.DMA((n,)), pltpu.SemaphoreType.DMA((n,))],
    )(*gs)


def _add_halves(G, recv, core, *, name):
    n, L, a, b = G.shape
    Lh = L // 2
    ta = _rows_per_block(a, b)

    def body(core_ref, g_ref, r_ref, o_ref, o16_ref):
        s = g_ref[...] + r_ref[...]
        o_ref[...] = s
        o16_ref[...] = s.astype(BF16)

    blk = pl.BlockSpec((None, None, ta, b), lambda p, l, i, core_ref: (p, l, i, 0))
    return pl.pallas_call(
        body, name=name,
        out_shape=(jax.ShapeDtypeStruct((n, Lh, a, b), F32), jax.ShapeDtypeStruct((n, Lh, a, b), BF16)),
        grid_spec=pltpu.PrefetchScalarGridSpec(
            num_scalar_prefetch=1, grid=(n, Lh, a // ta),
            in_specs=[pl.BlockSpec((None, None, ta, b), lambda p, l, i, core_ref: (p, core_ref[0] * Lh + l, i, 0)), blk],
            out_specs=(blk, blk)),
        compiler_params=_cparams("parallel", "parallel", "parallel"),
    )(core, G, recv)


def _sum_chips(A, rb, place, *, name):
    _, Lh, a, b = A.shape
    ta = _rows_per_block(a, b, 512 * 1024)

    def body(place_ref, a_ref, r_ref, o_ref):
        o_ref[...] = ((a_ref[...] + r_ref[0].astype(F32)) + r_ref[1].astype(F32)) + r_ref[2].astype(F32)

    return pl.pallas_call(
        body, name=name, out_shape=jax.ShapeDtypeStruct((2 * Lh, a, b), F32),
        grid_spec=pltpu.PrefetchScalarGridSpec(
            num_scalar_prefetch=1, grid=(Lh, a // ta),
            in_specs=[pl.BlockSpec((None, None, ta, b), lambda l, i, place_ref: (place_ref[0], l, i, 0)),
                      pl.BlockSpec((N_CHIPS - 1, None, ta, b), lambda l, i, place_ref: (0, l, i, 0))],
            out_specs=pl.BlockSpec((None, ta, b), lambda l, i, place_ref: (place_ref[1] * Lh + l, i, 0))),
        compiler_params=_cparams("parallel", "parallel"),
    )(place, A, rb)


def _sum_slots(a, *, name):
    n, rows, W = a.shape
    tr = _pick(rows, 512, 8)

    def body(a_ref, o_ref):
        s = a_ref[0]
        for q in range(1, n):
            s = s + a_ref[q]
        o_ref[...] = s

    return pl.pallas_call(
        body, name=name, out_shape=jax.ShapeDtypeStruct((rows, W), F32), grid=(rows // tr,),
        in_specs=[pl.BlockSpec((n, tr, W), lambda i: (0, i, 0))], out_specs=pl.BlockSpec((tr, W), lambda i: (i, 0)),
        compiler_params=_cparams("parallel"),
    )(a)


def _adamw(w, g, m, v, *, name):
    L, a, b = w.shape
    ta = _rows_per_block(a, b, 512 * 1024)

    def body(w_ref, g_ref, m_ref, v_ref, d_ref, nm_ref, nv_ref):
        gg = g_ref[...]
        nm = ADAM_B1 * m_ref[...] + (1.0 - ADAM_B1) * gg
        nv = ADAM_B2 * v_ref[...] + (1.0 - ADAM_B2) * jnp.square(gg)
        m_hat = nm / (1.0 - ADAM_B1 ** ADAM_STEP)
        v_hat = nv / (1.0 - ADAM_B2 ** ADAM_STEP)
        d_ref[...] = -ADAM_LR * (m_hat / (jnp.sqrt(v_hat) + ADAM_EPS) + ADAM_WD * w_ref[...])
        nm_ref[...] = nm
        nv_ref[...] = nv

    blk = pl.BlockSpec((None, ta, b), lambda l, i: (l, i, 0))
    shp = jax.ShapeDtypeStruct((L, a, b), F32)
    return pl.pallas_call(
        body, name=name, out_shape=(shp, shp, shp), grid=(L, a // ta),
        in_specs=[blk, blk, blk, blk], out_specs=(blk, blk, blk),
        compiler_params=_cparams("parallel", "parallel"),
    )(w, g, m, v)


INPUT_NAMES = (("x", "mem") + WEIGHT_NAMES + ("loss_target",) + tuple("m_" + n for n in WEIGHT_NAMES)
               + tuple("v_" + n for n in WEIGHT_NAMES))


def kernel(*args):
    a = dict(zip(INPUT_NAMES, args, strict=True))
    x, mem, target = a["x"][0], a["mem"][0], a["loss_target"][0]
    sh_names = list(SHARD_BY_ROWS)
    core = lax.axis_index("c").astype(jnp.int32)
    chip = (2 * lax.axis_index("x") + lax.axis_index("y")).astype(jnp.int32)
    place = jnp.stack([chip, core])

    placed = [_place_shard(a[n], chip.reshape(1), F32 if n == "conv_w" else BF16, name="place_" + n) for n in sh_names]
    wfull = dict(zip(sh_names, _gather_weights(placed)))
    p = {n: a[n] for n in REPLICATED}
    cw = wfull.pop("conv_w")
    p["conv_w"] = cw.transpose(1, 2, 0, 3).reshape(cw.shape[1], cw.shape[2], N_CHIPS * cw.shape[3])

    loss, dx, gfull = _local_step(x, mem, target, wfull, p)
    loss = lax.psum(loss, ("x", "y", "c"))

    Gs = [gfull[n] for n in sh_names]
    rep_shapes = [a[n].shape for n in REPLICATED]
    rows_r = _round_up(-(-sum(math.prod(s) for s in rep_shapes) // PACK_W), 64)
    rep = _pack([gfull[n] for n in REPLICATED], rows_r)
    recvs, repall = _exchange_halves(Gs, rep)
    As = [_add_halves(g, r, core.reshape(1), name="add_halves_" + n) for n, g, r in zip(sh_names, Gs, recvs)]
    rbs = _exchange_chips([a16 for _, a16 in As])
    gsh = _exchange_sibling([_sum_chips(A, rb, place, name="sum_chips_" + n)
                             for n, (A, _), rb in zip(sh_names, As, rbs)])
    grep = _sum_slots(repall, name="sum_devices")

    got = {"g": dict(zip(sh_names, gsh)), "d": {}, "m": {}, "v": {}}
    for n, g in zip(sh_names, gsh):
        got["d"][n], got["m"][n], got["v"][n] = _adamw(a[n], g, a["m_" + n], a["v_" + n], name="adamw_" + n)
    packed = [_pack([a[pre + n] for n in REPLICATED], rows_r)[None] for pre in ("", "m_", "v_")]
    d_r, m_r, v_r = _adamw(packed[0], grep[None], packed[1], packed[2], name="adamw_replicated")
    for key, buf in (("g", grep), ("d", d_r[0]), ("m", m_r[0]), ("v", v_r[0])):
        got[key].update(zip(REPLICATED, _unpack(buf, rep_shapes)))
    outs = [got[key][n] for key in ("g", "d", "m", "v") for n in WEIGHT_NAMES]
    return (loss, dx[None], *outs)
```

```python
import functools
import math

import jax
import jax.numpy as jnp
from jax import lax
from jax.experimental import pallas as pl
from jax.experimental.pallas import tpu as pltpu

F32 = jnp.float32
BF16 = jnp.bfloat16
MESH = pl.DeviceIdType.MESH

RMS_EPS = 1e-6
POOL_WINDOWS = (2, 4, 8, 16)
POOL_GROUP = 128
POOL_W = 512
FOX_HEADS = 8
FOX_DH = 64
FOX_W = 512
X_HEADS = 4
X_DH = 128
X_W = 512
N_FGATE = 8
LANES = 128
HALO = 16

ADAM_LR = 0.001
ADAM_B1 = 0.9
ADAM_B2 = 0.999
ADAM_EPS = 1e-08
ADAM_WD = 0.01
ADAM_STEP = 10

VMEM_LIMIT_BYTES = 56 * 1024 * 1024
MATMUL_VMEM_BUDGET = 44 * 1024 * 1024


def _cparams(*sem):
    return pltpu.CompilerParams(dimension_semantics=sem, vmem_limit_bytes=VMEM_LIMIT_BYTES)


def _pick(n, cap, align=LANES):
    if n <= cap:
        return n
    best = None
    for t in range(align, cap + 1, align):
        if n % t == 0:
            best = t
    assert best is not None, (n, cap, align)
    return best


def _sigmoid(x):
    return 1.0 / (1.0 + jnp.exp(-x))


class _ChipMajor:
    def __init__(self, arr, layer, by_rows):
        self.arr, self.layer, self.by_rows = arr, layer, by_rows
        n, _, a, b = arr.shape
        self.n_chips, self.per_chip = n, (a if by_rows else b)
        self.shape = (n * a, b) if by_rows else (a, n * b)


SPAN_CHIPS_BELOW = 512


def _matmul(a, b, *, ta=False, tb=False, out_dtype=F32, out_chips=None, out_rows=None, out_layer=None, name):
    view = b if isinstance(b, _ChipMajor) else None
    if ta:
        K, M = a.shape
    else:
        M, K = a.shape
    if tb:
        N, Kb = b.shape
    else:
        Kb, N = b.shape
    assert K == Kb, (a.shape, b.shape, ta, tb)
    by_rows = view is not None and view.by_rows
    by_cols = view is not None and not view.by_rows
    b_itemsize = (view.arr if view is not None else b).dtype.itemsize
    span_b = by_cols and view.per_chip < SPAN_CHIPS_BELOW
    span_o = bool(out_chips) and N // out_chips < SPAN_CHIPS_BELOW
    if by_rows and tb:
        tn = N
    elif by_cols and not tb:
        tn = N if span_b else _pick(view.per_chip, 1408)
    elif out_chips:
        tn = N if span_o else _pick(N // out_chips, 1408)
    elif out_rows:
        tn = _pick(N, 512)
    else:
        tn = _pick(N, 1408)
    tm = M if out_rows else _pick(M, 1024 if tn <= 1024 else 512)
    if by_rows and not tb:
        tk = K
    elif by_cols and tb:
        tk = K if span_b else (view.per_chip if view.per_chip <= 2048 else _pick(view.per_chip, 1024))
    else:
        ab, bb, ob = a.dtype.itemsize, b_itemsize, jnp.dtype(out_dtype).itemsize
        for cap in (K, 2048, 1024, 512, 128):
            tk = _pick(K, cap)
            need = (2 * tk * (tm * ab + tn * bb) + (2 * ob + 4 + (4 if tk < K else 0)) * tm * tn)
            if need <= MATMUL_VMEM_BUDGET:
                break
    nk = K // tk
    dims = (((0 if ta else 1,), (1 if tb else 0,)), ((), ()))

    aliased = out_layer is not None and out_layer[0] is not None

    def body(a_ref, b_ref, *rest):
        o_ref, scratch = (rest[1], rest[2:]) if aliased else (rest[0], rest[1:])
        bt = b_ref[...]
        if by_rows:
            bt = bt.reshape(bt.shape[0] * bt.shape[1], bt.shape[2])
        elif span_b:
            bt = jnp.concatenate([bt[q] for q in range(view.n_chips)], axis=1)
        p = lax.dot_general(a_ref[...].astype(BF16), bt.astype(BF16), dims, preferred_element_type=F32)

        def store(val):
            if span_o:
                w = N // out_chips
                for q in range(out_chips):
                    o_ref[q] = val[:, q * w:(q + 1) * w].astype(out_dtype)
            elif out_rows:
                h = M // out_rows
                for q in range(out_rows):
                    o_ref[q] = val[q * h:(q + 1) * h, :].astype(out_dtype)
            else:
                o_ref[...] = val.astype(out_dtype)

        if nk == 1:
            store(p)
        else:
            acc_ref, = scratch
            k = pl.program_id(2)

            @pl.when(k == 0)
            def _():
                acc_ref[...] = p

            @pl.when(k > 0)
            def _():
                acc_ref[...] += p

            @pl.when(k == nk - 1)
            def _():
                store(acc_ref[...])

    a_spec = pl.BlockSpec((tk, tm), lambda i, j, k: (k, i)) if ta else pl.BlockSpec((tm, tk), lambda i, j, k: (i, k))
    b_tile = (tn, tk) if tb else (tk, tn)
    b_rc = (lambda i, j, k: (j, k)) if tb else (lambda i, j, k: (k, j))
    if view is None:
        b_arr = b
        b_spec = pl.BlockSpec(b_tile, b_rc)
    elif by_rows:
        b_arr = view.arr
        assert b_tile[0] == view.shape[0]
        b_spec = pl.BlockSpec((view.n_chips, None, view.arr.shape[2], b_tile[1]),
                              lambda i, j, k: (0, view.layer, 0, b_rc(i, j, k)[1]))
    elif span_b:
        b_arr = view.arr
        assert b_tile[1] == view.shape[1]
        b_spec = pl.BlockSpec((view.n_chips, None, b_tile[0], view.per_chip),
                              lambda i, j, k: (0, view.layer, b_rc(i, j, k)[0], 0))
    else:
        b_arr = view.arr
        per = view.per_chip // b_tile[1]
        b_spec = pl.BlockSpec((None, None) + b_tile,
                              lambda i, j, k: (b_rc(i, j, k)[1] // per, view.layer, b_rc(i, j, k)[0],
                                               b_rc(i, j, k)[1] % per))
    if span_o:
        o_full, o_blk = (out_chips, M, N // out_chips), (out_chips, tm, N // out_chips)
        o_idx = lambda i, j, k: (0, i, 0)
    elif out_chips:
        per_o = (N // out_chips) // tn
        o_full, o_blk = (out_chips, M, N // out_chips), (None, tm, tn)
        o_idx = lambda i, j, k: (j // per_o, i, j % per_o)
    elif out_rows:
        o_full, o_blk = (out_rows, M // out_rows, N), (out_rows, M // out_rows, tn)
        o_idx = lambda i, j, k: (0, 0, j)
    else:
        o_full, o_blk = (M, N), (tm, tn)
        o_idx = lambda i, j, k: (i, j)
    if out_layer is not None:
        _, layer, n_layers = out_layer
        o_full, o_blk = o_full[:1] + (n_layers,) + o_full[1:], o_blk[:1] + (None,) + o_blk[1:]
        o_idx = functools.partial(lambda f, i, j, k: (f(i, j, k)[0], layer) + f(i, j, k)[1:], o_idx)
    out_shape = jax.ShapeDtypeStruct(o_full, out_dtype)
    out_spec = pl.BlockSpec(o_blk, o_idx)
    grid = (M // tm, N // tn, nk)
    specs = [a_spec, b_spec, out_spec]
    a_bytes, b_bytes = M * K * a.dtype.itemsize, K * N * b_itemsize
    if nk == 1 and b_bytes + a_bytes * grid[1] < a_bytes + b_bytes * grid[0]:
        grid = (grid[1], grid[0], nk)
        specs = [pl.BlockSpec(s.block_shape, functools.partial(lambda f, j, i, k: f(i, j, k), s.index_map))
                 for s in specs]
    extra = (out_layer[0],) if aliased else ()
    return pl.pallas_call(
        body, name=name, out_shape=out_shape, grid=grid,
        in_specs=specs[:2] + [pl.BlockSpec(memory_space=pl.ANY)] * len(extra), out_specs=specs[2],
        input_output_aliases={2: 0} if aliased else {},
        scratch_shapes=[pltpu.VMEM((tm, tn), F32)] if nk > 1 else [],
        compiler_params=_cparams("parallel", "parallel", "arbitrary"),
    )(a, b_arr, *extra)


def _row_block(S, D, cap_bytes=2 * 1024 * 1024):
    ts = max(8, min(S, cap_bytes // (4 * D)))
    return _pick(S, ts, 8)


def _rms_fwd(x, g, *, out_dtype, name):
    S, D = x.shape
    ts = _row_block(S, D)

    def body(x_ref, g_ref, o_ref):
        xf = x_ref[...]
        r = lax.rsqrt(jnp.mean(xf * xf, axis=-1, keepdims=True) + RMS_EPS)
        o_ref[...] = (xf * r * g_ref[...]).astype(out_dtype)

    return pl.pallas_call(
        body, name=name, out_shape=jax.ShapeDtypeStruct((S, D), out_dtype), grid=(S // ts,),
        in_specs=[pl.BlockSpec((ts, D), lambda i: (i, 0)), pl.BlockSpec((1, D), lambda i: (0, 0))],
        out_specs=pl.BlockSpec((ts, D), lambda i: (i, 0)),
        compiler_params=_cparams("parallel"),
    )(x, g)


def _add_rms(x, r, g, g_next=None, *, name):
    S, D = x.shape
    ts = _row_block(S, D)
    fused = g_next is not None

    def body(*refs):
        x_ref, r_ref, g_ref = refs[:3]
        rf = r_ref[...].astype(F32)
        s = lax.rsqrt(jnp.mean(rf * rf, axis=-1, keepdims=True) + RMS_EPS)
        y = x_ref[...] + rf * s * g_ref[...]
        if fused:
            gn_ref, o_ref, h_ref = refs[3:]
            t = lax.rsqrt(jnp.mean(y * y, axis=-1, keepdims=True) + RMS_EPS)
            h_ref[...] = (y * t * gn_ref[...]).astype(BF16)
        else:
            o_ref, = refs[3:]
        o_ref[...] = y

    row = pl.BlockSpec((ts, D), lambda i: (i, 0))
    vec = pl.BlockSpec((1, D), lambda i: (0, 0))
    out = pl.pallas_call(
        body, name=name,
        out_shape=(jax.ShapeDtypeStruct((S, D), F32),) + ((jax.ShapeDtypeStruct((S, D), BF16),) if fused else ()),
        grid=(S // ts,),
        in_specs=[row, row, vec] + ([vec] if fused else []),
        out_specs=(row,) + ((row,) if fused else ()),
        compiler_params=_cparams("parallel"),
    )(*((x, r, g) + ((g_next,) if fused else ())))
    return out if fused else out[0]


def _rms_bwd(x, g, dy, res=None, *, out_dtype, name):
    S, D = x.shape
    ts = _row_block(S, D)
    has_res = res is not None

    def body(*refs):
        if has_res:
            x_ref, g_ref, dy_ref, res_ref, dx_ref, dg_ref = refs
        else:
            x_ref, g_ref, dy_ref, dx_ref, dg_ref = refs
        i = pl.program_id(0)
        xf = x_ref[...].astype(F32)
        dyf = dy_ref[...].astype(F32)
        r = lax.rsqrt(jnp.mean(xf * xf, axis=-1, keepdims=True) + RMS_EPS)
        n = xf * r
        dn = dyf * g_ref[...]
        dx = r * (dn - n * jnp.mean(dn * n, axis=-1, keepdims=True))
        if has_res:
            dx = dx + res_ref[...]
        dx_ref[...] = dx.astype(out_dtype)
        part = jnp.sum((dyf * n).reshape(ts // 8, 8, D), axis=0)

        @pl.when(i == 0)
        def _():
            dg_ref[...] = part

        @pl.when(i > 0)
        def _():
            dg_ref[...] += part

    row = pl.BlockSpec((ts, D), lambda i: (i, 0))
    in_specs = [row, pl.BlockSpec((1, D), lambda i: (0, 0)), row] + ([row] if has_res else [])
    args = (x, g, dy) + ((res,) if has_res else ())
    dx, dg = pl.pallas_call(
        body, name=name,
        out_shape=(jax.ShapeDtypeStruct((S, D), out_dtype), jax.ShapeDtypeStruct((8, D), F32)),
        grid=(S // ts,), in_specs=in_specs,
        out_specs=(row, pl.BlockSpec((8, D), lambda i: (0, 0))),
        compiler_params=_cparams("arbitrary"),
    )(*args)
    return dx, dg


def _loss_head(y, t, *, name):
    S, D = y.shape
    ts = _row_block(S, D)

    def body(y_ref, t_ref, dy_ref, sq_ref):
        i = pl.program_id(0)
        e = y_ref[...] - t_ref[...]
        dy_ref[...] = e / D
        part = jnp.sum((e * e).reshape(ts // 8, 8, D), axis=0)

        @pl.when(i == 0)
        def _():
            sq_ref[...] = part

        @pl.when(i > 0)
        def _():
            sq_ref[...] += part

    row = pl.BlockSpec((ts, D), lambda i: (i, 0))
    return pl.pallas_call(
        body, name=name,
        out_shape=(jax.ShapeDtypeStruct((S, D), F32), jax.ShapeDtypeStruct((8, D), F32)),
        grid=(S // ts,), in_specs=[row, row],
        out_specs=(row, pl.BlockSpec((8, D), lambda i: (0, 0))),
        compiler_params=_cparams("arbitrary"),
    )(y, t)


def _window_counts(i, T, w):
    t = i * T + lax.broadcasted_iota(jnp.int32, (T, 1), 0)
    return jnp.minimum(t + 1, w).astype(F32)


def _pool_fwd(zu, pool_w, pool_scale, *, name):
    S, W = zu.shape
    T = _pick(S, 1024, 8)

    def body(u_ref, pw_ref, sc_ref, pooled_ref, mixed_ref, halo_ref):
        i = pl.program_id(0)

        @pl.when(i == 0)
        def _():
            halo_ref[...] = jnp.zeros_like(halo_ref)

        u = u_ref[...]
        ext = jnp.concatenate([halo_ref[...], u], axis=0)
        halo_ref[...] = u[T - HALO:, :]
        for g, w in enumerate(POOL_WINDOWS):
            cols = slice(g * POOL_GROUP, (g + 1) * POOL_GROUP)
            s = ext[:, cols]
            sh = 1
            while sh < w:
                s = s + pltpu.roll(s, sh, 0)
                sh *= 2
            pooled = s[HALO:, :] / _window_counts(i, T, w) - u[:, cols]
            pooled_bf = pooled.astype(BF16)
            pm = jnp.dot(pooled_bf, pw_ref[g].astype(BF16), preferred_element_type=F32)
            pooled_ref[:, cols] = pooled_bf
            mixed_ref[:, cols] = (pm * sc_ref[:, cols]).astype(BF16)

    row = pl.BlockSpec((T, W), lambda i: (i, 0))
    return pl.pallas_call(
        body, name=name,
        out_shape=(jax.ShapeDtypeStruct((S, W), BF16), jax.ShapeDtypeStruct((S, W), BF16)),
        grid=(S // T,),
        in_specs=[row, pl.BlockSpec(pool_w.shape, lambda i: (0, 0, 0)), pl.BlockSpec((1, W), lambda i: (0, 0))],
        out_specs=(row, row),
        scratch_shapes=[pltpu.VMEM((HALO, W), F32)],
        compiler_params=_cparams("arbitrary"),
    )(zu, pool_w, pool_scale)


def _pool_bwd(pooled, dmixed, pool_w, pool_scale, *, name):
    S, W = pooled.shape
    T = _pick(S, 1024, 8)
    nb = S // T

    def body(p_ref, dm_ref, pw_ref, sc_ref, dzu_ref, dpw_ref, dsc_ref, halo_ref):
        i = pl.program_id(0)
        blk = nb - 1 - i

        @pl.when(i == 0)
        def _():
            halo_ref[...] = jnp.zeros_like(halo_ref)
            dpw_ref[...] = jnp.zeros_like(dpw_ref)
            dsc_ref[...] = jnp.zeros_like(dsc_ref)

        for g, w in enumerate(POOL_WINDOWS):
            cols = slice(g * POOL_GROUP, (g + 1) * POOL_GROUP)
            p = p_ref[:, cols]
            dm = dm_ref[:, cols]
            pw = pw_ref[g].astype(BF16)
            pm = jnp.dot(p, pw, preferred_element_type=F32)
            dsc_ref[:, cols] += jnp.sum((dm * pm).reshape(T // 8, 8, POOL_GROUP), axis=0)
            dpm = (dm * sc_ref[:, cols]).astype(BF16)
            dpw_ref[g] += lax.dot_general(p, dpm, (((0,), (0,)), ((), ())), preferred_element_type=F32)
            dpooled = lax.dot_general(dpm, pw, (((1,), (1,)), ((), ())), preferred_element_type=F32)
            e = dpooled / _window_counts(blk, T, w)
            ext = jnp.concatenate([e, halo_ref[:, cols]], axis=0)
            halo_ref[:, cols] = e[:HALO, :]
            s = ext
            sh = 1
            while sh < w:
                s = s + pltpu.roll(s, T + HALO - sh, 0)
                sh *= 2
            dzu_ref[:, cols] = (s[:T, :] - dpooled).astype(BF16)

    row = pl.BlockSpec((T, W), lambda i: (nb - 1 - i, 0))
    return pl.pallas_call(
        body, name=name,
        out_shape=(jax.ShapeDtypeStruct((S, W), BF16), jax.ShapeDtypeStruct(pool_w.shape, F32),
                   jax.ShapeDtypeStruct((8, W), F32)),
        grid=(nb,),
        in_specs=[row, row, pl.BlockSpec(pool_w.shape, lambda i: (0, 0, 0)), pl.BlockSpec((1, W), lambda i: (0, 0))],
        out_specs=(row, pl.BlockSpec(pool_w.shape, lambda i: (0, 0, 0)), pl.BlockSpec((8, W), lambda i: (0, 0))),
        scratch_shapes=[pltpu.VMEM((HALO, W), F32)],
        compiler_params=_cparams("arbitrary"),
    )(pooled, dmixed, pool_w, pool_scale)


def _fgate_fwd(zf, bf, *, name):
    S, W = zf.shape
    T = _pick(S, 512, 8)

    def body(z_ref, b_ref, c_ref, aq_ref, ak_ref, carry_ref):
        i = pl.program_id(0)

        @pl.when(i == 0)
        def _():
            carry_ref[...] = jnp.zeros_like(carry_ref)

        a = z_ref[...] + b_ref[...]
        s = jnp.minimum(a, 0.0) - jnp.log(1.0 + jnp.exp(-jnp.abs(a)))
        row = lax.broadcasted_iota(jnp.int32, (T, W), 0)
        sh = 1
        while sh < T:
            s = s + jnp.where(row >= sh, pltpu.roll(s, sh, 0), 0.0)
            sh *= 2
        c = s + carry_ref[0:1, :]
        c_ref[...] = c
        carry_ref[...] = jnp.broadcast_to(c[T - 1:T, :], carry_ref.shape)
        lane = lax.broadcasted_iota(jnp.int32, (T, W), 1)
        for h in range(FOX_HEADS):
            ch = c[:, h:h + 1]
            hi = ch.astype(BF16).astype(F32)
            r1 = ch - hi
            lo = r1.astype(BF16).astype(F32)
            lo2 = (r1 - lo).astype(BF16).astype(F32)
            aq = jnp.where(lane == 0, hi, jnp.where(lane == 1, lo, jnp.where(lane == 2, lo2,
                                                                              jnp.where(lane < 6, 1.0, 0.0))))
            ak = jnp.where(lane < 3, 1.0, jnp.where(lane == 3, -hi, jnp.where(lane == 4, -lo,
                                                                               jnp.where(lane == 5, -lo2, 0.0))))
            aq_ref[h] = aq.astype(BF16)
            ak_ref[h] = ak.astype(BF16)

    aug = jax.ShapeDtypeStruct((FOX_HEADS, S, W), BF16)
    aug_spec = pl.BlockSpec((FOX_HEADS, T, W), lambda i: (0, i, 0))
    return pl.pallas_call(
        body, name=name, out_shape=(jax.ShapeDtypeStruct((S, W), F32), aug, aug), grid=(S // T,),
        in_specs=[pl.BlockSpec((T, W), lambda i: (i, 0)), pl.BlockSpec((1, W), lambda i: (0, 0))],
        out_specs=(pl.BlockSpec((T, W), lambda i: (i, 0)), aug_spec, aug_spec),
        scratch_shapes=[pltpu.VMEM((8, W), F32)],
        compiler_params=_cparams("arbitrary"),
    )(zf, bf)


def _fgate_bwd(zf, bf, dc, *, name):
    S, W = zf.shape
    T = _pick(S, 512, 8)
    nb = S // T

    def body(z_ref, b_ref, dc_ref, dz_ref, db_ref, carry_ref):
        i = pl.program_id(0)

        @pl.when(i == 0)
        def _():
            carry_ref[...] = jnp.zeros_like(carry_ref)
            db_ref[...] = jnp.zeros_like(db_ref)

        s = dc_ref[...]
        row = lax.broadcasted_iota(jnp.int32, (T, W), 0)
        sh = 1
        while sh < T:
            s = s + jnp.where(row < T - sh, pltpu.roll(s, T - sh, 0), 0.0)
            sh *= 2
        dlf = s + carry_ref[0:1, :]
        carry_ref[...] = jnp.broadcast_to(dlf[0:1, :], carry_ref.shape)
        dz = dlf * (1.0 - _sigmoid(z_ref[...] + b_ref[...]))
        dz_ref[...] = dz.astype(BF16)
        db_ref[...] += jnp.sum(dz.reshape(T // 8, 8, W), axis=0)

    row_spec = pl.BlockSpec((T, W), lambda i: (nb - 1 - i, 0))
    return pl.pallas_call(
        body, name=name,
        out_shape=(jax.ShapeDtypeStruct((S, W), BF16), jax.ShapeDtypeStruct((8, W), F32)),
        grid=(nb,),
        in_specs=[row_spec, pl.BlockSpec((1, W), lambda i: (0, 0)), row_spec],
        out_specs=(row_spec, pl.BlockSpec((8, W), lambda i: (0, 0))),
        scratch_shapes=[pltpu.VMEM((8, W), F32)],
        compiler_params=_cparams("arbitrary"),
    )(zf, bf, dc)


def _merge_fwd(zg, yp, yf, *, name):
    S, D = yp.shape
    ts = _row_block(S, D, 1024 * 1024)

    def body(zg_ref, yp_ref, yf_ref, o_ref):
        zg = zg_ref[...].astype(F32)
        o_ref[...] = (_sigmoid(zg[:, :D]) * yp_ref[...] + _sigmoid(zg[:, D:]) * yf_ref[...]).astype(BF16)

    row = pl.BlockSpec((ts, D), lambda i: (i, 0))
    return pl.pallas_call(
        body, name=name, out_shape=jax.ShapeDtypeStruct((S, D), BF16), grid=(S // ts,),
        in_specs=[pl.BlockSpec((ts, 2 * D), lambda i: (i, 0)), row, row], out_specs=row,
        compiler_params=_cparams("parallel"),
    )(zg, yp, yf)


def _merge_bwd(zg, yp, yf, dmerged, *, name):
    S, D = yp.shape
    ts = _row_block(S, D, 1024 * 1024)

    def body(zg_ref, yp_ref, yf_ref, dm_ref, dyp_ref, dyf_ref, dzg_ref):
        dm = dm_ref[...].astype(F32)
        zg = zg_ref[...].astype(F32)
        sp = _sigmoid(zg[:, :D])
        sf = _sigmoid(zg[:, D:])
        dyp_ref[...] = (dm * sp).astype(BF16)
        dyf_ref[...] = (dm * sf).astype(BF16)
        dzg_ref[:, :D] = (dm * yp_ref[...] * (sp * (1.0 - sp))).astype(BF16)
        dzg_ref[:, D:] = (dm * yf_ref[...] * (sf * (1.0 - sf))).astype(BF16)

    row = pl.BlockSpec((ts, D), lambda i: (i, 0))
    wide = pl.BlockSpec((ts, 2 * D), lambda i: (i, 0))
    return pl.pallas_call(
        body, name=name,
        out_shape=(jax.ShapeDtypeStruct((S, D), BF16), jax.ShapeDtypeStruct((S, D), BF16),
                   jax.ShapeDtypeStruct((S, 2 * D), BF16)),
        grid=(S // ts,), in_specs=[wide, row, row, row], out_specs=(row, row, wide),
        compiler_params=_cparams("parallel"),
    )(zg, yp, yf, dmerged)


NEG_BIG = -1e30


FOX_BLOCK = 1024
PAIR = LANES // FOX_DH
N_PAIRS = FOX_HEADS // PAIR


def _fox_fwd(zqkv, augq, augk, *, name):
    S = zqkv.shape[0]
    bq = _pick(S, FOX_BLOCK, 128)
    nq = S // bq
    scale = 1.0 / math.sqrt(FOX_DH)

    def body(q_ref, k_ref, v_ref, aq_ref, ak_ref, o_ref, lse_ref):
        i = pl.program_id(1)
        lane = lax.broadcasted_iota(jnp.int32, (1, LANES), 1)
        first = lane < FOX_DH
        q2 = q_ref[...] * scale
        zero = jnp.zeros_like(q2)
        qh = (jnp.concatenate([jnp.where(first, q2, zero), aq_ref[0]], axis=1),
              jnp.concatenate([jnp.where(first, zero, q2), aq_ref[1]], axis=1))

        def step(j, carry, masked):
            start = pl.multiple_of(j * bq, bq)
            kb = k_ref[pl.ds(start, bq), :]
            vb = v_ref[pl.ds(start, bq), :]
            one = jnp.ones_like(vb)
            vh = (jnp.where(first, vb, one), jnp.where(first, one, vb))
            out = []
            for h in range(PAIR):
                m, acc = carry[h]
                kh = jnp.concatenate([kb, ak_ref[h, pl.ds(start, bq), :]], axis=1)
                s = lax.dot_general(qh[h], kh, (((1,), (1,)), ((), ())), preferred_element_type=F32)
                if masked:
                    r = lax.broadcasted_iota(jnp.int32, (bq, bq), 0)
                    c = lax.broadcasted_iota(jnp.int32, (bq, bq), 1)
                    s = jnp.where(c <= r, s, NEG_BIG)
                m_new = jnp.maximum(m, jnp.max(s, axis=-1, keepdims=True))
                alpha = jnp.exp(m - m_new)
                p = jnp.exp(s - m_new).astype(BF16)
                acc = alpha * acc + jnp.dot(p, vh[h], preferred_element_type=F32)
                out.append((m_new, acc))
            return tuple(out)

        init = tuple((jnp.full((bq, 1), NEG_BIG, F32), jnp.zeros((bq, LANES), F32)) for _ in range(PAIR))
        carry = lax.fori_loop(0, i, lambda j, c: step(j, c, False), init)
        (ma, acca), (mb, accb) = step(i, carry, True)
        num = jnp.where(first, acca, accb)
        den = jnp.where(first, pltpu.roll(acca, FOX_DH, 1), pltpu.roll(accb, FOX_DH, 1))
        o_ref[...] = num / den
        lse_ref[...] = jnp.where(first, mb, ma) + jnp.log(jnp.where(first, accb, acca))

    npair = N_PAIRS
    return pl.pallas_call(
        body, name=name,
        out_shape=(jax.ShapeDtypeStruct((S, FOX_W), F32), jax.ShapeDtypeStruct((S, FOX_W), F32)),
        grid=(npair, nq),
        in_specs=[pl.BlockSpec((bq, LANES), lambda hp, i: (i, hp)),
                  pl.BlockSpec((S, LANES), lambda hp, i: (0, npair + hp)),
                  pl.BlockSpec((S, LANES), lambda hp, i: (0, 2 * npair + hp)),
                  pl.BlockSpec((PAIR, bq, LANES), lambda hp, i: (hp, i, 0)),
                  pl.BlockSpec((PAIR, S, LANES), lambda hp, i: (hp, 0, 0))],
        out_specs=(pl.BlockSpec((bq, LANES), lambda hp, i: (i, hp)),
                   pl.BlockSpec((bq, LANES), lambda hp, i: (i, hp))),
        compiler_params=_cparams("parallel", "arbitrary"),
    )(zqkv, zqkv, zqkv, augq, augk)


def _head_rowsum(a, b, *, name):
    S, W = a.shape
    ts = _pick(S, 1024, 8)

    def body(a_ref, b_ref, o_ref):
        prod = a_ref[...].astype(F32) * b_ref[...].astype(F32)
        hi = prod.astype(BF16)
        lo = (prod - hi.astype(F32)).astype(BF16)
        r = lax.broadcasted_iota(jnp.int32, (W, LANES), 0)
        c = lax.broadcasted_iota(jnp.int32, (W, LANES), 1)
        sel = jnp.where(r // FOX_DH == c, 1.0, 0.0).astype(BF16)
        o_ref[...] = (jnp.dot(hi, sel, preferred_element_type=F32) + jnp.dot(lo, sel, preferred_element_type=F32))

    return pl.pallas_call(
        body, name=name, out_shape=jax.ShapeDtypeStruct((S, LANES), F32), grid=(S // ts,),
        in_specs=[pl.BlockSpec((ts, W), lambda i: (i, 0)), pl.BlockSpec((ts, W), lambda i: (i, 0))],
        out_specs=pl.BlockSpec((ts, LANES), lambda i: (i, 0)),
        compiler_params=_cparams("parallel"),
    )(a, b)


def _fox_bwd(zqkv, do, augq, augk, lse_row, delta_row, *, name):
    S = zqkv.shape[0]
    bk = _pick(S, FOX_BLOCK, 128)
    nk = S // bk
    scale = 1.0 / math.sqrt(FOX_DH)
    npair = N_PAIRS

    def body(q_ref, k_ref, v_ref, do_ref, ak_ref, aq_ref, lse_ref, dl_ref, dq_ref, dk_ref, dv_ref, dck_ref, dcq_ref):
        j = pl.program_id(1)

        @pl.when(j == 0)
        def _():
            dq_ref[...] = jnp.zeros_like(dq_ref)
            dcq_ref[...] = jnp.zeros_like(dcq_ref)

        lane = lax.broadcasted_iota(jnp.int32, (1, LANES), 1)
        first = lane < FOX_DH
        kb = k_ref[...]
        vb = v_ref[...]
        kh = (jnp.concatenate([kb, ak_ref[0]], axis=1), jnp.concatenate([kb, ak_ref[1]], axis=1))

        def step(i, carry, masked):
            start = pl.multiple_of(i * bk, bk)
            qs = q_ref[pl.ds(start, bk), :] * scale
            dob = do_ref[pl.ds(start, bk), :]
            zero = jnp.zeros_like(qs)
            qh = (jnp.where(first, qs, zero), jnp.where(first, zero, qs))
            doh = (jnp.where(first, dob, zero), jnp.where(first, zero, dob))
            out = []
            dqc = []
            for h in range(PAIR):
                dk, dv, dc = carry[h]
                qaug = jnp.concatenate([qh[h], aq_ref[h, pl.ds(start, bk), :]], axis=1)
                lse = lse_ref[h, :, pl.ds(start, bk)]
                dl = dl_ref[h, :, pl.ds(start, bk)]
                st = lax.dot_general(kh[h], qaug, (((1,), (1,)), ((), ())), preferred_element_type=F32)
                pt = jnp.exp(st - lse)
                if masked:
                    r = lax.broadcasted_iota(jnp.int32, (bk, bk), 0)
                    c = lax.broadcasted_iota(jnp.int32, (bk, bk), 1)
                    pt = jnp.where(c >= r, pt, 0.0)
                dpt = lax.dot_general(vb, doh[h], (((1,), (1,)), ((), ())), preferred_element_type=F32)
                dst = pt * (dpt - dl)
                pt_bf = pt.astype(BF16)
                dst_bf = dst.astype(BF16)
                dv = dv + jnp.dot(pt_bf, dob, preferred_element_type=F32)
                dk = dk + jnp.dot(dst_bf, qs, preferred_element_type=F32)
                dc = dc - jnp.sum(dst, axis=-1, keepdims=True)
                dcq_ref[h, :, pl.ds(start, bk)] += jnp.sum(dst, axis=0, keepdims=True)
                dqc.append(lax.dot_general(dst_bf, kb, (((0,), (0,)), ((), ())), preferred_element_type=F32))
                out.append((dk, dv, dc))
            dq_ref[pl.ds(start, bk), :] += jnp.where(first, dqc[0], dqc[1])
            return tuple(out)

        init = tuple((jnp.zeros((bk, LANES), F32), jnp.zeros((bk, LANES), F32), jnp.zeros((bk, 1), F32))
                     for _ in range(PAIR))
        carry = step(j, init, True)
        (dka, dva, dca), (dkb, dvb, dcb) = lax.fori_loop(j + 1, nk, lambda i, c: step(i, c, False), carry)
        dk_ref[...] = jnp.where(first, dka, dkb).astype(BF16)
        dv_ref[...] = jnp.where(first, dva, dvb).astype(BF16)
        dck_ref[0] = dca
        dck_ref[1] = dcb

        @pl.when(j == nk - 1)
        def _():
            dq_ref[...] = dq_ref[...] * scale

    rowfull = pl.BlockSpec((PAIR, 1, S), lambda hp, j: (hp, 0, 0))
    return pl.pallas_call(
        body, name=name,
        out_shape=(jax.ShapeDtypeStruct((S, FOX_W), F32), jax.ShapeDtypeStruct((S, FOX_W), BF16),
                   jax.ShapeDtypeStruct((S, FOX_W), BF16), jax.ShapeDtypeStruct((FOX_HEADS, S, 1), F32),
                   jax.ShapeDtypeStruct((FOX_HEADS, 1, S), F32)),
        grid=(npair, nk),
        in_specs=[pl.BlockSpec((S, LANES), lambda hp, j: (0, hp)),
                  pl.BlockSpec((bk, LANES), lambda hp, j: (j, npair + hp)),
                  pl.BlockSpec((bk, LANES), lambda hp, j: (j, 2 * npair + hp)),
                  pl.BlockSpec((S, LANES), lambda hp, j: (0, hp)),
                  pl.BlockSpec((PAIR, bk, LANES), lambda hp, j: (hp, j, 0)),
                  pl.BlockSpec((PAIR, S, LANES), lambda hp, j: (hp, 0, 0)), rowfull, rowfull],
        out_specs=(pl.BlockSpec((S, LANES), lambda hp, j: (0, hp)),
                   pl.BlockSpec((bk, LANES), lambda hp, j: (j, hp)),
                   pl.BlockSpec((bk, LANES), lambda hp, j: (j, hp)),
                   pl.BlockSpec((PAIR, bk, 1), lambda hp, j: (hp, j, 0)), rowfull),
        compiler_params=_cparams("parallel", "arbitrary"),
    )(zqkv, zqkv, zqkv, do, augk, augq, lse_row, delta_row)


def _xattn_fwd(q, kv, *, name):
    S, W = q.shape
    M = kv.shape[0]
    tq = _pick(S, 512, 8)
    scale = 1.0 / math.sqrt(X_DH)

    def body(q_ref, kv_ref, o_ref):
        for h in range(X_HEADS):
            cols = slice(h * X_DH, (h + 1) * X_DH)
            vcols = slice(W + h * X_DH, W + (h + 1) * X_DH)
            s = lax.dot_general(q_ref[:, cols], kv_ref[:, cols], (((1,), (1,)), ((), ())),
                                preferred_element_type=F32) * scale
            e = jnp.exp(s - jnp.max(s, axis=-1, keepdims=True))
            p = e / jnp.sum(e, axis=-1, keepdims=True)
            o_ref[:, cols] = jnp.dot(p.astype(BF16), kv_ref[:, vcols], preferred_element_type=F32).astype(BF16)

    return pl.pallas_call(
        body, name=name, out_shape=jax.ShapeDtypeStruct((S, W), BF16), grid=(S // tq,),
        in_specs=[pl.BlockSpec((tq, W), lambda i: (i, 0)), pl.BlockSpec((M, 2 * W), lambda i: (0, 0))],
        out_specs=pl.BlockSpec((tq, W), lambda i: (i, 0)),
        compiler_params=_cparams("parallel"),
    )(q, kv)


def _xattn_bwd(q, kv, do, *, name):
    S, W = q.shape
    M = kv.shape[0]
    tq = _pick(S, 512, 8)
    scale = 1.0 / math.sqrt(X_DH)

    def body(q_ref, kv_ref, do_ref, dq_ref, dkv_ref):
        i = pl.program_id(0)

        @pl.when(i == 0)
        def _():
            dkv_ref[...] = jnp.zeros_like(dkv_ref)

        for h in range(X_HEADS):
            cols = slice(h * X_DH, (h + 1) * X_DH)
            vcols = slice(W + h * X_DH, W + (h + 1) * X_DH)
            qh = q_ref[:, cols]
            kh = kv_ref[:, cols]
            vh = kv_ref[:, vcols]
            doh = do_ref[:, cols]
            s = lax.dot_general(qh, kh, (((1,), (1,)), ((), ())), preferred_element_type=F32) * scale
            e = jnp.exp(s - jnp.max(s, axis=-1, keepdims=True))
            p = e / jnp.sum(e, axis=-1, keepdims=True)
            dp = lax.dot_general(doh, vh, (((1,), (1,)), ((), ())), preferred_element_type=F32)
            ds = (p * (dp - jnp.sum(p * dp, axis=-1, keepdims=True)) * scale).astype(BF16)
            dq_ref[:, cols] = jnp.dot(ds, kh, preferred_element_type=F32).astype(BF16)
            dkv_ref[:, cols] += lax.dot_general(ds, qh, (((0,), (0,)), ((), ())), preferred_element_type=F32)
            dkv_ref[:, vcols] += lax.dot_general(p.astype(BF16), doh, (((0,), (0,)), ((), ())),
                                                 preferred_element_type=F32)

    return pl.pallas_call(
        body, name=name,
        out_shape=(jax.ShapeDtypeStruct((S, W), BF16), jax.ShapeDtypeStruct((M, 2 * W), F32)),
        grid=(S // tq,),
        in_specs=[pl.BlockSpec((tq, W), lambda i: (i, 0)), pl.BlockSpec((M, 2 * W), lambda i: (0, 0)),
                  pl.BlockSpec((tq, W), lambda i: (i, 0))],
        out_specs=(pl.BlockSpec((tq, W), lambda i: (i, 0)), pl.BlockSpec((M, 2 * W), lambda i: (0, 0))),
        compiler_params=_cparams("arbitrary"),
    )(q, kv, do)


GELU_C = math.sqrt(2.0 / math.pi)
GELU_A = 0.044715
CONV_HALO = 16


def _gelu_parts(x):
    u = GELU_C * (x + GELU_A * x * x * x)
    t = jnp.tanh(u)
    g = 0.5 * x * (1.0 + t)
    dg = 0.5 * (1.0 + t) + 0.5 * x * (1.0 - t * t) * (GELU_C * (1.0 + 3.0 * GELU_A * x * x))
    return g, dg


def _conv3(ext, w_ref, b_ref):
    return (w_ref[2:3, :] * ext + w_ref[1:2, :] * pltpu.roll(ext, 1, 0)
            + w_ref[0:1, :] * pltpu.roll(ext, 2, 0) + b_ref[...])


def _convglu_fwd(z, conv_w, conv_b, *, name):
    S, F2 = z.shape
    F = F2 // 2
    tc = _pick(F, 1408)
    ncol = F // tc
    T = _pick(S, 512, 8)
    hb = T // CONV_HALO

    def body(zg_ref, zu_ref, zgp_ref, zup_ref, wg_ref, wu_ref, bg_ref, bu_ref, act_ref):
        i = pl.program_id(1)
        first = (i > 0).astype(F32)

        def conv(z_ref, zp_ref, w_ref, b_ref):
            ext = jnp.concatenate([zp_ref[...].astype(F32) * first, z_ref[...].astype(F32)], axis=0)
            return _conv3(ext, w_ref, b_ref)[CONV_HALO:, :]

        gc = conv(zg_ref, zgp_ref, wg_ref, bg_ref)
        uc = conv(zu_ref, zup_ref, wu_ref, bu_ref)
        act_ref[...] = (_gelu_parts(gc)[0] * uc).astype(BF16)

    cur = lambda off: pl.BlockSpec((T, tc), lambda j, i: (i, j + off))
    prev = lambda off: pl.BlockSpec((CONV_HALO, tc), lambda j, i: (jnp.maximum(i * hb - 1, 0), j + off))
    vec = lambda rows, off: pl.BlockSpec((rows, tc), lambda j, i: (0, j + off))
    return pl.pallas_call(
        body, name=name, out_shape=jax.ShapeDtypeStruct((S, F), BF16), grid=(ncol, S // T),
        in_specs=[cur(0), cur(ncol), prev(0), prev(ncol), vec(3, 0), vec(3, ncol), vec(1, 0), vec(1, ncol)],
        out_specs=pl.BlockSpec((T, tc), lambda j, i: (i, j)),
        compiler_params=_cparams("parallel", "parallel"),
    )(z, z, z, z, conv_w, conv_w, conv_b, conv_b)


def _convglu_bwd(z, dact, conv_w, conv_b, *, name):
    S, F2 = z.shape
    F = F2 // 2
    tc = _pick(F, 1408)
    ncol = F // tc
    T = _pick(S, 256, 8)
    nrow = S // T
    hb = T // CONV_HALO
    TE = T + CONV_HALO

    def body(z_ref, zp_ref, zn_ref, da_ref, dan_ref, w_ref, b_ref, dz_ref, dw_ref):
        i = pl.program_id(0)
        first = (i > 0).astype(F32)
        last = (i < nrow - 1).astype(F32)

        @pl.when(i == 0)
        def _():
            dw_ref[...] = jnp.zeros_like(dw_ref)

        def ext_of(cols):
            return jnp.concatenate([zp_ref[:, cols].astype(F32) * first, z_ref[:, cols].astype(F32),
                                    zn_ref[:, cols].astype(F32)], axis=0)

        def back(d, ext, w, cols):
            dz = w[2:3, :] * d + w[1:2, :] * pltpu.roll(d, TE - 1, 0) + w[0:1, :] * pltpu.roll(d, TE - 2, 0)
            dz_ref[:, cols] = dz[:T, :].astype(BF16)
            dc = d[:T, :]
            z0 = ext[CONV_HALO:CONV_HALO + T, :]
            z1 = pltpu.roll(ext, 1, 0)[CONV_HALO:CONV_HALO + T, :]
            z2 = pltpu.roll(ext, 2, 0)[CONV_HALO:CONV_HALO + T, :]
            rows = [jnp.sum(dc * z2, axis=0, keepdims=True), jnp.sum(dc * z1, axis=0, keepdims=True),
                    jnp.sum(dc * z0, axis=0, keepdims=True), jnp.sum(dc, axis=0, keepdims=True)]
            dw_ref[0:4, cols] += jnp.concatenate(rows, axis=0)

        for jj in range(ncol):
            cg = slice(jj * tc, (jj + 1) * tc)
            cu = slice(F + jj * tc, F + (jj + 1) * tc)
            extg, extu = ext_of(cg), ext_of(cu)
            wg, wu = w_ref[:, cg], w_ref[:, cu]
            gc = _conv3(extg, wg, b_ref[:, cg])[CONV_HALO:, :]
            uc = _conv3(extu, wu, b_ref[:, cu])[CONV_HALO:, :]
            da = jnp.concatenate([da_ref[:, cg].astype(F32), dan_ref[:, cg].astype(F32) * last], axis=0)
            gl, dgl = _gelu_parts(gc)
            back(da * uc * dgl, extg, wg, cg)
            back(da * gl, extu, wu, cu)

    halo_rows = S // CONV_HALO
    return pl.pallas_call(
        body, name=name,
        out_shape=(jax.ShapeDtypeStruct((S, F2), BF16), jax.ShapeDtypeStruct((8, F2), F32)),
        grid=(nrow,),
        in_specs=[pl.BlockSpec((T, F2), lambda i: (i, 0)),
                  pl.BlockSpec((CONV_HALO, F2), lambda i: (jnp.maximum(i * hb - 1, 0), 0)),
                  pl.BlockSpec((CONV_HALO, F2), lambda i: (jnp.minimum((i + 1) * hb, halo_rows - 1), 0)),
                  pl.BlockSpec((T, F), lambda i: (i, 0)),
                  pl.BlockSpec((CONV_HALO, F), lambda i: (jnp.minimum((i + 1) * hb, halo_rows - 1), 0)),
                  pl.BlockSpec((3, F2), lambda i: (0, 0)), pl.BlockSpec((1, F2), lambda i: (0, 0))],
        out_specs=(pl.BlockSpec((T, F2), lambda i: (i, 0)), pl.BlockSpec((8, F2), lambda i: (0, 0))),
        compiler_params=_cparams("arbitrary"),
    )(z, z, z, dact, dact, conv_w, conv_b)


OFF_QKV = POOL_W
OFF_G = POOL_W + 3 * FOX_W


SHARD_BY_ROWS = {"w_in": False, "w_pool_br": False, "w_fox_br": False, "w_mix_out": True, "w_xq": True,
                 "w_xkv": True, "w_xo": False, "w_up": False, "w_down": True, "conv_w": False}


def _prep_layer_weights(w, l):
    n, _, D, b = w["w_in"].shape
    w_in = w["w_in"][:, l].transpose(1, 0, 2).reshape(D, n * b)
    off_f = OFF_G
    pad = jnp.zeros((D, LANES - N_FGATE), w_in.dtype)
    w_in_r = jnp.concatenate([w_in[:, :off_f], w_in[:, off_f + N_FGATE:], w_in[:, off_f:off_f + N_FGATE], pad], axis=1)
    wl = {k: _ChipMajor(w[k], l, SHARD_BY_ROWS[k]) for k in MATMUL_WEIGHTS if k != "w_in"}
    wl["w_in_r"] = w_in_r
    return wl


def _row(v):
    return v.reshape(1, -1)


def _layer_fwd(x, h1, mem, wl, p, l):
    S, D = x.shape
    n = lambda s: f"l{l}_{s}"
    sv = {"x0": x}
    w_in_r = wl["w_in_r"]
    og = OFF_G
    zu = _matmul(h1, w_in_r[:, :OFF_QKV], out_dtype=F32, name=n("mm_zu"))
    zqkv = _matmul(h1, w_in_r[:, OFF_QKV:og], out_dtype=BF16, name=n("mm_zqkv"))
    zg = _matmul(h1, w_in_r[:, og:og + 2 * D], out_dtype=BF16, name=n("mm_zg"))
    zf = _matmul(h1, w_in_r[:, og + 2 * D:], out_dtype=F32, name=n("mm_zf"))
    bf = jnp.pad(p["b_forget"][l], (0, LANES - N_FGATE)).reshape(1, LANES)
    _, augq, augk = _fgate_fwd(zf, bf, name=n("fgate"))
    o_fox, lse2 = _fox_fwd(zqkv, augq, augk, name=n("fox"))
    pooled, mixed = _pool_fwd(zu, p["pool_w"][l], _row(p["pool_scale"][l]), name=n("pool"))
    y_pool = _matmul(mixed, wl["w_pool_br"], out_dtype=BF16, name=n("mm_ypool"))
    y_fox = _matmul(o_fox, wl["w_fox_br"], out_dtype=BF16, name=n("mm_yfox"))
    merged = _merge_fwd(zg, y_pool, y_fox, name=n("merge"))
    r1 = _matmul(merged, wl["w_mix_out"], out_dtype=BF16, name=n("mm_r1"))
    x1, h2 = _add_rms(x, r1, _row(p["mix_post_g"][l]), _row(p["xa_pre_g"][l]), name=n("addrms1"))
    sv.update(h1=h1, zg=zg, zf=zf, bf=bf, zqkv=zqkv, augq=augq, augk=augk, o_fox=o_fox, lse2=lse2,
              pooled=pooled, mixed=mixed, y_pool=y_pool, y_fox=y_fox, merged=merged, r1=r1, x1=x1)
    mem_n = _rms_fwd(mem, _row(p["mem_g"][l]), out_dtype=BF16, name=n("rms_mem"))
    q2 = _matmul(h2, wl["w_xq"], out_dtype=BF16, name=n("mm_q2"))
    kv = _matmul(mem_n, wl["w_xkv"], out_dtype=BF16, name=n("mm_kv"))
    o2 = _xattn_fwd(q2, kv, name=n("xattn"))
    a2 = _matmul(o2, wl["w_xo"], out_dtype=BF16, name=n("mm_a2"))
    x2, h3 = _add_rms(x1, a2, _row(p["xa_post_g"][l]), _row(p["ffn_pre_g"][l]), name=n("addrms2"))
    sv.update(h2=h2, mem_n=mem_n, q2=q2, kv=kv, o2=o2, a2=a2, x2=x2)
    z3 = _matmul(h3, wl["w_up"], out_dtype=BF16, name=n("mm_z3"))
    act = _convglu_fwd(z3, p["conv_w"][l], _row(p["conv_b"][l]), name=n("convglu"))
    d3 = _matmul(act, wl["w_down"], out_dtype=BF16, name=n("mm_d3"))
    if l + 1 < p["mix_pre_g"].shape[0]:
        x3, h_next = _add_rms(x2, d3, _row(p["ffn_post_g"][l]), _row(p["mix_pre_g"][l + 1]), name=n("addrms3"))
    else:
        x3, h_next = _add_rms(x2, d3, _row(p["ffn_post_g"][l]), name=n("addrms3")), None
    sv.update(h3=h3, z3=z3, act=act, d3=d3)
    return x3, h_next, sv


def _layer_bwd(dx, mem, wl, p, l, sv, gbuf):
    S, D = dx.shape
    n_layers = p["mix_pre_g"].shape[0]
    n = lambda s: f"l{l}_b_{s}"
    g = {}
    red = lambda part: jnp.sum(part, axis=0)

    def wgrad(k, lhs, rhs, nm):
        how = dict(out_rows=N_CHIPS) if SHARD_BY_ROWS[k] else dict(out_chips=N_CHIPS)
        return _matmul(lhs, rhs, ta=True, out_dtype=F32, out_layer=(gbuf.get(k), l, n_layers), name=n(nm), **how)

    dd3, dg = _rms_bwd(sv["d3"], _row(p["ffn_post_g"][l]), dx, out_dtype=BF16, name=n("rms3post"))
    g["ffn_post_g"] = red(dg)
    dact = _matmul(dd3, wl["w_down"], tb=True, out_dtype=BF16, name=n("mm_dact"))
    g["w_down"] = wgrad("w_down", sv["act"], dd3, "mm_dwdown")
    dz3, dconv = _convglu_bwd(sv["z3"], dact, p["conv_w"][l], _row(p["conv_b"][l]), name=n("convglu"))
    g["conv_w"] = dconv[:3]
    g["conv_b"] = dconv[3]
    dh3 = _matmul(dz3, wl["w_up"], tb=True, out_dtype=BF16, name=n("mm_dh3"))
    g["w_up"] = wgrad("w_up", sv["h3"], dz3, "mm_dwup")
    dx, dg = _rms_bwd(sv["x2"], _row(p["ffn_pre_g"][l]), dh3, dx, out_dtype=F32, name=n("rms3pre"))
    g["ffn_pre_g"] = red(dg)
    da2, dg = _rms_bwd(sv["a2"], _row(p["xa_post_g"][l]), dx, out_dtype=BF16, name=n("rms2post"))
    g["xa_post_g"] = red(dg)
    do2 = _matmul(da2, wl["w_xo"], tb=True, out_dtype=BF16, name=n("mm_do2"))
    g["w_xo"] = wgrad("w_xo", sv["o2"], da2, "mm_dwxo")
    dq2, dkv = _xattn_bwd(sv["q2"], sv["kv"], do2, name=n("xattn"))
    dh2 = _matmul(dq2, wl["w_xq"], tb=True, out_dtype=BF16, name=n("mm_dh2"))
    g["w_xq"] = wgrad("w_xq", sv["h2"], dq2, "mm_dwxq")
    dmem_n = _matmul(dkv, wl["w_xkv"], tb=True, out_dtype=F32, name=n("mm_dmemn"))
    g["w_xkv"] = wgrad("w_xkv", sv["mem_n"], dkv, "mm_dwxkv")
    _, dg = _rms_bwd(mem, _row(p["mem_g"][l]), dmem_n, out_dtype=BF16, name=n("rms_mem"))
    g["mem_g"] = red(dg)
    dx, dg = _rms_bwd(sv["x1"], _row(p["xa_pre_g"][l]), dh2, dx, out_dtype=F32, name=n("rms2pre"))
    g["xa_pre_g"] = red(dg)
    dr1, dg = _rms_bwd(sv["r1"], _row(p["mix_post_g"][l]), dx, out_dtype=BF16, name=n("rms1post"))
    g["mix_post_g"] = red(dg)
    dmerged = _matmul(dr1, wl["w_mix_out"], tb=True, out_dtype=BF16, name=n("mm_dmerged"))
    g["w_mix_out"] = wgrad("w_mix_out", sv["merged"], dr1, "mm_dwmo")
    dyp, dyf, dzg = _merge_bwd(sv["zg"], sv["y_pool"], sv["y_fox"], dmerged, name=n("merge"))
    dmixed = _matmul(dyp, wl["w_pool_br"], tb=True, out_dtype=F32, name=n("mm_dmixed"))
    g["w_pool_br"] = wgrad("w_pool_br", sv["mixed"], dyp, "mm_dwpb")
    dofox = _matmul(dyf, wl["w_fox_br"], tb=True, out_dtype=F32, name=n("mm_dofox"))
    g["w_fox_br"] = wgrad("w_fox_br", sv["o_fox"], dyf, "mm_dwfb")
    dzu, dpw, dsc = _pool_bwd(sv["pooled"], dmixed, p["pool_w"][l], _row(p["pool_scale"][l]), name=n("pool"))
    g["pool_w"] = dpw
    g["pool_scale"] = red(dsc)
    delta = _head_rowsum(dofox, sv["o_fox"], name=n("delta"))
    lse = sv["lse2"].reshape(S, N_PAIRS, PAIR, FOX_DH)[:, :, ::-1, 0].reshape(S, FOX_HEADS)
    dq, dk, dv, dck, dcq = _fox_bwd(sv["zqkv"], dofox.astype(BF16), sv["augq"], sv["augk"],
                                    lse.T.reshape(FOX_HEADS, 1, S), delta[:, :FOX_HEADS].T.reshape(FOX_HEADS, 1, S),
                                    name=n("fox"))
    dc = dcq.reshape(FOX_HEADS, S) + dck.reshape(FOX_HEADS, S)
    dc_pad = jnp.pad(dc.T, ((0, 0), (0, LANES - FOX_HEADS)))
    dzf, db = _fgate_bwd(sv["zf"], sv["bf"], dc_pad, name=n("fgate"))
    g["b_forget"] = red(db)[:N_FGATE]
    dz_cat = jnp.concatenate([dzu, dq.astype(BF16), dk, dv, dzg, dzf], axis=1)
    dh1 = _matmul(dz_cat, wl["w_in_r"], tb=True, out_dtype=BF16, name=n("mm_dh1"))
    dw_in_r = _matmul(sv["h1"], dz_cat, ta=True, out_dtype=F32, name=n("mm_dwin"))
    og = OFF_G
    g["w_in"] = jnp.concatenate([dw_in_r[:, :og], dw_in_r[:, og + 2 * D:og + 2 * D + N_FGATE],
                                 dw_in_r[:, og:og + 2 * D]], axis=1)
    dx, dg = _rms_bwd(sv["x0"], _row(p["mix_pre_g"][l]), dh1, dx, out_dtype=F32, name=n("rms1pre"))
    g["mix_pre_g"] = red(dg)
    return dx, g


MATMUL_WEIGHTS = ("w_in", "w_pool_br", "w_fox_br", "w_mix_out", "w_xq", "w_xkv", "w_xo", "w_up", "w_down")
WEIGHT_NAMES = ("mix_pre_g", "mix_post_g", "w_in", "b_forget", "pool_w", "pool_scale", "w_pool_br", "w_fox_br",
                "w_mix_out", "xa_pre_g", "xa_post_g", "mem_g", "w_xq", "w_xkv", "w_xo", "ffn_pre_g", "ffn_post_g",
                "w_up", "conv_w", "conv_b", "w_down")


def _local_step(x, mem, loss_target, wfull, p):
    L = p["mix_pre_g"].shape[0]
    saved, wls = [], []
    h = x
    hn = _rms_fwd(x, _row(p["mix_pre_g"][0]), out_dtype=BF16, name="rms_first")
    for l in range(L):
        wl = _prep_layer_weights(wfull, l)
        h, hn, sv = _layer_fwd(h, hn, mem, wl, p, l)
        saved.append(sv)
        wls.append(wl)
    D = x.shape[1]
    dy, sq = _loss_head(h, loss_target, name="loss_head")
    loss = 0.5 * jnp.sum(sq) / D
    grads = []
    gbuf = {}
    dx = dy
    for l in reversed(range(L)):
        dx, g = _layer_bwd(dx, mem, wls[l], p, l, saved[l], gbuf)
        gbuf = {k: g[k] for k in MATMUL_WEIGHTS if k != "w_in"}
        grads.append(g)
    grads = grads[::-1]

    def chip_major(g):
        return g.reshape(g.shape[0], N_CHIPS, g.shape[1] // N_CHIPS).transpose(1, 0, 2)

    gfull = {k: jnp.stack([grads[l][k] for l in range(L)]) for k in REPLICATED}
    gfull.update({k: jnp.stack([chip_major(grads[l][k]) for l in range(L)], axis=1) for k in ("w_in", "conv_w")})
    gfull.update(gbuf)
    return loss, dx, gfull


PACK_W = 512
PACK_ROW_ALIGN = 1024
N_CHIPS = 4
N_DEV = 8
SHARDED = (("w_in", 2), ("w_pool_br", 2), ("w_fox_br", 2), ("w_mix_out", 1), ("w_xq", 1), ("w_xkv", 1),
           ("w_xo", 2), ("w_up", 2), ("w_down", 1), ("conv_w", 2))
REPLICATED = ("mix_pre_g", "mix_post_g", "b_forget", "pool_w", "pool_scale", "xa_pre_g", "xa_post_g", "mem_g",
              "ffn_pre_g", "ffn_post_g", "conv_b")


def _round_up(n, m):
    return -(-n // m) * m


def _pack(arrs, rows):
    flat = jnp.concatenate([a.reshape(-1) for a in arrs])
    return jnp.pad(flat, (0, rows * PACK_W - flat.shape[0])).reshape(rows, PACK_W)


def _unpack(buf, shapes):
    flat = buf.reshape(-1)
    out, off = [], 0
    for s in shapes:
        n = math.prod(s)
        out.append(flat[off:off + n].reshape(s))
        off += n
    return out


ANY = pl.BlockSpec(memory_space=pl.ANY)


def _remote(send_sems, recv_sems, k, src, dst, to):
    return pltpu.make_async_remote_copy(src_ref=src, dst_ref=dst, send_sem=send_sems.at[k], recv_sem=recv_sems.at[k],
                                        device_id=to, device_id_type=MESH)


def _my_place():
    return lax.axis_index("x"), lax.axis_index("y"), lax.axis_index("c")


def _rows_per_block(a, b, cap_bytes=1024 * 1024):
    if a % 8:
        return a
    return _pick(a, max(8, cap_bytes // (4 * b) // 8 * 8), 8)


def _place_shard(w, chip, dtype, *, name):
    L, a, b = w.shape
    ta = _rows_per_block(a, b)

    def body(chip_ref, w_ref, o_ref):
        o_ref[...] = w_ref[...].astype(dtype)

    return pl.pallas_call(
        body, name=name, out_shape=jax.ShapeDtypeStruct((N_CHIPS, L, a, b), dtype),
        grid_spec=pltpu.PrefetchScalarGridSpec(
            num_scalar_prefetch=1, grid=(L, a // ta),
            in_specs=[pl.BlockSpec((None, ta, b), lambda l, i, chip_ref: (l, i, 0))],
            out_specs=pl.BlockSpec((None, None, ta, b), lambda l, i, chip_ref: (chip_ref[0], l, i, 0))),
        compiler_params=_cparams("parallel", "parallel"),
    )(chip, w)


def _layer_halves(L, c):
    assert L % 2 == 0
    return pl.ds(c * (L // 2), L // 2), pl.ds((1 - c) * (L // 2), L // 2)


def _gather_weights(bufs):
    n = len(bufs)
    L = bufs[0].shape[1]

    def body(*refs):
        outs, (send_sems, recv_sems) = refs[n:2 * n], refs[2 * n:]
        x, y, c = _my_place()
        me = 2 * x + y
        sibling = (x, y, 1 - c)
        chips = [(1 - x, y), (x, 1 - y), (1 - x, 1 - y)]
        half, other = _layer_halves(L, c)
        rc = functools.partial(_remote, send_sems, recv_sems)
        first = [rc(6 * w + k, o.at[me, half], o.at[me, half], (px, py, c))
                 for w, o in enumerate(outs) for k, (px, py) in enumerate(chips)]
        for cp in first:
            cp.start()
        passed = []
        for k, (px, py) in enumerate(chips):
            src = 2 * px + py
            for w, o in enumerate(outs):
                rc(6 * w + k, o.at[src, half], o.at[src, half], (px, py, c)).wait_recv()
                fwd = rc(6 * w + 3 + k, o.at[src, half], o.at[src, half], sibling)
                fwd.start()
                passed.append(fwd)
        for k, (px, py) in enumerate(chips):
            src = 2 * px + py
            for w, o in enumerate(outs):
                rc(6 * w + 3 + k, o.at[src, other], o.at[src, other], sibling).wait_recv()
        for cp in first + passed:
            cp.wait_send()

    return pl.pallas_call(
        body, name="gather_weights",
        out_shape=tuple(jax.ShapeDtypeStruct(b.shape, b.dtype) for b in bufs),
        in_specs=[ANY] * n, out_specs=tuple([ANY] * n), input_output_aliases={i: i for i in range(n)},
        scratch_shapes=[pltpu.SemaphoreType.DMA((6 * n,)), pltpu.SemaphoreType.DMA((6 * n,))],
    )(*bufs)


def _exchange_halves(Gs, rep):
    n = len(Gs)
    L = Gs[0].shape[1]
    RR, W = rep.shape

    def body(*refs):
        g_refs, rep_ref, ra_refs, rall_ref = refs[:n], refs[n], refs[n + 1:2 * n + 1], refs[2 * n + 1]
        send_sems, recv_sems, local_sem = refs[2 * n + 2:]
        x, y, c = _my_place()
        me = 4 * x + 2 * y + c
        rc = functools.partial(_remote, send_sems, recv_sems)
        _, other = _layer_halves(L, c)

        def peer(idx):
            px = (1 - x) if (idx >> 2) & 1 else x
            py = (1 - y) if (idx >> 1) & 1 else y
            pc = (1 - c) if idx & 1 else c
            return px, py, pc

        loc = pltpu.make_async_copy(rep_ref, rall_ref.at[me], local_sem)
        loc.start()
        cps = [rc(N_DEV + w, g.at[:, other], ra, (x, y, 1 - c)) for w, (g, ra) in enumerate(zip(g_refs, ra_refs))]
        for idx in range(1, N_DEV):
            cps.append(rc(idx, rep_ref, rall_ref.at[me], peer(idx)))
        for cp in cps:
            cp.start()
        for w, (g, ra) in enumerate(zip(g_refs, ra_refs)):
            rc(N_DEV + w, g.at[:, other], ra, (x, y, 1 - c)).wait_recv()
        for idx in range(1, N_DEV):
            px, py, pc = peer(idx)
            rc(idx, rep_ref, rall_ref.at[4 * px + 2 * py + pc], (px, py, pc)).wait_recv()
        for cp in cps:
            cp.wait_send()
        loc.wait()

    halves = tuple(jax.ShapeDtypeStruct((g.shape[0], L // 2) + g.shape[2:], F32) for g in Gs)
    out = pl.pallas_call(
        body, name="exchange_halves",
        out_shape=halves + (jax.ShapeDtypeStruct((N_DEV, RR, W), F32),),
        in_specs=[ANY] * (n + 1), out_specs=tuple([ANY] * (n + 1)),
        scratch_shapes=[pltpu.SemaphoreType.DMA((N_DEV + n,)), pltpu.SemaphoreType.DMA((N_DEV + n,)),
                        pltpu.SemaphoreType.DMA],
    )(*Gs, rep)
    return out[:n], out[n]


def _exchange_chips(As):
    n = len(As)

    def body(*refs):
        a_refs, rb_refs, (send_sems, recv_sems) = refs[:n], refs[n:2 * n], refs[2 * n:]
        x, y, c = _my_place()
        me = 2 * x + y
        chips = [(1 - x, y), (x, 1 - y), (1 - x, 1 - y)]
        rc = functools.partial(_remote, send_sems, recv_sems)
        cps = [rc(3 * w + k, a.at[2 * px + py], rb.at[k], (px, py, c))
               for w, (a, rb) in enumerate(zip(a_refs, rb_refs)) for k, (px, py) in enumerate(chips)]
        for cp in cps:
            cp.start()
        for w, (a, rb) in enumerate(zip(a_refs, rb_refs)):
            for k, (px, py) in enumerate(chips):
                rc(3 * w + k, a.at[me], rb.at[k], (px, py, c)).wait_recv()
        for cp in cps:
            cp.wait_send()

    return pl.pallas_call(
        body, name="exchange_chips",
        out_shape=tuple(jax.ShapeDtypeStruct((N_CHIPS - 1,) + a.shape[1:], a.dtype) for a in As),
        in_specs=[ANY] * n, out_specs=tuple([ANY] * n),
        scratch_shapes=[pltpu.SemaphoreType.DMA((3 * n,)), pltpu.SemaphoreType.DMA((3 * n,))],
    )(*As)


def _exchange_sibling(gs):
    n = len(gs)
    L = gs[0].shape[0]

    def body(*refs):
        g_refs, (send_sems, recv_sems) = refs[n:2 * n], refs[2 * n:]
        x, y, c = _my_place()
        half, other = _layer_halves(L, c)
        rc = functools.partial(_remote, send_sems, recv_sems)
        cps = [rc(w, g.at[half], g.at[half], (x, y, 1 - c)) for w, g in enumerate(g_refs)]
        for cp in cps:
            cp.start()
        for w, g in enumerate(g_refs):
            rc(w, g.at[other], g.at[other], (x, y, 1 - c)).wait_recv()
        for cp in cps:
            cp.wait_send()

    return pl.pallas_call(
        body, name="exchange_sibling", out_shape=tuple(jax.ShapeDtypeStruct(g.shape, F32) for g in gs),
        in_specs=[ANY] * n, out_specs=tuple([ANY] * n), input_output_aliases={i: i for i in range(n)},
        scratch_shapes=[pltpu.SemaphoreType.DMA((n,)), pltpu.SemaphoreType.DMA((n,))],
    )(*gs)


def _add_halves(G, recv, core, *, name):
    n, L, a, b = G.shape
    Lh = L // 2
    ta = _rows_per_block(a, b)

    def body(core_ref, g_ref, r_ref, o_ref, o16_ref):
        s = g_ref[...] + r_ref[...]
        o_ref[...] = s
        o16_ref[...] = s.astype(BF16)

    blk = pl.BlockSpec((None, None, ta, b), lambda p, l, i, core_ref: (p, l, i, 0))
    return pl.pallas_call(
        body, name=name,
        out_shape=(jax.ShapeDtypeStruct((n, Lh, a, b), F32), jax.ShapeDtypeStruct((n, Lh, a, b), BF16)),
        grid_spec=pltpu.PrefetchScalarGridSpec(
            num_scalar_prefetch=1, grid=(n, Lh, a // ta),
            in_specs=[pl.BlockSpec((None, None, ta, b), lambda p, l, i, core_ref: (p, core_ref[0] * Lh + l, i, 0)), blk],
            out_specs=(blk, blk)),
        compiler_params=_cparams("parallel", "parallel", "parallel"),
    )(core, G, recv)


def _sum_chips(A, rb, place, *, name):
    _, Lh, a, b = A.shape
    ta = _rows_per_block(a, b, 512 * 1024)

    def body(place_ref, a_ref, r_ref, o_ref):
        o_ref[...] = ((a_ref[...] + r_ref[0].astype(F32)) + r_ref[1].astype(F32)) + r_ref[2].astype(F32)

    return pl.pallas_call(
        body, name=name, out_shape=jax.ShapeDtypeStruct((2 * Lh, a, b), F32),
        grid_spec=pltpu.PrefetchScalarGridSpec(
            num_scalar_prefetch=1, grid=(Lh, a // ta),
            in_specs=[pl.BlockSpec((None, None, ta, b), lambda l, i, place_ref: (place_ref[0], l, i, 0)),
                      pl.BlockSpec((N_CHIPS - 1, None, ta, b), lambda l, i, place_ref: (0, l, i, 0))],
            out_specs=pl.BlockSpec((None, ta, b), lambda l, i, place_ref: (place_ref[1] * Lh + l, i, 0))),
        compiler_params=_cparams("parallel", "parallel"),
    )(place, A, rb)


def _sum_slots(a, *, name):
    n, rows, W = a.shape
    tr = _pick(rows, 512, 8)

    def body(a_ref, o_ref):
        s = a_ref[0]
        for q in range(1, n):
            s = s + a_ref[q]
        o_ref[...] = s

    return pl.pallas_call(
        body, name=name, out_shape=jax.ShapeDtypeStruct((rows, W), F32), grid=(rows // tr,),
        in_specs=[pl.BlockSpec((n, tr, W), lambda i: (0, i, 0))], out_specs=pl.BlockSpec((tr, W), lambda i: (i, 0)),
        compiler_params=_cparams("parallel"),
    )(a)


def _adamw(w, g, m, v, *, name):
    L, a, b = w.shape
    ta = _rows_per_block(a, b, 512 * 1024)

    def body(w_ref, g_ref, m_ref, v_ref, d_ref, nm_ref, nv_ref):
        gg = g_ref[...]
        nm = ADAM_B1 * m_ref[...] + (1.0 - ADAM_B1) * gg
        nv = ADAM_B2 * v_ref[...] + (1.0 - ADAM_B2) * jnp.square(gg)
        m_hat = nm / (1.0 - ADAM_B1 ** ADAM_STEP)
        v_hat = nv / (1.0 - ADAM_B2 ** ADAM_STEP)
        d_ref[...] = -ADAM_LR * (m_hat / (jnp.sqrt(v_hat) + ADAM_EPS) + ADAM_WD * w_ref[...])
        nm_ref[...] = nm
        nv_ref[...] = nv

    blk = pl.BlockSpec((None, ta, b), lambda l, i: (l, i, 0))
    shp = jax.ShapeDtypeStruct((L, a, b), F32)
    return pl.pallas_call(
        body, name=name, out_shape=(shp, shp, shp), grid=(L, a // ta),
        in_specs=[blk, blk, blk, blk], out_specs=(blk, blk, blk),
        compiler_params=_cparams("parallel", "parallel"),
    )(w, g, m, v)


INPUT_NAMES = (("x", "mem") + WEIGHT_NAMES + ("loss_target",) + tuple("m_" + n for n in WEIGHT_NAMES)
               + tuple("v_" + n for n in WEIGHT_NAMES))


def kernel(*args):
    a = dict(zip(INPUT_NAMES, args, strict=True))
    x, mem, target = a["x"][0], a["mem"][0], a["loss_target"][0]
    sh_names = list(SHARD_BY_ROWS)
    core = lax.axis_index("c").astype(jnp.int32)
    chip = (2 * lax.axis_index("x") + lax.axis_index("y")).astype(jnp.int32)
    place = jnp.stack([chip, core])

    placed = [_place_shard(a[n], chip.reshape(1), F32 if n == "conv_w" else BF16, name="place_" + n) for n in sh_names]
    wfull = dict(zip(sh_names, _gather_weights(placed)))
    p = {n: a[n] for n in REPLICATED}
    cw = wfull.pop("conv_w")
    p["conv_w"] = cw.transpose(1, 2, 0, 3).reshape(cw.shape[1], cw.shape[2], N_CHIPS * cw.shape[3])

    loss, dx, gfull = _local_step(x, mem, target, wfull, p)
    loss = lax.psum(loss, ("x", "y", "c"))

    Gs = [gfull[n] for n in sh_names]
    rep_shapes = [a[n].shape for n in REPLICATED]
    rows_r = _round_up(-(-sum(math.prod(s) for s in rep_shapes) // PACK_W), 64)
    rep = _pack([gfull[n] for n in REPLICATED], rows_r)
    recvs, repall = _exchange_halves(Gs, rep)
    As = [_add_halves(g, r, core.reshape(1), name="add_halves_" + n) for n, g, r in zip(sh_names, Gs, recvs)]
    rbs = _exchange_chips([a16 for _, a16 in As])
    gsh = _exchange_sibling([_sum_chips(A, rb, place, name="sum_chips_" + n)
                             for n, (A, _), rb in zip(sh_names, As, rbs)])
    grep = _sum_slots(repall, name="sum_devices")

    got = {"g": dict(zip(sh_names, gsh)), "d": {}, "m": {}, "v": {}}
    for n, g in zip(sh_names, gsh):
        got["d"][n], got["m"][n], got["v"][n] = _adamw(a[n], g, a["m_" + n], a["v_" + n], name="adamw_" + n)
    packed = [_pack([a[pre + n] for n in REPLICATED], rows_r)[None] for pre in ("", "m_", "v_")]
    d_r, m_r, v_r = _adamw(packed[0], grep[None], packed[1], packed[2], name="adamw_replicated")
    for key, buf in (("g", grep), ("d", d_r[0]), ("m", m_r[0]), ("v", v_r[0])):
        got[key].update(zip(REPLICATED, _unpack(buf, rep_shapes)))
    outs = [got[key][n] for key in ("g", "d", "m", "v") for n in WEIGHT_NAMES]
    return (loss, dx[None], *outs)
```

```python
import functools
import math

import jax
import jax.numpy as jnp
from jax import lax
from jax.experimental import pallas as pl
from jax.experimental.pallas import tpu as pltpu

F32 = jnp.float32
BF16 = jnp.bfloat16
MESH = pl.DeviceIdType.MESH

RMS_EPS = 1e-6
POOL_WINDOWS = (2, 4, 8, 16)
POOL_GROUP = 128
POOL_W = 512
FOX_HEADS = 8
FOX_DH = 64
FOX_W = 512
X_HEADS = 4
X_DH = 128
X_W = 512
N_FGATE = 8
LANES = 128
HALO = 16

ADAM_LR = 0.001
ADAM_B1 = 0.9
ADAM_B2 = 0.999
ADAM_EPS = 1e-08
ADAM_WD = 0.01
ADAM_STEP = 10

VMEM_LIMIT_BYTES = 56 * 1024 * 1024
MATMUL_VMEM_BUDGET = 42 * 1024 * 1024


def _cparams(*sem):
    return pltpu.CompilerParams(dimension_semantics=sem, vmem_limit_bytes=VMEM_LIMIT_BYTES)


def _pick(n, cap, align=LANES):
    if n <= cap:
        return n
    best = None
    for t in range(align, cap + 1, align):
        if n % t == 0:
            best = t
    assert best is not None, (n, cap, align)
    return best


def _sigmoid(x):
    return 1.0 / (1.0 + jnp.exp(-x))


class _ChipMajor:
    def __init__(self, arr, layer, by_rows):
        self.arr, self.layer, self.by_rows = arr, layer, by_rows
        n, _, a, b = arr.shape
        self.n_chips, self.per_chip = n, (a if by_rows else b)
        self.shape = (n * a, b) if by_rows else (a, n * b)


SPAN_CHIPS_BELOW = 512


def _matmul(a, b, *, ta=False, tb=False, out_dtype=F32, out_chips=None, out_rows=None, out_layer=None, name):
    view = b if isinstance(b, _ChipMajor) else None
    if ta:
        K, M = a.shape
    else:
        M, K = a.shape
    if tb:
        N, Kb = b.shape
    else:
        Kb, N = b.shape
    assert K == Kb, (a.shape, b.shape, ta, tb)
    by_rows = view is not None and view.by_rows
    by_cols = view is not None and not view.by_rows
    b_itemsize = (view.arr if view is not None else b).dtype.itemsize
    span_b = by_cols and view.per_chip < SPAN_CHIPS_BELOW
    span_o = bool(out_chips) and N // out_chips < SPAN_CHIPS_BELOW
    if by_rows and tb:
        tn = N
    elif by_cols and not tb:
        tn = N if span_b else _pick(view.per_chip, 1408)
    elif out_chips:
        tn = N if span_o else _pick(N // out_chips, 1408)
    elif out_rows:
        tn = _pick(N, 512)
    else:
        tn = _pick(N, 1408)
    tm = M if out_rows else _pick(M, 1024 if tn <= 1024 else 512)
    if by_rows and not tb:
        tk = K
    elif by_cols and tb:
        tk = K if span_b else (view.per_chip if view.per_chip <= 2048 else _pick(view.per_chip, 1024))
    else:
        ab, bb, ob = a.dtype.itemsize, b_itemsize, jnp.dtype(out_dtype).itemsize
        for tm in ((M,) if out_rows else (_pick(M, 1024), tm)):
            for cap in (K, 2048, 1024, 512, 128):
                tk = _pick(K, cap)
                need = (2 * tk * (tm * ab + tn * bb) + (2 * ob + 4 + (4 if tk < K else 0)) * tm * tn)
                if need <= MATMUL_VMEM_BUDGET:
                    break
            if tk >= min(K, 1024):
                break
    nk = K // tk
    dims = (((0 if ta else 1,), (1 if tb else 0,)), ((), ()))

    aliased = out_layer is not None and out_layer[0] is not None

    def body(a_ref, b_ref, *rest):
        o_ref, scratch = (rest[1], rest[2:]) if aliased else (rest[0], rest[1:])
        bt = b_ref[...]
        if by_rows:
            bt = bt.reshape(bt.shape[0] * bt.shape[1], bt.shape[2])
        elif span_b:
            bt = jnp.concatenate([bt[q] for q in range(view.n_chips)], axis=1)
        p = lax.dot_general(a_ref[...].astype(BF16), bt.astype(BF16), dims, preferred_element_type=F32)

        def store(val):
            if span_o:
                w = N // out_chips
                for q in range(out_chips):
                    o_ref[q] = val[:, q * w:(q + 1) * w].astype(out_dtype)
            elif out_rows:
                h = M // out_rows
                for q in range(out_rows):
                    o_ref[q] = val[q * h:(q + 1) * h, :].astype(out_dtype)
            else:
                o_ref[...] = val.astype(out_dtype)

        if nk == 1:
            store(p)
        else:
            acc_ref, = scratch
            k = pl.program_id(2)

            @pl.when(k == 0)
            def _():
                acc_ref[...] = p

            @pl.when(k > 0)
            def _():
                acc_ref[...] += p

            @pl.when(k == nk - 1)
            def _():
                store(acc_ref[...])

    a_spec = pl.BlockSpec((tk, tm), lambda i, j, k: (k, i)) if ta else pl.BlockSpec((tm, tk), lambda i, j, k: (i, k))
    b_tile = (tn, tk) if tb else (tk, tn)
    b_rc = (lambda i, j, k: (j, k)) if tb else (lambda i, j, k: (k, j))
    if view is None:
        b_arr = b
        b_spec = pl.BlockSpec(b_tile, b_rc)
    elif by_rows:
        b_arr = view.arr
        assert b_tile[0] == view.shape[0]
        b_spec = pl.BlockSpec((view.n_chips, None, view.arr.shape[2], b_tile[1]),
                              lambda i, j, k: (0, view.layer, 0, b_rc(i, j, k)[1]))
    elif span_b:
        b_arr = view.arr
        assert b_tile[1] == view.shape[1]
        b_spec = pl.BlockSpec((view.n_chips, None, b_tile[0], view.per_chip),
                              lambda i, j, k: (0, view.layer, b_rc(i, j, k)[0], 0))
    else:
        b_arr = view.arr
        per = view.per_chip // b_tile[1]
        b_spec = pl.BlockSpec((None, None) + b_tile,
                              lambda i, j, k: (b_rc(i, j, k)[1] // per, view.layer, b_rc(i, j, k)[0],
                                               b_rc(i, j, k)[1] % per))
    if span_o:
        o_full, o_blk = (out_chips, M, N // out_chips), (out_chips, tm, N // out_chips)
        o_idx = lambda i, j, k: (0, i, 0)
    elif out_chips:
        per_o = (N // out_chips) // tn
        o_full, o_blk = (out_chips, M, N // out_chips), (None, tm, tn)
        o_idx = lambda i, j, k: (j // per_o, i, j % per_o)
    elif out_rows:
        o_full, o_blk = (out_rows, M // out_rows, N), (out_rows, M // out_rows, tn)
        o_idx = lambda i, j, k: (0, 0, j)
    else:
        o_full, o_blk = (M, N), (tm, tn)
        o_idx = lambda i, j, k: (i, j)
    if out_layer is not None:
        _, layer, n_layers = out_layer
        o_full, o_blk = o_full[:1] + (n_layers,) + o_full[1:], o_blk[:1] + (None,) + o_blk[1:]
        o_idx = functools.partial(lambda f, i, j, k: (f(i, j, k)[0], layer) + f(i, j, k)[1:], o_idx)
    out_shape = jax.ShapeDtypeStruct(o_full, out_dtype)
    out_spec = pl.BlockSpec(o_blk, o_idx)
    grid = (M // tm, N // tn, nk)
    specs = [a_spec, b_spec, out_spec]
    a_bytes, b_bytes = M * K * a.dtype.itemsize, K * N * b_itemsize
    if nk == 1 and b_bytes + a_bytes * grid[1] < a_bytes + b_bytes * grid[0]:
        grid = (grid[1], grid[0], nk)
        specs = [pl.BlockSpec(s.block_shape, functools.partial(lambda f, j, i, k: f(i, j, k), s.index_map))
                 for s in specs]
    extra = (out_layer[0],) if aliased else ()
    return pl.pallas_call(
        body, name=name, out_shape=out_shape, grid=grid,
        in_specs=specs[:2] + [pl.BlockSpec(memory_space=pl.ANY)] * len(extra), out_specs=specs[2],
        input_output_aliases={2: 0} if aliased else {},
        scratch_shapes=[pltpu.VMEM((tm, tn), F32)] if nk > 1 else [],
        compiler_params=_cparams("parallel", "parallel", "arbitrary"),
    )(a, b_arr, *extra)


def _row_block(S, D, cap_bytes=2 * 1024 * 1024):
    ts = max(8, min(S, cap_bytes // (4 * D)))
    return _pick(S, ts, 8)


def _rms_fwd(x, g, *, out_dtype, name):
    S, D = x.shape
    ts = _row_block(S, D)

    def body(x_ref, g_ref, o_ref):
        xf = x_ref[...]
        r = lax.rsqrt(jnp.mean(xf * xf, axis=-1, keepdims=True) + RMS_EPS)
        o_ref[...] = (xf * r * g_ref[...]).astype(out_dtype)

    return pl.pallas_call(
        body, name=name, out_shape=jax.ShapeDtypeStruct((S, D), out_dtype), grid=(S // ts,),
        in_specs=[pl.BlockSpec((ts, D), lambda i: (i, 0)), pl.BlockSpec((1, D), lambda i: (0, 0))],
        out_specs=pl.BlockSpec((ts, D), lambda i: (i, 0)),
        compiler_params=_cparams("parallel"),
    )(x, g)


def _add_rms(x, r, g, g_next=None, *, name):
    S, D = x.shape
    ts = _row_block(S, D)
    fused = g_next is not None

    def body(*refs):
        x_ref, r_ref, g_ref = refs[:3]
        rf = r_ref[...].astype(F32)
        s = lax.rsqrt(jnp.mean(rf * rf, axis=-1, keepdims=True) + RMS_EPS)
        y = x_ref[...] + rf * s * g_ref[...]
        if fused:
            gn_ref, o_ref, h_ref = refs[3:]
            t = lax.rsqrt(jnp.mean(y * y, axis=-1, keepdims=True) + RMS_EPS)
            h_ref[...] = (y * t * gn_ref[...]).astype(BF16)
        else:
            o_ref, = refs[3:]
        o_ref[...] = y

    row = pl.BlockSpec((ts, D), lambda i: (i, 0))
    vec = pl.BlockSpec((1, D), lambda i: (0, 0))
    out = pl.pallas_call(
        body, name=name,
        out_shape=(jax.ShapeDtypeStruct((S, D), F32),) + ((jax.ShapeDtypeStruct((S, D), BF16),) if fused else ()),
        grid=(S // ts,),
        in_specs=[row, row, vec] + ([vec] if fused else []),
        out_specs=(row,) + ((row,) if fused else ()),
        compiler_params=_cparams("parallel"),
    )(*((x, r, g) + ((g_next,) if fused else ())))
    return out if fused else out[0]


def _rms_bwd(x, g, dy, res=None, *, out_dtype, name):
    S, D = x.shape
    ts = _row_block(S, D)
    has_res = res is not None

    def body(*refs):
        if has_res:
            x_ref, g_ref, dy_ref, res_ref, dx_ref, dg_ref = refs
        else:
            x_ref, g_ref, dy_ref, dx_ref, dg_ref = refs
        i = pl.program_id(0)
        xf = x_ref[...].astype(F32)
        dyf = dy_ref[...].astype(F32)
        r = lax.rsqrt(jnp.mean(xf * xf, axis=-1, keepdims=True) + RMS_EPS)
        n = xf * r
        dn = dyf * g_ref[...]
        dx = r * (dn - n * jnp.mean(dn * n, axis=-1, keepdims=True))
        if has_res:
            dx = dx + res_ref[...]
        dx_ref[...] = dx.astype(out_dtype)
        part = jnp.sum((dyf * n).reshape(ts // 8, 8, D), axis=0)

        @pl.when(i == 0)
        def _():
            dg_ref[...] = part

        @pl.when(i > 0)
        def _():
            dg_ref[...] += part

    row = pl.BlockSpec((ts, D), lambda i: (i, 0))
    in_specs = [row, pl.BlockSpec((1, D), lambda i: (0, 0)), row] + ([row] if has_res else [])
    args = (x, g, dy) + ((res,) if has_res else ())
    dx, dg = pl.pallas_call(
        body, name=name,
        out_shape=(jax.ShapeDtypeStruct((S, D), out_dtype), jax.ShapeDtypeStruct((8, D), F32)),
        grid=(S // ts,), in_specs=in_specs,
        out_specs=(row, pl.BlockSpec((8, D), lambda i: (0, 0))),
        compiler_params=_cparams("arbitrary"),
    )(*args)
    return dx, dg


def _loss_head(y, t, *, name):
    S, D = y.shape
    ts = _row_block(S, D)

    def body(y_ref, t_ref, dy_ref, sq_ref):
        i = pl.program_id(0)
        e = y_ref[...] - t_ref[...]
        dy_ref[...] = e / D
        part = jnp.sum((e * e).reshape(ts // 8, 8, D), axis=0)

        @pl.when(i == 0)
        def _():
            sq_ref[...] = part

        @pl.when(i > 0)
        def _():
            sq_ref[...] += part

    row = pl.BlockSpec((ts, D), lambda i: (i, 0))
    return pl.pallas_call(
        body, name=name,
        out_shape=(jax.ShapeDtypeStruct((S, D), F32), jax.ShapeDtypeStruct((8, D), F32)),
        grid=(S // ts,), in_specs=[row, row],
        out_specs=(row, pl.BlockSpec((8, D), lambda i: (0, 0))),
        compiler_params=_cparams("arbitrary"),
    )(y, t)


def _window_counts(i, T, w):
    t = i * T + lax.broadcasted_iota(jnp.int32, (T, 1), 0)
    return jnp.minimum(t + 1, w).astype(F32)


def _pool_fwd(zu, pool_w, pool_scale, *, name):
    S, W = zu.shape
    T = _pick(S, 1024, 8)

    def body(u_ref, pw_ref, sc_ref, pooled_ref, mixed_ref, halo_ref):
        i = pl.program_id(0)

        @pl.when(i == 0)
        def _():
            halo_ref[...] = jnp.zeros_like(halo_ref)

        u = u_ref[...]
        ext = jnp.concatenate([halo_ref[...], u], axis=0)
        halo_ref[...] = u[T - HALO:, :]
        for g, w in enumerate(POOL_WINDOWS):
            cols = slice(g * POOL_GROUP, (g + 1) * POOL_GROUP)
            s = ext[:, cols]
            sh = 1
            while sh < w:
                s = s + pltpu.roll(s, sh, 0)
                sh *= 2
            pooled = s[HALO:, :] / _window_counts(i, T, w) - u[:, cols]
            pooled_bf = pooled.astype(BF16)
            pm = jnp.dot(pooled_bf, pw_ref[g].astype(BF16), preferred_element_type=F32)
            pooled_ref[:, cols] = pooled_bf
            mixed_ref[:, cols] = (pm * sc_ref[:, cols]).astype(BF16)

    row = pl.BlockSpec((T, W), lambda i: (i, 0))
    return pl.pallas_call(
        body, name=name,
        out_shape=(jax.ShapeDtypeStruct((S, W), BF16), jax.ShapeDtypeStruct((S, W), BF16)),
        grid=(S // T,),
        in_specs=[row, pl.BlockSpec(pool_w.shape, lambda i: (0, 0, 0)), pl.BlockSpec((1, W), lambda i: (0, 0))],
        out_specs=(row, row),
        scratch_shapes=[pltpu.VMEM((HALO, W), F32)],
        compiler_params=_cparams("arbitrary"),
    )(zu, pool_w, pool_scale)


def _pool_bwd(pooled, dmixed, pool_w, pool_scale, *, name):
    S, W = pooled.shape
    T = _pick(S, 1024, 8)
    nb = S // T

    def body(p_ref, dm_ref, pw_ref, sc_ref, dzu_ref, dpw_ref, dsc_ref, halo_ref):
        i = pl.program_id(0)
        blk = nb - 1 - i

        @pl.when(i == 0)
        def _():
            halo_ref[...] = jnp.zeros_like(halo_ref)
            dpw_ref[...] = jnp.zeros_like(dpw_ref)
            dsc_ref[...] = jnp.zeros_like(dsc_ref)

        for g, w in enumerate(POOL_WINDOWS):
            cols = slice(g * POOL_GROUP, (g + 1) * POOL_GROUP)
            p = p_ref[:, cols]
            dm = dm_ref[:, cols]
            pw = pw_ref[g].astype(BF16)
            pm = jnp.dot(p, pw, preferred_element_type=F32)
            dsc_ref[:, cols] += jnp.sum((dm * pm).reshape(T // 8, 8, POOL_GROUP), axis=0)
            dpm = (dm * sc_ref[:, cols]).astype(BF16)
            dpw_ref[g] += lax.dot_general(p, dpm, (((0,), (0,)), ((), ())), preferred_element_type=F32)
            dpooled = lax.dot_general(dpm, pw, (((1,), (1,)), ((), ())), preferred_element_type=F32)
            e = dpooled / _window_counts(blk, T, w)
            ext = jnp.concatenate([e, halo_ref[:, cols]], axis=0)
            halo_ref[:, cols] = e[:HALO, :]
            s = ext
            sh = 1
            while sh < w:
                s = s + pltpu.roll(s, T + HALO - sh, 0)
                sh *= 2
            dzu_ref[:, cols] = (s[:T, :] - dpooled).astype(BF16)

    row = pl.BlockSpec((T, W), lambda i: (nb - 1 - i, 0))
    return pl.pallas_call(
        body, name=name,
        out_shape=(jax.ShapeDtypeStruct((S, W), BF16), jax.ShapeDtypeStruct(pool_w.shape, F32),
                   jax.ShapeDtypeStruct((8, W), F32)),
        grid=(nb,),
        in_specs=[row, row, pl.BlockSpec(pool_w.shape, lambda i: (0, 0, 0)), pl.BlockSpec((1, W), lambda i: (0, 0))],
        out_specs=(row, pl.BlockSpec(pool_w.shape, lambda i: (0, 0, 0)), pl.BlockSpec((8, W), lambda i: (0, 0))),
        scratch_shapes=[pltpu.VMEM((HALO, W), F32)],
        compiler_params=_cparams("arbitrary"),
    )(pooled, dmixed, pool_w, pool_scale)


def _fgate_fwd(zf, bf, *, name):
    S, W = zf.shape
    T = _pick(S, 512, 8)

    def body(z_ref, b_ref, c_ref, aq_ref, ak_ref, carry_ref):
        i = pl.program_id(0)

        @pl.when(i == 0)
        def _():
            carry_ref[...] = jnp.zeros_like(carry_ref)

        a = z_ref[...] + b_ref[...]
        s = jnp.minimum(a, 0.0) - jnp.log(1.0 + jnp.exp(-jnp.abs(a)))
        row = lax.broadcasted_iota(jnp.int32, (T, W), 0)
        sh = 1
        while sh < T:
            s = s + jnp.where(row >= sh, pltpu.roll(s, sh, 0), 0.0)
            sh *= 2
        c = s + carry_ref[0:1, :]
        c_ref[...] = c
        carry_ref[...] = jnp.broadcast_to(c[T - 1:T, :], carry_ref.shape)
        lane = lax.broadcasted_iota(jnp.int32, (T, W), 1)
        for h in range(FOX_HEADS):
            ch = c[:, h:h + 1]
            hi = ch.astype(BF16).astype(F32)
            r1 = ch - hi
            lo = r1.astype(BF16).astype(F32)
            lo2 = (r1 - lo).astype(BF16).astype(F32)
            aq = jnp.where(lane == 0, hi, jnp.where(lane == 1, lo, jnp.where(lane == 2, lo2,
                                                                              jnp.where(lane < 6, 1.0, 0.0))))
            ak = jnp.where(lane < 3, 1.0, jnp.where(lane == 3, -hi, jnp.where(lane == 4, -lo,
                                                                               jnp.where(lane == 5, -lo2, 0.0))))
            aq_ref[h] = aq.astype(BF16)
            ak_ref[h] = ak.astype(BF16)

    aug = jax.ShapeDtypeStruct((FOX_HEADS, S, W), BF16)
    aug_spec = pl.BlockSpec((FOX_HEADS, T, W), lambda i: (0, i, 0))
    return pl.pallas_call(
        body, name=name, out_shape=(jax.ShapeDtypeStruct((S, W), F32), aug, aug), grid=(S // T,),
        in_specs=[pl.BlockSpec((T, W), lambda i: (i, 0)), pl.BlockSpec((1, W), lambda i: (0, 0))],
        out_specs=(pl.BlockSpec((T, W), lambda i: (i, 0)), aug_spec, aug_spec),
        scratch_shapes=[pltpu.VMEM((8, W), F32)],
        compiler_params=_cparams("arbitrary"),
    )(zf, bf)


def _fgate_bwd(zf, bf, dc, *, name):
    S, W = zf.shape
    T = _pick(S, 512, 8)
    nb = S // T

    def body(z_ref, b_ref, dc_ref, dz_ref, db_ref, carry_ref):
        i = pl.program_id(0)

        @pl.when(i == 0)
        def _():
            carry_ref[...] = jnp.zeros_like(carry_ref)
            db_ref[...] = jnp.zeros_like(db_ref)

        s = dc_ref[...]
        row = lax.broadcasted_iota(jnp.int32, (T, W), 0)
        sh = 1
        while sh < T:
            s = s + jnp.where(row < T - sh, pltpu.roll(s, T - sh, 0), 0.0)
            sh *= 2
        dlf = s + carry_ref[0:1, :]
        carry_ref[...] = jnp.broadcast_to(dlf[0:1, :], carry_ref.shape)
        dz = dlf * (1.0 - _sigmoid(z_ref[...] + b_ref[...]))
        dz_ref[...] = dz.astype(BF16)
        db_ref[...] += jnp.sum(dz.reshape(T // 8, 8, W), axis=0)

    row_spec = pl.BlockSpec((T, W), lambda i: (nb - 1 - i, 0))
    return pl.pallas_call(
        body, name=name,
        out_shape=(jax.ShapeDtypeStruct((S, W), BF16), jax.ShapeDtypeStruct((8, W), F32)),
        grid=(nb,),
        in_specs=[row_spec, pl.BlockSpec((1, W), lambda i: (0, 0)), row_spec],
        out_specs=(row_spec, pl.BlockSpec((8, W), lambda i: (0, 0))),
        scratch_shapes=[pltpu.VMEM((8, W), F32)],
        compiler_params=_cparams("arbitrary"),
    )(zf, bf, dc)


def _merge_fwd(zg, yp, yf, *, name):
    S, D = yp.shape
    ts = _row_block(S, D, 1024 * 1024)

    def body(zg_ref, yp_ref, yf_ref, o_ref):
        zg = zg_ref[...].astype(F32)
        o_ref[...] = (_sigmoid(zg[:, :D]) * yp_ref[...] + _sigmoid(zg[:, D:]) * yf_ref[...]).astype(BF16)

    row = pl.BlockSpec((ts, D), lambda i: (i, 0))
    return pl.pallas_call(
        body, name=name, out_shape=jax.ShapeDtypeStruct((S, D), BF16), grid=(S // ts,),
        in_specs=[pl.BlockSpec((ts, 2 * D), lambda i: (i, 0)), row, row], out_specs=row,
        compiler_params=_cparams("parallel"),
    )(zg, yp, yf)


def _merge_bwd(zg, yp, yf, dmerged, *, name):
    S, D = yp.shape
    ts = _row_block(S, D, 1024 * 1024)

    def body(zg_ref, yp_ref, yf_ref, dm_ref, dyp_ref, dyf_ref, dzg_ref):
        dm = dm_ref[...].astype(F32)
        zg = zg_ref[...].astype(F32)
        sp = _sigmoid(zg[:, :D])
        sf = _sigmoid(zg[:, D:])
        dyp_ref[...] = (dm * sp).astype(BF16)
        dyf_ref[...] = (dm * sf).astype(BF16)
        dzg_ref[:, :D] = (dm * yp_ref[...] * (sp * (1.0 - sp))).astype(BF16)
        dzg_ref[:, D:] = (dm * yf_ref[...] * (sf * (1.0 - sf))).astype(BF16)

    row = pl.BlockSpec((ts, D), lambda i: (i, 0))
    wide = pl.BlockSpec((ts, 2 * D), lambda i: (i, 0))
    return pl.pallas_call(
        body, name=name,
        out_shape=(jax.ShapeDtypeStruct((S, D), BF16), jax.ShapeDtypeStruct((S, D), BF16),
                   jax.ShapeDtypeStruct((S, 2 * D), BF16)),
        grid=(S // ts,), in_specs=[wide, row, row, row], out_specs=(row, row, wide),
        compiler_params=_cparams("parallel"),
    )(zg, yp, yf, dmerged)


NEG_BIG = -1e30


FOX_BLOCK = 1024
PAIR = LANES // FOX_DH
N_PAIRS = FOX_HEADS // PAIR


def _fox_fwd(zqkv, augq, augk, *, name):
    S = zqkv.shape[0]
    bq = _pick(S, FOX_BLOCK, 128)
    nq = S // bq
    scale = 1.0 / math.sqrt(FOX_DH)

    def body(q_ref, k_ref, v_ref, aq_ref, ak_ref, o_ref, lse_ref):
        i = pl.program_id(1)
        lane = lax.broadcasted_iota(jnp.int32, (1, LANES), 1)
        first = lane < FOX_DH
        q2 = q_ref[...] * scale
        zero = jnp.zeros_like(q2)
        qh = (jnp.concatenate([jnp.where(first, q2, zero), aq_ref[0]], axis=1),
              jnp.concatenate([jnp.where(first, zero, q2), aq_ref[1]], axis=1))

        def step(j, carry, masked):
            start = pl.multiple_of(j * bq, bq)
            kb = k_ref[pl.ds(start, bq), :]
            vb = v_ref[pl.ds(start, bq), :]
            one = jnp.ones_like(vb)
            vh = (jnp.where(first, vb, one), jnp.where(first, one, vb))
            out = []
            for h in range(PAIR):
                m, acc = carry[h]
                kh = jnp.concatenate([kb, ak_ref[h, pl.ds(start, bq), :]], axis=1)
                s = lax.dot_general(qh[h], kh, (((1,), (1,)), ((), ())), preferred_element_type=F32)
                if masked:
                    r = lax.broadcasted_iota(jnp.int32, (bq, bq), 0)
                    c = lax.broadcasted_iota(jnp.int32, (bq, bq), 1)
                    s = jnp.where(c <= r, s, NEG_BIG)
                m_new = jnp.maximum(m, jnp.max(s, axis=-1, keepdims=True))
                alpha = jnp.exp(m - m_new)
                p = jnp.exp(s - m_new).astype(BF16)
                acc = alpha * acc + jnp.dot(p, vh[h], preferred_element_type=F32)
                out.append((m_new, acc))
            return tuple(out)

        init = tuple((jnp.full((bq, 1), NEG_BIG, F32), jnp.zeros((bq, LANES), F32)) for _ in range(PAIR))
        carry = lax.fori_loop(0, i, lambda j, c: step(j, c, False), init)
        (ma, acca), (mb, accb) = step(i, carry, True)
        num = jnp.where(first, acca, accb)
        den = jnp.where(first, pltpu.roll(acca, FOX_DH, 1), pltpu.roll(accb, FOX_DH, 1))
        o_ref[...] = num / den
        lse_ref[...] = jnp.where(first, mb, ma) + jnp.log(jnp.where(first, accb, acca))

    npair = N_PAIRS
    return pl.pallas_call(
        body, name=name,
        out_shape=(jax.ShapeDtypeStruct((S, FOX_W), F32), jax.ShapeDtypeStruct((S, FOX_W), F32)),
        grid=(npair, nq),
        in_specs=[pl.BlockSpec((bq, LANES), lambda hp, i: (i, hp)),
                  pl.BlockSpec((S, LANES), lambda hp, i: (0, npair + hp)),
                  pl.BlockSpec((S, LANES), lambda hp, i: (0, 2 * npair + hp)),
                  pl.BlockSpec((PAIR, bq, LANES), lambda hp, i: (hp, i, 0)),
                  pl.BlockSpec((PAIR, S, LANES), lambda hp, i: (hp, 0, 0))],
        out_specs=(pl.BlockSpec((bq, LANES), lambda hp, i: (i, hp)),
                   pl.BlockSpec((bq, LANES), lambda hp, i: (i, hp))),
        compiler_params=_cparams("parallel", "arbitrary"),
    )(zqkv, zqkv, zqkv, augq, augk)


def _head_rowsum(a, b, *, name):
    S, W = a.shape
    ts = _pick(S, 1024, 8)

    def body(a_ref, b_ref, o_ref):
        prod = a_ref[...].astype(F32) * b_ref[...].astype(F32)
        hi = prod.astype(BF16)
        lo = (prod - hi.astype(F32)).astype(BF16)
        r = lax.broadcasted_iota(jnp.int32, (W, LANES), 0)
        c = lax.broadcasted_iota(jnp.int32, (W, LANES), 1)
        sel = jnp.where(r // FOX_DH == c, 1.0, 0.0).astype(BF16)
        o_ref[...] = (jnp.dot(hi, sel, preferred_element_type=F32) + jnp.dot(lo, sel, preferred_element_type=F32))

    return pl.pallas_call(
        body, name=name, out_shape=jax.ShapeDtypeStruct((S, LANES), F32), grid=(S // ts,),
        in_specs=[pl.BlockSpec((ts, W), lambda i: (i, 0)), pl.BlockSpec((ts, W), lambda i: (i, 0))],
        out_specs=pl.BlockSpec((ts, LANES), lambda i: (i, 0)),
        compiler_params=_cparams("parallel"),
    )(a, b)


def _fox_bwd(zqkv, do, augq, augk, lse_row, delta_row, *, name):
    S = zqkv.shape[0]
    bk = _pick(S, FOX_BLOCK, 128)
    nk = S // bk
    scale = 1.0 / math.sqrt(FOX_DH)
    npair = N_PAIRS

    def body(q_ref, k_ref, v_ref, do_ref, ak_ref, aq_ref, lse_ref, dl_ref, dq_ref, dk_ref, dv_ref, dck_ref, dcq_ref):
        j = pl.program_id(1)

        @pl.when(j == 0)
        def _():
            dq_ref[...] = jnp.zeros_like(dq_ref)
            dcq_ref[...] = jnp.zeros_like(dcq_ref)

        lane = lax.broadcasted_iota(jnp.int32, (1, LANES), 1)
        first = lane < FOX_DH
        kb = k_ref[...]
        vb = v_ref[...]
        kh = (jnp.concatenate([kb, ak_ref[0]], axis=1), jnp.concatenate([kb, ak_ref[1]], axis=1))

        def step(i, carry, masked):
            start = pl.multiple_of(i * bk, bk)
            qs = q_ref[pl.ds(start, bk), :] * scale
            dob = do_ref[pl.ds(start, bk), :]
            zero = jnp.zeros_like(qs)
            qh = (jnp.where(first, qs, zero), jnp.where(first, zero, qs))
            doh = (jnp.where(first, dob, zero), jnp.where(first, zero, dob))
            out = []
            dqc = []
            for h in range(PAIR):
                dk, dv, dc = carry[h]
                qaug = jnp.concatenate([qh[h], aq_ref[h, pl.ds(start, bk), :]], axis=1)
                lse = lse_ref[h, :, pl.ds(start, bk)]
                dl = dl_ref[h, :, pl.ds(start, bk)]
                st = lax.dot_general(kh[h], qaug, (((1,), (1,)), ((), ())), preferred_element_type=F32)
                pt = jnp.exp(st - lse)
                if masked:
                    r = lax.broadcasted_iota(jnp.int32, (bk, bk), 0)
                    c = lax.broadcasted_iota(jnp.int32, (bk, bk), 1)
                    pt = jnp.where(c >= r, pt, 0.0)
                dpt = lax.dot_general(vb, doh[h], (((1,), (1,)), ((), ())), preferred_element_type=F32)
                dst = pt * (dpt - dl)
                pt_bf = pt.astype(BF16)
                dst_bf = dst.astype(BF16)
                dv = dv + jnp.dot(pt_bf, dob, preferred_element_type=F32)
                dk = dk + jnp.dot(dst_bf, qs, preferred_element_type=F32)
                dc = dc - jnp.sum(dst, axis=-1, keepdims=True)
                dcq_ref[h, :, pl.ds(start, bk)] += jnp.sum(dst, axis=0, keepdims=True)
                dqc.append(lax.dot_general(dst_bf, kb, (((0,), (0,)), ((), ())), preferred_element_type=F32))
                out.append((dk, dv, dc))
            dq_ref[pl.ds(start, bk), :] += jnp.where(first, dqc[0], dqc[1])
            return tuple(out)

        init = tuple((jnp.zeros((bk, LANES), F32), jnp.zeros((bk, LANES), F32), jnp.zeros((bk, 1), F32))
                     for _ in range(PAIR))
        carry = step(j, init, True)
        (dka, dva, dca), (dkb, dvb, dcb) = lax.fori_loop(j + 1, nk, lambda i, c: step(i, c, False), carry)
        dk_ref[...] = jnp.where(first, dka, dkb).astype(BF16)
        dv_ref[...] = jnp.where(first, dva, dvb).astype(BF16)
        dck_ref[0] = dca
        dck_ref[1] = dcb

        @pl.when(j == nk - 1)
        def _():
            dq_ref[...] = dq_ref[...] * scale

    rowfull = pl.BlockSpec((PAIR, 1, S), lambda hp, j: (hp, 0, 0))
    return pl.pallas_call(
        body, name=name,
        out_shape=(jax.ShapeDtypeStruct((S, FOX_W), F32), jax.ShapeDtypeStruct((S, FOX_W), BF16),
                   jax.ShapeDtypeStruct((S, FOX_W), BF16), jax.ShapeDtypeStruct((FOX_HEADS, S, 1), F32),
                   jax.ShapeDtypeStruct((FOX_HEADS, 1, S), F32)),
        grid=(npair, nk),
        in_specs=[pl.BlockSpec((S, LANES), lambda hp, j: (0, hp)),
                  pl.BlockSpec((bk, LANES), lambda hp, j: (j, npair + hp)),
                  pl.BlockSpec((bk, LANES), lambda hp, j: (j, 2 * npair + hp)),
                  pl.BlockSpec((S, LANES), lambda hp, j: (0, hp)),
                  pl.BlockSpec((PAIR, bk, LANES), lambda hp, j: (hp, j, 0)),
                  pl.BlockSpec((PAIR, S, LANES), lambda hp, j: (hp, 0, 0)), rowfull, rowfull],
        out_specs=(pl.BlockSpec((S, LANES), lambda hp, j: (0, hp)),
                   pl.BlockSpec((bk, LANES), lambda hp, j: (j, hp)),
                   pl.BlockSpec((bk, LANES), lambda hp, j: (j, hp)),
                   pl.BlockSpec((PAIR, bk, 1), lambda hp, j: (hp, j, 0)), rowfull),
        compiler_params=_cparams("parallel", "arbitrary"),
    )(zqkv, zqkv, zqkv, do, augk, augq, lse_row, delta_row)


def _xattn_fwd(q, kv, *, name):
    S, W = q.shape
    M = kv.shape[0]
    tq = _pick(S, 512, 8)
    scale = 1.0 / math.sqrt(X_DH)

    def body(q_ref, kv_ref, o_ref):
        for h in range(X_HEADS):
            cols = slice(h * X_DH, (h + 1) * X_DH)
            vcols = slice(W + h * X_DH, W + (h + 1) * X_DH)
            s = lax.dot_general(q_ref[:, cols], kv_ref[:, cols], (((1,), (1,)), ((), ())),
                                preferred_element_type=F32) * scale
            e = jnp.exp(s - jnp.max(s, axis=-1, keepdims=True))
            p = e / jnp.sum(e, axis=-1, keepdims=True)
            o_ref[:, cols] = jnp.dot(p.astype(BF16), kv_ref[:, vcols], preferred_element_type=F32).astype(BF16)

    return pl.pallas_call(
        body, name=name, out_shape=jax.ShapeDtypeStruct((S, W), BF16), grid=(S // tq,),
        in_specs=[pl.BlockSpec((tq, W), lambda i: (i, 0)), pl.BlockSpec((M, 2 * W), lambda i: (0, 0))],
        out_specs=pl.BlockSpec((tq, W), lambda i: (i, 0)),
        compiler_params=_cparams("parallel"),
    )(q, kv)


def _xattn_bwd(q, kv, do, *, name):
    S, W = q.shape
    M = kv.shape[0]
    tq = _pick(S, 512, 8)
    scale = 1.0 / math.sqrt(X_DH)

    def body(q_ref, kv_ref, do_ref, dq_ref, dkv_ref):
        i = pl.program_id(0)

        @pl.when(i == 0)
        def _():
            dkv_ref[...] = jnp.zeros_like(dkv_ref)

        for h in range(X_HEADS):
            cols = slice(h * X_DH, (h + 1) * X_DH)
            vcols = slice(W + h * X_DH, W + (h + 1) * X_DH)
            qh = q_ref[:, cols]
            kh = kv_ref[:, cols]
            vh = kv_ref[:, vcols]
            doh = do_ref[:, cols]
            s = lax.dot_general(qh, kh, (((1,), (1,)), ((), ())), preferred_element_type=F32) * scale
            e = jnp.exp(s - jnp.max(s, axis=-1, keepdims=True))
            p = e / jnp.sum(e, axis=-1, keepdims=True)
            dp = lax.dot_general(doh, vh, (((1,), (1,)), ((), ())), preferred_element_type=F32)
            ds = (p * (dp - jnp.sum(p * dp, axis=-1, keepdims=True)) * scale).astype(BF16)
            dq_ref[:, cols] = jnp.dot(ds, kh, preferred_element_type=F32).astype(BF16)
            dkv_ref[:, cols] += lax.dot_general(ds, qh, (((0,), (0,)), ((), ())), preferred_element_type=F32)
            dkv_ref[:, vcols] += lax.dot_general(p.astype(BF16), doh, (((0,), (0,)), ((), ())),
                                                 preferred_element_type=F32)

    return pl.pallas_call(
        body, name=name,
        out_shape=(jax.ShapeDtypeStruct((S, W), BF16), jax.ShapeDtypeStruct((M, 2 * W), F32)),
        grid=(S // tq,),
        in_specs=[pl.BlockSpec((tq, W), lambda i: (i, 0)), pl.BlockSpec((M, 2 * W), lambda i: (0, 0)),
                  pl.BlockSpec((tq, W), lambda i: (i, 0))],
        out_specs=(pl.BlockSpec((tq, W), lambda i: (i, 0)), pl.BlockSpec((M, 2 * W), lambda i: (0, 0))),
        compiler_params=_cparams("arbitrary"),
    )(q, kv, do)


GELU_C = math.sqrt(2.0 / math.pi)
GELU_A = 0.044715
CONV_HALO = 16


def _gelu_parts(x):
    u = GELU_C * (x + GELU_A * x * x * x)
    t = jnp.tanh(u)
    g = 0.5 * x * (1.0 + t)
    dg = 0.5 * (1.0 + t) + 0.5 * x * (1.0 - t * t) * (GELU_C * (1.0 + 3.0 * GELU_A * x * x))
    return g, dg


def _conv3(ext, w_ref, b_ref):
    return (w_ref[2:3, :] * ext + w_ref[1:2, :] * pltpu.roll(ext, 1, 0)
            + w_ref[0:1, :] * pltpu.roll(ext, 2, 0) + b_ref[...])


def _convglu_fwd(z, conv_w, conv_b, *, name):
    S, F2 = z.shape
    F = F2 // 2
    tc = _pick(F, 1408)
    ncol = F // tc
    T = _pick(S, 512, 8)
    hb = T // CONV_HALO

    def body(zg_ref, zu_ref, zgp_ref, zup_ref, wg_ref, wu_ref, bg_ref, bu_ref, act_ref):
        i = pl.program_id(1)
        first = (i > 0).astype(F32)

        def conv(z_ref, zp_ref, w_ref, b_ref):
            ext = jnp.concatenate([zp_ref[...].astype(F32) * first, z_ref[...].astype(F32)], axis=0)
            return _conv3(ext, w_ref, b_ref)[CONV_HALO:, :]

        gc = conv(zg_ref, zgp_ref, wg_ref, bg_ref)
        uc = conv(zu_ref, zup_ref, wu_ref, bu_ref)
        act_ref[...] = (_gelu_parts(gc)[0] * uc).astype(BF16)

    cur = lambda off: pl.BlockSpec((T, tc), lambda j, i: (i, j + off))
    prev = lambda off: pl.BlockSpec((CONV_HALO, tc), lambda j, i: (jnp.maximum(i * hb - 1, 0), j + off))
    vec = lambda rows, off: pl.BlockSpec((rows, tc), lambda j, i: (0, j + off))
    return pl.pallas_call(
        body, name=name, out_shape=jax.ShapeDtypeStruct((S, F), BF16), grid=(ncol, S // T),
        in_specs=[cur(0), cur(ncol), prev(0), prev(ncol), vec(3, 0), vec(3, ncol), vec(1, 0), vec(1, ncol)],
        out_specs=pl.BlockSpec((T, tc), lambda j, i: (i, j)),
        compiler_params=_cparams("parallel", "parallel"),
    )(z, z, z, z, conv_w, conv_w, conv_b, conv_b)


def _convglu_bwd(z, dact, conv_w, conv_b, *, name):
    S, F2 = z.shape
    F = F2 // 2
    tc = _pick(F, 1408)
    ncol = F // tc
    T = _pick(S, 256, 8)
    nrow = S // T
    hb = T // CONV_HALO
    TE = T + CONV_HALO

    def body(z_ref, zp_ref, zn_ref, da_ref, dan_ref, w_ref, b_ref, dz_ref, dw_ref):
        i = pl.program_id(0)
        first = (i > 0).astype(F32)
        last = (i < nrow - 1).astype(F32)

        @pl.when(i == 0)
        def _():
            dw_ref[...] = jnp.zeros_like(dw_ref)

        def ext_of(cols):
            return jnp.concatenate([zp_ref[:, cols].astype(F32) * first, z_ref[:, cols].astype(F32),
                                    zn_ref[:, cols].astype(F32)], axis=0)

        def back(d, ext, w, cols):
            dz = w[2:3, :] * d + w[1:2, :] * pltpu.roll(d, TE - 1, 0) + w[0:1, :] * pltpu.roll(d, TE - 2, 0)
            dz_ref[:, cols] = dz[:T, :].astype(BF16)
            dc = d[:T, :]
            z0 = ext[CONV_HALO:CONV_HALO + T, :]
            z1 = pltpu.roll(ext, 1, 0)[CONV_HALO:CONV_HALO + T, :]
            z2 = pltpu.roll(ext, 2, 0)[CONV_HALO:CONV_HALO + T, :]
            rows = [jnp.sum(dc * z2, axis=0, keepdims=True), jnp.sum(dc * z1, axis=0, keepdims=True),
                    jnp.sum(dc * z0, axis=0, keepdims=True), jnp.sum(dc, axis=0, keepdims=True)]
            dw_ref[0:4, cols] += jnp.concatenate(rows, axis=0)

        for jj in range(ncol):
            cg = slice(jj * tc, (jj + 1) * tc)
            cu = slice(F + jj * tc, F + (jj + 1) * tc)
            extg, extu = ext_of(cg), ext_of(cu)
            wg, wu = w_ref[:, cg], w_ref[:, cu]
            gc = _conv3(extg, wg, b_ref[:, cg])[CONV_HALO:, :]
            uc = _conv3(extu, wu, b_ref[:, cu])[CONV_HALO:, :]
            da = jnp.concatenate([da_ref[:, cg].astype(F32), dan_ref[:, cg].astype(F32) * last], axis=0)
            gl, dgl = _gelu_parts(gc)
            back(da * uc * dgl, extg, wg, cg)
            back(da * gl, extu, wu, cu)

    halo_rows = S // CONV_HALO
    return pl.pallas_call(
        body, name=name,
        out_shape=(jax.ShapeDtypeStruct((S, F2), BF16), jax.ShapeDtypeStruct((8, F2), F32)),
        grid=(nrow,),
        in_specs=[pl.BlockSpec((T, F2), lambda i: (i, 0)),
                  pl.BlockSpec((CONV_HALO, F2), lambda i: (jnp.maximum(i * hb - 1, 0), 0)),
                  pl.BlockSpec((CONV_HALO, F2), lambda i: (jnp.minimum((i + 1) * hb, halo_rows - 1), 0)),
                  pl.BlockSpec((T, F), lambda i: (i, 0)),
                  pl.BlockSpec((CONV_HALO, F), lambda i: (jnp.minimum((i + 1) * hb, halo_rows - 1), 0)),
                  pl.BlockSpec((3, F2), lambda i: (0, 0)), pl.BlockSpec((1, F2), lambda i: (0, 0))],
        out_specs=(pl.BlockSpec((T, F2), lambda i: (i, 0)), pl.BlockSpec((8, F2), lambda i: (0, 0))),
        compiler_params=_cparams("arbitrary"),
    )(z, z, z, dact, dact, conv_w, conv_b)


OFF_QKV = POOL_W
OFF_G = POOL_W + 3 * FOX_W


SHARD_BY_ROWS = {"w_in": False, "w_pool_br": False, "w_fox_br": False, "w_mix_out": True, "w_xq": True,
                 "w_xkv": True, "w_xo": False, "w_up": False, "w_down": True, "conv_w": False}


def _prep_layer_weights(w, l):
    n, _, D, b = w["w_in"].shape
    w_in = w["w_in"][:, l].transpose(1, 0, 2).reshape(D, n * b)
    off_f = OFF_G
    pad = jnp.zeros((D, LANES - N_FGATE), w_in.dtype)
    w_in_r = jnp.concatenate([w_in[:, :off_f], w_in[:, off_f + N_FGATE:], w_in[:, off_f:off_f + N_FGATE], pad], axis=1)
    wl = {k: _ChipMajor(w[k], l, SHARD_BY_ROWS[k]) for k in MATMUL_WEIGHTS if k != "w_in"}
    wl["w_in_r"] = w_in_r
    return wl


def _row(v):
    return v.reshape(1, -1)


def _layer_fwd(x, h1, mem, wl, p, l):
    S, D = x.shape
    n = lambda s: f"l{l}_{s}"
    sv = {"x0": x}
    w_in_r = wl["w_in_r"]
    og = OFF_G
    zu = _matmul(h1, w_in_r[:, :OFF_QKV], out_dtype=F32, name=n("mm_zu"))
    zqkv = _matmul(h1, w_in_r[:, OFF_QKV:og], out_dtype=BF16, name=n("mm_zqkv"))
    zg = _matmul(h1, w_in_r[:, og:og + 2 * D], out_dtype=BF16, name=n("mm_zg"))
    zf = _matmul(h1, w_in_r[:, og + 2 * D:], out_dtype=F32, name=n("mm_zf"))
    bf = jnp.pad(p["b_forget"][l], (0, LANES - N_FGATE)).reshape(1, LANES)
    _, augq, augk = _fgate_fwd(zf, bf, name=n("fgate"))
    o_fox, lse2 = _fox_fwd(zqkv, augq, augk, name=n("fox"))
    pooled, mixed = _pool_fwd(zu, p["pool_w"][l], _row(p["pool_scale"][l]), name=n("pool"))
    y_pool = _matmul(mixed, wl["w_pool_br"], out_dtype=BF16, name=n("mm_ypool"))
    y_fox = _matmul(o_fox, wl["w_fox_br"], out_dtype=BF16, name=n("mm_yfox"))
    merged = _merge_fwd(zg, y_pool, y_fox, name=n("merge"))
    r1 = _matmul(merged, wl["w_mix_out"], out_dtype=BF16, name=n("mm_r1"))
    x1, h2 = _add_rms(x, r1, _row(p["mix_post_g"][l]), _row(p["xa_pre_g"][l]), name=n("addrms1"))
    sv.update(h1=h1, zg=zg, zf=zf, bf=bf, zqkv=zqkv, augq=augq, augk=augk, o_fox=o_fox, lse2=lse2,
              pooled=pooled, mixed=mixed, y_pool=y_pool, y_fox=y_fox, merged=merged, r1=r1, x1=x1)
    mem_n = _rms_fwd(mem, _row(p["mem_g"][l]), out_dtype=BF16, name=n("rms_mem"))
    q2 = _matmul(h2, wl["w_xq"], out_dtype=BF16, name=n("mm_q2"))
    kv = _matmul(mem_n, wl["w_xkv"], out_dtype=BF16, name=n("mm_kv"))
    o2 = _xattn_fwd(q2, kv, name=n("xattn"))
    a2 = _matmul(o2, wl["w_xo"], out_dtype=BF16, name=n("mm_a2"))
    x2, h3 = _add_rms(x1, a2, _row(p["xa_post_g"][l]), _row(p["ffn_pre_g"][l]), name=n("addrms2"))
    sv.update(h2=h2, mem_n=mem_n, q2=q2, kv=kv, o2=o2, a2=a2, x2=x2)
    z3 = _matmul(h3, wl["w_up"], out_dtype=BF16, name=n("mm_z3"))
    act = _convglu_fwd(z3, p["conv_w"][l], _row(p["conv_b"][l]), name=n("convglu"))
    d3 = _matmul(act, wl["w_down"], out_dtype=BF16, name=n("mm_d3"))
    if l + 1 < p["mix_pre_g"].shape[0]:
        x3, h_next = _add_rms(x2, d3, _row(p["ffn_post_g"][l]), _row(p["mix_pre_g"][l + 1]), name=n("addrms3"))
    else:
        x3, h_next = _add_rms(x2, d3, _row(p["ffn_post_g"][l]), name=n("addrms3")), None
    sv.update(h3=h3, z3=z3, act=act, d3=d3)
    return x3, h_next, sv


def _layer_bwd(dx, mem, wl, p, l, sv, gbuf):
    S, D = dx.shape
    n_layers = p["mix_pre_g"].shape[0]
    n = lambda s: f"l{l}_b_{s}"
    g = {}
    red = lambda part: jnp.sum(part, axis=0)

    def wgrad(k, lhs, rhs, nm):
        how = dict(out_rows=N_CHIPS) if SHARD_BY_ROWS[k] else dict(out_chips=N_CHIPS)
        return _matmul(lhs, rhs, ta=True, out_dtype=F32, out_layer=(gbuf.get(k), l, n_layers), name=n(nm), **how)

    dd3, dg = _rms_bwd(sv["d3"], _row(p["ffn_post_g"][l]), dx, out_dtype=BF16, name=n("rms3post"))
    g["ffn_post_g"] = red(dg)
    dact = _matmul(dd3, wl["w_down"], tb=True, out_dtype=BF16, name=n("mm_dact"))
    g["w_down"] = wgrad("w_down", sv["act"], dd3, "mm_dwdown")
    dz3, dconv = _convglu_bwd(sv["z3"], dact, p["conv_w"][l], _row(p["conv_b"][l]), name=n("convglu"))
    g["conv_w"] = dconv[:3]
    g["conv_b"] = dconv[3]
    dh3 = _matmul(dz3, wl["w_up"], tb=True, out_dtype=BF16, name=n("mm_dh3"))
    g["w_up"] = wgrad("w_up", sv["h3"], dz3, "mm_dwup")
    dx, dg = _rms_bwd(sv["x2"], _row(p["ffn_pre_g"][l]), dh3, dx, out_dtype=F32, name=n("rms3pre"))
    g["ffn_pre_g"] = red(dg)
    da2, dg = _rms_bwd(sv["a2"], _row(p["xa_post_g"][l]), dx, out_dtype=BF16, name=n("rms2post"))
    g["xa_post_g"] = red(dg)
    do2 = _matmul(da2, wl["w_xo"], tb=True, out_dtype=BF16, name=n("mm_do2"))
    g["w_xo"] = wgrad("w_xo", sv["o2"], da2, "mm_dwxo")
    dq2, dkv = _xattn_bwd(sv["q2"], sv["kv"], do2, name=n("xattn"))
    dh2 = _matmul(dq2, wl["w_xq"], tb=True, out_dtype=BF16, name=n("mm_dh2"))
    g["w_xq"] = wgrad("w_xq", sv["h2"], dq2, "mm_dwxq")
    dmem_n = _matmul(dkv, wl["w_xkv"], tb=True, out_dtype=F32, name=n("mm_dmemn"))
    g["w_xkv"] = wgrad("w_xkv", sv["mem_n"], dkv, "mm_dwxkv")
    _, dg = _rms_bwd(mem, _row(p["mem_g"][l]), dmem_n, out_dtype=BF16, name=n("rms_mem"))
    g["mem_g"] = red(dg)
    dx, dg = _rms_bwd(sv["x1"], _row(p["xa_pre_g"][l]), dh2, dx, out_dtype=F32, name=n("rms2pre"))
    g["xa_pre_g"] = red(dg)
    dr1, dg = _rms_bwd(sv["r1"], _row(p["mix_post_g"][l]), dx, out_dtype=BF16, name=n("rms1post"))
    g["mix_post_g"] = red(dg)
    dmerged = _matmul(dr1, wl["w_mix_out"], tb=True, out_dtype=BF16, name=n("mm_dmerged"))
    g["w_mix_out"] = wgrad("w_mix_out", sv["merged"], dr1, "mm_dwmo")
    dyp, dyf, dzg = _merge_bwd(sv["zg"], sv["y_pool"], sv["y_fox"], dmerged, name=n("merge"))
    dmixed = _matmul(dyp, wl["w_pool_br"], tb=True, out_dtype=F32, name=n("mm_dmixed"))
    g["w_pool_br"] = wgrad("w_pool_br", sv["mixed"], dyp, "mm_dwpb")
    dofox = _matmul(dyf, wl["w_fox_br"], tb=True, out_dtype=F32, name=n("mm_dofox"))
    g["w_fox_br"] = wgrad("w_fox_br", sv["o_fox"], dyf, "mm_dwfb")
    dzu, dpw, dsc = _pool_bwd(sv["pooled"], dmixed, p["pool_w"][l], _row(p["pool_scale"][l]), name=n("pool"))
    g["pool_w"] = dpw
    g["pool_scale"] = red(dsc)
    delta = _head_rowsum(dofox, sv["o_fox"], name=n("delta"))
    lse = sv["lse2"].reshape(S, N_PAIRS, PAIR, FOX_DH)[:, :, ::-1, 0].reshape(S, FOX_HEADS)
    dq, dk, dv, dck, dcq = _fox_bwd(sv["zqkv"], dofox.astype(BF16), sv["augq"], sv["augk"],
                                    lse.T.reshape(FOX_HEADS, 1, S), delta[:, :FOX_HEADS].T.reshape(FOX_HEADS, 1, S),
                                    name=n("fox"))
    dc = dcq.reshape(FOX_HEADS, S) + dck.reshape(FOX_HEADS, S)
    dc_pad = jnp.pad(dc.T, ((0, 0), (0, LANES - FOX_HEADS)))
    dzf, db = _fgate_bwd(sv["zf"], sv["bf"], dc_pad, name=n("fgate"))
    g["b_forget"] = red(db)[:N_FGATE]
    dz_cat = jnp.concatenate([dzu, dq.astype(BF16), dk, dv, dzg, dzf], axis=1)
    dh1 = _matmul(dz_cat, wl["w_in_r"], tb=True, out_dtype=BF16, name=n("mm_dh1"))
    dw_in_r = _matmul(sv["h1"], dz_cat, ta=True, out_dtype=F32, name=n("mm_dwin"))
    og = OFF_G
    g["w_in"] = jnp.concatenate([dw_in_r[:, :og], dw_in_r[:, og + 2 * D:og + 2 * D + N_FGATE],
                                 dw_in_r[:, og:og + 2 * D]], axis=1)
    dx, dg = _rms_bwd(sv["x0"], _row(p["mix_pre_g"][l]), dh1, dx, out_dtype=F32, name=n("rms1pre"))
    g["mix_pre_g"] = red(dg)
    return dx, g


MATMUL_WEIGHTS = ("w_in", "w_pool_br", "w_fox_br", "w_mix_out", "w_xq", "w_xkv", "w_xo", "w_up", "w_down")
WEIGHT_NAMES = ("mix_pre_g", "mix_post_g", "w_in", "b_forget", "pool_w", "pool_scale", "w_pool_br", "w_fox_br",
                "w_mix_out", "xa_pre_g", "xa_post_g", "mem_g", "w_xq", "w_xkv", "w_xo", "ffn_pre_g", "ffn_post_g",
                "w_up", "conv_w", "conv_b", "w_down")


def _local_step(x, mem, loss_target, wfull, p):
    L = p["mix_pre_g"].shape[0]
    saved, wls = [], []
    h = x
    hn = _rms_fwd(x, _row(p["mix_pre_g"][0]), out_dtype=BF16, name="rms_first")
    for l in range(L):
        wl = _prep_layer_weights(wfull, l)
        h, hn, sv = _layer_fwd(h, hn, mem, wl, p, l)
        saved.append(sv)
        wls.append(wl)
    D = x.shape[1]
    dy, sq = _loss_head(h, loss_target, name="loss_head")
    loss = 0.5 * jnp.sum(sq) / D
    grads = []
    gbuf = {}
    dx = dy
    for l in reversed(range(L)):
        dx, g = _layer_bwd(dx, mem, wls[l], p, l, saved[l], gbuf)
        gbuf = {k: g[k] for k in MATMUL_WEIGHTS if k != "w_in"}
        grads.append(g)
    grads = grads[::-1]

    def chip_major(g):
        return g.reshape(g.shape[0], N_CHIPS, g.shape[1] // N_CHIPS).transpose(1, 0, 2)

    gfull = {k: jnp.stack([grads[l][k] for l in range(L)]) for k in REPLICATED}
    gfull.update({k: jnp.stack([chip_major(grads[l][k]) for l in range(L)], axis=1) for k in ("w_in", "conv_w")})
    gfull.update(gbuf)
    return loss, dx, gfull


PACK_W = 512
PACK_ROW_ALIGN = 1024
N_CHIPS = 4
N_DEV = 8
SHARDED = (("w_in", 2), ("w_pool_br", 2), ("w_fox_br", 2), ("w_mix_out", 1), ("w_xq", 1), ("w_xkv", 1),
           ("w_xo", 2), ("w_up", 2), ("w_down", 1), ("conv_w", 2))
REPLICATED = ("mix_pre_g", "mix_post_g", "b_forget", "pool_w", "pool_scale", "xa_pre_g", "xa_post_g", "mem_g",
              "ffn_pre_g", "ffn_post_g", "conv_b")


def _round_up(n, m):
    return -(-n // m) * m


def _pack(arrs, rows):
    flat = jnp.concatenate([a.reshape(-1) for a in arrs])
    return jnp.pad(flat, (0, rows * PACK_W - flat.shape[0])).reshape(rows, PACK_W)


def _unpack(buf, shapes):
    flat = buf.reshape(-1)
    out, off = [], 0
    for s in shapes:
        n = math.prod(s)
        out.append(flat[off:off + n].reshape(s))
        off += n
    return out


ANY = pl.BlockSpec(memory_space=pl.ANY)


def _remote(send_sems, recv_sems, k, src, dst, to):
    return pltpu.make_async_remote_copy(src_ref=src, dst_ref=dst, send_sem=send_sems.at[k], recv_sem=recv_sems.at[k],
                                        device_id=to, device_id_type=MESH)


def _my_place():
    return lax.axis_index("x"), lax.axis_index("y"), lax.axis_index("c")


def _rows_per_block(a, b, cap_bytes=1024 * 1024):
    if a % 8:
        return a
    return _pick(a, max(8, cap_bytes // (4 * b) // 8 * 8), 8)


def _place_shard(w, chip, dtype, *, name):
    L, a, b = w.shape
    ta = _rows_per_block(a, b)

    def body(chip_ref, w_ref, o_ref):
        o_ref[...] = w_ref[...].astype(dtype)

    return pl.pallas_call(
        body, name=name, out_shape=jax.ShapeDtypeStruct((N_CHIPS, L, a, b), dtype),
        grid_spec=pltpu.PrefetchScalarGridSpec(
            num_scalar_prefetch=1, grid=(L, a // ta),
            in_specs=[pl.BlockSpec((None, ta, b), lambda l, i, chip_ref: (l, i, 0))],
            out_specs=pl.BlockSpec((None, None, ta, b), lambda l, i, chip_ref: (chip_ref[0], l, i, 0))),
        compiler_params=_cparams("parallel", "parallel"),
    )(chip, w)


def _layer_halves(L, c):
    assert L % 2 == 0
    return pl.ds(c * (L // 2), L // 2), pl.ds((1 - c) * (L // 2), L // 2)


def _gather_weights(bufs):
    n = len(bufs)
    L = bufs[0].shape[1]

    def body(*refs):
        outs, (send_sems, recv_sems) = refs[n:2 * n], refs[2 * n:]
        x, y, c = _my_place()
        me = 2 * x + y
        sibling = (x, y, 1 - c)
        chips = [(1 - x, y), (x, 1 - y), (1 - x, 1 - y)]
        half, other = _layer_halves(L, c)
        rc = functools.partial(_remote, send_sems, recv_sems)
        first = [rc(6 * w + k, o.at[me, half], o.at[me, half], (px, py, c))
                 for w, o in enumerate(outs) for k, (px, py) in enumerate(chips)]
        for cp in first:
            cp.start()
        passed = []
        for k, (px, py) in enumerate(chips):
            src = 2 * px + py
            for w, o in enumerate(outs):
                rc(6 * w + k, o.at[src, half], o.at[src, half], (px, py, c)).wait_recv()
                fwd = rc(6 * w + 3 + k, o.at[src, half], o.at[src, half], sibling)
                fwd.start()
                passed.append(fwd)
        for k, (px, py) in enumerate(chips):
            src = 2 * px + py
            for w, o in enumerate(outs):
                rc(6 * w + 3 + k, o.at[src, other], o.at[src, other], sibling).wait_recv()
        for cp in first + passed:
            cp.wait_send()

    return pl.pallas_call(
        body, name="gather_weights",
        out_shape=tuple(jax.ShapeDtypeStruct(b.shape, b.dtype) for b in bufs),
        in_specs=[ANY] * n, out_specs=tuple([ANY] * n), input_output_aliases={i: i for i in range(n)},
        scratch_shapes=[pltpu.SemaphoreType.DMA((6 * n,)), pltpu.SemaphoreType.DMA((6 * n,))],
    )(*bufs)


def _exchange_halves(Gs, rep):
    n = len(Gs)
    L = Gs[0].shape[1]
    RR, W = rep.shape

    def body(*refs):
        g_refs, rep_ref, ra_refs, rall_ref = refs[:n], refs[n], refs[n + 1:2 * n + 1], refs[2 * n + 1]
        send_sems, recv_sems, local_sem = refs[2 * n + 2:]
        x, y, c = _my_place()
        me = 4 * x + 2 * y + c
        rc = functools.partial(_remote, send_sems, recv_sems)
        _, other = _layer_halves(L, c)

        def peer(idx):
            px = (1 - x) if (idx >> 2) & 1 else x
            py = (1 - y) if (idx >> 1) & 1 else y
            pc = (1 - c) if idx & 1 else c
            return px, py, pc

        loc = pltpu.make_async_copy(rep_ref, rall_ref.at[me], local_sem)
        loc.start()
        cps = [rc(N_DEV + w, g.at[:, other], ra, (x, y, 1 - c)) for w, (g, ra) in enumerate(zip(g_refs, ra_refs))]
        for idx in range(1, N_DEV):
            cps.append(rc(idx, rep_ref, rall_ref.at[me], peer(idx)))
        for cp in cps:
            cp.start()
        for w, (g, ra) in enumerate(zip(g_refs, ra_refs)):
            rc(N_DEV + w, g.at[:, other], ra, (x, y, 1 - c)).wait_recv()
        for idx in range(1, N_DEV):
            px, py, pc = peer(idx)
            rc(idx, rep_ref, rall_ref.at[4 * px + 2 * py + pc], (px, py, pc)).wait_recv()
        for cp in cps:
            cp.wait_send()
        loc.wait()

    halves = tuple(jax.ShapeDtypeStruct((g.shape[0], L // 2) + g.shape[2:], F32) for g in Gs)
    out = pl.pallas_call(
        body, name="exchange_halves",
        out_shape=halves + (jax.ShapeDtypeStruct((N_DEV, RR, W), F32),),
        in_specs=[ANY] * (n + 1), out_specs=tuple([ANY] * (n + 1)),
        scratch_shapes=[pltpu.SemaphoreType.DMA((N_DEV + n,)), pltpu.SemaphoreType.DMA((N_DEV + n,)),
                        pltpu.SemaphoreType.DMA],
    )(*Gs, rep)
    return out[:n], out[n]


def _exchange_chips(As):
    n = len(As)

    def body(*refs):
        a_refs, rb_refs, (send_sems, recv_sems) = refs[:n], refs[n:2 * n], refs[2 * n:]
        x, y, c = _my_place()
        me = 2 * x + y
        chips = [(1 - x, y), (x, 1 - y), (1 - x, 1 - y)]
        rc = functools.partial(_remote, send_sems, recv_sems)
        cps = [rc(3 * w + k, a.at[2 * px + py], rb.at[k], (px, py, c))
               for w, (a, rb) in enumerate(zip(a_refs, rb_refs)) for k, (px, py) in enumerate(chips)]
        for cp in cps:
            cp.start()
        for w, (a, rb) in enumerate(zip(a_refs, rb_refs)):
            for k, (px, py) in enumerate(chips):
                rc(3 * w + k, a.at[me], rb.at[k], (px, py, c)).wait_recv()
        for cp in cps:
            cp.wait_send()

    return pl.pallas_call(
        body, name="exchange_chips",
        out_shape=tuple(jax.ShapeDtypeStruct((N_CHIPS - 1,) + a.shape[1:], a.dtype) for a in As),
        in_specs=[ANY] * n, out_specs=tuple([ANY] * n),
        scratch_shapes=[pltpu.SemaphoreType.DMA((3 * n,)), pltpu.SemaphoreType.DMA((3 * n,))],
    )(*As)


def _exchange_sibling(gs):
    n = len(gs)
    L = gs[0].shape[0]

    def body(*refs):
        g_refs, (send_sems, recv_sems) = refs[n:2 * n], refs[2 * n:]
        x, y, c = _my_place()
        half, other = _layer_halves(L, c)
        rc = functools.partial(_remote, send_sems, recv_sems)
        cps = [rc(w, g.at[half], g.at[half], (x, y, 1 - c)) for w, g in enumerate(g_refs)]
        for cp in cps:
            cp.start()
        for w, g in enumerate(g_refs):
            rc(w, g.at[other], g.at[other], (x, y, 1 - c)).wait_recv()
        for cp in cps:
            cp.wait_send()

    return pl.pallas_call(
        body, name="exchange_sibling", out_shape=tuple(jax.ShapeDtypeStruct(g.shape, F32) for g in gs),
        in_specs=[ANY] * n, out_specs=tuple([ANY] * n), input_output_aliases={i: i for i in range(n)},
        scratch_shapes=[pltpu.SemaphoreType.DMA((n,)), pltpu.SemaphoreType.DMA((n,))],
    )(*gs)


def _add_halves(G, recv, core, *, name):
    n, L, a, b = G.shape
    Lh = L // 2
    ta = _rows_per_block(a, b)

    def body(core_ref, g_ref, r_ref, o_ref, o16_ref):
        s = g_ref[...] + r_ref[...]
        o_ref[...] = s
        o16_ref[...] = s.astype(BF16)

    blk = pl.BlockSpec((None, None, ta, b), lambda p, l, i, core_ref: (p, l, i, 0))
    return pl.pallas_call(
        body, name=name,
        out_shape=(jax.ShapeDtypeStruct((n, Lh, a, b), F32), jax.ShapeDtypeStruct((n, Lh, a, b), BF16)),
        grid_spec=pltpu.PrefetchScalarGridSpec(
            num_scalar_prefetch=1, grid=(n, Lh, a // ta),
            in_specs=[pl.BlockSpec((None, None, ta, b), lambda p, l, i, core_ref: (p, core_ref[0] * Lh + l, i, 0)), blk],
            out_specs=(blk, blk)),
        compiler_params=_cparams("parallel", "parallel", "parallel"),
    )(core, G, recv)


def _sum_chips(A, rb, place, *, name):
    _, Lh, a, b = A.shape
    ta = _rows_per_block(a, b, 512 * 1024)

    def body(place_ref, a_ref, r_ref, o_ref):
        o_ref[...] = ((a_ref[...] + r_ref[0].astype(F32)) + r_ref[1].astype(F32)) + r_ref[2].astype(F32)

    return pl.pallas_call(
        body, name=name, out_shape=jax.ShapeDtypeStruct((2 * Lh, a, b), F32),
        grid_spec=pltpu.PrefetchScalarGridSpec(
            num_scalar_prefetch=1, grid=(Lh, a // ta),
            in_specs=[pl.BlockSpec((None, None, ta, b), lambda l, i, place_ref: (place_ref[0], l, i, 0)),
                      pl.BlockSpec((N_CHIPS - 1, None, ta, b), lambda l, i, place_ref: (0, l, i, 0))],
            out_specs=pl.BlockSpec((None, ta, b), lambda l, i, place_ref: (place_ref[1] * Lh + l, i, 0))),
        compiler_params=_cparams("parallel", "parallel"),
    )(place, A, rb)


def _sum_slots(a, *, name):
    n, rows, W = a.shape
    tr = _pick(rows, 512, 8)

    def body(a_ref, o_ref):
        s = a_ref[0]
        for q in range(1, n):
            s = s + a_ref[q]
        o_ref[...] = s

    return pl.pallas_call(
        body, name=name, out_shape=jax.ShapeDtypeStruct((rows, W), F32), grid=(rows // tr,),
        in_specs=[pl.BlockSpec((n, tr, W), lambda i: (0, i, 0))], out_specs=pl.BlockSpec((tr, W), lambda i: (i, 0)),
        compiler_params=_cparams("parallel"),
    )(a)


def _adamw(w, g, m, v, *, name):
    L, a, b = w.shape
    ta = _rows_per_block(a, b, 512 * 1024)

    def body(w_ref, g_ref, m_ref, v_ref, d_ref, nm_ref, nv_ref):
        gg = g_ref[...]
        nm = ADAM_B1 * m_ref[...] + (1.0 - ADAM_B1) * gg
        nv = ADAM_B2 * v_ref[...] + (1.0 - ADAM_B2) * jnp.square(gg)
        m_hat = nm / (1.0 - ADAM_B1 ** ADAM_STEP)
        v_hat = nv / (1.0 - ADAM_B2 ** ADAM_STEP)
        d_ref[...] = -ADAM_LR * (m_hat / (jnp.sqrt(v_hat) + ADAM_EPS) + ADAM_WD * w_ref[...])
        nm_ref[...] = nm
        nv_ref[...] = nv

    blk = pl.BlockSpec((None, ta, b), lambda l, i: (l, i, 0))
    shp = jax.ShapeDtypeStruct((L, a, b), F32)
    return pl.pallas_call(
        body, name=name, out_shape=(shp, shp, shp), grid=(L, a // ta),
        in_specs=[blk, blk, blk, blk], out_specs=(blk, blk, blk),
        compiler_params=_cparams("parallel", "parallel"),
    )(w, g, m, v)


INPUT_NAMES = (("x", "mem") + WEIGHT_NAMES + ("loss_target",) + tuple("m_" + n for n in WEIGHT_NAMES)
               + tuple("v_" + n for n in WEIGHT_NAMES))


def kernel(*args):
    a = dict(zip(INPUT_NAMES, args, strict=True))
    x, mem, target = a["x"][0], a["mem"][0], a["loss_target"][0]
    sh_names = list(SHARD_BY_ROWS)
    core = lax.axis_index("c").astype(jnp.int32)
    chip = (2 * lax.axis_index("x") + lax.axis_index("y")).astype(jnp.int32)
    place = jnp.stack([chip, core])

    placed = [_place_shard(a[n], chip.reshape(1), F32 if n == "conv_w" else BF16, name="place_" + n) for n in sh_names]
    wfull = dict(zip(sh_names, _gather_weights(placed)))
    p = {n: a[n] for n in REPLICATED}
    cw = wfull.pop("conv_w")
    p["conv_w"] = cw.transpose(1, 2, 0, 3).reshape(cw.shape[1], cw.shape[2], N_CHIPS * cw.shape[3])

    loss, dx, gfull = _local_step(x, mem, target, wfull, p)
    loss = lax.psum(loss, ("x", "y", "c"))

    Gs = [gfull[n] for n in sh_names]
    rep_shapes = [a[n].shape for n in REPLICATED]
    rows_r = _round_up(-(-sum(math.prod(s) for s in rep_shapes) // PACK_W), 64)
    rep = _pack([gfull[n] for n in REPLICATED], rows_r)
    recvs, repall = _exchange_halves(Gs, rep)
    As = [_add_halves(g, r, core.reshape(1), name="add_halves_" + n) for n, g, r in zip(sh_names, Gs, recvs)]
    rbs = _exchange_chips([a16 for _, a16 in As])
    gsh = _exchange_sibling([_sum_chips(A, rb, place, name="sum_chips_" + n)
                             for n, (A, _), rb in zip(sh_names, As, rbs)])
    grep = _sum_slots(repall, name="sum_devices")

    got = {"g": dict(zip(sh_names, gsh)), "d": {}, "m": {}, "v": {}}
    for n, g in zip(sh_names, gsh):
        got["d"][n], got["m"][n], got["v"][n] = _adamw(a[n], g, a["m_" + n], a["v_" + n], name="adamw_" + n)
    packed = [_pack([a[pre + n] for n in REPLICATED], rows_r)[None] for pre in ("", "m_", "v_")]
    d_r, m_r, v_r = _adamw(packed[0], grep[None], packed[1], packed[2], name="adamw_replicated")
    for key, buf in (("g", grep), ("d", d_r[0]), ("m", m_r[0]), ("v", v_r[0])):
        got[key].update(zip(REPLICATED, _unpack(buf, rep_shapes)))
    outs = [got[key][n] for key in ("g", "d", "m", "v") for n in WEIGHT_NAMES]
    return (loss, dx[None], *outs)
```

```python
import functools
import math

import jax
import jax.numpy as jnp
from jax import lax
from jax.experimental import pallas as pl
from jax.experimental.pallas import tpu as pltpu

F32 = jnp.float32
BF16 = jnp.bfloat16
MESH = pl.DeviceIdType.MESH

RMS_EPS = 1e-6
POOL_WINDOWS = (2, 4, 8, 16)
POOL_GROUP = 128
POOL_W = 512
FOX_HEADS = 8
FOX_DH = 64
FOX_W = 512
X_HEADS = 4
X_DH = 128
X_W = 512
N_FGATE = 8
LANES = 128
HALO = 16

ADAM_LR = 0.001
ADAM_B1 = 0.9
ADAM_B2 = 0.999
ADAM_EPS = 1e-08
ADAM_WD = 0.01
ADAM_STEP = 10

VMEM_LIMIT_BYTES = 56 * 1024 * 1024
MATMUL_VMEM_BUDGET = 42 * 1024 * 1024


def _cparams(*sem):
    return pltpu.CompilerParams(dimension_semantics=sem, vmem_limit_bytes=VMEM_LIMIT_BYTES)


def _pick(n, cap, align=LANES):
    if n <= cap:
        return n
    best = None
    for t in range(align, cap + 1, align):
        if n % t == 0:
            best = t
    assert best is not None, (n, cap, align)
    return best


def _sigmoid(x):
    return 1.0 / (1.0 + jnp.exp(-x))


class _ChipMajor:
    def __init__(self, arr, layer, by_rows):
        self.arr, self.layer, self.by_rows = arr, layer, by_rows
        n, _, a, b = arr.shape
        self.n_chips, self.per_chip = n, (a if by_rows else b)
        self.shape = (n * a, b) if by_rows else (a, n * b)


SPAN_CHIPS_BELOW = 512


def _matmul(a, b, *, ta=False, tb=False, out_dtype=F32, out_chips=None, out_rows=None, out_layer=None, name):
    view = b if isinstance(b, _ChipMajor) else None
    if ta:
        K, M = a.shape
    else:
        M, K = a.shape
    if tb:
        N, Kb = b.shape
    else:
        Kb, N = b.shape
    assert K == Kb, (a.shape, b.shape, ta, tb)
    by_rows = view is not None and view.by_rows
    by_cols = view is not None and not view.by_rows
    b_itemsize = (view.arr if view is not None else b).dtype.itemsize
    span_b = by_cols and view.per_chip < SPAN_CHIPS_BELOW
    span_o = bool(out_chips) and N // out_chips < SPAN_CHIPS_BELOW
    if by_rows and tb:
        tn = N
    elif by_cols and not tb:
        tn = N if span_b else _pick(view.per_chip, 1408)
    elif out_chips:
        tn = N if span_o else _pick(N // out_chips, 1408)
    elif out_rows:
        tn = _pick(N, 512)
    else:
        tn = _pick(N, 1408)
    tm = M if out_rows else _pick(M, 1024 if tn <= 1024 else 512)
    if by_rows and not tb:
        tk = K
    elif by_cols and tb:
        tk = K if span_b else (view.per_chip if view.per_chip <= 2048 else _pick(view.per_chip, 1024))
    else:
        ab, bb, ob = a.dtype.itemsize, b_itemsize, jnp.dtype(out_dtype).itemsize
        for tm in ((M,) if out_rows else (_pick(M, 1024), tm)):
            for cap in (K, 2048, 1024, 512, 128):
                tk = _pick(K, cap)
                need = (2 * tk * (tm * ab + tn * bb) + (2 * ob + 4 + (4 if tk < K else 0)) * tm * tn)
                if need <= MATMUL_VMEM_BUDGET:
                    break
            if tk >= min(K, 1024):
                break
    nk = K // tk
    dims = (((0 if ta else 1,), (1 if tb else 0,)), ((), ()))

    aliased = out_layer is not None and out_layer[0] is not None

    def body(a_ref, b_ref, *rest):
        o_ref, scratch = (rest[1], rest[2:]) if aliased else (rest[0], rest[1:])
        bt = b_ref[...]
        if by_rows:
            bt = bt.reshape(bt.shape[0] * bt.shape[1], bt.shape[2])
        elif span_b:
            bt = jnp.concatenate([bt[q] for q in range(view.n_chips)], axis=1)
        p = lax.dot_general(a_ref[...].astype(BF16), bt.astype(BF16), dims, preferred_element_type=F32)

        def store(val):
            if span_o:
                w = N // out_chips
                for q in range(out_chips):
                    o_ref[q] = val[:, q * w:(q + 1) * w].astype(out_dtype)
            elif out_rows:
                h = M // out_rows
                for q in range(out_rows):
                    o_ref[q] = val[q * h:(q + 1) * h, :].astype(out_dtype)
            else:
                o_ref[...] = val.astype(out_dtype)

        if nk == 1:
            store(p)
        else:
            acc_ref, = scratch
            k = pl.program_id(2)

            @pl.when(k == 0)
            def _():
                acc_ref[...] = p

            @pl.when(k > 0)
            def _():
                acc_ref[...] += p

            @pl.when(k == nk - 1)
            def _():
                store(acc_ref[...])

    a_spec = pl.BlockSpec((tk, tm), lambda i, j, k: (k, i)) if ta else pl.BlockSpec((tm, tk), lambda i, j, k: (i, k))
    b_tile = (tn, tk) if tb else (tk, tn)
    b_rc = (lambda i, j, k: (j, k)) if tb else (lambda i, j, k: (k, j))
    if view is None:
        b_arr = b
        b_spec = pl.BlockSpec(b_tile, b_rc)
    elif by_rows:
        b_arr = view.arr
        assert b_tile[0] == view.shape[0]
        b_spec = pl.BlockSpec((view.n_chips, None, view.arr.shape[2], b_tile[1]),
                              lambda i, j, k: (0, view.layer, 0, b_rc(i, j, k)[1]))
    elif span_b:
        b_arr = view.arr
        assert b_tile[1] == view.shape[1]
        b_spec = pl.BlockSpec((view.n_chips, None, b_tile[0], view.per_chip),
                              lambda i, j, k: (0, view.layer, b_rc(i, j, k)[0], 0))
    else:
        b_arr = view.arr
        per = view.per_chip // b_tile[1]
        b_spec = pl.BlockSpec((None, None) + b_tile,
                              lambda i, j, k: (b_rc(i, j, k)[1] // per, view.layer, b_rc(i, j, k)[0],
                                               b_rc(i, j, k)[1] % per))
    if span_o:
        o_full, o_blk = (out_chips, M, N // out_chips), (out_chips, tm, N // out_chips)
        o_idx = lambda i, j, k: (0, i, 0)
    elif out_chips:
        per_o = (N // out_chips) // tn
        o_full, o_blk = (out_chips, M, N // out_chips), (None, tm, tn)
        o_idx = lambda i, j, k: (j // per_o, i, j % per_o)
    elif out_rows:
        o_full, o_blk = (out_rows, M // out_rows, N), (out_rows, M // out_rows, tn)
        o_idx = lambda i, j, k: (0, 0, j)
    else:
        o_full, o_blk = (M, N), (tm, tn)
        o_idx = lambda i, j, k: (i, j)
    if out_layer is not None:
        _, layer, n_layers = out_layer
        o_full, o_blk = o_full[:1] + (n_layers,) + o_full[1:], o_blk[:1] + (None,) + o_blk[1:]
        o_idx = functools.partial(lambda f, i, j, k: (f(i, j, k)[0], layer) + f(i, j, k)[1:], o_idx)
    out_shape = jax.ShapeDtypeStruct(o_full, out_dtype)
    out_spec = pl.BlockSpec(o_blk, o_idx)
    grid = (M // tm, N // tn, nk)
    specs = [a_spec, b_spec, out_spec]
    a_bytes, b_bytes = M * K * a.dtype.itemsize, K * N * b_itemsize
    if nk == 1 and b_bytes + a_bytes * grid[1] < a_bytes + b_bytes * grid[0]:
        grid = (grid[1], grid[0], nk)
        specs = [pl.BlockSpec(s.block_shape, functools.partial(lambda f, j, i, k: f(i, j, k), s.index_map))
                 for s in specs]
    extra = (out_layer[0],) if aliased else ()
    return pl.pallas_call(
        body, name=name, out_shape=out_shape, grid=grid,
        in_specs=specs[:2] + [pl.BlockSpec(memory_space=pl.ANY)] * len(extra), out_specs=specs[2],
        input_output_aliases={2: 0} if aliased else {},
        scratch_shapes=[pltpu.VMEM((tm, tn), F32)] if nk > 1 else [],
        compiler_params=_cparams("parallel", "parallel", "arbitrary"),
    )(a, b_arr, *extra)


def _row_block(S, D, cap_bytes=2 * 1024 * 1024):
    ts = max(8, min(S, cap_bytes // (4 * D)))
    return _pick(S, ts, 8)


def _rms_fwd(x, g, *, out_dtype, name):
    S, D = x.shape
    ts = _row_block(S, D)

    def body(x_ref, g_ref, o_ref):
        xf = x_ref[...]
        r = lax.rsqrt(jnp.mean(xf * xf, axis=-1, keepdims=True) + RMS_EPS)
        o_ref[...] = (xf * r * g_ref[...]).astype(out_dtype)

    return pl.pallas_call(
        body, name=name, out_shape=jax.ShapeDtypeStruct((S, D), out_dtype), grid=(S // ts,),
        in_specs=[pl.BlockSpec((ts, D), lambda i: (i, 0)), pl.BlockSpec((1, D), lambda i: (0, 0))],
        out_specs=pl.BlockSpec((ts, D), lambda i: (i, 0)),
        compiler_params=_cparams("parallel"),
    )(x, g)


def _add_rms(x, r, g, g_next=None, *, name):
    S, D = x.shape
    ts = _row_block(S, D)
    fused = g_next is not None

    def body(*refs):
        x_ref, r_ref, g_ref = refs[:3]
        rf = r_ref[...].astype(F32)
        s = lax.rsqrt(jnp.mean(rf * rf, axis=-1, keepdims=True) + RMS_EPS)
        y = x_ref[...] + rf * s * g_ref[...]
        if fused:
            gn_ref, o_ref, h_ref = refs[3:]
            t = lax.rsqrt(jnp.mean(y * y, axis=-1, keepdims=True) + RMS_EPS)
            h_ref[...] = (y * t * gn_ref[...]).astype(BF16)
        else:
            o_ref, = refs[3:]
        o_ref[...] = y

    row = pl.BlockSpec((ts, D), lambda i: (i, 0))
    vec = pl.BlockSpec((1, D), lambda i: (0, 0))
    out = pl.pallas_call(
        body, name=name,
        out_shape=(jax.ShapeDtypeStruct((S, D), F32),) + ((jax.ShapeDtypeStruct((S, D), BF16),) if fused else ()),
        grid=(S // ts,),
        in_specs=[row, row, vec] + ([vec] if fused else []),
        out_specs=(row,) + ((row,) if fused else ()),
        compiler_params=_cparams("parallel"),
    )(*((x, r, g) + ((g_next,) if fused else ())))
    return out if fused else out[0]


def _rms_bwd(x, g, dy, res=None, *, out_dtype, name):
    S, D = x.shape
    ts = _row_block(S, D)
    has_res = res is not None

    def body(*refs):
        if has_res:
            x_ref, g_ref, dy_ref, res_ref, dx_ref, dg_ref = refs
        else:
            x_ref, g_ref, dy_ref, dx_ref, dg_ref = refs
        i = pl.program_id(0)
        xf = x_ref[...].astype(F32)
        dyf = dy_ref[...].astype(F32)
        r = lax.rsqrt(jnp.mean(xf * xf, axis=-1, keepdims=True) + RMS_EPS)
        n = xf * r
        dn = dyf * g_ref[...]
        dx = r * (dn - n * jnp.mean(dn * n, axis=-1, keepdims=True))
        if has_res:
            dx = dx + res_ref[...]
        dx_ref[...] = dx.astype(out_dtype)
        part = jnp.sum((dyf * n).reshape(ts // 8, 8, D), axis=0)

        @pl.when(i == 0)
        def _():
            dg_ref[...] = part

        @pl.when(i > 0)
        def _():
            dg_ref[...] += part

    row = pl.BlockSpec((ts, D), lambda i: (i, 0))
    in_specs = [row, pl.BlockSpec((1, D), lambda i: (0, 0)), row] + ([row] if has_res else [])
    args = (x, g, dy) + ((res,) if has_res else ())
    dx, dg = pl.pallas_call(
        body, name=name,
        out_shape=(jax.ShapeDtypeStruct((S, D), out_dtype), jax.ShapeDtypeStruct((8, D), F32)),
        grid=(S // ts,), in_specs=in_specs,
        out_specs=(row, pl.BlockSpec((8, D), lambda i: (0, 0))),
        compiler_params=_cparams("arbitrary"),
    )(*args)
    return dx, dg


def _loss_head(y, t, *, name):
    S, D = y.shape
    ts = _row_block(S, D)

    def body(y_ref, t_ref, dy_ref, sq_ref):
        i = pl.program_id(0)
        e = y_ref[...] - t_ref[...]
        dy_ref[...] = e / D
        part = jnp.sum((e * e).reshape(ts // 8, 8, D), axis=0)

        @pl.when(i == 0)
        def _():
            sq_ref[...] = part

        @pl.when(i > 0)
        def _():
            sq_ref[...] += part

    row = pl.BlockSpec((ts, D), lambda i: (i, 0))
    return pl.pallas_call(
        body, name=name,
        out_shape=(jax.ShapeDtypeStruct((S, D), F32), jax.ShapeDtypeStruct((8, D), F32)),
        grid=(S // ts,), in_specs=[row, row],
        out_specs=(row, pl.BlockSpec((8, D), lambda i: (0, 0))),
        compiler_params=_cparams("arbitrary"),
    )(y, t)


def _window_counts(i, T, w):
    t = i * T + lax.broadcasted_iota(jnp.int32, (T, 1), 0)
    return jnp.minimum(t + 1, w).astype(F32)


def _pool_fwd(zu, pool_w, pool_scale, *, name):
    S, W = zu.shape
    T = _pick(S, 1024, 8)

    def body(u_ref, pw_ref, sc_ref, pooled_ref, mixed_ref, halo_ref):
        i = pl.program_id(0)

        @pl.when(i == 0)
        def _():
            halo_ref[...] = jnp.zeros_like(halo_ref)

        u = u_ref[...]
        ext = jnp.concatenate([halo_ref[...], u], axis=0)
        halo_ref[...] = u[T - HALO:, :]
        for g, w in enumerate(POOL_WINDOWS):
            cols = slice(g * POOL_GROUP, (g + 1) * POOL_GROUP)
            s = ext[:, cols]
            sh = 1
            while sh < w:
                s = s + pltpu.roll(s, sh, 0)
                sh *= 2
            pooled = s[HALO:, :] / _window_counts(i, T, w) - u[:, cols]
            pooled_bf = pooled.astype(BF16)
            pm = jnp.dot(pooled_bf, pw_ref[g].astype(BF16), preferred_element_type=F32)
            pooled_ref[:, cols] = pooled_bf
            mixed_ref[:, cols] = (pm * sc_ref[:, cols]).astype(BF16)

    row = pl.BlockSpec((T, W), lambda i: (i, 0))
    return pl.pallas_call(
        body, name=name,
        out_shape=(jax.ShapeDtypeStruct((S, W), BF16), jax.ShapeDtypeStruct((S, W), BF16)),
        grid=(S // T,),
        in_specs=[row, pl.BlockSpec(pool_w.shape, lambda i: (0, 0, 0)), pl.BlockSpec((1, W), lambda i: (0, 0))],
        out_specs=(row, row),
        scratch_shapes=[pltpu.VMEM((HALO, W), F32)],
        compiler_params=_cparams("arbitrary"),
    )(zu, pool_w, pool_scale)


def _pool_bwd(pooled, dmixed, pool_w, pool_scale, *, name):
    S, W = pooled.shape
    T = _pick(S, 1024, 8)
    nb = S // T

    def body(p_ref, dm_ref, pw_ref, sc_ref, dzu_ref, dpw_ref, dsc_ref, halo_ref):
        i = pl.program_id(0)
        blk = nb - 1 - i

        @pl.when(i == 0)
        def _():
            halo_ref[...] = jnp.zeros_like(halo_ref)
            dpw_ref[...] = jnp.zeros_like(dpw_ref)
            dsc_ref[...] = jnp.zeros_like(dsc_ref)

        for g, w in enumerate(POOL_WINDOWS):
            cols = slice(g * POOL_GROUP, (g + 1) * POOL_GROUP)
            p = p_ref[:, cols]
            dm = dm_ref[:, cols]
            pw = pw_ref[g].astype(BF16)
            pm = jnp.dot(p, pw, preferred_element_type=F32)
            dsc_ref[:, cols] += jnp.sum((dm * pm).reshape(T // 8, 8, POOL_GROUP), axis=0)
            dpm = (dm * sc_ref[:, cols]).astype(BF16)
            dpw_ref[g] += lax.dot_general(p, dpm, (((0,), (0,)), ((), ())), preferred_element_type=F32)
            dpooled = lax.dot_general(dpm, pw, (((1,), (1,)), ((), ())), preferred_element_type=F32)
            e = dpooled / _window_counts(blk, T, w)
            ext = jnp.concatenate([e, halo_ref[:, cols]], axis=0)
            halo_ref[:, cols] = e[:HALO, :]
            s = ext
            sh = 1
            while sh < w:
                s = s + pltpu.roll(s, T + HALO - sh, 0)
                sh *= 2
            dzu_ref[:, cols] = (s[:T, :] - dpooled).astype(BF16)

    row = pl.BlockSpec((T, W), lambda i: (nb - 1 - i, 0))
    return pl.pallas_call(
        body, name=name,
        out_shape=(jax.ShapeDtypeStruct((S, W), BF16), jax.ShapeDtypeStruct(pool_w.shape, F32),
                   jax.ShapeDtypeStruct((8, W), F32)),
        grid=(nb,),
        in_specs=[row, row, pl.BlockSpec(pool_w.shape, lambda i: (0, 0, 0)), pl.BlockSpec((1, W), lambda i: (0, 0))],
        out_specs=(row, pl.BlockSpec(pool_w.shape, lambda i: (0, 0, 0)), pl.BlockSpec((8, W), lambda i: (0, 0))),
        scratch_shapes=[pltpu.VMEM((HALO, W), F32)],
        compiler_params=_cparams("arbitrary"),
    )(pooled, dmixed, pool_w, pool_scale)


def _fgate_fwd(zf, bf, *, name):
    S, W = zf.shape
    T = _pick(S, 512, 8)

    def body(z_ref, b_ref, c_ref, aq_ref, ak_ref, carry_ref):
        i = pl.program_id(0)

        @pl.when(i == 0)
        def _():
            carry_ref[...] = jnp.zeros_like(carry_ref)

        a = z_ref[...] + b_ref[...]
        s = jnp.minimum(a, 0.0) - jnp.log(1.0 + jnp.exp(-jnp.abs(a)))
        row = lax.broadcasted_iota(jnp.int32, (T, W), 0)
        sh = 1
        while sh < T:
            s = s + jnp.where(row >= sh, pltpu.roll(s, sh, 0), 0.0)
            sh *= 2
        c = s + carry_ref[0:1, :]
        c_ref[...] = c
        carry_ref[...] = jnp.broadcast_to(c[T - 1:T, :], carry_ref.shape)
        lane = lax.broadcasted_iota(jnp.int32, (T, W), 1)
        for h in range(FOX_HEADS):
            ch = c[:, h:h + 1]
            hi = ch.astype(BF16).astype(F32)
            r1 = ch - hi
            lo = r1.astype(BF16).astype(F32)
            lo2 = (r1 - lo).astype(BF16).astype(F32)
            aq = jnp.where(lane == 0, hi, jnp.where(lane == 1, lo, jnp.where(lane == 2, lo2,
                                                                              jnp.where(lane < 6, 1.0, 0.0))))
            ak = jnp.where(lane < 3, 1.0, jnp.where(lane == 3, -hi, jnp.where(lane == 4, -lo,
                                                                               jnp.where(lane == 5, -lo2, 0.0))))
            aq_ref[h] = aq.astype(BF16)
            ak_ref[h] = ak.astype(BF16)

    aug = jax.ShapeDtypeStruct((FOX_HEADS, S, W), BF16)
    aug_spec = pl.BlockSpec((FOX_HEADS, T, W), lambda i: (0, i, 0))
    return pl.pallas_call(
        body, name=name, out_shape=(jax.ShapeDtypeStruct((S, W), F32), aug, aug), grid=(S // T,),
        in_specs=[pl.BlockSpec((T, W), lambda i: (i, 0)), pl.BlockSpec((1, W), lambda i: (0, 0))],
        out_specs=(pl.BlockSpec((T, W), lambda i: (i, 0)), aug_spec, aug_spec),
        scratch_shapes=[pltpu.VMEM((8, W), F32)],
        compiler_params=_cparams("arbitrary"),
    )(zf, bf)


def _fgate_bwd(zf, bf, dc, *, name):
    S, W = zf.shape
    T = _pick(S, 512, 8)
    nb = S // T

    def body(z_ref, b_ref, dc_ref, dz_ref, db_ref, carry_ref):
        i = pl.program_id(0)

        @pl.when(i == 0)
        def _():
            carry_ref[...] = jnp.zeros_like(carry_ref)
            db_ref[...] = jnp.zeros_like(db_ref)

        s = dc_ref[...]
        row = lax.broadcasted_iota(jnp.int32, (T, W), 0)
        sh = 1
        while sh < T:
            s = s + jnp.where(row < T - sh, pltpu.roll(s, T - sh, 0), 0.0)
            sh *= 2
        dlf = s + carry_ref[0:1, :]
        carry_ref[...] = jnp.broadcast_to(dlf[0:1, :], carry_ref.shape)
        dz = dlf * (1.0 - _sigmoid(z_ref[...] + b_ref[...]))
        dz_ref[...] = dz.astype(BF16)
        db_ref[...] += jnp.sum(dz.reshape(T // 8, 8, W), axis=0)

    row_spec = pl.BlockSpec((T, W), lambda i: (nb - 1 - i, 0))
    return pl.pallas_call(
        body, name=name,
        out_shape=(jax.ShapeDtypeStruct((S, W), BF16), jax.ShapeDtypeStruct((8, W), F32)),
        grid=(nb,),
        in_specs=[row_spec, pl.BlockSpec((1, W), lambda i: (0, 0)), row_spec],
        out_specs=(row_spec, pl.BlockSpec((8, W), lambda i: (0, 0))),
        scratch_shapes=[pltpu.VMEM((8, W), F32)],
        compiler_params=_cparams("arbitrary"),
    )(zf, bf, dc)


def _merge_fwd(zg, yp, yf, *, name):
    S, D = yp.shape
    ts = _row_block(S, D, 1024 * 1024)

    def body(zg_ref, yp_ref, yf_ref, o_ref):
        zg = zg_ref[...].astype(F32)
        o_ref[...] = (_sigmoid(zg[:, :D]) * yp_ref[...] + _sigmoid(zg[:, D:]) * yf_ref[...]).astype(BF16)

    row = pl.BlockSpec((ts, D), lambda i: (i, 0))
    return pl.pallas_call(
        body, name=name, out_shape=jax.ShapeDtypeStruct((S, D), BF16), grid=(S // ts,),
        in_specs=[pl.BlockSpec((ts, 2 * D), lambda i: (i, 0)), row, row], out_specs=row,
        compiler_params=_cparams("parallel"),
    )(zg, yp, yf)


def _merge_bwd(zg, yp, yf, dmerged, *, name):
    S, D = yp.shape
    ts = _row_block(S, D, 1024 * 1024)

    def body(zg_ref, yp_ref, yf_ref, dm_ref, dyp_ref, dyf_ref, dzg_ref):
        dm = dm_ref[...].astype(F32)
        zg = zg_ref[...].astype(F32)
        sp = _sigmoid(zg[:, :D])
        sf = _sigmoid(zg[:, D:])
        dyp_ref[...] = (dm * sp).astype(BF16)
        dyf_ref[...] = (dm * sf).astype(BF16)
        dzg_ref[:, :D] = (dm * yp_ref[...] * (sp * (1.0 - sp))).astype(BF16)
        dzg_ref[:, D:] = (dm * yf_ref[...] * (sf * (1.0 - sf))).astype(BF16)

    row = pl.BlockSpec((ts, D), lambda i: (i, 0))
    wide = pl.BlockSpec((ts, 2 * D), lambda i: (i, 0))
    return pl.pallas_call(
        body, name=name,
        out_shape=(jax.ShapeDtypeStruct((S, D), BF16), jax.ShapeDtypeStruct((S, D), BF16),
                   jax.ShapeDtypeStruct((S, 2 * D), BF16)),
        grid=(S // ts,), in_specs=[wide, row, row, row], out_specs=(row, row, wide),
        compiler_params=_cparams("parallel"),
    )(zg, yp, yf, dmerged)


NEG_BIG = -1e30


FOX_BLOCK = 1024
PAIR = LANES // FOX_DH
N_PAIRS = FOX_HEADS // PAIR


def _fox_fwd(zqkv, augq, augk, *, name):
    S = zqkv.shape[0]
    bq = _pick(S, FOX_BLOCK, 128)
    nq = S // bq
    scale = 1.0 / math.sqrt(FOX_DH)

    def body(q_ref, k_ref, v_ref, aq_ref, ak_ref, o_ref, lse_ref):
        i = pl.program_id(1)
        lane = lax.broadcasted_iota(jnp.int32, (1, LANES), 1)
        first = lane < FOX_DH
        q2 = q_ref[...] * scale
        zero = jnp.zeros_like(q2)
        qh = (jnp.concatenate([jnp.where(first, q2, zero), aq_ref[0]], axis=1),
              jnp.concatenate([jnp.where(first, zero, q2), aq_ref[1]], axis=1))

        def step(j, carry, masked):
            start = pl.multiple_of(j * bq, bq)
            kb = k_ref[pl.ds(start, bq), :]
            vb = v_ref[pl.ds(start, bq), :]
            one = jnp.ones_like(vb)
            vh = (jnp.where(first, vb, one), jnp.where(first, one, vb))
            out = []
            for h in range(PAIR):
                m, acc = carry[h]
                kh = jnp.concatenate([kb, ak_ref[h, pl.ds(start, bq), :]], axis=1)
                s = lax.dot_general(qh[h], kh, (((1,), (1,)), ((), ())), preferred_element_type=F32)
                if masked:
                    r = lax.broadcasted_iota(jnp.int32, (bq, bq), 0)
                    c = lax.broadcasted_iota(jnp.int32, (bq, bq), 1)
                    s = jnp.where(c <= r, s, NEG_BIG)
                m_new = jnp.maximum(m, jnp.max(s, axis=-1, keepdims=True))
                alpha = jnp.exp(m - m_new)
                p = jnp.exp(s - m_new).astype(BF16)
                acc = alpha * acc + jnp.dot(p, vh[h], preferred_element_type=F32)
                out.append((m_new, acc))
            return tuple(out)

        init = tuple((jnp.full((bq, 1), NEG_BIG, F32), jnp.zeros((bq, LANES), F32)) for _ in range(PAIR))
        carry = lax.fori_loop(0, i, lambda j, c: step(j, c, False), init)
        (ma, acca), (mb, accb) = step(i, carry, True)
        num = jnp.where(first, acca, accb)
        den = jnp.where(first, pltpu.roll(acca, FOX_DH, 1), pltpu.roll(accb, FOX_DH, 1))
        o_ref[...] = num / den
        lse_t = (jnp.where(first, mb, ma) + jnp.log(jnp.where(first, accb, acca))).T
        lse_ref[0] = lse_t[FOX_DH:FOX_DH + 1, :]
        lse_ref[1] = lse_t[0:1, :]

    npair = N_PAIRS
    return pl.pallas_call(
        body, name=name,
        out_shape=(jax.ShapeDtypeStruct((S, FOX_W), F32), jax.ShapeDtypeStruct((FOX_HEADS, 1, S), F32)),
        grid=(npair, nq),
        in_specs=[pl.BlockSpec((bq, LANES), lambda hp, i: (i, hp)),
                  pl.BlockSpec((S, LANES), lambda hp, i: (0, npair + hp)),
                  pl.BlockSpec((S, LANES), lambda hp, i: (0, 2 * npair + hp)),
                  pl.BlockSpec((PAIR, bq, LANES), lambda hp, i: (hp, i, 0)),
                  pl.BlockSpec((PAIR, S, LANES), lambda hp, i: (hp, 0, 0))],
        out_specs=(pl.BlockSpec((bq, LANES), lambda hp, i: (i, hp)),
                   pl.BlockSpec((PAIR, 1, bq), lambda hp, i: (hp, 0, i))),
        compiler_params=_cparams("parallel", "arbitrary"),
    )(zqkv, zqkv, zqkv, augq, augk)


def _head_rowsum(a, b, *, name):
    S, W = a.shape
    ts = _pick(S, 1024, 8)

    def body(a_ref, b_ref, o_ref):
        prod = a_ref[...].astype(F32) * b_ref[...].astype(F32)
        hi = prod.astype(BF16)
        lo = (prod - hi.astype(F32)).astype(BF16)
        r = lax.broadcasted_iota(jnp.int32, (W, LANES), 0)
        c = lax.broadcasted_iota(jnp.int32, (W, LANES), 1)
        sel = jnp.where(r // FOX_DH == c, 1.0, 0.0).astype(BF16)
        d_t = (jnp.dot(hi, sel, preferred_element_type=F32) + jnp.dot(lo, sel, preferred_element_type=F32)).T
        for h in range(FOX_HEADS):
            o_ref[h] = d_t[h:h + 1, :]

    return pl.pallas_call(
        body, name=name, out_shape=jax.ShapeDtypeStruct((FOX_HEADS, 1, S), F32), grid=(S // ts,),
        in_specs=[pl.BlockSpec((ts, W), lambda i: (i, 0)), pl.BlockSpec((ts, W), lambda i: (i, 0))],
        out_specs=pl.BlockSpec((FOX_HEADS, 1, ts), lambda i: (0, 0, i)),
        compiler_params=_cparams("parallel"),
    )(a, b)


def _fox_bwd(zqkv, do, augq, augk, lse_row, delta_row, *, name):
    S = zqkv.shape[0]
    bk = _pick(S, FOX_BLOCK, 128)
    nk = S // bk
    scale = 1.0 / math.sqrt(FOX_DH)
    npair = N_PAIRS

    def body(q_ref, k_ref, v_ref, do_ref, ak_ref, aq_ref, lse_ref, dl_ref, dq_ref, dk_ref, dv_ref, dck_ref, dcq_ref):
        j = pl.program_id(1)

        @pl.when(j == 0)
        def _():
            dq_ref[...] = jnp.zeros_like(dq_ref)
            dcq_ref[...] = jnp.zeros_like(dcq_ref)

        lane = lax.broadcasted_iota(jnp.int32, (1, LANES), 1)
        first = lane < FOX_DH
        kb = k_ref[...]
        vb = v_ref[...]
        kh = (jnp.concatenate([kb, ak_ref[0]], axis=1), jnp.concatenate([kb, ak_ref[1]], axis=1))

        def step(i, carry, masked):
            start = pl.multiple_of(i * bk, bk)
            qs = q_ref[pl.ds(start, bk), :] * scale
            dob = do_ref[pl.ds(start, bk), :]
            zero = jnp.zeros_like(qs)
            qh = (jnp.where(first, qs, zero), jnp.where(first, zero, qs))
            doh = (jnp.where(first, dob, zero), jnp.where(first, zero, dob))
            out = []
            dqc = []
            for h in range(PAIR):
                dk, dv, dc = carry[h]
                qaug = jnp.concatenate([qh[h], aq_ref[h, pl.ds(start, bk), :]], axis=1)
                lse = lse_ref[h, :, pl.ds(start, bk)]
                dl = dl_ref[h, :, pl.ds(start, bk)]
                st = lax.dot_general(kh[h], qaug, (((1,), (1,)), ((), ())), preferred_element_type=F32)
                pt = jnp.exp(st - lse)
                if masked:
                    r = lax.broadcasted_iota(jnp.int32, (bk, bk), 0)
                    c = lax.broadcasted_iota(jnp.int32, (bk, bk), 1)
                    pt = jnp.where(c >= r, pt, 0.0)
                dpt = lax.dot_general(vb, doh[h], (((1,), (1,)), ((), ())), preferred_element_type=F32)
                dst = pt * (dpt - dl)
                pt_bf = pt.astype(BF16)
                dst_bf = dst.astype(BF16)
                dv = dv + jnp.dot(pt_bf, dob, preferred_element_type=F32)
                dk = dk + jnp.dot(dst_bf, qs, preferred_element_type=F32)
                dc = dc - jnp.sum(dst, axis=-1, keepdims=True)
                dcq_ref[h, :, pl.ds(start, bk)] += jnp.sum(dst, axis=0, keepdims=True)
                dqc.append(lax.dot_general(dst_bf, kb, (((0,), (0,)), ((), ())), preferred_element_type=F32))
                out.append((dk, dv, dc))
            dq_ref[pl.ds(start, bk), :] += jnp.where(first, dqc[0], dqc[1])
            return tuple(out)

        init = tuple((jnp.zeros((bk, LANES), F32), jnp.zeros((bk, LANES), F32), jnp.zeros((bk, 1), F32))
                     for _ in range(PAIR))
        carry = step(j, init, True)
        (dka, dva, dca), (dkb, dvb, dcb) = lax.fori_loop(j + 1, nk, lambda i, c: step(i, c, False), carry)
        dk_ref[...] = jnp.where(first, dka, dkb).astype(BF16)
        dv_ref[...] = jnp.where(first, dva, dvb).astype(BF16)
        dck_ref[0] = jnp.broadcast_to(dca, (bk, LANES)).T[0:1, :]
        dck_ref[1] = jnp.broadcast_to(dcb, (bk, LANES)).T[0:1, :]

        @pl.when(j == nk - 1)
        def _():
            dq_ref[...] = dq_ref[...] * scale

    rowfull = pl.BlockSpec((PAIR, 1, S), lambda hp, j: (hp, 0, 0))
    return pl.pallas_call(
        body, name=name,
        out_shape=(jax.ShapeDtypeStruct((S, FOX_W), F32), jax.ShapeDtypeStruct((S, FOX_W), BF16),
                   jax.ShapeDtypeStruct((S, FOX_W), BF16), jax.ShapeDtypeStruct((FOX_HEADS, 1, S), F32),
                   jax.ShapeDtypeStruct((FOX_HEADS, 1, S), F32)),
        grid=(npair, nk),
        in_specs=[pl.BlockSpec((S, LANES), lambda hp, j: (0, hp)),
                  pl.BlockSpec((bk, LANES), lambda hp, j: (j, npair + hp)),
                  pl.BlockSpec((bk, LANES), lambda hp, j: (j, 2 * npair + hp)),
                  pl.BlockSpec((S, LANES), lambda hp, j: (0, hp)),
                  pl.BlockSpec((PAIR, bk, LANES), lambda hp, j: (hp, j, 0)),
                  pl.BlockSpec((PAIR, S, LANES), lambda hp, j: (hp, 0, 0)), rowfull, rowfull],
        out_specs=(pl.BlockSpec((S, LANES), lambda hp, j: (0, hp)),
                   pl.BlockSpec((bk, LANES), lambda hp, j: (j, hp)),
                   pl.BlockSpec((bk, LANES), lambda hp, j: (j, hp)),
                   pl.BlockSpec((PAIR, 1, bk), lambda hp, j: (hp, 0, j)), rowfull),
        compiler_params=_cparams("parallel", "arbitrary"),
    )(zqkv, zqkv, zqkv, do, augk, augq, lse_row, delta_row)


def _xattn_fwd(q, kv, *, name):
    S, W = q.shape
    M = kv.shape[0]
    tq = _pick(S, 512, 8)
    scale = 1.0 / math.sqrt(X_DH)

    def body(q_ref, kv_ref, o_ref):
        for h in range(X_HEADS):
            cols = slice(h * X_DH, (h + 1) * X_DH)
            vcols = slice(W + h * X_DH, W + (h + 1) * X_DH)
            s = lax.dot_general(q_ref[:, cols], kv_ref[:, cols], (((1,), (1,)), ((), ())),
                                preferred_element_type=F32) * scale
            e = jnp.exp(s - jnp.max(s, axis=-1, keepdims=True))
            p = e / jnp.sum(e, axis=-1, keepdims=True)
            o_ref[:, cols] = jnp.dot(p.astype(BF16), kv_ref[:, vcols], preferred_element_type=F32).astype(BF16)

    return pl.pallas_call(
        body, name=name, out_shape=jax.ShapeDtypeStruct((S, W), BF16), grid=(S // tq,),
        in_specs=[pl.BlockSpec((tq, W), lambda i: (i, 0)), pl.BlockSpec((M, 2 * W), lambda i: (0, 0))],
        out_specs=pl.BlockSpec((tq, W), lambda i: (i, 0)),
        compiler_params=_cparams("parallel"),
    )(q, kv)


def _xattn_bwd(q, kv, do, *, name):
    S, W = q.shape
    M = kv.shape[0]
    tq = _pick(S, 512, 8)
    scale = 1.0 / math.sqrt(X_DH)

    def body(q_ref, kv_ref, do_ref, dq_ref, dkv_ref):
        i = pl.program_id(0)

        @pl.when(i == 0)
        def _():
            dkv_ref[...] = jnp.zeros_like(dkv_ref)

        for h in range(X_HEADS):
            cols = slice(h * X_DH, (h + 1) * X_DH)
            vcols = slice(W + h * X_DH, W + (h + 1) * X_DH)
            qh = q_ref[:, cols]
            kh = kv_ref[:, cols]
            vh = kv_ref[:, vcols]
            doh = do_ref[:, cols]
            s = lax.dot_general(qh, kh, (((1,), (1,)), ((), ())), preferred_element_type=F32) * scale
            e = jnp.exp(s - jnp.max(s, axis=-1, keepdims=True))
            p = e / jnp.sum(e, axis=-1, keepdims=True)
            dp = lax.dot_general(doh, vh, (((1,), (1,)), ((), ())), preferred_element_type=F32)
            ds = (p * (dp - jnp.sum(p * dp, axis=-1, keepdims=True)) * scale).astype(BF16)
            dq_ref[:, cols] = jnp.dot(ds, kh, preferred_element_type=F32).astype(BF16)
            dkv_ref[:, cols] += lax.dot_general(ds, qh, (((0,), (0,)), ((), ())), preferred_element_type=F32)
            dkv_ref[:, vcols] += lax.dot_general(p.astype(BF16), doh, (((0,), (0,)), ((), ())),
                                                 preferred_element_type=F32)

    return pl.pallas_call(
        body, name=name,
        out_shape=(jax.ShapeDtypeStruct((S, W), BF16), jax.ShapeDtypeStruct((M, 2 * W), F32)),
        grid=(S // tq,),
        in_specs=[pl.BlockSpec((tq, W), lambda i: (i, 0)), pl.BlockSpec((M, 2 * W), lambda i: (0, 0)),
                  pl.BlockSpec((tq, W), lambda i: (i, 0))],
        out_specs=(pl.BlockSpec((tq, W), lambda i: (i, 0)), pl.BlockSpec((M, 2 * W), lambda i: (0, 0))),
        compiler_params=_cparams("arbitrary"),
    )(q, kv, do)


GELU_C = math.sqrt(2.0 / math.pi)
GELU_A = 0.044715
CONV_HALO = 16


def _gelu_parts(x):
    u = GELU_C * (x + GELU_A * x * x * x)
    t = jnp.tanh(u)
    g = 0.5 * x * (1.0 + t)
    dg = 0.5 * (1.0 + t) + 0.5 * x * (1.0 - t * t) * (GELU_C * (1.0 + 3.0 * GELU_A * x * x))
    return g, dg


def _conv3(ext, w_ref, b_ref):
    return (w_ref[2:3, :] * ext + w_ref[1:2, :] * pltpu.roll(ext, 1, 0)
            + w_ref[0:1, :] * pltpu.roll(ext, 2, 0) + b_ref[...])


def _convglu_fwd(z, conv_w, conv_b, *, name):
    S, F2 = z.shape
    F = F2 // 2
    tc = _pick(F, 1408)
    ncol = F // tc
    T = _pick(S, 512, 8)
    hb = T // CONV_HALO

    def body(zg_ref, zu_ref, zgp_ref, zup_ref, wg_ref, wu_ref, bg_ref, bu_ref, act_ref):
        i = pl.program_id(1)
        first = (i > 0).astype(F32)

        def conv(z_ref, zp_ref, w_ref, b_ref):
            ext = jnp.concatenate([zp_ref[...].astype(F32) * first, z_ref[...].astype(F32)], axis=0)
            return _conv3(ext, w_ref, b_ref)[CONV_HALO:, :]

        gc = conv(zg_ref, zgp_ref, wg_ref, bg_ref)
        uc = conv(zu_ref, zup_ref, wu_ref, bu_ref)
        act_ref[...] = (_gelu_parts(gc)[0] * uc).astype(BF16)

    cur = lambda off: pl.BlockSpec((T, tc), lambda j, i: (i, j + off))
    prev = lambda off: pl.BlockSpec((CONV_HALO, tc), lambda j, i: (jnp.maximum(i * hb - 1, 0), j + off))
    vec = lambda rows, off: pl.BlockSpec((rows, tc), lambda j, i: (0, j + off))
    return pl.pallas_call(
        body, name=name, out_shape=jax.ShapeDtypeStruct((S, F), BF16), grid=(ncol, S // T),
        in_specs=[cur(0), cur(ncol), prev(0), prev(ncol), vec(3, 0), vec(3, ncol), vec(1, 0), vec(1, ncol)],
        out_specs=pl.BlockSpec((T, tc), lambda j, i: (i, j)),
        compiler_params=_cparams("parallel", "parallel"),
    )(z, z, z, z, conv_w, conv_w, conv_b, conv_b)


def _convglu_bwd(z, dact, conv_w, conv_b, *, name):
    S, F2 = z.shape
    F = F2 // 2
    tc = _pick(F, 1408)
    ncol = F // tc
    T = _pick(S, 256, 8)
    nrow = S // T
    hb = T // CONV_HALO
    TE = T + CONV_HALO

    def body(z_ref, zp_ref, zn_ref, da_ref, dan_ref, w_ref, b_ref, dz_ref, dw_ref):
        i = pl.program_id(0)
        first = (i > 0).astype(F32)
        last = (i < nrow - 1).astype(F32)

        @pl.when(i == 0)
        def _():
            dw_ref[...] = jnp.zeros_like(dw_ref)

        def ext_of(cols):
            return jnp.concatenate([zp_ref[:, cols].astype(F32) * first, z_ref[:, cols].astype(F32),
                                    zn_ref[:, cols].astype(F32)], axis=0)

        def back(d, ext, w, cols):
            dz = w[2:3, :] * d + w[1:2, :] * pltpu.roll(d, TE - 1, 0) + w[0:1, :] * pltpu.roll(d, TE - 2, 0)
            dz_ref[:, cols] = dz[:T, :].astype(BF16)
            dc = d[:T, :]
            z0 = ext[CONV_HALO:CONV_HALO + T, :]
            z1 = pltpu.roll(ext, 1, 0)[CONV_HALO:CONV_HALO + T, :]
            z2 = pltpu.roll(ext, 2, 0)[CONV_HALO:CONV_HALO + T, :]
            rows = [jnp.sum(dc * z2, axis=0, keepdims=True), jnp.sum(dc * z1, axis=0, keepdims=True),
                    jnp.sum(dc * z0, axis=0, keepdims=True), jnp.sum(dc, axis=0, keepdims=True)]
            dw_ref[0:4, cols] += jnp.concatenate(rows, axis=0)

        for jj in range(ncol):
            cg = slice(jj * tc, (jj + 1) * tc)
            cu = slice(F + jj * tc, F + (jj + 1) * tc)
            extg, extu = ext_of(cg), ext_of(cu)
            wg, wu = w_ref[:, cg], w_ref[:, cu]
            gc = _conv3(extg, wg, b_ref[:, cg])[CONV_HALO:, :]
            uc = _conv3(extu, wu, b_ref[:, cu])[CONV_HALO:, :]
            da = jnp.concatenate([da_ref[:, cg].astype(F32), dan_ref[:, cg].astype(F32) * last], axis=0)
            gl, dgl = _gelu_parts(gc)
            back(da * uc * dgl, extg, wg, cg)
            back(da * gl, extu, wu, cu)

    halo_rows = S // CONV_HALO
    return pl.pallas_call(
        body, name=name,
        out_shape=(jax.ShapeDtypeStruct((S, F2), BF16), jax.ShapeDtypeStruct((8, F2), F32)),
        grid=(nrow,),
        in_specs=[pl.BlockSpec((T, F2), lambda i: (i, 0)),
                  pl.BlockSpec((CONV_HALO, F2), lambda i: (jnp.maximum(i * hb - 1, 0), 0)),
                  pl.BlockSpec((CONV_HALO, F2), lambda i: (jnp.minimum((i + 1) * hb, halo_rows - 1), 0)),
                  pl.BlockSpec((T, F), lambda i: (i, 0)),
                  pl.BlockSpec((CONV_HALO, F), lambda i: (jnp.minimum((i + 1) * hb, halo_rows - 1), 0)),
                  pl.BlockSpec((3, F2), lambda i: (0, 0)), pl.BlockSpec((1, F2), lambda i: (0, 0))],
        out_specs=(pl.BlockSpec((T, F2), lambda i: (i, 0)), pl.BlockSpec((8, F2), lambda i: (0, 0))),
        compiler_params=_cparams("arbitrary"),
    )(z, z, z, dact, dact, conv_w, conv_b)


OFF_QKV = POOL_W
OFF_G = POOL_W + 3 * FOX_W


SHARD_BY_ROWS = {"w_in": False, "w_pool_br": False, "w_fox_br": False, "w_mix_out": True, "w_xq": True,
                 "w_xkv": True, "w_xo": False, "w_up": False, "w_down": True, "conv_w": False}


def _prep_layer_weights(w, l):
    n, _, D, b = w["w_in"].shape
    w_in = w["w_in"][:, l].transpose(1, 0, 2).reshape(D, n * b)
    off_f = OFF_G
    pad = jnp.zeros((D, LANES - N_FGATE), w_in.dtype)
    w_in_r = jnp.concatenate([w_in[:, :off_f], w_in[:, off_f + N_FGATE:], w_in[:, off_f:off_f + N_FGATE], pad], axis=1)
    wl = {k: _ChipMajor(w[k], l, SHARD_BY_ROWS[k]) for k in MATMUL_WEIGHTS if k != "w_in"}
    wl["w_in_r"] = w_in_r
    return wl


def _row(v):
    return v.reshape(1, -1)


def _layer_fwd(x, h1, mem, wl, p, l):
    S, D = x.shape
    n = lambda s: f"l{l}_{s}"
    sv = {"x0": x}
    w_in_r = wl["w_in_r"]
    og = OFF_G
    zu = _matmul(h1, w_in_r[:, :OFF_QKV], out_dtype=F32, name=n("mm_zu"))
    zqkv = _matmul(h1, w_in_r[:, OFF_QKV:og], out_dtype=BF16, name=n("mm_zqkv"))
    zg = _matmul(h1, w_in_r[:, og:og + 2 * D], out_dtype=BF16, name=n("mm_zg"))
    zf = _matmul(h1, w_in_r[:, og + 2 * D:], out_dtype=F32, name=n("mm_zf"))
    bf = jnp.pad(p["b_forget"][l], (0, LANES - N_FGATE)).reshape(1, LANES)
    _, augq, augk = _fgate_fwd(zf, bf, name=n("fgate"))
    o_fox, lse = _fox_fwd(zqkv, augq, augk, name=n("fox"))
    pooled, mixed = _pool_fwd(zu, p["pool_w"][l], _row(p["pool_scale"][l]), name=n("pool"))
    y_pool = _matmul(mixed, wl["w_pool_br"], out_dtype=BF16, name=n("mm_ypool"))
    y_fox = _matmul(o_fox, wl["w_fox_br"], out_dtype=BF16, name=n("mm_yfox"))
    merged = _merge_fwd(zg, y_pool, y_fox, name=n("merge"))
    r1 = _matmul(merged, wl["w_mix_out"], out_dtype=BF16, name=n("mm_r1"))
    x1, h2 = _add_rms(x, r1, _row(p["mix_post_g"][l]), _row(p["xa_pre_g"][l]), name=n("addrms1"))
    sv.update(h1=h1, zg=zg, zf=zf, bf=bf, zqkv=zqkv, augq=augq, augk=augk, o_fox=o_fox, lse=lse,
              pooled=pooled, mixed=mixed, y_pool=y_pool, y_fox=y_fox, merged=merged, r1=r1, x1=x1)
    mem_n = _rms_fwd(mem, _row(p["mem_g"][l]), out_dtype=BF16, name=n("rms_mem"))
    q2 = _matmul(h2, wl["w_xq"], out_dtype=BF16, name=n("mm_q2"))
    kv = _matmul(mem_n, wl["w_xkv"], out_dtype=BF16, name=n("mm_kv"))
    o2 = _xattn_fwd(q2, kv, name=n("xattn"))
    a2 = _matmul(o2, wl["w_xo"], out_dtype=BF16, name=n("mm_a2"))
    x2, h3 = _add_rms(x1, a2, _row(p["xa_post_g"][l]), _row(p["ffn_pre_g"][l]), name=n("addrms2"))
    sv.update(h2=h2, mem_n=mem_n, q2=q2, kv=kv, o2=o2, a2=a2, x2=x2)
    z3 = _matmul(h3, wl["w_up"], out_dtype=BF16, name=n("mm_z3"))
    act = _convglu_fwd(z3, p["conv_w"][l], _row(p["conv_b"][l]), name=n("convglu"))
    d3 = _matmul(act, wl["w_down"], out_dtype=BF16, name=n("mm_d3"))
    if l + 1 < p["mix_pre_g"].shape[0]:
        x3, h_next = _add_rms(x2, d3, _row(p["ffn_post_g"][l]), _row(p["mix_pre_g"][l + 1]), name=n("addrms3"))
    else:
        x3, h_next = _add_rms(x2, d3, _row(p["ffn_post_g"][l]), name=n("addrms3")), None
    sv.update(h3=h3, z3=z3, act=act, d3=d3)
    return x3, h_next, sv


def _layer_bwd(dx, mem, wl, p, l, sv, gbuf):
    S, D = dx.shape
    n_layers = p["mix_pre_g"].shape[0]
    n = lambda s: f"l{l}_b_{s}"
    g = {}
    red = lambda part: jnp.sum(part, axis=0)

    def wgrad(k, lhs, rhs, nm):
        how = dict(out_rows=N_CHIPS) if SHARD_BY_ROWS[k] else dict(out_chips=N_CHIPS)
        return _matmul(lhs, rhs, ta=True, out_dtype=F32, out_layer=(gbuf.get(k), l, n_layers), name=n(nm), **how)

    dd3, dg = _rms_bwd(sv["d3"], _row(p["ffn_post_g"][l]), dx, out_dtype=BF16, name=n("rms3post"))
    g["ffn_post_g"] = red(dg)
    dact = _matmul(dd3, wl["w_down"], tb=True, out_dtype=BF16, name=n("mm_dact"))
    g["w_down"] = wgrad("w_down", sv["act"], dd3, "mm_dwdown")
    dz3, dconv = _convglu_bwd(sv["z3"], dact, p["conv_w"][l], _row(p["conv_b"][l]), name=n("convglu"))
    g["conv_w"] = dconv[:3]
    g["conv_b"] = dconv[3]
    dh3 = _matmul(dz3, wl["w_up"], tb=True, out_dtype=BF16, name=n("mm_dh3"))
    g["w_up"] = wgrad("w_up", sv["h3"], dz3, "mm_dwup")
    dx, dg = _rms_bwd(sv["x2"], _row(p["ffn_pre_g"][l]), dh3, dx, out_dtype=F32, name=n("rms3pre"))
    g["ffn_pre_g"] = red(dg)
    da2, dg = _rms_bwd(sv["a2"], _row(p["xa_post_g"][l]), dx, out_dtype=BF16, name=n("rms2post"))
    g["xa_post_g"] = red(dg)
    do2 = _matmul(da2, wl["w_xo"], tb=True, out_dtype=BF16, name=n("mm_do2"))
    g["w_xo"] = wgrad("w_xo", sv["o2"], da2, "mm_dwxo")
    dq2, dkv = _xattn_bwd(sv["q2"], sv["kv"], do2, name=n("xattn"))
    dh2 = _matmul(dq2, wl["w_xq"], tb=True, out_dtype=BF16, name=n("mm_dh2"))
    g["w_xq"] = wgrad("w_xq", sv["h2"], dq2, "mm_dwxq")
    dmem_n = _matmul(dkv, wl["w_xkv"], tb=True, out_dtype=F32, name=n("mm_dmemn"))
    g["w_xkv"] = wgrad("w_xkv", sv["mem_n"], dkv, "mm_dwxkv")
    _, dg = _rms_bwd(mem, _row(p["mem_g"][l]), dmem_n, out_dtype=BF16, name=n("rms_mem"))
    g["mem_g"] = red(dg)
    dx, dg = _rms_bwd(sv["x1"], _row(p["xa_pre_g"][l]), dh2, dx, out_dtype=F32, name=n("rms2pre"))
    g["xa_pre_g"] = red(dg)
    dr1, dg = _rms_bwd(sv["r1"], _row(p["mix_post_g"][l]), dx, out_dtype=BF16, name=n("rms1post"))
    g["mix_post_g"] = red(dg)
    dmerged = _matmul(dr1, wl["w_mix_out"], tb=True, out_dtype=BF16, name=n("mm_dmerged"))
    g["w_mix_out"] = wgrad("w_mix_out", sv["merged"], dr1, "mm_dwmo")
    dyp, dyf, dzg = _merge_bwd(sv["zg"], sv["y_pool"], sv["y_fox"], dmerged, name=n("merge"))
    dmixed = _matmul(dyp, wl["w_pool_br"], tb=True, out_dtype=F32, name=n("mm_dmixed"))
    g["w_pool_br"] = wgrad("w_pool_br", sv["mixed"], dyp, "mm_dwpb")
    dofox = _matmul(dyf, wl["w_fox_br"], tb=True, out_dtype=F32, name=n("mm_dofox"))
    g["w_fox_br"] = wgrad("w_fox_br", sv["o_fox"], dyf, "mm_dwfb")
    dzu, dpw, dsc = _pool_bwd(sv["pooled"], dmixed, p["pool_w"][l], _row(p["pool_scale"][l]), name=n("pool"))
    g["pool_w"] = dpw
    g["pool_scale"] = red(dsc)
    delta = _head_rowsum(dofox, sv["o_fox"], name=n("delta"))
    dq, dk, dv, dck, dcq = _fox_bwd(sv["zqkv"], dofox.astype(BF16), sv["augq"], sv["augk"], sv["lse"], delta,
                                    name=n("fox"))
    dc_pad = jnp.pad((dcq + dck).reshape(FOX_HEADS, S).T, ((0, 0), (0, LANES - FOX_HEADS)))
    dzf, db = _fgate_bwd(sv["zf"], sv["bf"], dc_pad, name=n("fgate"))
    g["b_forget"] = red(db)[:N_FGATE]
    dz_cat = jnp.concatenate([dzu, dq.astype(BF16), dk, dv, dzg, dzf], axis=1)
    dh1 = _matmul(dz_cat, wl["w_in_r"], tb=True, out_dtype=BF16, name=n("mm_dh1"))
    dw_in_r = _matmul(sv["h1"], dz_cat, ta=True, out_dtype=F32, name=n("mm_dwin"))
    og = OFF_G
    g["w_in"] = jnp.concatenate([dw_in_r[:, :og], dw_in_r[:, og + 2 * D:og + 2 * D + N_FGATE],
                                 dw_in_r[:, og:og + 2 * D]], axis=1)
    dx, dg = _rms_bwd(sv["x0"], _row(p["mix_pre_g"][l]), dh1, dx, out_dtype=F32, name=n("rms1pre"))
    g["mix_pre_g"] = red(dg)
    return dx, g


MATMUL_WEIGHTS = ("w_in", "w_pool_br", "w_fox_br", "w_mix_out", "w_xq", "w_xkv", "w_xo", "w_up", "w_down")
WEIGHT_NAMES = ("mix_pre_g", "mix_post_g", "w_in", "b_forget", "pool_w", "pool_scale", "w_pool_br", "w_fox_br",
                "w_mix_out", "xa_pre_g", "xa_post_g", "mem_g", "w_xq", "w_xkv", "w_xo", "ffn_pre_g", "ffn_post_g",
                "w_up", "conv_w", "conv_b", "w_down")


def _local_step(x, mem, loss_target, wfull, p):
    L = p["mix_pre_g"].shape[0]
    saved, wls = [], []
    h = x
    hn = _rms_fwd(x, _row(p["mix_pre_g"][0]), out_dtype=BF16, name="rms_first")
    for l in range(L):
        wl = _prep_layer_weights(wfull, l)
        h, hn, sv = _layer_fwd(h, hn, mem, wl, p, l)
        saved.append(sv)
        wls.append(wl)
    D = x.shape[1]
    dy, sq = _loss_head(h, loss_target, name="loss_head")
    loss = 0.5 * jnp.sum(sq) / D
    grads = []
    gbuf = {}
    dx = dy
    for l in reversed(range(L)):
        dx, g = _layer_bwd(dx, mem, wls[l], p, l, saved[l], gbuf)
        gbuf = {k: g[k] for k in MATMUL_WEIGHTS if k != "w_in"}
        grads.append(g)
    grads = grads[::-1]

    def chip_major(g):
        return g.reshape(g.shape[0], N_CHIPS, g.shape[1] // N_CHIPS).transpose(1, 0, 2)

    gfull = {k: jnp.stack([grads[l][k] for l in range(L)]) for k in REPLICATED}
    gfull.update({k: jnp.stack([chip_major(grads[l][k]) for l in range(L)], axis=1) for k in ("w_in", "conv_w")})
    gfull.update(gbuf)
    return loss, dx, gfull


PACK_W = 512
PACK_ROW_ALIGN = 1024
N_CHIPS = 4
N_DEV = 8
SHARDED = (("w_in", 2), ("w_pool_br", 2), ("w_fox_br", 2), ("w_mix_out", 1), ("w_xq", 1), ("w_xkv", 1),
           ("w_xo", 2), ("w_up", 2), ("w_down", 1), ("conv_w", 2))
REPLICATED = ("mix_pre_g", "mix_post_g", "b_forget", "pool_w", "pool_scale", "xa_pre_g", "xa_post_g", "mem_g",
              "ffn_pre_g", "ffn_post_g", "conv_b")


def _round_up(n, m):
    return -(-n // m) * m


def _pack(arrs, rows):
    flat = jnp.concatenate([a.reshape(-1) for a in arrs])
    return jnp.pad(flat, (0, rows * PACK_W - flat.shape[0])).reshape(rows, PACK_W)


def _unpack(buf, shapes):
    flat = buf.reshape(-1)
    out, off = [], 0
    for s in shapes:
        n = math.prod(s)
        out.append(flat[off:off + n].reshape(s))
        off += n
    return out


ANY = pl.BlockSpec(memory_space=pl.ANY)


def _remote(send_sems, recv_sems, k, src, dst, to):
    return pltpu.make_async_remote_copy(src_ref=src, dst_ref=dst, send_sem=send_sems.at[k], recv_sem=recv_sems.at[k],
                                        device_id=to, device_id_type=MESH)


def _my_place():
    return lax.axis_index("x"), lax.axis_index("y"), lax.axis_index("c")


def _rows_per_block(a, b, cap_bytes=1024 * 1024):
    if a % 8:
        return a
    return _pick(a, max(8, cap_bytes // (4 * b) // 8 * 8), 8)


def _place_shard(w, chip, dtype, *, name):
    L, a, b = w.shape
    ta = _rows_per_block(a, b)

    def body(chip_ref, w_ref, o_ref):
        o_ref[...] = w_ref[...].astype(dtype)

    return pl.pallas_call(
        body, name=name, out_shape=jax.ShapeDtypeStruct((N_CHIPS, L, a, b), dtype),
        grid_spec=pltpu.PrefetchScalarGridSpec(
            num_scalar_prefetch=1, grid=(L, a // ta),
            in_specs=[pl.BlockSpec((None, ta, b), lambda l, i, chip_ref: (l, i, 0))],
            out_specs=pl.BlockSpec((None, None, ta, b), lambda l, i, chip_ref: (chip_ref[0], l, i, 0))),
        compiler_params=_cparams("parallel", "parallel"),
    )(chip, w)


def _layer_halves(L, c):
    assert L % 2 == 0
    return pl.ds(c * (L // 2), L // 2), pl.ds((1 - c) * (L // 2), L // 2)


def _gather_weights(bufs):
    n = len(bufs)
    L = bufs[0].shape[1]

    def body(*refs):
        outs, (send_sems, recv_sems) = refs[n:2 * n], refs[2 * n:]
        x, y, c = _my_place()
        me = 2 * x + y
        sibling = (x, y, 1 - c)
        chips = [(1 - x, y), (x, 1 - y), (1 - x, 1 - y)]
        half, other = _layer_halves(L, c)
        rc = functools.partial(_remote, send_sems, recv_sems)
        first = [rc(6 * w + k, o.at[me, half], o.at[me, half], (px, py, c))
                 for w, o in enumerate(outs) for k, (px, py) in enumerate(chips)]
        for cp in first:
            cp.start()
        passed = []
        for k, (px, py) in enumerate(chips):
            src = 2 * px + py
            for w, o in enumerate(outs):
                rc(6 * w + k, o.at[src, half], o.at[src, half], (px, py, c)).wait_recv()
                fwd = rc(6 * w + 3 + k, o.at[src, half], o.at[src, half], sibling)
                fwd.start()
                passed.append(fwd)
        for k, (px, py) in enumerate(chips):
            src = 2 * px + py
            for w, o in enumerate(outs):
                rc(6 * w + 3 + k, o.at[src, other], o.at[src, other], sibling).wait_recv()
        for cp in first + passed:
            cp.wait_send()

    return pl.pallas_call(
        body, name="gather_weights",
        out_shape=tuple(jax.ShapeDtypeStruct(b.shape, b.dtype) for b in bufs),
        in_specs=[ANY] * n, out_specs=tuple([ANY] * n), input_output_aliases={i: i for i in range(n)},
        scratch_shapes=[pltpu.SemaphoreType.DMA((6 * n,)), pltpu.SemaphoreType.DMA((6 * n,))],
    )(*bufs)


def _exchange_halves(Gs, rep):
    n = len(Gs)
    L = Gs[0].shape[1]
    RR, W = rep.shape

    def body(*refs):
        g_refs, rep_ref, ra_refs, rall_ref = refs[:n], refs[n], refs[n + 1:2 * n + 1], refs[2 * n + 1]
        send_sems, recv_sems, local_sem = refs[2 * n + 2:]
        x, y, c = _my_place()
        me = 4 * x + 2 * y + c
        rc = functools.partial(_remote, send_sems, recv_sems)
        _, other = _layer_halves(L, c)

        def peer(idx):
            px = (1 - x) if (idx >> 2) & 1 else x
            py = (1 - y) if (idx >> 1) & 1 else y
            pc = (1 - c) if idx & 1 else c
            return px, py, pc

        loc = pltpu.make_async_copy(rep_ref, rall_ref.at[me], local_sem)
        loc.start()
        cps = [rc(N_DEV + w, g.at[:, other], ra, (x, y, 1 - c)) for w, (g, ra) in enumerate(zip(g_refs, ra_refs))]
        for idx in range(1, N_DEV):
            cps.append(rc(idx, rep_ref, rall_ref.at[me], peer(idx)))
        for cp in cps:
            cp.start()
        for w, (g, ra) in enumerate(zip(g_refs, ra_refs)):
            rc(N_DEV + w, g.at[:, other], ra, (x, y, 1 - c)).wait_recv()
        for idx in range(1, N_DEV):
            px, py, pc = peer(idx)
            rc(idx, rep_ref, rall_ref.at[4 * px + 2 * py + pc], (px, py, pc)).wait_recv()
        for cp in cps:
            cp.wait_send()
        loc.wait()

    halves = tuple(jax.ShapeDtypeStruct((g.shape[0], L // 2) + g.shape[2:], F32) for g in Gs)
    out = pl.pallas_call(
        body, name="exchange_halves",
        out_shape=halves + (jax.ShapeDtypeStruct((N_DEV, RR, W), F32),),
        in_specs=[ANY] * (n + 1), out_specs=tuple([ANY] * (n + 1)),
        scratch_shapes=[pltpu.SemaphoreType.DMA((N_DEV + n,)), pltpu.SemaphoreType.DMA((N_DEV + n,)),
                        pltpu.SemaphoreType.DMA],
    )(*Gs, rep)
    return out[:n], out[n]


def _exchange_chips(As):
    n = len(As)

    def body(*refs):
        a_refs, rb_refs, (send_sems, recv_sems) = refs[:n], refs[n:2 * n], refs[2 * n:]
        x, y, c = _my_place()
        me = 2 * x + y
        chips = [(1 - x, y), (x, 1 - y), (1 - x, 1 - y)]
        rc = functools.partial(_remote, send_sems, recv_sems)
        cps = [rc(3 * w + k, a.at[2 * px + py], rb.at[k], (px, py, c))
               for w, (a, rb) in enumerate(zip(a_refs, rb_refs)) for k, (px, py) in enumerate(chips)]
        for cp in cps:
            cp.start()
        for w, (a, rb) in enumerate(zip(a_refs, rb_refs)):
            for k, (px, py) in enumerate(chips):
                rc(3 * w + k, a.at[me], rb.at[k], (px, py, c)).wait_recv()
        for cp in cps:
            cp.wait_send()

    return pl.pallas_call(
        body, name="exchange_chips",
        out_shape=tuple(jax.ShapeDtypeStruct((N_CHIPS - 1,) + a.shape[1:], a.dtype) for a in As),
        in_specs=[ANY] * n, out_specs=tuple([ANY] * n),
        scratch_shapes=[pltpu.SemaphoreType.DMA((3 * n,)), pltpu.SemaphoreType.DMA((3 * n,))],
    )(*As)


def _exchange_sibling(gs):
    n = len(gs)
    L = gs[0].shape[0]

    def body(*refs):
        g_refs, (send_sems, recv_sems) = refs[n:2 * n], refs[2 * n:]
        x, y, c = _my_place()
        half, other = _layer_halves(L, c)
        rc = functools.partial(_remote, send_sems, recv_sems)
        cps = [rc(w, g.at[half], g.at[half], (x, y, 1 - c)) for w, g in enumerate(g_refs)]
        for cp in cps:
            cp.start()
        for w, g in enumerate(g_refs):
            rc(w, g.at[other], g.at[other], (x, y, 1 - c)).wait_recv()
        for cp in cps:
            cp.wait_send()

    return pl.pallas_call(
        body, name="exchange_sibling", out_shape=tuple(jax.ShapeDtypeStruct(g.shape, F32) for g in gs),
        in_specs=[ANY] * n, out_specs=tuple([ANY] * n), input_output_aliases={i: i for i in range(n)},
        scratch_shapes=[pltpu.SemaphoreType.DMA((n,)), pltpu.SemaphoreType.DMA((n,))],
    )(*gs)


def _add_halves(G, recv, core, *, name):
    n, L, a, b = G.shape
    Lh = L // 2
    ta = _rows_per_block(a, b)

    def body(core_ref, g_ref, r_ref, o_ref, o16_ref):
        s = g_ref[...] + r_ref[...]
        o_ref[...] = s
        o16_ref[...] = s.astype(BF16)

    blk = pl.BlockSpec((None, None, ta, b), lambda p, l, i, core_ref: (p, l, i, 0))
    return pl.pallas_call(
        body, name=name,
        out_shape=(jax.ShapeDtypeStruct((n, Lh, a, b), F32), jax.ShapeDtypeStruct((n, Lh, a, b), BF16)),
        grid_spec=pltpu.PrefetchScalarGridSpec(
            num_scalar_prefetch=1, grid=(n, Lh, a // ta),
            in_specs=[pl.BlockSpec((None, None, ta, b), lambda p, l, i, core_ref: (p, core_ref[0] * Lh + l, i, 0)), blk],
            out_specs=(blk, blk)),
        compiler_params=_cparams("parallel", "parallel", "parallel"),
    )(core, G, recv)


def _sum_chips(A, rb, place, *, name):
    _, Lh, a, b = A.shape
    ta = _rows_per_block(a, b, 512 * 1024)

    def body(place_ref, a_ref, r_ref, o_ref):
        o_ref[...] = ((a_ref[...] + r_ref[0].astype(F32)) + r_ref[1].astype(F32)) + r_ref[2].astype(F32)

    return pl.pallas_call(
        body, name=name, out_shape=jax.ShapeDtypeStruct((2 * Lh, a, b), F32),
        grid_spec=pltpu.PrefetchScalarGridSpec(
            num_scalar_prefetch=1, grid=(Lh, a // ta),
            in_specs=[pl.BlockSpec((None, None, ta, b), lambda l, i, place_ref: (place_ref[0], l, i, 0)),
                      pl.BlockSpec((N_CHIPS - 1, None, ta, b), lambda l, i, place_ref: (0, l, i, 0))],
            out_specs=pl.BlockSpec((None, ta, b), lambda l, i, place_ref: (place_ref[1] * Lh + l, i, 0))),
        compiler_params=_cparams("parallel", "parallel"),
    )(place, A, rb)


def _sum_slots(a, *, name):
    n, rows, W = a.shape
    tr = _pick(rows, 512, 8)

    def body(a_ref, o_ref):
        s = a_ref[0]
        for q in range(1, n):
            s = s + a_ref[q]
        o_ref[...] = s

    return pl.pallas_call(
        body, name=name, out_shape=jax.ShapeDtypeStruct((rows, W), F32), grid=(rows // tr,),
        in_specs=[pl.BlockSpec((n, tr, W), lambda i: (0, i, 0))], out_specs=pl.BlockSpec((tr, W), lambda i: (i, 0)),
        compiler_params=_cparams("parallel"),
    )(a)


def _adamw(w, g, m, v, *, name):
    L, a, b = w.shape
    ta = _rows_per_block(a, b, 512 * 1024)

    def body(w_ref, g_ref, m_ref, v_ref, d_ref, nm_ref, nv_ref):
        gg = g_ref[...]
        nm = ADAM_B1 * m_ref[...] + (1.0 - ADAM_B1) * gg
        nv = ADAM_B2 * v_ref[...] + (1.0 - ADAM_B2) * jnp.square(gg)
        m_hat = nm / (1.0 - ADAM_B1 ** ADAM_STEP)
        v_hat = nv / (1.0 - ADAM_B2 ** ADAM_STEP)
        d_ref[...] = -ADAM_LR * (m_hat / (jnp.sqrt(v_hat) + ADAM_EPS) + ADAM_WD * w_ref[...])
        nm_ref[...] = nm
        nv_ref[...] = nv

    blk = pl.BlockSpec((None, ta, b), lambda l, i: (l, i, 0))
    shp = jax.ShapeDtypeStruct((L, a, b), F32)
    return pl.pallas_call(
        body, name=name, out_shape=(shp, shp, shp), grid=(L, a // ta),
        in_specs=[blk, blk, blk, blk], out_specs=(blk, blk, blk),
        compiler_params=_cparams("parallel", "parallel"),
    )(w, g, m, v)


INPUT_NAMES = (("x", "mem") + WEIGHT_NAMES + ("loss_target",) + tuple("m_" + n for n in WEIGHT_NAMES)
               + tuple("v_" + n for n in WEIGHT_NAMES))


def kernel(*args):
    a = dict(zip(INPUT_NAMES, args, strict=True))
    x, mem, target = a["x"][0], a["mem"][0], a["loss_target"][0]
    sh_names = list(SHARD_BY_ROWS)
    core = lax.axis_index("c").astype(jnp.int32)
    chip = (2 * lax.axis_index("x") + lax.axis_index("y")).astype(jnp.int32)
    place = jnp.stack([chip, core])

    placed = [_place_shard(a[n], chip.reshape(1), F32 if n == "conv_w" else BF16, name="place_" + n) for n in sh_names]
    wfull = dict(zip(sh_names, _gather_weights(placed)))
    p = {n: a[n] for n in REPLICATED}
    cw = wfull.pop("conv_w")
    p["conv_w"] = cw.transpose(1, 2, 0, 3).reshape(cw.shape[1], cw.shape[2], N_CHIPS * cw.shape[3])

    loss, dx, gfull = _local_step(x, mem, target, wfull, p)
    loss = lax.psum(loss, ("x", "y", "c"))

    Gs = [gfull[n] for n in sh_names]
    rep_shapes = [a[n].shape for n in REPLICATED]
    rows_r = _round_up(-(-sum(math.prod(s) for s in rep_shapes) // PACK_W), 64)
    rep = _pack([gfull[n] for n in REPLICATED], rows_r)
    recvs, repall = _exchange_halves(Gs, rep)
    As = [_add_halves(g, r, core.reshape(1), name="add_halves_" + n) for n, g, r in zip(sh_names, Gs, recvs)]
    rbs = _exchange_chips([a16 for _, a16 in As])
    gsh = _exchange_sibling([_sum_chips(A, rb, place, name="sum_chips_" + n)
                             for n, (A, _), rb in zip(sh_names, As, rbs)])
    grep = _sum_slots(repall, name="sum_devices")

    got = {"g": dict(zip(sh_names, gsh)), "d": {}, "m": {}, "v": {}}
    for n, g in zip(sh_names, gsh):
        got["d"][n], got["m"][n], got["v"][n] = _adamw(a[n], g, a["m_" + n], a["v_" + n], name="adamw_" + n)
    packed = [_pack([a[pre + n] for n in REPLICATED], rows_r)[None] for pre in ("", "m_", "v_")]
    d_r, m_r, v_r = _adamw(packed[0], grep[None], packed[1], packed[2], name="adamw_replicated")
    for key, buf in (("g", grep), ("d", d_r[0]), ("m", m_r[0]), ("v", v_r[0])):
        got[key].update(zip(REPLICATED, _unpack(buf, rep_shapes)))
    outs = [got[key][n] for key in ("g", "d", "m", "v") for n in WEIGHT_NAMES]
    return (loss, dx[None], *outs)
```

```python
import functools
import math

import jax
import jax.numpy as jnp
from jax import lax
from jax.experimental import pallas as pl
from jax.experimental.pallas import tpu as pltpu

F32 = jnp.float32
BF16 = jnp.bfloat16
MESH = pl.DeviceIdType.MESH

RMS_EPS = 1e-6
POOL_WINDOWS = (2, 4, 8, 16)
POOL_GROUP = 128
POOL_W = 512
FOX_HEADS = 8
FOX_DH = 64
FOX_W = 512
X_HEADS = 4
X_DH = 128
X_W = 512
N_FGATE = 8
LANES = 128
HALO = 16

ADAM_LR = 0.001
ADAM_B1 = 0.9
ADAM_B2 = 0.999
ADAM_EPS = 1e-08
ADAM_WD = 0.01
ADAM_STEP = 10

VMEM_LIMIT_BYTES = 56 * 1024 * 1024
MATMUL_VMEM_BUDGET = 42 * 1024 * 1024


def _cparams(*sem):
    return pltpu.CompilerParams(dimension_semantics=sem, vmem_limit_bytes=VMEM_LIMIT_BYTES)


def _pick(n, cap, align=LANES):
    if n <= cap:
        return n
    best = None
    for t in range(align, cap + 1, align):
        if n % t == 0:
            best = t
    assert best is not None, (n, cap, align)
    return best


def _sigmoid(x):
    return 1.0 / (1.0 + jnp.exp(-x))


class _ChipMajor:
    def __init__(self, arr, layer, by_rows):
        self.arr, self.layer, self.by_rows = arr, layer, by_rows
        n, _, a, b = arr.shape
        self.n_chips, self.per_chip = n, (a if by_rows else b)
        self.shape = (n * a, b) if by_rows else (a, n * b)


SPAN_CHIPS_BELOW = 512


def _matmul(a, b, *, ta=False, tb=False, out_dtype=F32, out_chips=None, out_rows=None, out_layer=None, name):
    view = b if isinstance(b, _ChipMajor) else None
    if ta:
        K, M = a.shape
    else:
        M, K = a.shape
    if tb:
        N, Kb = b.shape
    else:
        Kb, N = b.shape
    assert K == Kb, (a.shape, b.shape, ta, tb)
    by_rows = view is not None and view.by_rows
    by_cols = view is not None and not view.by_rows
    b_itemsize = (view.arr if view is not None else b).dtype.itemsize
    span_b = by_cols and view.per_chip < SPAN_CHIPS_BELOW
    span_o = bool(out_chips) and N // out_chips < SPAN_CHIPS_BELOW
    if by_rows and tb:
        tn = N
    elif by_cols and not tb:
        tn = N if span_b else _pick(view.per_chip, 1408)
    elif out_chips:
        tn = N if span_o else _pick(N // out_chips, 1408)
    elif out_rows:
        tn = _pick(N, 512)
    else:
        tn = _pick(N, 1408)
    tm = M if out_rows else _pick(M, 1024 if tn <= 1024 else 512)
    if by_rows and not tb:
        tk = K
    elif by_cols and tb:
        tk = K if span_b else (view.per_chip if view.per_chip <= 2048 else _pick(view.per_chip, 1024))
    else:
        ab, bb, ob = a.dtype.itemsize, b_itemsize, jnp.dtype(out_dtype).itemsize
        for tm in ((M,) if out_rows else (_pick(M, 1024), tm)):
            for cap in (K, 2048, 1024, 512, 128):
                tk = _pick(K, cap)
                need = (2 * tk * (tm * ab + tn * bb) + (2 * ob + 4 + (4 if tk < K else 0)) * tm * tn)
                if need <= MATMUL_VMEM_BUDGET:
                    break
            if tk >= min(K, 1024):
                break
    nk = K // tk
    dims = (((0 if ta else 1,), (1 if tb else 0,)), ((), ()))

    aliased = out_layer is not None and out_layer[0] is not None

    def body(a_ref, b_ref, *rest):
        o_ref, scratch = (rest[1], rest[2:]) if aliased else (rest[0], rest[1:])
        bt = b_ref[...]
        if by_rows:
            bt = bt.reshape(bt.shape[0] * bt.shape[1], bt.shape[2])
        elif span_b:
            bt = jnp.concatenate([bt[q] for q in range(view.n_chips)], axis=1)
        p = lax.dot_general(a_ref[...].astype(BF16), bt.astype(BF16), dims, preferred_element_type=F32)

        def store(val):
            if span_o:
                w = N // out_chips
                for q in range(out_chips):
                    o_ref[q] = val[:, q * w:(q + 1) * w].astype(out_dtype)
            elif out_rows:
                h = M // out_rows
                for q in range(out_rows):
                    o_ref[q] = val[q * h:(q + 1) * h, :].astype(out_dtype)
            else:
                o_ref[...] = val.astype(out_dtype)

        if nk == 1:
            store(p)
        else:
            acc_ref, = scratch
            k = pl.program_id(2)

            @pl.when(k == 0)
            def _():
                acc_ref[...] = p

            @pl.when(k > 0)
            def _():
                acc_ref[...] += p

            @pl.when(k == nk - 1)
            def _():
                store(acc_ref[...])

    a_spec = pl.BlockSpec((tk, tm), lambda i, j, k: (k, i)) if ta else pl.BlockSpec((tm, tk), lambda i, j, k: (i, k))
    b_tile = (tn, tk) if tb else (tk, tn)
    b_rc = (lambda i, j, k: (j, k)) if tb else (lambda i, j, k: (k, j))
    if view is None:
        b_arr = b
        b_spec = pl.BlockSpec(b_tile, b_rc)
    elif by_rows:
        b_arr = view.arr
        assert b_tile[0] == view.shape[0]
        b_spec = pl.BlockSpec((view.n_chips, None, view.arr.shape[2], b_tile[1]),
                              lambda i, j, k: (0, view.layer, 0, b_rc(i, j, k)[1]))
    elif span_b:
        b_arr = view.arr
        assert b_tile[1] == view.shape[1]
        b_spec = pl.BlockSpec((view.n_chips, None, b_tile[0], view.per_chip),
                              lambda i, j, k: (0, view.layer, b_rc(i, j, k)[0], 0))
    else:
        b_arr = view.arr
        per = view.per_chip // b_tile[1]
        b_spec = pl.BlockSpec((None, None) + b_tile,
                              lambda i, j, k: (b_rc(i, j, k)[1] // per, view.layer, b_rc(i, j, k)[0],
                                               b_rc(i, j, k)[1] % per))
    if span_o:
        o_full, o_blk = (out_chips, M, N // out_chips), (out_chips, tm, N // out_chips)
        o_idx = lambda i, j, k: (0, i, 0)
    elif out_chips:
        per_o = (N // out_chips) // tn
        o_full, o_blk = (out_chips, M, N // out_chips), (None, tm, tn)
        o_idx = lambda i, j, k: (j // per_o, i, j % per_o)
    elif out_rows:
        o_full, o_blk = (out_rows, M // out_rows, N), (out_rows, M // out_rows, tn)
        o_idx = lambda i, j, k: (0, 0, j)
    else:
        o_full, o_blk = (M, N), (tm, tn)
        o_idx = lambda i, j, k: (i, j)
    if out_layer is not None:
        _, layer, n_layers = out_layer
        o_full, o_blk = o_full[:1] + (n_layers,) + o_full[1:], o_blk[:1] + (None,) + o_blk[1:]
        o_idx = functools.partial(lambda f, i, j, k: (f(i, j, k)[0], layer) + f(i, j, k)[1:], o_idx)
    out_shape = jax.ShapeDtypeStruct(o_full, out_dtype)
    out_spec = pl.BlockSpec(o_blk, o_idx)
    grid = (M // tm, N // tn, nk)
    specs = [a_spec, b_spec, out_spec]
    a_bytes, b_bytes = M * K * a.dtype.itemsize, K * N * b_itemsize
    if nk == 1 and b_bytes + a_bytes * grid[1] < a_bytes + b_bytes * grid[0]:
        grid = (grid[1], grid[0], nk)
        specs = [pl.BlockSpec(s.block_shape, functools.partial(lambda f, j, i, k: f(i, j, k), s.index_map))
                 for s in specs]
    extra = (out_layer[0],) if aliased else ()
    return pl.pallas_call(
        body, name=name, out_shape=out_shape, grid=grid,
        in_specs=specs[:2] + [pl.BlockSpec(memory_space=pl.ANY)] * len(extra), out_specs=specs[2],
        input_output_aliases={2: 0} if aliased else {},
        scratch_shapes=[pltpu.VMEM((tm, tn), F32)] if nk > 1 else [],
        compiler_params=_cparams("parallel", "parallel", "arbitrary"),
    )(a, b_arr, *extra)


def _row_block(S, D, cap_bytes=2 * 1024 * 1024):
    ts = max(8, min(S, cap_bytes // (4 * D)))
    return _pick(S, ts, 8)


def _rms_fwd(x, g, *, out_dtype, name):
    S, D = x.shape
    ts = _row_block(S, D)

    def body(x_ref, g_ref, o_ref):
        xf = x_ref[...]
        r = lax.rsqrt(jnp.mean(xf * xf, axis=-1, keepdims=True) + RMS_EPS)
        o_ref[...] = (xf * r * g_ref[...]).astype(out_dtype)

    return pl.pallas_call(
        body, name=name, out_shape=jax.ShapeDtypeStruct((S, D), out_dtype), grid=(S // ts,),
        in_specs=[pl.BlockSpec((ts, D), lambda i: (i, 0)), pl.BlockSpec((1, D), lambda i: (0, 0))],
        out_specs=pl.BlockSpec((ts, D), lambda i: (i, 0)),
        compiler_params=_cparams("parallel"),
    )(x, g)


def _add_rms(x, r, g, g_next=None, *, name):
    S, D = x.shape
    ts = _row_block(S, D)
    fused = g_next is not None

    def body(*refs):
        x_ref, r_ref, g_ref = refs[:3]
        rf = r_ref[...].astype(F32)
        s = lax.rsqrt(jnp.mean(rf * rf, axis=-1, keepdims=True) + RMS_EPS)
        y = x_ref[...] + rf * s * g_ref[...]
        if fused:
            gn_ref, o_ref, h_ref = refs[3:]
            t = lax.rsqrt(jnp.mean(y * y, axis=-1, keepdims=True) + RMS_EPS)
            h_ref[...] = (y * t * gn_ref[...]).astype(BF16)
        else:
            o_ref, = refs[3:]
        o_ref[...] = y

    row = pl.BlockSpec((ts, D), lambda i: (i, 0))
    vec = pl.BlockSpec((1, D), lambda i: (0, 0))
    out = pl.pallas_call(
        body, name=name,
        out_shape=(jax.ShapeDtypeStruct((S, D), F32),) + ((jax.ShapeDtypeStruct((S, D), BF16),) if fused else ()),
        grid=(S // ts,),
        in_specs=[row, row, vec] + ([vec] if fused else []),
        out_specs=(row,) + ((row,) if fused else ()),
        compiler_params=_cparams("parallel"),
    )(*((x, r, g) + ((g_next,) if fused else ())))
    return out if fused else out[0]


def _rms_bwd(x, g, dy, res=None, *, out_dtype, name):
    S, D = x.shape
    ts = _row_block(S, D)
    has_res = res is not None

    def body(*refs):
        if has_res:
            x_ref, g_ref, dy_ref, res_ref, dx_ref, dg_ref = refs
        else:
            x_ref, g_ref, dy_ref, dx_ref, dg_ref = refs
        i = pl.program_id(0)
        xf = x_ref[...].astype(F32)
        dyf = dy_ref[...].astype(F32)
        r = lax.rsqrt(jnp.mean(xf * xf, axis=-1, keepdims=True) + RMS_EPS)
        n = xf * r
        dn = dyf * g_ref[...]
        dx = r * (dn - n * jnp.mean(dn * n, axis=-1, keepdims=True))
        if has_res:
            dx = dx + res_ref[...]
        dx_ref[...] = dx.astype(out_dtype)
        part = jnp.sum((dyf * n).reshape(ts // 8, 8, D), axis=0)

        @pl.when(i == 0)
        def _():
            dg_ref[...] = part

        @pl.when(i > 0)
        def _():
            dg_ref[...] += part

    row = pl.BlockSpec((ts, D), lambda i: (i, 0))
    in_specs = [row, pl.BlockSpec((1, D), lambda i: (0, 0)), row] + ([row] if has_res else [])
    args = (x, g, dy) + ((res,) if has_res else ())
    dx, dg = pl.pallas_call(
        body, name=name,
        out_shape=(jax.ShapeDtypeStruct((S, D), out_dtype), jax.ShapeDtypeStruct((8, D), F32)),
        grid=(S // ts,), in_specs=in_specs,
        out_specs=(row, pl.BlockSpec((8, D), lambda i: (0, 0))),
        compiler_params=_cparams("arbitrary"),
    )(*args)
    return dx, dg


def _loss_head(y, t, *, name):
    S, D = y.shape
    ts = _row_block(S, D)

    def body(y_ref, t_ref, dy_ref, sq_ref):
        i = pl.program_id(0)
        e = y_ref[...] - t_ref[...]
        dy_ref[...] = e / D
        part = jnp.sum((e * e).reshape(ts // 8, 8, D), axis=0)

        @pl.when(i == 0)
        def _():
            sq_ref[...] = part

        @pl.when(i > 0)
        def _():
            sq_ref[...] += part

    row = pl.BlockSpec((ts, D), lambda i: (i, 0))
    return pl.pallas_call(
        body, name=name,
        out_shape=(jax.ShapeDtypeStruct((S, D), F32), jax.ShapeDtypeStruct((8, D), F32)),
        grid=(S // ts,), in_specs=[row, row],
        out_specs=(row, pl.BlockSpec((8, D), lambda i: (0, 0))),
        compiler_params=_cparams("arbitrary"),
    )(y, t)


def _window_counts(i, T, w):
    t = i * T + lax.broadcasted_iota(jnp.int32, (T, 1), 0)
    return jnp.minimum(t + 1, w).astype(F32)


def _pool_fwd(zu, pool_w, pool_scale, *, name):
    S, W = zu.shape
    T = _pick(S, 1024, 8)

    def body(u_ref, pw_ref, sc_ref, pooled_ref, mixed_ref, halo_ref):
        i = pl.program_id(0)

        @pl.when(i == 0)
        def _():
            halo_ref[...] = jnp.zeros_like(halo_ref)

        u = u_ref[...]
        ext = jnp.concatenate([halo_ref[...], u], axis=0)
        halo_ref[...] = u[T - HALO:, :]
        for g, w in enumerate(POOL_WINDOWS):
            cols = slice(g * POOL_GROUP, (g + 1) * POOL_GROUP)
            s = ext[:, cols]
            sh = 1
            while sh < w:
                s = s + pltpu.roll(s, sh, 0)
                sh *= 2
            pooled = s[HALO:, :] / _window_counts(i, T, w) - u[:, cols]
            pooled_bf = pooled.astype(BF16)
            pm = jnp.dot(pooled_bf, pw_ref[g].astype(BF16), preferred_element_type=F32)
            pooled_ref[:, cols] = pooled_bf
            mixed_ref[:, cols] = (pm * sc_ref[:, cols]).astype(BF16)

    row = pl.BlockSpec((T, W), lambda i: (i, 0))
    return pl.pallas_call(
        body, name=name,
        out_shape=(jax.ShapeDtypeStruct((S, W), BF16), jax.ShapeDtypeStruct((S, W), BF16)),
        grid=(S // T,),
        in_specs=[row, pl.BlockSpec(pool_w.shape, lambda i: (0, 0, 0)), pl.BlockSpec((1, W), lambda i: (0, 0))],
        out_specs=(row, row),
        scratch_shapes=[pltpu.VMEM((HALO, W), F32)],
        compiler_params=_cparams("arbitrary"),
    )(zu, pool_w, pool_scale)


def _pool_bwd(pooled, dmixed, pool_w, pool_scale, *, name):
    S, W = pooled.shape
    T = _pick(S, 1024, 8)
    nb = S // T

    def body(p_ref, dm_ref, pw_ref, sc_ref, dzu_ref, dpw_ref, dsc_ref, halo_ref):
        i = pl.program_id(0)
        blk = nb - 1 - i

        @pl.when(i == 0)
        def _():
            halo_ref[...] = jnp.zeros_like(halo_ref)
            dpw_ref[...] = jnp.zeros_like(dpw_ref)
            dsc_ref[...] = jnp.zeros_like(dsc_ref)

        for g, w in enumerate(POOL_WINDOWS):
            cols = slice(g * POOL_GROUP, (g + 1) * POOL_GROUP)
            p = p_ref[:, cols]
            dm = dm_ref[:, cols]
            pw = pw_ref[g].astype(BF16)
            pm = jnp.dot(p, pw, preferred_element_type=F32)
            dsc_ref[:, cols] += jnp.sum((dm * pm).reshape(T // 8, 8, POOL_GROUP), axis=0)
            dpm = (dm * sc_ref[:, cols]).astype(BF16)
            dpw_ref[g] += lax.dot_general(p, dpm, (((0,), (0,)), ((), ())), preferred_element_type=F32)
            dpooled = lax.dot_general(dpm, pw, (((1,), (1,)), ((), ())), preferred_element_type=F32)
            e = dpooled / _window_counts(blk, T, w)
            ext = jnp.concatenate([e, halo_ref[:, cols]], axis=0)
            halo_ref[:, cols] = e[:HALO, :]
            s = ext
            sh = 1
            while sh < w:
                s = s + pltpu.roll(s, T + HALO - sh, 0)
                sh *= 2
            dzu_ref[:, cols] = (s[:T, :] - dpooled).astype(BF16)

    row = pl.BlockSpec((T, W), lambda i: (nb - 1 - i, 0))
    return pl.pallas_call(
        body, name=name,
        out_shape=(jax.ShapeDtypeStruct((S, W), BF16), jax.ShapeDtypeStruct(pool_w.shape, F32),
                   jax.ShapeDtypeStruct((8, W), F32)),
        grid=(nb,),
        in_specs=[row, row, pl.BlockSpec(pool_w.shape, lambda i: (0, 0, 0)), pl.BlockSpec((1, W), lambda i: (0, 0))],
        out_specs=(row, pl.BlockSpec(pool_w.shape, lambda i: (0, 0, 0)), pl.BlockSpec((8, W), lambda i: (0, 0))),
        scratch_shapes=[pltpu.VMEM((HALO, W), F32)],
        compiler_params=_cparams("arbitrary"),
    )(pooled, dmixed, pool_w, pool_scale)


def _fgate_fwd(zf, bf, *, name):
    S, W = zf.shape
    T = _pick(S, 512, 8)

    def body(z_ref, b_ref, aq_ref, ak_ref, carry_ref):
        i = pl.program_id(0)

        @pl.when(i == 0)
        def _():
            carry_ref[...] = jnp.zeros_like(carry_ref)

        a = z_ref[...] + b_ref[...]
        s = jnp.minimum(a, 0.0) - jnp.log(1.0 + jnp.exp(-jnp.abs(a)))
        row = lax.broadcasted_iota(jnp.int32, (T, W), 0)
        sh = 1
        while sh < T:
            s = s + jnp.where(row >= sh, pltpu.roll(s, sh, 0), 0.0)
            sh *= 2
        c = s + carry_ref[0:1, :]
        carry_ref[...] = jnp.broadcast_to(c[T - 1:T, :], carry_ref.shape)
        lane = lax.broadcasted_iota(jnp.int32, (T, W), 1)
        for h in range(FOX_HEADS):
            ch = c[:, h:h + 1]
            hi = ch.astype(BF16).astype(F32)
            r1 = ch - hi
            lo = r1.astype(BF16).astype(F32)
            lo2 = (r1 - lo).astype(BF16).astype(F32)
            aq = jnp.where(lane == 0, hi, jnp.where(lane == 1, lo, jnp.where(lane == 2, lo2,
                                                                              jnp.where(lane < 6, 1.0, 0.0))))
            ak = jnp.where(lane < 3, 1.0, jnp.where(lane == 3, -hi, jnp.where(lane == 4, -lo,
                                                                               jnp.where(lane == 5, -lo2, 0.0))))
            aq_ref[h] = aq.astype(BF16)
            ak_ref[h] = ak.astype(BF16)

    aug = jax.ShapeDtypeStruct((FOX_HEADS, S, W), BF16)
    aug_spec = pl.BlockSpec((FOX_HEADS, T, W), lambda i: (0, i, 0))
    return pl.pallas_call(
        body, name=name, out_shape=(aug, aug), grid=(S // T,),
        in_specs=[pl.BlockSpec((T, W), lambda i: (i, 0)), pl.BlockSpec((1, W), lambda i: (0, 0))],
        out_specs=(aug_spec, aug_spec),
        scratch_shapes=[pltpu.VMEM((8, W), F32)],
        compiler_params=_cparams("arbitrary"),
    )(zf, bf)


def _fgate_bwd(zf, bf, dc, *, name):
    S, W = zf.shape
    T = _pick(S, 512, 8)
    nb = S // T

    def body(z_ref, b_ref, dc_ref, dz_ref, db_ref, carry_ref):
        i = pl.program_id(0)

        @pl.when(i == 0)
        def _():
            carry_ref[...] = jnp.zeros_like(carry_ref)
            db_ref[...] = jnp.zeros_like(db_ref)

        s = dc_ref[...]
        row = lax.broadcasted_iota(jnp.int32, (T, W), 0)
        sh = 1
        while sh < T:
            s = s + jnp.where(row < T - sh, pltpu.roll(s, T - sh, 0), 0.0)
            sh *= 2
        dlf = s + carry_ref[0:1, :]
        carry_ref[...] = jnp.broadcast_to(dlf[0:1, :], carry_ref.shape)
        dz = dlf * (1.0 - _sigmoid(z_ref[...] + b_ref[...]))
        dz_ref[...] = dz.astype(BF16)
        db_ref[...] += jnp.sum(dz.reshape(T // 8, 8, W), axis=0)

    row_spec = pl.BlockSpec((T, W), lambda i: (nb - 1 - i, 0))
    return pl.pallas_call(
        body, name=name,
        out_shape=(jax.ShapeDtypeStruct((S, W), BF16), jax.ShapeDtypeStruct((8, W), F32)),
        grid=(nb,),
        in_specs=[row_spec, pl.BlockSpec((1, W), lambda i: (0, 0)), row_spec],
        out_specs=(row_spec, pl.BlockSpec((8, W), lambda i: (0, 0))),
        scratch_shapes=[pltpu.VMEM((8, W), F32)],
        compiler_params=_cparams("arbitrary"),
    )(zf, bf, dc)


def _merge_fwd(zg, yp, yf, *, name):
    S, D = yp.shape
    ts = _row_block(S, D, 1024 * 1024)

    def body(zg_ref, yp_ref, yf_ref, o_ref):
        zg = zg_ref[...].astype(F32)
        o_ref[...] = (_sigmoid(zg[:, :D]) * yp_ref[...] + _sigmoid(zg[:, D:]) * yf_ref[...]).astype(BF16)

    row = pl.BlockSpec((ts, D), lambda i: (i, 0))
    return pl.pallas_call(
        body, name=name, out_shape=jax.ShapeDtypeStruct((S, D), BF16), grid=(S // ts,),
        in_specs=[pl.BlockSpec((ts, 2 * D), lambda i: (i, 0)), row, row], out_specs=row,
        compiler_params=_cparams("parallel"),
    )(zg, yp, yf)


def _merge_bwd(zg, yp, yf, dmerged, *, name):
    S, D = yp.shape
    ts = _row_block(S, D, 1024 * 1024)

    def body(zg_ref, yp_ref, yf_ref, dm_ref, dyp_ref, dyf_ref, dzg_ref):
        dm = dm_ref[...].astype(F32)
        zg = zg_ref[...].astype(F32)
        sp = _sigmoid(zg[:, :D])
        sf = _sigmoid(zg[:, D:])
        dyp_ref[...] = (dm * sp).astype(BF16)
        dyf_ref[...] = (dm * sf).astype(BF16)
        dzg_ref[:, :D] = (dm * yp_ref[...] * (sp * (1.0 - sp))).astype(BF16)
        dzg_ref[:, D:] = (dm * yf_ref[...] * (sf * (1.0 - sf))).astype(BF16)

    row = pl.BlockSpec((ts, D), lambda i: (i, 0))
    wide = pl.BlockSpec((ts, 2 * D), lambda i: (i, 0))
    return pl.pallas_call(
        body, name=name,
        out_shape=(jax.ShapeDtypeStruct((S, D), BF16), jax.ShapeDtypeStruct((S, D), BF16),
                   jax.ShapeDtypeStruct((S, 2 * D), BF16)),
        grid=(S // ts,), in_specs=[wide, row, row, row], out_specs=(row, row, wide),
        compiler_params=_cparams("parallel"),
    )(zg, yp, yf, dmerged)


NEG_BIG = -1e30


FOX_BLOCK = 1024
PAIR = LANES // FOX_DH
N_PAIRS = FOX_HEADS // PAIR


def _fox_fwd(zqkv, augq, augk, *, name):
    S = zqkv.shape[0]
    bq = _pick(S, FOX_BLOCK, 128)
    nq = S // bq
    scale = 1.0 / math.sqrt(FOX_DH)

    def body(q_ref, k_ref, v_ref, aq_ref, ak_ref, o_ref, lse_ref):
        i = pl.program_id(1)
        lane = lax.broadcasted_iota(jnp.int32, (1, LANES), 1)
        first = lane < FOX_DH
        q2 = q_ref[...] * scale
        zero = jnp.zeros_like(q2)
        qh = (jnp.concatenate([jnp.where(first, q2, zero), aq_ref[0]], axis=1),
              jnp.concatenate([jnp.where(first, zero, q2), aq_ref[1]], axis=1))

        def step(j, carry, masked):
            start = pl.multiple_of(j * bq, bq)
            kb = k_ref[pl.ds(start, bq), :]
            vb = v_ref[pl.ds(start, bq), :]
            one = jnp.ones_like(vb)
            vh = (jnp.where(first, vb, one), jnp.where(first, one, vb))
            out = []
            for h in range(PAIR):
                m, acc = carry[h]
                kh = jnp.concatenate([kb, ak_ref[h, pl.ds(start, bq), :]], axis=1)
                s = lax.dot_general(qh[h], kh, (((1,), (1,)), ((), ())), preferred_element_type=F32)
                if masked:
                    r = lax.broadcasted_iota(jnp.int32, (bq, bq), 0)
                    c = lax.broadcasted_iota(jnp.int32, (bq, bq), 1)
                    s = jnp.where(c <= r, s, NEG_BIG)
                m_new = jnp.maximum(m, jnp.max(s, axis=-1, keepdims=True))
                alpha = jnp.exp(m - m_new)
                p = jnp.exp(s - m_new).astype(BF16)
                acc = alpha * acc + jnp.dot(p, vh[h], preferred_element_type=F32)
                out.append((m_new, acc))
            return tuple(out)

        init = tuple((jnp.full((bq, 1), NEG_BIG, F32), jnp.zeros((bq, LANES), F32)) for _ in range(PAIR))
        carry = lax.fori_loop(0, i, lambda j, c: step(j, c, False), init)
        (ma, acca), (mb, accb) = step(i, carry, True)
        num = jnp.where(first, acca, accb)
        den = jnp.where(first, pltpu.roll(acca, FOX_DH, 1), pltpu.roll(accb, FOX_DH, 1))
        o_ref[...] = num / den
        lse_t = (jnp.where(first, mb, ma) + jnp.log(jnp.where(first, accb, acca))).T
        lse_ref[0] = lse_t[FOX_DH:FOX_DH + 1, :]
        lse_ref[1] = lse_t[0:1, :]

    npair = N_PAIRS
    return pl.pallas_call(
        body, name=name,
        out_shape=(jax.ShapeDtypeStruct((S, FOX_W), F32), jax.ShapeDtypeStruct((FOX_HEADS, 1, S), F32)),
        grid=(npair, nq),
        in_specs=[pl.BlockSpec((bq, LANES), lambda hp, i: (i, hp)),
                  pl.BlockSpec((S, LANES), lambda hp, i: (0, npair + hp)),
                  pl.BlockSpec((S, LANES), lambda hp, i: (0, 2 * npair + hp)),
                  pl.BlockSpec((PAIR, bq, LANES), lambda hp, i: (hp, i, 0)),
                  pl.BlockSpec((PAIR, S, LANES), lambda hp, i: (hp, 0, 0))],
        out_specs=(pl.BlockSpec((bq, LANES), lambda hp, i: (i, hp)),
                   pl.BlockSpec((PAIR, 1, bq), lambda hp, i: (hp, 0, i))),
        compiler_params=_cparams("parallel", "arbitrary"),
    )(zqkv, zqkv, zqkv, augq, augk)


def _head_rowsum(a, b, *, name):
    S, W = a.shape
    ts = _pick(S, 1024, 8)

    def body(a_ref, b_ref, o_ref):
        prod = a_ref[...].astype(F32) * b_ref[...].astype(F32)
        hi = prod.astype(BF16)
        lo = (prod - hi.astype(F32)).astype(BF16)
        r = lax.broadcasted_iota(jnp.int32, (W, LANES), 0)
        c = lax.broadcasted_iota(jnp.int32, (W, LANES), 1)
        sel = jnp.where(r // FOX_DH == c, 1.0, 0.0).astype(BF16)
        d_t = (jnp.dot(hi, sel, preferred_element_type=F32) + jnp.dot(lo, sel, preferred_element_type=F32)).T
        for h in range(FOX_HEADS):
            o_ref[h] = d_t[h:h + 1, :]

    return pl.pallas_call(
        body, name=name, out_shape=jax.ShapeDtypeStruct((FOX_HEADS, 1, S), F32), grid=(S // ts,),
        in_specs=[pl.BlockSpec((ts, W), lambda i: (i, 0)), pl.BlockSpec((ts, W), lambda i: (i, 0))],
        out_specs=pl.BlockSpec((FOX_HEADS, 1, ts), lambda i: (0, 0, i)),
        compiler_params=_cparams("parallel"),
    )(a, b)


def _fox_bwd(zqkv, do, augq, augk, lse_row, delta_row, *, name):
    S = zqkv.shape[0]
    bk = _pick(S, FOX_BLOCK, 128)
    nk = S // bk
    scale = 1.0 / math.sqrt(FOX_DH)
    npair = N_PAIRS

    def body(q_ref, k_ref, v_ref, do_ref, ak_ref, aq_ref, lse_ref, dl_ref, dq_ref, dk_ref, dv_ref, dck_ref, dcq_ref):
        j = pl.program_id(1)

        @pl.when(j == 0)
        def _():
            dq_ref[...] = jnp.zeros_like(dq_ref)
            dcq_ref[...] = jnp.zeros_like(dcq_ref)

        lane = lax.broadcasted_iota(jnp.int32, (1, LANES), 1)
        first = lane < FOX_DH
        kb = k_ref[...]
        vb = v_ref[...]
        kh = (jnp.concatenate([kb, ak_ref[0]], axis=1), jnp.concatenate([kb, ak_ref[1]], axis=1))

        def step(i, carry, masked):
            start = pl.multiple_of(i * bk, bk)
            qs = q_ref[pl.ds(start, bk), :] * scale
            dob = do_ref[pl.ds(start, bk), :]
            zero = jnp.zeros_like(qs)
            qh = (jnp.where(first, qs, zero), jnp.where(first, zero, qs))
            doh = (jnp.where(first, dob, zero), jnp.where(first, zero, dob))
            out = []
            dqc = []
            for h in range(PAIR):
                dk, dv, dc = carry[h]
                qaug = jnp.concatenate([qh[h], aq_ref[h, pl.ds(start, bk), :]], axis=1)
                lse = lse_ref[h, :, pl.ds(start, bk)]
                dl = dl_ref[h, :, pl.ds(start, bk)]
                st = lax.dot_general(kh[h], qaug, (((1,), (1,)), ((), ())), preferred_element_type=F32)
                pt = jnp.exp(st - lse)
                if masked:
                    r = lax.broadcasted_iota(jnp.int32, (bk, bk), 0)
                    c = lax.broadcasted_iota(jnp.int32, (bk, bk), 1)
                    pt = jnp.where(c >= r, pt, 0.0)
                dpt = lax.dot_general(vb, doh[h], (((1,), (1,)), ((), ())), preferred_element_type=F32)
                dst = pt * (dpt - dl)
                pt_bf = pt.astype(BF16)
                dst_bf = dst.astype(BF16)
                dv = dv + jnp.dot(pt_bf, dob, preferred_element_type=F32)
                dk = dk + jnp.dot(dst_bf, qs, preferred_element_type=F32)
                dc = dc - jnp.sum(dst, axis=-1, keepdims=True)
                dcq_ref[h, :, pl.ds(start, bk)] += jnp.sum(dst, axis=0, keepdims=True)
                dqc.append(lax.dot_general(dst_bf, kb, (((0,), (0,)), ((), ())), preferred_element_type=F32))
                out.append((dk, dv, dc))
            dq_ref[pl.ds(start, bk), :] += jnp.where(first, dqc[0], dqc[1])
            return tuple(out)

        init = tuple((jnp.zeros((bk, LANES), F32), jnp.zeros((bk, LANES), F32), jnp.zeros((bk, 1), F32))
                     for _ in range(PAIR))
        carry = step(j, init, True)
        (dka, dva, dca), (dkb, dvb, dcb) = lax.fori_loop(j + 1, nk, lambda i, c: step(i, c, False), carry)
        dk_ref[...] = jnp.where(first, dka, dkb).astype(BF16)
        dv_ref[...] = jnp.where(first, dva, dvb).astype(BF16)
        dck_ref[0] = jnp.broadcast_to(dca, (bk, LANES)).T[0:1, :]
        dck_ref[1] = jnp.broadcast_to(dcb, (bk, LANES)).T[0:1, :]

        @pl.when(j == nk - 1)
        def _():
            dq_ref[...] = dq_ref[...] * scale

    rowfull = pl.BlockSpec((PAIR, 1, S), lambda hp, j: (hp, 0, 0))
    return pl.pallas_call(
        body, name=name,
        out_shape=(jax.ShapeDtypeStruct((S, FOX_W), F32), jax.ShapeDtypeStruct((S, FOX_W), BF16),
                   jax.ShapeDtypeStruct((S, FOX_W), BF16), jax.ShapeDtypeStruct((FOX_HEADS, 1, S), F32),
                   jax.ShapeDtypeStruct((FOX_HEADS, 1, S), F32)),
        grid=(npair, nk),
        in_specs=[pl.BlockSpec((S, LANES), lambda hp, j: (0, hp)),
                  pl.BlockSpec((bk, LANES), lambda hp, j: (j, npair + hp)),
                  pl.BlockSpec((bk, LANES), lambda hp, j: (j, 2 * npair + hp)),
                  pl.BlockSpec((S, LANES), lambda hp, j: (0, hp)),
                  pl.BlockSpec((PAIR, bk, LANES), lambda hp, j: (hp, j, 0)),
                  pl.BlockSpec((PAIR, S, LANES), lambda hp, j: (hp, 0, 0)), rowfull, rowfull],
        out_specs=(pl.BlockSpec((S, LANES), lambda hp, j: (0, hp)),
                   pl.BlockSpec((bk, LANES), lambda hp, j: (j, hp)),
                   pl.BlockSpec((bk, LANES), lambda hp, j: (j, hp)),
                   pl.BlockSpec((PAIR, 1, bk), lambda hp, j: (hp, 0, j)), rowfull),
        compiler_params=_cparams("parallel", "arbitrary"),
    )(zqkv, zqkv, zqkv, do, augk, augq, lse_row, delta_row)


def _xattn_fwd(q, kv, *, name):
    S, W = q.shape
    M = kv.shape[0]
    tq = _pick(S, 512, 8)
    scale = 1.0 / math.sqrt(X_DH)

    def body(q_ref, kv_ref, o_ref):
        for h in range(X_HEADS):
            cols = slice(h * X_DH, (h + 1) * X_DH)
            vcols = slice(W + h * X_DH, W + (h + 1) * X_DH)
            s = lax.dot_general(q_ref[:, cols], kv_ref[:, cols], (((1,), (1,)), ((), ())),
                                preferred_element_type=F32) * scale
            e = jnp.exp(s - jnp.max(s, axis=-1, keepdims=True))
            p = e / jnp.sum(e, axis=-1, keepdims=True)
            o_ref[:, cols] = jnp.dot(p.astype(BF16), kv_ref[:, vcols], preferred_element_type=F32).astype(BF16)

    return pl.pallas_call(
        body, name=name, out_shape=jax.ShapeDtypeStruct((S, W), BF16), grid=(S // tq,),
        in_specs=[pl.BlockSpec((tq, W), lambda i: (i, 0)), pl.BlockSpec((M, 2 * W), lambda i: (0, 0))],
        out_specs=pl.BlockSpec((tq, W), lambda i: (i, 0)),
        compiler_params=_cparams("parallel"),
    )(q, kv)


def _xattn_bwd(q, kv, do, *, name):
    S, W = q.shape
    M = kv.shape[0]
    tq = _pick(S, 512, 8)
    scale = 1.0 / math.sqrt(X_DH)

    def body(q_ref, kv_ref, do_ref, dq_ref, dkv_ref):
        i = pl.program_id(0)

        @pl.when(i == 0)
        def _():
            dkv_ref[...] = jnp.zeros_like(dkv_ref)

        for h in range(X_HEADS):
            cols = slice(h * X_DH, (h + 1) * X_DH)
            vcols = slice(W + h * X_DH, W + (h + 1) * X_DH)
            qh = q_ref[:, cols]
            kh = kv_ref[:, cols]
            vh = kv_ref[:, vcols]
            doh = do_ref[:, cols]
            s = lax.dot_general(qh, kh, (((1,), (1,)), ((), ())), preferred_element_type=F32) * scale
            e = jnp.exp(s - jnp.max(s, axis=-1, keepdims=True))
            p = e / jnp.sum(e, axis=-1, keepdims=True)
            dp = lax.dot_general(doh, vh, (((1,), (1,)), ((), ())), preferred_element_type=F32)
            ds = (p * (dp - jnp.sum(p * dp, axis=-1, keepdims=True)) * scale).astype(BF16)
            dq_ref[:, cols] = jnp.dot(ds, kh, preferred_element_type=F32).astype(BF16)
            dkv_ref[:, cols] += lax.dot_general(ds, qh, (((0,), (0,)), ((), ())), preferred_element_type=F32)
            dkv_ref[:, vcols] += lax.dot_general(p.astype(BF16), doh, (((0,), (0,)), ((), ())),
                                                 preferred_element_type=F32)

    return pl.pallas_call(
        body, name=name,
        out_shape=(jax.ShapeDtypeStruct((S, W), BF16), jax.ShapeDtypeStruct((M, 2 * W), F32)),
        grid=(S // tq,),
        in_specs=[pl.BlockSpec((tq, W), lambda i: (i, 0)), pl.BlockSpec((M, 2 * W), lambda i: (0, 0)),
                  pl.BlockSpec((tq, W), lambda i: (i, 0))],
        out_specs=(pl.BlockSpec((tq, W), lambda i: (i, 0)), pl.BlockSpec((M, 2 * W), lambda i: (0, 0))),
        compiler_params=_cparams("arbitrary"),
    )(q, kv, do)


GELU_C = math.sqrt(2.0 / math.pi)
GELU_A = 0.044715
CONV_HALO = 16


def _gelu_parts(x):
    u = GELU_C * (x + GELU_A * x * x * x)
    t = jnp.tanh(u)
    g = 0.5 * x * (1.0 + t)
    dg = 0.5 * (1.0 + t) + 0.5 * x * (1.0 - t * t) * (GELU_C * (1.0 + 3.0 * GELU_A * x * x))
    return g, dg


def _conv3(ext, w_ref, b_ref):
    return (w_ref[2:3, :] * ext + w_ref[1:2, :] * pltpu.roll(ext, 1, 0)
            + w_ref[0:1, :] * pltpu.roll(ext, 2, 0) + b_ref[...])


def _convglu_fwd(z, conv_w, conv_b, *, name):
    S, F2 = z.shape
    F = F2 // 2
    tc = _pick(F, 1408)
    ncol = F // tc
    T = _pick(S, 512, 8)
    hb = T // CONV_HALO

    def body(zg_ref, zu_ref, zgp_ref, zup_ref, wg_ref, wu_ref, bg_ref, bu_ref, act_ref):
        i = pl.program_id(1)
        first = (i > 0).astype(F32)

        def conv(z_ref, zp_ref, w_ref, b_ref):
            ext = jnp.concatenate([zp_ref[...].astype(F32) * first, z_ref[...].astype(F32)], axis=0)
            return _conv3(ext, w_ref, b_ref)[CONV_HALO:, :]

        gc = conv(zg_ref, zgp_ref, wg_ref, bg_ref)
        uc = conv(zu_ref, zup_ref, wu_ref, bu_ref)
        act_ref[...] = (_gelu_parts(gc)[0] * uc).astype(BF16)

    cur = lambda off: pl.BlockSpec((T, tc), lambda j, i: (i, j + off))
    prev = lambda off: pl.BlockSpec((CONV_HALO, tc), lambda j, i: (jnp.maximum(i * hb - 1, 0), j + off))
    vec = lambda rows, off: pl.BlockSpec((rows, tc), lambda j, i: (0, j + off))
    return pl.pallas_call(
        body, name=name, out_shape=jax.ShapeDtypeStruct((S, F), BF16), grid=(ncol, S // T),
        in_specs=[cur(0), cur(ncol), prev(0), prev(ncol), vec(3, 0), vec(3, ncol), vec(1, 0), vec(1, ncol)],
        out_specs=pl.BlockSpec((T, tc), lambda j, i: (i, j)),
        compiler_params=_cparams("parallel", "parallel"),
    )(z, z, z, z, conv_w, conv_w, conv_b, conv_b)


def _convglu_bwd(z, dact, conv_w, conv_b, *, name):
    S, F2 = z.shape
    F = F2 // 2
    tc = _pick(F, 1408)
    ncol = F // tc
    T = _pick(S, 256, 8)
    nrow = S // T
    hb = T // CONV_HALO
    TE = T + CONV_HALO

    def body(z_ref, zp_ref, zn_ref, da_ref, dan_ref, w_ref, b_ref, dz_ref, dw_ref):
        i = pl.program_id(0)
        first = (i > 0).astype(F32)
        last = (i < nrow - 1).astype(F32)

        @pl.when(i == 0)
        def _():
            dw_ref[...] = jnp.zeros_like(dw_ref)

        def ext_of(cols):
            return jnp.concatenate([zp_ref[:, cols].astype(F32) * first, z_ref[:, cols].astype(F32),
                                    zn_ref[:, cols].astype(F32)], axis=0)

        def back(d, ext, w, cols):
            dz = w[2:3, :] * d + w[1:2, :] * pltpu.roll(d, TE - 1, 0) + w[0:1, :] * pltpu.roll(d, TE - 2, 0)
            dz_ref[:, cols] = dz[:T, :].astype(BF16)
            dc = d[:T, :]
            z0 = ext[CONV_HALO:CONV_HALO + T, :]
            z1 = pltpu.roll(ext, 1, 0)[CONV_HALO:CONV_HALO + T, :]
            z2 = pltpu.roll(ext, 2, 0)[CONV_HALO:CONV_HALO + T, :]
            rows = [jnp.sum(dc * z2, axis=0, keepdims=True), jnp.sum(dc * z1, axis=0, keepdims=True),
                    jnp.sum(dc * z0, axis=0, keepdims=True), jnp.sum(dc, axis=0, keepdims=True)]
            dw_ref[0:4, cols] += jnp.concatenate(rows, axis=0)

        for jj in range(ncol):
            cg = slice(jj * tc, (jj + 1) * tc)
            cu = slice(F + jj * tc, F + (jj + 1) * tc)
            extg, extu = ext_of(cg), ext_of(cu)
            wg, wu = w_ref[:, cg], w_ref[:, cu]
            gc = _conv3(extg, wg, b_ref[:, cg])[CONV_HALO:, :]
            uc = _conv3(extu, wu, b_ref[:, cu])[CONV_HALO:, :]
            da = jnp.concatenate([da_ref[:, cg].astype(F32), dan_ref[:, cg].astype(F32) * last], axis=0)
            gl, dgl = _gelu_parts(gc)
            back(da * uc * dgl, extg, wg, cg)
            back(da * gl, extu, wu, cu)

    halo_rows = S // CONV_HALO
    return pl.pallas_call(
        body, name=name,
        out_shape=(jax.ShapeDtypeStruct((S, F2), BF16), jax.ShapeDtypeStruct((8, F2), F32)),
        grid=(nrow,),
        in_specs=[pl.BlockSpec((T, F2), lambda i: (i, 0)),
                  pl.BlockSpec((CONV_HALO, F2), lambda i: (jnp.maximum(i * hb - 1, 0), 0)),
                  pl.BlockSpec((CONV_HALO, F2), lambda i: (jnp.minimum((i + 1) * hb, halo_rows - 1), 0)),
                  pl.BlockSpec((T, F), lambda i: (i, 0)),
                  pl.BlockSpec((CONV_HALO, F), lambda i: (jnp.minimum((i + 1) * hb, halo_rows - 1), 0)),
                  pl.BlockSpec((3, F2), lambda i: (0, 0)), pl.BlockSpec((1, F2), lambda i: (0, 0))],
        out_specs=(pl.BlockSpec((T, F2), lambda i: (i, 0)), pl.BlockSpec((8, F2), lambda i: (0, 0))),
        compiler_params=_cparams("arbitrary"),
    )(z, z, z, dact, dact, conv_w, conv_b)


OFF_QKV = POOL_W
OFF_G = POOL_W + 3 * FOX_W


SHARD_BY_ROWS = {"w_in": False, "w_pool_br": False, "w_fox_br": False, "w_mix_out": True, "w_xq": True,
                 "w_xkv": True, "w_xo": False, "w_up": False, "w_down": True, "conv_w": False}


def _prep_layer_weights(w, l):
    n, _, D, b = w["w_in"].shape
    w_in = w["w_in"][:, l].transpose(1, 0, 2).reshape(D, n * b)
    off_f = OFF_G
    pad = jnp.zeros((D, LANES - N_FGATE), w_in.dtype)
    w_in_r = jnp.concatenate([w_in[:, :off_f], w_in[:, off_f + N_FGATE:], w_in[:, off_f:off_f + N_FGATE], pad], axis=1)
    wl = {k: _ChipMajor(w[k], l, SHARD_BY_ROWS[k]) for k in MATMUL_WEIGHTS if k != "w_in"}
    wl["w_in_r"] = w_in_r
    return wl


def _row(v):
    return v.reshape(1, -1)


def _layer_fwd(x, h1, mem, wl, p, l):
    S, D = x.shape
    n = lambda s: f"l{l}_{s}"
    sv = {"x0": x}
    w_in_r = wl["w_in_r"]
    og = OFF_G
    zu = _matmul(h1, w_in_r[:, :OFF_QKV], out_dtype=F32, name=n("mm_zu"))
    zqkv = _matmul(h1, w_in_r[:, OFF_QKV:og], out_dtype=BF16, name=n("mm_zqkv"))
    zg = _matmul(h1, w_in_r[:, og:og + 2 * D], out_dtype=BF16, name=n("mm_zg"))
    zf = _matmul(h1, w_in_r[:, og + 2 * D:], out_dtype=F32, name=n("mm_zf"))
    bf = jnp.pad(p["b_forget"][l], (0, LANES - N_FGATE)).reshape(1, LANES)
    augq, augk = _fgate_fwd(zf, bf, name=n("fgate"))
    o_fox, lse = _fox_fwd(zqkv, augq, augk, name=n("fox"))
    pooled, mixed = _pool_fwd(zu, p["pool_w"][l], _row(p["pool_scale"][l]), name=n("pool"))
    y_pool = _matmul(mixed, wl["w_pool_br"], out_dtype=BF16, name=n("mm_ypool"))
    y_fox = _matmul(o_fox, wl["w_fox_br"], out_dtype=BF16, name=n("mm_yfox"))
    merged = _merge_fwd(zg, y_pool, y_fox, name=n("merge"))
    r1 = _matmul(merged, wl["w_mix_out"], out_dtype=BF16, name=n("mm_r1"))
    x1, h2 = _add_rms(x, r1, _row(p["mix_post_g"][l]), _row(p["xa_pre_g"][l]), name=n("addrms1"))
    sv.update(h1=h1, zg=zg, zf=zf, bf=bf, zqkv=zqkv, augq=augq, augk=augk, o_fox=o_fox, lse=lse,
              pooled=pooled, mixed=mixed, y_pool=y_pool, y_fox=y_fox, merged=merged, r1=r1, x1=x1)
    mem_n = _rms_fwd(mem, _row(p["mem_g"][l]), out_dtype=BF16, name=n("rms_mem"))
    q2 = _matmul(h2, wl["w_xq"], out_dtype=BF16, name=n("mm_q2"))
    kv = _matmul(mem_n, wl["w_xkv"], out_dtype=BF16, name=n("mm_kv"))
    o2 = _xattn_fwd(q2, kv, name=n("xattn"))
    a2 = _matmul(o2, wl["w_xo"], out_dtype=BF16, name=n("mm_a2"))
    x2, h3 = _add_rms(x1, a2, _row(p["xa_post_g"][l]), _row(p["ffn_pre_g"][l]), name=n("addrms2"))
    sv.update(h2=h2, mem_n=mem_n, q2=q2, kv=kv, o2=o2, a2=a2, x2=x2)
    z3 = _matmul(h3, wl["w_up"], out_dtype=BF16, name=n("mm_z3"))
    act = _convglu_fwd(z3, p["conv_w"][l], _row(p["conv_b"][l]), name=n("convglu"))
    d3 = _matmul(act, wl["w_down"], out_dtype=BF16, name=n("mm_d3"))
    if l + 1 < p["mix_pre_g"].shape[0]:
        x3, h_next = _add_rms(x2, d3, _row(p["ffn_post_g"][l]), _row(p["mix_pre_g"][l + 1]), name=n("addrms3"))
    else:
        x3, h_next = _add_rms(x2, d3, _row(p["ffn_post_g"][l]), name=n("addrms3")), None
    sv.update(h3=h3, z3=z3, act=act, d3=d3)
    return x3, h_next, sv


def _layer_bwd(dx, mem, wl, p, l, sv, gbuf):
    S, D = dx.shape
    n_layers = p["mix_pre_g"].shape[0]
    n = lambda s: f"l{l}_b_{s}"
    g = {}
    red = lambda part: jnp.sum(part, axis=0)

    def wgrad(k, lhs, rhs, nm):
        how = dict(out_rows=N_CHIPS) if SHARD_BY_ROWS[k] else dict(out_chips=N_CHIPS)
        return _matmul(lhs, rhs, ta=True, out_dtype=F32, out_layer=(gbuf.get(k), l, n_layers), name=n(nm), **how)

    dd3, dg = _rms_bwd(sv["d3"], _row(p["ffn_post_g"][l]), dx, out_dtype=BF16, name=n("rms3post"))
    g["ffn_post_g"] = red(dg)
    dact = _matmul(dd3, wl["w_down"], tb=True, out_dtype=BF16, name=n("mm_dact"))
    g["w_down"] = wgrad("w_down", sv["act"], dd3, "mm_dwdown")
    dz3, dconv = _convglu_bwd(sv["z3"], dact, p["conv_w"][l], _row(p["conv_b"][l]), name=n("convglu"))
    g["conv_w"] = dconv[:3]
    g["conv_b"] = dconv[3]
    dh3 = _matmul(dz3, wl["w_up"], tb=True, out_dtype=BF16, name=n("mm_dh3"))
    g["w_up"] = wgrad("w_up", sv["h3"], dz3, "mm_dwup")
    dx, dg = _rms_bwd(sv["x2"], _row(p["ffn_pre_g"][l]), dh3, dx, out_dtype=F32, name=n("rms3pre"))
    g["ffn_pre_g"] = red(dg)
    da2, dg = _rms_bwd(sv["a2"], _row(p["xa_post_g"][l]), dx, out_dtype=BF16, name=n("rms2post"))
    g["xa_post_g"] = red(dg)
    do2 = _matmul(da2, wl["w_xo"], tb=True, out_dtype=BF16, name=n("mm_do2"))
    g["w_xo"] = wgrad("w_xo", sv["o2"], da2, "mm_dwxo")
    dq2, dkv = _xattn_bwd(sv["q2"], sv["kv"], do2, name=n("xattn"))
    dh2 = _matmul(dq2, wl["w_xq"], tb=True, out_dtype=BF16, name=n("mm_dh2"))
    g["w_xq"] = wgrad("w_xq", sv["h2"], dq2, "mm_dwxq")
    dmem_n = _matmul(dkv, wl["w_xkv"], tb=True, out_dtype=F32, name=n("mm_dmemn"))
    g["w_xkv"] = wgrad("w_xkv", sv["mem_n"], dkv, "mm_dwxkv")
    _, dg = _rms_bwd(mem, _row(p["mem_g"][l]), dmem_n, out_dtype=BF16, name=n("rms_mem"))
    g["mem_g"] = red(dg)
    dx, dg = _rms_bwd(sv["x1"], _row(p["xa_pre_g"][l]), dh2, dx, out_dtype=F32, name=n("rms2pre"))
    g["xa_pre_g"] = red(dg)
    dr1, dg = _rms_bwd(sv["r1"], _row(p["mix_post_g"][l]), dx, out_dtype=BF16, name=n("rms1post"))
    g["mix_post_g"] = red(dg)
    dmerged = _matmul(dr1, wl["w_mix_out"], tb=True, out_dtype=BF16, name=n("mm_dmerged"))
    g["w_mix_out"] = wgrad("w_mix_out", sv["merged"], dr1, "mm_dwmo")
    dyp, dyf, dzg = _merge_bwd(sv["zg"], sv["y_pool"], sv["y_fox"], dmerged, name=n("merge"))
    dmixed = _matmul(dyp, wl["w_pool_br"], tb=True, out_dtype=F32, name=n("mm_dmixed"))
    g["w_pool_br"] = wgrad("w_pool_br", sv["mixed"], dyp, "mm_dwpb")
    dofox = _matmul(dyf, wl["w_fox_br"], tb=True, out_dtype=F32, name=n("mm_dofox"))
    g["w_fox_br"] = wgrad("w_fox_br", sv["o_fox"], dyf, "mm_dwfb")
    dzu, dpw, dsc = _pool_bwd(sv["pooled"], dmixed, p["pool_w"][l], _row(p["pool_scale"][l]), name=n("pool"))
    g["pool_w"] = dpw
    g["pool_scale"] = red(dsc)
    delta = _head_rowsum(dofox, sv["o_fox"], name=n("delta"))
    dq, dk, dv, dck, dcq = _fox_bwd(sv["zqkv"], dofox.astype(BF16), sv["augq"], sv["augk"], sv["lse"], delta,
                                    name=n("fox"))
    dc_pad = jnp.pad((dcq + dck).reshape(FOX_HEADS, S).T, ((0, 0), (0, LANES - FOX_HEADS)))
    dzf, db = _fgate_bwd(sv["zf"], sv["bf"], dc_pad, name=n("fgate"))
    g["b_forget"] = red(db)[:N_FGATE]
    dz_cat = jnp.concatenate([dzu, dq.astype(BF16), dk, dv, dzg, dzf], axis=1)
    dh1 = _matmul(dz_cat, wl["w_in_r"], tb=True, out_dtype=BF16, name=n("mm_dh1"))
    dw_in_r = _matmul(sv["h1"], dz_cat, ta=True, out_dtype=F32, name=n("mm_dwin"))
    og = OFF_G
    g["w_in"] = jnp.concatenate([dw_in_r[:, :og], dw_in_r[:, og + 2 * D:og + 2 * D + N_FGATE],
                                 dw_in_r[:, og:og + 2 * D]], axis=1)
    dx, dg = _rms_bwd(sv["x0"], _row(p["mix_pre_g"][l]), dh1, dx, out_dtype=F32, name=n("rms1pre"))
    g["mix_pre_g"] = red(dg)
    return dx, g


MATMUL_WEIGHTS = ("w_in", "w_pool_br", "w_fox_br", "w_mix_out", "w_xq", "w_xkv", "w_xo", "w_up", "w_down")
WEIGHT_NAMES = ("mix_pre_g", "mix_post_g", "w_in", "b_forget", "pool_w", "pool_scale", "w_pool_br", "w_fox_br",
                "w_mix_out", "xa_pre_g", "xa_post_g", "mem_g", "w_xq", "w_xkv", "w_xo", "ffn_pre_g", "ffn_post_g",
                "w_up", "conv_w", "conv_b", "w_down")


def _local_step(x, mem, loss_target, wfull, p):
    L = p["mix_pre_g"].shape[0]
    saved, wls = [], []
    h = x
    hn = _rms_fwd(x, _row(p["mix_pre_g"][0]), out_dtype=BF16, name="rms_first")
    for l in range(L):
        wl = _prep_layer_weights(wfull, l)
        h, hn, sv = _layer_fwd(h, hn, mem, wl, p, l)
        saved.append(sv)
        wls.append(wl)
    D = x.shape[1]
    dy, sq = _loss_head(h, loss_target, name="loss_head")
    loss = 0.5 * jnp.sum(sq) / D
    grads = []
    gbuf = {}
    dx = dy
    for l in reversed(range(L)):
        dx, g = _layer_bwd(dx, mem, wls[l], p, l, saved[l], gbuf)
        gbuf = {k: g[k] for k in MATMUL_WEIGHTS if k != "w_in"}
        grads.append(g)
    grads = grads[::-1]

    def chip_major(g):
        return g.reshape(g.shape[0], N_CHIPS, g.shape[1] // N_CHIPS).transpose(1, 0, 2)

    gfull = {k: jnp.stack([grads[l][k] for l in range(L)]) for k in REPLICATED}
    gfull.update({k: jnp.stack([chip_major(grads[l][k]) for l in range(L)], axis=1) for k in ("w_in", "conv_w")})
    gfull.update(gbuf)
    return loss, dx, gfull


PACK_W = 512
N_CHIPS = 4
N_DEV = 8
REPLICATED = ("mix_pre_g", "mix_post_g", "b_forget", "pool_w", "pool_scale", "xa_pre_g", "xa_post_g", "mem_g",
              "ffn_pre_g", "ffn_post_g", "conv_b")


def _round_up(n, m):
    return -(-n // m) * m


def _pack(arrs, rows):
    flat = jnp.concatenate([a.reshape(-1) for a in arrs])
    return jnp.pad(flat, (0, rows * PACK_W - flat.shape[0])).reshape(rows, PACK_W)


def _unpack(buf, shapes):
    flat = buf.reshape(-1)
    out, off = [], 0
    for s in shapes:
        n = math.prod(s)
        out.append(flat[off:off + n].reshape(s))
        off += n
    return out


ANY = pl.BlockSpec(memory_space=pl.ANY)


def _remote(send_sems, recv_sems, k, src, dst, to):
    return pltpu.make_async_remote_copy(src_ref=src, dst_ref=dst, send_sem=send_sems.at[k], recv_sem=recv_sems.at[k],
                                        device_id=to, device_id_type=MESH)


def _my_place():
    return lax.axis_index("x"), lax.axis_index("y"), lax.axis_index("c")


def _rows_per_block(a, b, cap_bytes=1024 * 1024):
    if a % 8:
        return a
    return _pick(a, max(8, cap_bytes // (4 * b) // 8 * 8), 8)


def _place_shard(w, chip, dtype, *, name):
    L, a, b = w.shape
    ta = _rows_per_block(a, b)

    def body(chip_ref, w_ref, o_ref):
        o_ref[...] = w_ref[...].astype(dtype)

    return pl.pallas_call(
        body, name=name, out_shape=jax.ShapeDtypeStruct((N_CHIPS, L, a, b), dtype),
        grid_spec=pltpu.PrefetchScalarGridSpec(
            num_scalar_prefetch=1, grid=(L, a // ta),
            in_specs=[pl.BlockSpec((None, ta, b), lambda l, i, chip_ref: (l, i, 0))],
            out_specs=pl.BlockSpec((None, None, ta, b), lambda l, i, chip_ref: (chip_ref[0], l, i, 0))),
        compiler_params=_cparams("parallel", "parallel"),
    )(chip, w)


def _layer_halves(L, c):
    assert L % 2 == 0
    return pl.ds(c * (L // 2), L // 2), pl.ds((1 - c) * (L // 2), L // 2)


def _gather_weights(bufs):
    n = len(bufs)
    L = bufs[0].shape[1]

    def body(*refs):
        outs, (send_sems, recv_sems) = refs[n:2 * n], refs[2 * n:]
        x, y, c = _my_place()
        me = 2 * x + y
        sibling = (x, y, 1 - c)
        chips = [(1 - x, y), (x, 1 - y), (1 - x, 1 - y)]
        half, other = _layer_halves(L, c)
        rc = functools.partial(_remote, send_sems, recv_sems)
        first = [rc(6 * w + k, o.at[me, half], o.at[me, half], (px, py, c))
                 for w, o in enumerate(outs) for k, (px, py) in enumerate(chips)]
        for cp in first:
            cp.start()
        passed = []
        for k, (px, py) in enumerate(chips):
            src = 2 * px + py
            for w, o in enumerate(outs):
                rc(6 * w + k, o.at[src, half], o.at[src, half], (px, py, c)).wait_recv()
                fwd = rc(6 * w + 3 + k, o.at[src, half], o.at[src, half], sibling)
                fwd.start()
                passed.append(fwd)
        for k, (px, py) in enumerate(chips):
            src = 2 * px + py
            for w, o in enumerate(outs):
                rc(6 * w + 3 + k, o.at[src, other], o.at[src, other], sibling).wait_recv()
        for cp in first + passed:
            cp.wait_send()

    return pl.pallas_call(
        body, name="gather_weights",
        out_shape=tuple(jax.ShapeDtypeStruct(b.shape, b.dtype) for b in bufs),
        in_specs=[ANY] * n, out_specs=tuple([ANY] * n), input_output_aliases={i: i for i in range(n)},
        scratch_shapes=[pltpu.SemaphoreType.DMA((6 * n,)), pltpu.SemaphoreType.DMA((6 * n,))],
    )(*bufs)


def _exchange_halves(Gs, rep):
    n = len(Gs)
    L = Gs[0].shape[1]
    RR, W = rep.shape

    def body(*refs):
        g_refs, rep_ref, ra_refs, rall_ref = refs[:n], refs[n], refs[n + 1:2 * n + 1], refs[2 * n + 1]
        send_sems, recv_sems, local_sem = refs[2 * n + 2:]
        x, y, c = _my_place()
        me = 4 * x + 2 * y + c
        rc = functools.partial(_remote, send_sems, recv_sems)
        _, other = _layer_halves(L, c)

        def peer(idx):
            px = (1 - x) if (idx >> 2) & 1 else x
            py = (1 - y) if (idx >> 1) & 1 else y
            pc = (1 - c) if idx & 1 else c
            return px, py, pc

        loc = pltpu.make_async_copy(rep_ref, rall_ref.at[me], local_sem)
        loc.start()
        cps = [rc(N_DEV + w, g.at[:, other], ra, (x, y, 1 - c)) for w, (g, ra) in enumerate(zip(g_refs, ra_refs))]
        for idx in range(1, N_DEV):
            cps.append(rc(idx, rep_ref, rall_ref.at[me], peer(idx)))
        for cp in cps:
            cp.start()
        for w, (g, ra) in enumerate(zip(g_refs, ra_refs)):
            rc(N_DEV + w, g.at[:, other], ra, (x, y, 1 - c)).wait_recv()
        for idx in range(1, N_DEV):
            px, py, pc = peer(idx)
            rc(idx, rep_ref, rall_ref.at[4 * px + 2 * py + pc], (px, py, pc)).wait_recv()
        for cp in cps:
            cp.wait_send()
        loc.wait()

    halves = tuple(jax.ShapeDtypeStruct((g.shape[0], L // 2) + g.shape[2:], F32) for g in Gs)
    out = pl.pallas_call(
        body, name="exchange_halves",
        out_shape=halves + (jax.ShapeDtypeStruct((N_DEV, RR, W), F32),),
        in_specs=[ANY] * (n + 1), out_specs=tuple([ANY] * (n + 1)),
        scratch_shapes=[pltpu.SemaphoreType.DMA((N_DEV + n,)), pltpu.SemaphoreType.DMA((N_DEV + n,)),
                        pltpu.SemaphoreType.DMA],
    )(*Gs, rep)
    return out[:n], out[n]


def _exchange_chips(As):
    n = len(As)

    def body(*refs):
        a_refs, rb_refs, (send_sems, recv_sems) = refs[:n], refs[n:2 * n], refs[2 * n:]
        x, y, c = _my_place()
        me = 2 * x + y
        chips = [(1 - x, y), (x, 1 - y), (1 - x, 1 - y)]
        rc = functools.partial(_remote, send_sems, recv_sems)
        cps = [rc(3 * w + k, a.at[2 * px + py], rb.at[k], (px, py, c))
               for w, (a, rb) in enumerate(zip(a_refs, rb_refs)) for k, (px, py) in enumerate(chips)]
        for cp in cps:
            cp.start()
        for w, (a, rb) in enumerate(zip(a_refs, rb_refs)):
            for k, (px, py) in enumerate(chips):
                rc(3 * w + k, a.at[me], rb.at[k], (px, py, c)).wait_recv()
        for cp in cps:
            cp.wait_send()

    return pl.pallas_call(
        body, name="exchange_chips",
        out_shape=tuple(jax.ShapeDtypeStruct((N_CHIPS - 1,) + a.shape[1:], a.dtype) for a in As),
        in_specs=[ANY] * n, out_specs=tuple([ANY] * n),
        scratch_shapes=[pltpu.SemaphoreType.DMA((3 * n,)), pltpu.SemaphoreType.DMA((3 * n,))],
    )(*As)


def _exchange_sibling(gs):
    n = len(gs)
    L = gs[0].shape[0]

    def body(*refs):
        g_refs, (send_sems, recv_sems) = refs[n:2 * n], refs[2 * n:]
        x, y, c = _my_place()
        half, other = _layer_halves(L, c)
        rc = functools.partial(_remote, send_sems, recv_sems)
        cps = [rc(w, g.at[half], g.at[half], (x, y, 1 - c)) for w, g in enumerate(g_refs)]
        for cp in cps:
            cp.start()
        for w, g in enumerate(g_refs):
            rc(w, g.at[other], g.at[other], (x, y, 1 - c)).wait_recv()
        for cp in cps:
            cp.wait_send()

    return pl.pallas_call(
        body, name="exchange_sibling", out_shape=tuple(jax.ShapeDtypeStruct(g.shape, F32) for g in gs),
        in_specs=[ANY] * n, out_specs=tuple([ANY] * n), input_output_aliases={i: i for i in range(n)},
        scratch_shapes=[pltpu.SemaphoreType.DMA((n,)), pltpu.SemaphoreType.DMA((n,))],
    )(*gs)


def _add_halves(G, recv, core, *, name):
    n, L, a, b = G.shape
    Lh = L // 2
    ta = _rows_per_block(a, b)

    def body(core_ref, g_ref, r_ref, o_ref, o16_ref):
        s = g_ref[...] + r_ref[...]
        o_ref[...] = s
        o16_ref[...] = s.astype(BF16)

    blk = pl.BlockSpec((None, None, ta, b), lambda p, l, i, core_ref: (p, l, i, 0))
    return pl.pallas_call(
        body, name=name,
        out_shape=(jax.ShapeDtypeStruct((n, Lh, a, b), F32), jax.ShapeDtypeStruct((n, Lh, a, b), BF16)),
        grid_spec=pltpu.PrefetchScalarGridSpec(
            num_scalar_prefetch=1, grid=(n, Lh, a // ta),
            in_specs=[pl.BlockSpec((None, None, ta, b), lambda p, l, i, core_ref: (p, core_ref[0] * Lh + l, i, 0)), blk],
            out_specs=(blk, blk)),
        compiler_params=_cparams("parallel", "parallel", "parallel"),
    )(core, G, recv)


def _sum_chips(A, rb, place, *, name):
    _, Lh, a, b = A.shape
    ta = _rows_per_block(a, b, 512 * 1024)

    def body(place_ref, a_ref, r_ref, o_ref):
        o_ref[...] = ((a_ref[...] + r_ref[0].astype(F32)) + r_ref[1].astype(F32)) + r_ref[2].astype(F32)

    return pl.pallas_call(
        body, name=name, out_shape=jax.ShapeDtypeStruct((2 * Lh, a, b), F32),
        grid_spec=pltpu.PrefetchScalarGridSpec(
            num_scalar_prefetch=1, grid=(Lh, a // ta),
            in_specs=[pl.BlockSpec((None, None, ta, b), lambda l, i, place_ref: (place_ref[0], l, i, 0)),
                      pl.BlockSpec((N_CHIPS - 1, None, ta, b), lambda l, i, place_ref: (0, l, i, 0))],
            out_specs=pl.BlockSpec((None, ta, b), lambda l, i, place_ref: (place_ref[1] * Lh + l, i, 0))),
        compiler_params=_cparams("parallel", "parallel"),
    )(place, A, rb)


def _sum_slots(a, *, name):
    n, rows, W = a.shape
    tr = _pick(rows, 512, 8)

    def body(a_ref, o_ref):
        s = a_ref[0]
        for q in range(1, n):
            s = s + a_ref[q]
        o_ref[...] = s

    return pl.pallas_call(
        body, name=name, out_shape=jax.ShapeDtypeStruct((rows, W), F32), grid=(rows // tr,),
        in_specs=[pl.BlockSpec((n, tr, W), lambda i: (0, i, 0))], out_specs=pl.BlockSpec((tr, W), lambda i: (i, 0)),
        compiler_params=_cparams("parallel"),
    )(a)


def _adamw(w, g, m, v, *, name):
    L, a, b = w.shape
    ta = _rows_per_block(a, b, 512 * 1024)

    def body(w_ref, g_ref, m_ref, v_ref, d_ref, nm_ref, nv_ref):
        gg = g_ref[...]
        nm = ADAM_B1 * m_ref[...] + (1.0 - ADAM_B1) * gg
        nv = ADAM_B2 * v_ref[...] + (1.0 - ADAM_B2) * jnp.square(gg)
        m_hat = nm / (1.0 - ADAM_B1 ** ADAM_STEP)
        v_hat = nv / (1.0 - ADAM_B2 ** ADAM_STEP)
        d_ref[...] = -ADAM_LR * (m_hat / (jnp.sqrt(v_hat) + ADAM_EPS) + ADAM_WD * w_ref[...])
        nm_ref[...] = nm
        nv_ref[...] = nv

    blk = pl.BlockSpec((None, ta, b), lambda l, i: (l, i, 0))
    shp = jax.ShapeDtypeStruct((L, a, b), F32)
    return pl.pallas_call(
        body, name=name, out_shape=(shp, shp, shp), grid=(L, a // ta),
        in_specs=[blk, blk, blk, blk], out_specs=(blk, blk, blk),
        compiler_params=_cparams("parallel", "parallel"),
    )(w, g, m, v)


INPUT_NAMES = (("x", "mem") + WEIGHT_NAMES + ("loss_target",) + tuple("m_" + n for n in WEIGHT_NAMES)
               + tuple("v_" + n for n in WEIGHT_NAMES))


def kernel(*args):
    a = dict(zip(INPUT_NAMES, args, strict=True))
    x, mem, target = a["x"][0], a["mem"][0], a["loss_target"][0]
    sh_names = list(SHARD_BY_ROWS)
    core = lax.axis_index("c").astype(jnp.int32)
    chip = (2 * lax.axis_index("x") + lax.axis_index("y")).astype(jnp.int32)
    place = jnp.stack([chip, core])

    placed = [_place_shard(a[n], chip.reshape(1), F32 if n == "conv_w" else BF16, name="place_" + n) for n in sh_names]
    wfull = dict(zip(sh_names, _gather_weights(placed)))
    p = {n: a[n] for n in REPLICATED}
    cw = wfull.pop("conv_w")
    p["conv_w"] = cw.transpose(1, 2, 0, 3).reshape(cw.shape[1], cw.shape[2], N_CHIPS * cw.shape[3])

    loss, dx, gfull = _local_step(x, mem, target, wfull, p)
    loss = lax.psum(loss, ("x", "y", "c"))

    Gs = [gfull[n] for n in sh_names]
    rep_shapes = [a[n].shape for n in REPLICATED]
    rows_r = _round_up(-(-sum(math.prod(s) for s in rep_shapes) // PACK_W), 64)
    rep = _pack([gfull[n] for n in REPLICATED], rows_r)
    recvs, repall = _exchange_halves(Gs, rep)
    As = [_add_halves(g, r, core.reshape(1), name="add_halves_" + n) for n, g, r in zip(sh_names, Gs, recvs)]
    rbs = _exchange_chips([a16 for _, a16 in As])
    gsh = _exchange_sibling([_sum_chips(A, rb, place, name="sum_chips_" + n)
                             for n, (A, _), rb in zip(sh_names, As, rbs)])
    grep = _sum_slots(repall, name="sum_devices")

    got = {"g": dict(zip(sh_names, gsh)), "d": {}, "m": {}, "v": {}}
    for n, g in zip(sh_names, gsh):
        got["d"][n], got["m"][n], got["v"][n] = _adamw(a[n], g, a["m_" + n], a["v_" + n], name="adamw_" + n)
    packed = [_pack([a[pre + n] for n in REPLICATED], rows_r)[None] for pre in ("", "m_", "v_")]
    d_r, m_r, v_r = _adamw(packed[0], grep[None], packed[1], packed[2], name="adamw_replicated")
    for key, buf in (("g", grep), ("d", d_r[0]), ("m", m_r[0]), ("v", v_r[0])):
        got[key].update(zip(REPLICATED, _unpack(buf, rep_shapes)))
    outs = [got[key][n] for key in ("g", "d", "m", "v") for n in WEIGHT_NAMES]
    return (loss, dx[None], *outs)
```

```python
import functools
import math

import jax
import jax.numpy as jnp
from jax import lax
from jax.experimental import pallas as pl
from jax.experimental.pallas import tpu as pltpu

F32 = jnp.float32
BF16 = jnp.bfloat16
MESH = pl.DeviceIdType.MESH

RMS_EPS = 1e-6
POOL_WINDOWS = (2, 4, 8, 16)
POOL_GROUP = 128
POOL_W = 512
FOX_HEADS = 8
FOX_DH = 64
FOX_W = 512
X_HEADS = 4
X_DH = 128
X_W = 512
N_FGATE = 8
LANES = 128
HALO = 16

ADAM_LR = 0.001
ADAM_B1 = 0.9
ADAM_B2 = 0.999
ADAM_EPS = 1e-08
ADAM_WD = 0.01
ADAM_STEP = 10

VMEM_LIMIT_BYTES = 56 * 1024 * 1024
MATMUL_VMEM_BUDGET = 42 * 1024 * 1024


def _cparams(*sem):
    return pltpu.CompilerParams(dimension_semantics=sem, vmem_limit_bytes=VMEM_LIMIT_BYTES)


def _pick(n, cap, align=LANES):
    if n <= cap:
        return n
    best = None
    for t in range(align, cap + 1, align):
        if n % t == 0:
            best = t
    assert best is not None, (n, cap, align)
    return best


def _sigmoid(x):
    return 1.0 / (1.0 + jnp.exp(-x))


class _ChipMajor:
    def __init__(self, arr, layer, by_rows):
        self.arr, self.layer, self.by_rows = arr, layer, by_rows
        n, _, a, b = arr.shape
        self.n_chips, self.per_chip = n, (a if by_rows else b)
        self.shape = (n * a, b) if by_rows else (a, n * b)


SPAN_CHIPS_BELOW = 512


def _matmul(a, b, *, ta=False, tb=False, out_dtype=F32, out_chips=None, out_rows=None, out_layer=None, name):
    view = b if isinstance(b, _ChipMajor) else None
    if ta:
        K, M = a.shape
    else:
        M, K = a.shape
    if tb:
        N, Kb = b.shape
    else:
        Kb, N = b.shape
    assert K == Kb, (a.shape, b.shape, ta, tb)
    by_rows = view is not None and view.by_rows
    by_cols = view is not None and not view.by_rows
    b_itemsize = (view.arr if view is not None else b).dtype.itemsize
    span_b = by_cols and view.per_chip < SPAN_CHIPS_BELOW
    span_o = bool(out_chips) and N // out_chips < SPAN_CHIPS_BELOW
    if by_rows and tb:
        tn = N
    elif by_cols and not tb:
        tn = N if span_b else _pick(view.per_chip, 1408)
    elif out_chips:
        tn = N if span_o else _pick(N // out_chips, 1408)
    elif out_rows:
        tn = _pick(N, 512)
    else:
        tn = _pick(N, 1408)
    tm = M if out_rows else _pick(M, 1024 if tn <= 1024 else 512)
    if by_rows and not tb:
        tk = K
    elif by_cols and tb:
        tk = K if span_b else (view.per_chip if view.per_chip <= 2048 else _pick(view.per_chip, 1024))
    else:
        ab, bb, ob = a.dtype.itemsize, b_itemsize, jnp.dtype(out_dtype).itemsize
        for tm in ((M,) if out_rows else (_pick(M, 1024), tm)):
            for cap in (K, 2048, 1024, 512, 128):
                tk = _pick(K, cap)
                need = (2 * tk * (tm * ab + tn * bb) + (2 * ob + 4 + (4 if tk < K else 0)) * tm * tn)
                if need <= MATMUL_VMEM_BUDGET:
                    break
            if tk >= min(K, 1024):
                break
    nk = K // tk
    dims = (((0 if ta else 1,), (1 if tb else 0,)), ((), ()))

    aliased = out_layer is not None and out_layer[0] is not None

    def body(a_ref, b_ref, *rest):
        o_ref, scratch = (rest[1], rest[2:]) if aliased else (rest[0], rest[1:])
        bt = b_ref[...]
        if by_rows:
            bt = bt.reshape(bt.shape[0] * bt.shape[1], bt.shape[2])
        elif span_b:
            bt = jnp.concatenate([bt[q] for q in range(view.n_chips)], axis=1)
        p = lax.dot_general(a_ref[...].astype(BF16), bt.astype(BF16), dims, preferred_element_type=F32)

        def store(val):
            if span_o:
                w = N // out_chips
                for q in range(out_chips):
                    o_ref[q] = val[:, q * w:(q + 1) * w].astype(out_dtype)
            elif out_rows:
                h = M // out_rows
                for q in range(out_rows):
                    o_ref[q] = val[q * h:(q + 1) * h, :].astype(out_dtype)
            else:
                o_ref[...] = val.astype(out_dtype)

        if nk == 1:
            store(p)
        else:
            acc_ref, = scratch
            k = pl.program_id(2)

            @pl.when(k == 0)
            def _():
                acc_ref[...] = p

            @pl.when(k > 0)
            def _():
                acc_ref[...] += p

            @pl.when(k == nk - 1)
            def _():
                store(acc_ref[...])

    a_spec = pl.BlockSpec((tk, tm), lambda i, j, k: (k, i)) if ta else pl.BlockSpec((tm, tk), lambda i, j, k: (i, k))
    b_tile = (tn, tk) if tb else (tk, tn)
    b_rc = (lambda i, j, k: (j, k)) if tb else (lambda i, j, k: (k, j))
    if view is None:
        b_arr = b
        b_spec = pl.BlockSpec(b_tile, b_rc)
    elif by_rows:
        b_arr = view.arr
        assert b_tile[0] == view.shape[0]
        b_spec = pl.BlockSpec((view.n_chips, None, view.arr.shape[2], b_tile[1]),
                              lambda i, j, k: (0, view.layer, 0, b_rc(i, j, k)[1]))
    elif span_b:
        b_arr = view.arr
        assert b_tile[1] == view.shape[1]
        b_spec = pl.BlockSpec((view.n_chips, None, b_tile[0], view.per_chip),
                              lambda i, j, k: (0, view.layer, b_rc(i, j, k)[0], 0))
    else:
        b_arr = view.arr
        per = view.per_chip // b_tile[1]
        b_spec = pl.BlockSpec((None, None) + b_tile,
                              lambda i, j, k: (b_rc(i, j, k)[1] // per, view.layer, b_rc(i, j, k)[0],
                                               b_rc(i, j, k)[1] % per))
    if span_o:
        o_full, o_blk = (out_chips, M, N // out_chips), (out_chips, tm, N // out_chips)
        o_idx = lambda i, j, k: (0, i, 0)
    elif out_chips:
        per_o = (N // out_chips) // tn
        o_full, o_blk = (out_chips, M, N // out_chips), (None, tm, tn)
        o_idx = lambda i, j, k: (j // per_o, i, j % per_o)
    elif out_rows:
        o_full, o_blk = (out_rows, M // out_rows, N), (out_rows, M // out_rows, tn)
        o_idx = lambda i, j, k: (0, 0, j)
    else:
        o_full, o_blk = (M, N), (tm, tn)
        o_idx = lambda i, j, k: (i, j)
    if out_layer is not None:
        _, layer, n_layers = out_layer
        o_full, o_blk = o_full[:1] + (n_layers,) + o_full[1:], o_blk[:1] + (None,) + o_blk[1:]
        o_idx = functools.partial(lambda f, i, j, k: (f(i, j, k)[0], layer) + f(i, j, k)[1:], o_idx)
    out_shape = jax.ShapeDtypeStruct(o_full, out_dtype)
    out_spec = pl.BlockSpec(o_blk, o_idx)
    grid = (M // tm, N // tn, nk)
    specs = [a_spec, b_spec, out_spec]
    a_bytes, b_bytes = M * K * a.dtype.itemsize, K * N * b_itemsize
    if nk == 1 and b_bytes + a_bytes * grid[1] < a_bytes + b_bytes * grid[0]:
        grid = (grid[1], grid[0], nk)
        specs = [pl.BlockSpec(s.block_shape, functools.partial(lambda f, j, i, k: f(i, j, k), s.index_map))
                 for s in specs]
    extra = (out_layer[0],) if aliased else ()
    return pl.pallas_call(
        body, name=name, out_shape=out_shape, grid=grid,
        in_specs=specs[:2] + [pl.BlockSpec(memory_space=pl.ANY)] * len(extra), out_specs=specs[2],
        input_output_aliases={2: 0} if aliased else {},
        scratch_shapes=[pltpu.VMEM((tm, tn), F32)] if nk > 1 else [],
        compiler_params=_cparams("parallel", "parallel", "arbitrary"),
    )(a, b_arr, *extra)


def _row_block(S, D, cap_bytes=2 * 1024 * 1024):
    ts = max(8, min(S, cap_bytes // (4 * D)))
    return _pick(S, ts, 8)


def _rms_fwd(x, g, *, out_dtype, name):
    S, D = x.shape
    ts = _row_block(S, D)

    def body(x_ref, g_ref, o_ref):
        xf = x_ref[...]
        r = lax.rsqrt(jnp.mean(xf * xf, axis=-1, keepdims=True) + RMS_EPS)
        o_ref[...] = (xf * r * g_ref[...]).astype(out_dtype)

    return pl.pallas_call(
        body, name=name, out_shape=jax.ShapeDtypeStruct((S, D), out_dtype), grid=(S // ts,),
        in_specs=[pl.BlockSpec((ts, D), lambda i: (i, 0)), pl.BlockSpec((1, D), lambda i: (0, 0))],
        out_specs=pl.BlockSpec((ts, D), lambda i: (i, 0)),
        compiler_params=_cparams("parallel"),
    )(x, g)


def _add_rms(x, r, g, g_next=None, *, name):
    S, D = x.shape
    ts = _row_block(S, D)
    fused = g_next is not None

    def body(*refs):
        x_ref, r_ref, g_ref = refs[:3]
        rf = r_ref[...].astype(F32)
        s = lax.rsqrt(jnp.mean(rf * rf, axis=-1, keepdims=True) + RMS_EPS)
        y = x_ref[...] + rf * s * g_ref[...]
        if fused:
            gn_ref, o_ref, h_ref = refs[3:]
            t = lax.rsqrt(jnp.mean(y * y, axis=-1, keepdims=True) + RMS_EPS)
            h_ref[...] = (y * t * gn_ref[...]).astype(BF16)
        else:
            o_ref, = refs[3:]
        o_ref[...] = y

    row = pl.BlockSpec((ts, D), lambda i: (i, 0))
    vec = pl.BlockSpec((1, D), lambda i: (0, 0))
    out = pl.pallas_call(
        body, name=name,
        out_shape=(jax.ShapeDtypeStruct((S, D), F32),) + ((jax.ShapeDtypeStruct((S, D), BF16),) if fused else ()),
        grid=(S // ts,),
        in_specs=[row, row, vec] + ([vec] if fused else []),
        out_specs=(row,) + ((row,) if fused else ()),
        compiler_params=_cparams("parallel"),
    )(*((x, r, g) + ((g_next,) if fused else ())))
    return out if fused else out[0]


def _rms_bwd(x, g, dy, res=None, *, out_dtype, name):
    S, D = x.shape
    ts = _row_block(S, D)
    has_res = res is not None

    def body(*refs):
        if has_res:
            x_ref, g_ref, dy_ref, res_ref, dx_ref, dg_ref = refs
        else:
            x_ref, g_ref, dy_ref, dx_ref, dg_ref = refs
        i = pl.program_id(0)
        xf = x_ref[...].astype(F32)
        dyf = dy_ref[...].astype(F32)
        r = lax.rsqrt(jnp.mean(xf * xf, axis=-1, keepdims=True) + RMS_EPS)
        n = xf * r
        dn = dyf * g_ref[...]
        dx = r * (dn - n * jnp.mean(dn * n, axis=-1, keepdims=True))
        if has_res:
            dx = dx + res_ref[...]
        dx_ref[...] = dx.astype(out_dtype)
        part = jnp.sum((dyf * n).reshape(ts // 8, 8, D), axis=0)

        @pl.when(i == 0)
        def _():
            dg_ref[...] = part

        @pl.when(i > 0)
        def _():
            dg_ref[...] += part

    row = pl.BlockSpec((ts, D), lambda i: (i, 0))
    in_specs = [row, pl.BlockSpec((1, D), lambda i: (0, 0)), row] + ([row] if has_res else [])
    args = (x, g, dy) + ((res,) if has_res else ())
    dx, dg = pl.pallas_call(
        body, name=name,
        out_shape=(jax.ShapeDtypeStruct((S, D), out_dtype), jax.ShapeDtypeStruct((8, D), F32)),
        grid=(S // ts,), in_specs=in_specs,
        out_specs=(row, pl.BlockSpec((8, D), lambda i: (0, 0))),
        compiler_params=_cparams("arbitrary"),
    )(*args)
    return dx, dg


def _loss_head(y, t, *, name):
    S, D = y.shape
    ts = _row_block(S, D)

    def body(y_ref, t_ref, dy_ref, sq_ref):
        i = pl.program_id(0)
        e = y_ref[...] - t_ref[...]
        dy_ref[...] = e / D
        part = jnp.sum((e * e).reshape(ts // 8, 8, D), axis=0)

        @pl.when(i == 0)
        def _():
            sq_ref[...] = part

        @pl.when(i > 0)
        def _():
            sq_ref[...] += part

    row = pl.BlockSpec((ts, D), lambda i: (i, 0))
    return pl.pallas_call(
        body, name=name,
        out_shape=(jax.ShapeDtypeStruct((S, D), F32), jax.ShapeDtypeStruct((8, D), F32)),
        grid=(S // ts,), in_specs=[row, row],
        out_specs=(row, pl.BlockSpec((8, D), lambda i: (0, 0))),
        compiler_params=_cparams("arbitrary"),
    )(y, t)


def _window_counts(i, T, w):
    t = i * T + lax.broadcasted_iota(jnp.int32, (T, 1), 0)
    return jnp.minimum(t + 1, w).astype(F32)


def _pool_fwd(zu, pool_w, pool_scale, *, name):
    S, W = zu.shape
    T = _pick(S, 1024, 8)

    def body(u_ref, pw_ref, sc_ref, pooled_ref, mixed_ref, halo_ref):
        i = pl.program_id(0)

        @pl.when(i == 0)
        def _():
            halo_ref[...] = jnp.zeros_like(halo_ref)

        u = u_ref[...]
        ext = jnp.concatenate([halo_ref[...], u], axis=0)
        halo_ref[...] = u[T - HALO:, :]
        for g, w in enumerate(POOL_WINDOWS):
            cols = slice(g * POOL_GROUP, (g + 1) * POOL_GROUP)
            s = ext[:, cols]
            sh = 1
            while sh < w:
                s = s + pltpu.roll(s, sh, 0)
                sh *= 2
            pooled = s[HALO:, :] / _window_counts(i, T, w) - u[:, cols]
            pooled_bf = pooled.astype(BF16)
            pm = jnp.dot(pooled_bf, pw_ref[g].astype(BF16), preferred_element_type=F32)
            pooled_ref[:, cols] = pooled_bf
            mixed_ref[:, cols] = (pm * sc_ref[:, cols]).astype(BF16)

    row = pl.BlockSpec((T, W), lambda i: (i, 0))
    return pl.pallas_call(
        body, name=name,
        out_shape=(jax.ShapeDtypeStruct((S, W), BF16), jax.ShapeDtypeStruct((S, W), BF16)),
        grid=(S // T,),
        in_specs=[row, pl.BlockSpec(pool_w.shape, lambda i: (0, 0, 0)), pl.BlockSpec((1, W), lambda i: (0, 0))],
        out_specs=(row, row),
        scratch_shapes=[pltpu.VMEM((HALO, W), F32)],
        compiler_params=_cparams("arbitrary"),
    )(zu, pool_w, pool_scale)


def _pool_bwd(pooled, dmixed, pool_w, pool_scale, *, name):
    S, W = pooled.shape
    T = _pick(S, 1024, 8)
    nb = S // T

    def body(p_ref, dm_ref, pw_ref, sc_ref, dzu_ref, dpw_ref, dsc_ref, halo_ref):
        i = pl.program_id(0)
        blk = nb - 1 - i

        @pl.when(i == 0)
        def _():
            halo_ref[...] = jnp.zeros_like(halo_ref)
            dpw_ref[...] = jnp.zeros_like(dpw_ref)
            dsc_ref[...] = jnp.zeros_like(dsc_ref)

        for g, w in enumerate(POOL_WINDOWS):
            cols = slice(g * POOL_GROUP, (g + 1) * POOL_GROUP)
            p = p_ref[:, cols]
            dm = dm_ref[:, cols]
            pw = pw_ref[g].astype(BF16)
            pm = jnp.dot(p, pw, preferred_element_type=F32)
            dsc_ref[:, cols] += jnp.sum((dm * pm).reshape(T // 8, 8, POOL_GROUP), axis=0)
            dpm = (dm * sc_ref[:, cols]).astype(BF16)
            dpw_ref[g] += lax.dot_general(p, dpm, (((0,), (0,)), ((), ())), preferred_element_type=F32)
            dpooled = lax.dot_general(dpm, pw, (((1,), (1,)), ((), ())), preferred_element_type=F32)
            e = dpooled / _window_counts(blk, T, w)
            ext = jnp.concatenate([e, halo_ref[:, cols]], axis=0)
            halo_ref[:, cols] = e[:HALO, :]
            s = ext
            sh = 1
            while sh < w:
                s = s + pltpu.roll(s, T + HALO - sh, 0)
                sh *= 2
            dzu_ref[:, cols] = (s[:T, :] - dpooled).astype(BF16)

    row = pl.BlockSpec((T, W), lambda i: (nb - 1 - i, 0))
    return pl.pallas_call(
        body, name=name,
        out_shape=(jax.ShapeDtypeStruct((S, W), BF16), jax.ShapeDtypeStruct(pool_w.shape, F32),
                   jax.ShapeDtypeStruct((8, W), F32)),
        grid=(nb,),
        in_specs=[row, row, pl.BlockSpec(pool_w.shape, lambda i: (0, 0, 0)), pl.BlockSpec((1, W), lambda i: (0, 0))],
        out_specs=(row, pl.BlockSpec(pool_w.shape, lambda i: (0, 0, 0)), pl.BlockSpec((8, W), lambda i: (0, 0))),
        scratch_shapes=[pltpu.VMEM((HALO, W), F32)],
        compiler_params=_cparams("arbitrary"),
    )(pooled, dmixed, pool_w, pool_scale)


def _fgate_fwd(zf, bf, *, name):
    S, W = zf.shape
    T = _pick(S, 512, 8)

    def body(z_ref, b_ref, aq_ref, ak_ref, carry_ref):
        i = pl.program_id(0)

        @pl.when(i == 0)
        def _():
            carry_ref[...] = jnp.zeros_like(carry_ref)

        a = z_ref[...] + b_ref[...]
        s = jnp.minimum(a, 0.0) - jnp.log(1.0 + jnp.exp(-jnp.abs(a)))
        row = lax.broadcasted_iota(jnp.int32, (T, W), 0)
        sh = 1
        while sh < T:
            s = s + jnp.where(row >= sh, pltpu.roll(s, sh, 0), 0.0)
            sh *= 2
        c = s + carry_ref[0:1, :]
        carry_ref[...] = jnp.broadcast_to(c[T - 1:T, :], carry_ref.shape)
        lane = lax.broadcasted_iota(jnp.int32, (T, W), 1)
        for h in range(FOX_HEADS):
            ch = c[:, h:h + 1]
            hi = ch.astype(BF16).astype(F32)
            r1 = ch - hi
            lo = r1.astype(BF16).astype(F32)
            lo2 = (r1 - lo).astype(BF16).astype(F32)
            aq = jnp.where(lane == 0, hi, jnp.where(lane == 1, lo, jnp.where(lane == 2, lo2,
                                                                              jnp.where(lane < 6, 1.0, 0.0))))
            ak = jnp.where(lane < 3, 1.0, jnp.where(lane == 3, -hi, jnp.where(lane == 4, -lo,
                                                                               jnp.where(lane == 5, -lo2, 0.0))))
            aq_ref[h] = aq.astype(BF16)
            ak_ref[h] = ak.astype(BF16)

    aug = jax.ShapeDtypeStruct((FOX_HEADS, S, W), BF16)
    aug_spec = pl.BlockSpec((FOX_HEADS, T, W), lambda i: (0, i, 0))
    return pl.pallas_call(
        body, name=name, out_shape=(aug, aug), grid=(S // T,),
        in_specs=[pl.BlockSpec((T, W), lambda i: (i, 0)), pl.BlockSpec((1, W), lambda i: (0, 0))],
        out_specs=(aug_spec, aug_spec),
        scratch_shapes=[pltpu.VMEM((8, W), F32)],
        compiler_params=_cparams("arbitrary"),
    )(zf, bf)


def _fgate_bwd(zf, bf, dc, *, name):
    S, W = zf.shape
    T = _pick(S, 512, 8)
    nb = S // T

    def body(z_ref, b_ref, dc_ref, dz_ref, db_ref, carry_ref):
        i = pl.program_id(0)

        @pl.when(i == 0)
        def _():
            carry_ref[...] = jnp.zeros_like(carry_ref)
            db_ref[...] = jnp.zeros_like(db_ref)

        s = dc_ref[...]
        row = lax.broadcasted_iota(jnp.int32, (T, W), 0)
        sh = 1
        while sh < T:
            s = s + jnp.where(row < T - sh, pltpu.roll(s, T - sh, 0), 0.0)
            sh *= 2
        dlf = s + carry_ref[0:1, :]
        carry_ref[...] = jnp.broadcast_to(dlf[0:1, :], carry_ref.shape)
        dz = dlf * (1.0 - _sigmoid(z_ref[...] + b_ref[...]))
        dz_ref[...] = dz.astype(BF16)
        db_ref[...] += jnp.sum(dz.reshape(T // 8, 8, W), axis=0)

    row_spec = pl.BlockSpec((T, W), lambda i: (nb - 1 - i, 0))
    return pl.pallas_call(
        body, name=name,
        out_shape=(jax.ShapeDtypeStruct((S, W), BF16), jax.ShapeDtypeStruct((8, W), F32)),
        grid=(nb,),
        in_specs=[row_spec, pl.BlockSpec((1, W), lambda i: (0, 0)), row_spec],
        out_specs=(row_spec, pl.BlockSpec((8, W), lambda i: (0, 0))),
        scratch_shapes=[pltpu.VMEM((8, W), F32)],
        compiler_params=_cparams("arbitrary"),
    )(zf, bf, dc)


def _merge_fwd(zg, yp, yf, *, name):
    S, D = yp.shape
    ts = _row_block(S, D, 1024 * 1024)

    def body(zg_ref, yp_ref, yf_ref, o_ref):
        zg = zg_ref[...].astype(F32)
        o_ref[...] = (_sigmoid(zg[:, :D]) * yp_ref[...] + _sigmoid(zg[:, D:]) * yf_ref[...]).astype(BF16)

    row = pl.BlockSpec((ts, D), lambda i: (i, 0))
    return pl.pallas_call(
        body, name=name, out_shape=jax.ShapeDtypeStruct((S, D), BF16), grid=(S // ts,),
        in_specs=[pl.BlockSpec((ts, 2 * D), lambda i: (i, 0)), row, row], out_specs=row,
        compiler_params=_cparams("parallel"),
    )(zg, yp, yf)


def _merge_bwd(zg, yp, yf, dmerged, *, name):
    S, D = yp.shape
    ts = _row_block(S, D, 1024 * 1024)

    def body(zg_ref, yp_ref, yf_ref, dm_ref, dyp_ref, dyf_ref, dzg_ref):
        dm = dm_ref[...].astype(F32)
        zg = zg_ref[...].astype(F32)
        sp = _sigmoid(zg[:, :D])
        sf = _sigmoid(zg[:, D:])
        dyp_ref[...] = (dm * sp).astype(BF16)
        dyf_ref[...] = (dm * sf).astype(BF16)
        dzg_ref[:, :D] = (dm * yp_ref[...] * (sp * (1.0 - sp))).astype(BF16)
        dzg_ref[:, D:] = (dm * yf_ref[...] * (sf * (1.0 - sf))).astype(BF16)

    row = pl.BlockSpec((ts, D), lambda i: (i, 0))
    wide = pl.BlockSpec((ts, 2 * D), lambda i: (i, 0))
    return pl.pallas_call(
        body, name=name,
        out_shape=(jax.ShapeDtypeStruct((S, D), BF16), jax.ShapeDtypeStruct((S, D), BF16),
                   jax.ShapeDtypeStruct((S, 2 * D), BF16)),
        grid=(S // ts,), in_specs=[wide, row, row, row], out_specs=(row, row, wide),
        compiler_params=_cparams("parallel"),
    )(zg, yp, yf, dmerged)


NEG_BIG = -1e30


FOX_BLOCK = 1024
PAIR = LANES // FOX_DH
N_PAIRS = FOX_HEADS // PAIR


def _fox_fwd(zqkv, augq, augk, *, name):
    S = zqkv.shape[0]
    bq = _pick(S, FOX_BLOCK, 128)
    nq = S // bq
    scale = 1.0 / math.sqrt(FOX_DH)

    def body(q_ref, k_ref, v_ref, aq_ref, ak_ref, o_ref, lse_ref):
        i = pl.program_id(1)
        lane = lax.broadcasted_iota(jnp.int32, (1, LANES), 1)
        first = lane < FOX_DH
        q2 = q_ref[...] * scale
        zero = jnp.zeros_like(q2)
        qh = (jnp.concatenate([jnp.where(first, q2, zero), aq_ref[0]], axis=1),
              jnp.concatenate([jnp.where(first, zero, q2), aq_ref[1]], axis=1))

        def step(j, carry, masked):
            start = pl.multiple_of(j * bq, bq)
            kb = k_ref[pl.ds(start, bq), :]
            vb = v_ref[pl.ds(start, bq), :]
            one = jnp.ones_like(vb)
            vh = (jnp.where(first, vb, one), jnp.where(first, one, vb))
            out = []
            for h in range(PAIR):
                m, acc = carry[h]
                kh = jnp.concatenate([kb, ak_ref[h, pl.ds(start, bq), :]], axis=1)
                s = lax.dot_general(qh[h], kh, (((1,), (1,)), ((), ())), preferred_element_type=F32)
                if masked:
                    r = lax.broadcasted_iota(jnp.int32, (bq, bq), 0)
                    c = lax.broadcasted_iota(jnp.int32, (bq, bq), 1)
                    s = jnp.where(c <= r, s, NEG_BIG)
                m_new = jnp.maximum(m, jnp.max(s, axis=-1, keepdims=True))
                alpha = jnp.exp(m - m_new)
                p = jnp.exp(s - m_new).astype(BF16)
                acc = alpha * acc + jnp.dot(p, vh[h], preferred_element_type=F32)
                out.append((m_new, acc))
            return tuple(out)

        init = tuple((jnp.full((bq, 1), NEG_BIG, F32), jnp.zeros((bq, LANES), F32)) for _ in range(PAIR))
        carry = lax.fori_loop(0, i, lambda j, c: step(j, c, False), init)
        (ma, acca), (mb, accb) = step(i, carry, True)
        num = jnp.where(first, acca, accb)
        den = jnp.where(first, pltpu.roll(acca, FOX_DH, 1), pltpu.roll(accb, FOX_DH, 1))
        o_ref[...] = num / den
        lse_t = (jnp.where(first, mb, ma) + jnp.log(jnp.where(first, accb, acca))).T
        lse_ref[0] = lse_t[FOX_DH:FOX_DH + 1, :]
        lse_ref[1] = lse_t[0:1, :]

    npair = N_PAIRS
    return pl.pallas_call(
        body, name=name,
        out_shape=(jax.ShapeDtypeStruct((S, FOX_W), F32), jax.ShapeDtypeStruct((FOX_HEADS, 1, S), F32)),
        grid=(npair, nq),
        in_specs=[pl.BlockSpec((bq, LANES), lambda hp, i: (i, hp)),
                  pl.BlockSpec((S, LANES), lambda hp, i: (0, npair + hp)),
                  pl.BlockSpec((S, LANES), lambda hp, i: (0, 2 * npair + hp)),
                  pl.BlockSpec((PAIR, bq, LANES), lambda hp, i: (hp, i, 0)),
                  pl.BlockSpec((PAIR, S, LANES), lambda hp, i: (hp, 0, 0))],
        out_specs=(pl.BlockSpec((bq, LANES), lambda hp, i: (i, hp)),
                   pl.BlockSpec((PAIR, 1, bq), lambda hp, i: (hp, 0, i))),
        compiler_params=_cparams("parallel", "arbitrary"),
    )(zqkv, zqkv, zqkv, augq, augk)


def _head_rowsum(a, b, *, name):
    S, W = a.shape
    ts = _pick(S, 1024, 8)

    def body(a_ref, b_ref, o_ref):
        prod = a_ref[...].astype(F32) * b_ref[...].astype(F32)
        hi = prod.astype(BF16)
        lo = (prod - hi.astype(F32)).astype(BF16)
        r = lax.broadcasted_iota(jnp.int32, (W, LANES), 0)
        c = lax.broadcasted_iota(jnp.int32, (W, LANES), 1)
        sel = jnp.where(r // FOX_DH == c, 1.0, 0.0).astype(BF16)
        d_t = (jnp.dot(hi, sel, preferred_element_type=F32) + jnp.dot(lo, sel, preferred_element_type=F32)).T
        for h in range(FOX_HEADS):
            o_ref[h] = d_t[h:h + 1, :]

    return pl.pallas_call(
        body, name=name, out_shape=jax.ShapeDtypeStruct((FOX_HEADS, 1, S), F32), grid=(S // ts,),
        in_specs=[pl.BlockSpec((ts, W), lambda i: (i, 0)), pl.BlockSpec((ts, W), lambda i: (i, 0))],
        out_specs=pl.BlockSpec((FOX_HEADS, 1, ts), lambda i: (0, 0, i)),
        compiler_params=_cparams("parallel"),
    )(a, b)


def _fox_bwd(zqkv, do, augq, augk, lse_row, delta_row, *, name):
    S = zqkv.shape[0]
    bk = _pick(S, FOX_BLOCK, 128)
    nk = S // bk
    bq = bk // 2
    nq = S // bq
    scale = 1.0 / math.sqrt(FOX_DH)
    npair = N_PAIRS

    def body(q_ref, k_ref, v_ref, do_ref, ak_ref, aq_ref, lse_ref, dl_ref, dq_ref, dk_ref, dv_ref, dck_ref, dcq_ref):
        j = pl.program_id(1)

        @pl.when(j == 0)
        def _():
            dq_ref[...] = jnp.zeros_like(dq_ref)
            dcq_ref[...] = jnp.zeros_like(dcq_ref)

        lane = lax.broadcasted_iota(jnp.int32, (1, LANES), 1)
        first = lane < FOX_DH
        kb = k_ref[...]
        vb = v_ref[...]
        kh = (jnp.concatenate([kb, ak_ref[0]], axis=1), jnp.concatenate([kb, ak_ref[1]], axis=1))

        def step(i, carry, masked):
            start = pl.multiple_of(i * bq, bq)
            qs = q_ref[pl.ds(start, bq), :] * scale
            dob = do_ref[pl.ds(start, bq), :]
            zero = jnp.zeros_like(qs)
            qh = (jnp.where(first, qs, zero), jnp.where(first, zero, qs))
            doh = (jnp.where(first, dob, zero), jnp.where(first, zero, dob))
            out = []
            dqc = []
            for h in range(PAIR):
                dk, dv, dc = carry[h]
                qaug = jnp.concatenate([qh[h], aq_ref[h, pl.ds(start, bq), :]], axis=1)
                lse = lse_ref[h, :, pl.ds(start, bq)]
                dl = dl_ref[h, :, pl.ds(start, bq)]
                st = lax.dot_general(kh[h], qaug, (((1,), (1,)), ((), ())), preferred_element_type=F32)
                pt = jnp.exp(st - lse)
                if masked:
                    r = lax.broadcasted_iota(jnp.int32, (bk, bq), 0) + j * bk
                    c = lax.broadcasted_iota(jnp.int32, (bk, bq), 1) + i * bq
                    pt = jnp.where(c >= r, pt, 0.0)
                dpt = lax.dot_general(vb, doh[h], (((1,), (1,)), ((), ())), preferred_element_type=F32)
                dst = pt * (dpt - dl)
                pt_bf = pt.astype(BF16)
                dst_bf = dst.astype(BF16)
                dv = dv + jnp.dot(pt_bf, dob, preferred_element_type=F32)
                dk = dk + jnp.dot(dst_bf, qs, preferred_element_type=F32)
                dc = dc - jnp.sum(dst, axis=-1, keepdims=True)
                dcq_ref[h, :, pl.ds(start, bq)] += jnp.sum(dst, axis=0, keepdims=True)
                dqc.append(lax.dot_general(dst_bf, kb, (((0,), (0,)), ((), ())), preferred_element_type=F32))
                out.append((dk, dv, dc))
            dq_ref[pl.ds(start, bq), :] += jnp.where(first, dqc[0], dqc[1])
            return tuple(out)

        init = tuple((jnp.zeros((bk, LANES), F32), jnp.zeros((bk, LANES), F32), jnp.zeros((bk, 1), F32))
                     for _ in range(PAIR))
        carry = step(2 * j + 1, step(2 * j, init, True), True)
        (dka, dva, dca), (dkb, dvb, dcb) = lax.fori_loop(2 * j + 2, nq, lambda i, c: step(i, c, False), carry)
        dk_ref[...] = jnp.where(first, dka, dkb).astype(BF16)
        dv_ref[...] = jnp.where(first, dva, dvb).astype(BF16)
        dck_ref[0] = jnp.broadcast_to(dca, (bk, LANES)).T[0:1, :]
        dck_ref[1] = jnp.broadcast_to(dcb, (bk, LANES)).T[0:1, :]

        @pl.when(j == nk - 1)
        def _():
            dq_ref[...] = dq_ref[...] * scale

    rowfull = pl.BlockSpec((PAIR, 1, S), lambda hp, j: (hp, 0, 0))
    return pl.pallas_call(
        body, name=name,
        out_shape=(jax.ShapeDtypeStruct((S, FOX_W), F32), jax.ShapeDtypeStruct((S, FOX_W), BF16),
                   jax.ShapeDtypeStruct((S, FOX_W), BF16), jax.ShapeDtypeStruct((FOX_HEADS, 1, S), F32),
                   jax.ShapeDtypeStruct((FOX_HEADS, 1, S), F32)),
        grid=(npair, nk),
        in_specs=[pl.BlockSpec((S, LANES), lambda hp, j: (0, hp)),
                  pl.BlockSpec((bk, LANES), lambda hp, j: (j, npair + hp)),
                  pl.BlockSpec((bk, LANES), lambda hp, j: (j, 2 * npair + hp)),
                  pl.BlockSpec((S, LANES), lambda hp, j: (0, hp)),
                  pl.BlockSpec((PAIR, bk, LANES), lambda hp, j: (hp, j, 0)),
                  pl.BlockSpec((PAIR, S, LANES), lambda hp, j: (hp, 0, 0)), rowfull, rowfull],
        out_specs=(pl.BlockSpec((S, LANES), lambda hp, j: (0, hp)),
                   pl.BlockSpec((bk, LANES), lambda hp, j: (j, hp)),
                   pl.BlockSpec((bk, LANES), lambda hp, j: (j, hp)),
                   pl.BlockSpec((PAIR, 1, bk), lambda hp, j: (hp, 0, j)), rowfull),
        compiler_params=_cparams("parallel", "arbitrary"),
    )(zqkv, zqkv, zqkv, do, augk, augq, lse_row, delta_row)


def _xattn_fwd(q, kv, *, name):
    S, W = q.shape
    M = kv.shape[0]
    tq = _pick(S, 512, 8)
    scale = 1.0 / math.sqrt(X_DH)

    def body(q_ref, kv_ref, o_ref):
        for h in range(X_HEADS):
            cols = slice(h * X_DH, (h + 1) * X_DH)
            vcols = slice(W + h * X_DH, W + (h + 1) * X_DH)
            s = lax.dot_general(q_ref[:, cols], kv_ref[:, cols], (((1,), (1,)), ((), ())),
                                preferred_element_type=F32) * scale
            e = jnp.exp(s - jnp.max(s, axis=-1, keepdims=True))
            p = e / jnp.sum(e, axis=-1, keepdims=True)
            o_ref[:, cols] = jnp.dot(p.astype(BF16), kv_ref[:, vcols], preferred_element_type=F32).astype(BF16)

    return pl.pallas_call(
        body, name=name, out_shape=jax.ShapeDtypeStruct((S, W), BF16), grid=(S // tq,),
        in_specs=[pl.BlockSpec((tq, W), lambda i: (i, 0)), pl.BlockSpec((M, 2 * W), lambda i: (0, 0))],
        out_specs=pl.BlockSpec((tq, W), lambda i: (i, 0)),
        compiler_params=_cparams("parallel"),
    )(q, kv)


def _xattn_bwd(q, kv, do, *, name):
    S, W = q.shape
    M = kv.shape[0]
    tq = _pick(S, 512, 8)
    scale = 1.0 / math.sqrt(X_DH)

    def body(q_ref, kv_ref, do_ref, dq_ref, dkv_ref):
        i = pl.program_id(0)

        @pl.when(i == 0)
        def _():
            dkv_ref[...] = jnp.zeros_like(dkv_ref)

        for h in range(X_HEADS):
            cols = slice(h * X_DH, (h + 1) * X_DH)
            vcols = slice(W + h * X_DH, W + (h + 1) * X_DH)
            qh = q_ref[:, cols]
            kh = kv_ref[:, cols]
            vh = kv_ref[:, vcols]
            doh = do_ref[:, cols]
            s = lax.dot_general(qh, kh, (((1,), (1,)), ((), ())), preferred_element_type=F32) * scale
            e = jnp.exp(s - jnp.max(s, axis=-1, keepdims=True))
            p = e / jnp.sum(e, axis=-1, keepdims=True)
            dp = lax.dot_general(doh, vh, (((1,), (1,)), ((), ())), preferred_element_type=F32)
            ds = (p * (dp - jnp.sum(p * dp, axis=-1, keepdims=True)) * scale).astype(BF16)
            dq_ref[:, cols] = jnp.dot(ds, kh, preferred_element_type=F32).astype(BF16)
            dkv_ref[:, cols] += lax.dot_general(ds, qh, (((0,), (0,)), ((), ())), preferred_element_type=F32)
            dkv_ref[:, vcols] += lax.dot_general(p.astype(BF16), doh, (((0,), (0,)), ((), ())),
                                                 preferred_element_type=F32)

    return pl.pallas_call(
        body, name=name,
        out_shape=(jax.ShapeDtypeStruct((S, W), BF16), jax.ShapeDtypeStruct((M, 2 * W), F32)),
        grid=(S // tq,),
        in_specs=[pl.BlockSpec((tq, W), lambda i: (i, 0)), pl.BlockSpec((M, 2 * W), lambda i: (0, 0)),
                  pl.BlockSpec((tq, W), lambda i: (i, 0))],
        out_specs=(pl.BlockSpec((tq, W), lambda i: (i, 0)), pl.BlockSpec((M, 2 * W), lambda i: (0, 0))),
        compiler_params=_cparams("arbitrary"),
    )(q, kv, do)


GELU_C = math.sqrt(2.0 / math.pi)
GELU_A = 0.044715
CONV_HALO = 16


def _gelu_parts(x):
    u = GELU_C * (x + GELU_A * x * x * x)
    t = jnp.tanh(u)
    g = 0.5 * x * (1.0 + t)
    dg = 0.5 * (1.0 + t) + 0.5 * x * (1.0 - t * t) * (GELU_C * (1.0 + 3.0 * GELU_A * x * x))
    return g, dg


def _conv3(ext, w_ref, b_ref):
    return (w_ref[2:3, :] * ext + w_ref[1:2, :] * pltpu.roll(ext, 1, 0)
            + w_ref[0:1, :] * pltpu.roll(ext, 2, 0) + b_ref[...])


def _convglu_fwd(z, conv_w, conv_b, *, name):
    S, F2 = z.shape
    F = F2 // 2
    tc = _pick(F, 1408)
    ncol = F // tc
    T = _pick(S, 512, 8)
    hb = T // CONV_HALO

    def body(zg_ref, zu_ref, zgp_ref, zup_ref, wg_ref, wu_ref, bg_ref, bu_ref, act_ref):
        i = pl.program_id(1)
        first = (i > 0).astype(F32)

        def conv(z_ref, zp_ref, w_ref, b_ref):
            ext = jnp.concatenate([zp_ref[...].astype(F32) * first, z_ref[...].astype(F32)], axis=0)
            return _conv3(ext, w_ref, b_ref)[CONV_HALO:, :]

        gc = conv(zg_ref, zgp_ref, wg_ref, bg_ref)
        uc = conv(zu_ref, zup_ref, wu_ref, bu_ref)
        act_ref[...] = (_gelu_parts(gc)[0] * uc).astype(BF16)

    cur = lambda off: pl.BlockSpec((T, tc), lambda j, i: (i, j + off))
    prev = lambda off: pl.BlockSpec((CONV_HALO, tc), lambda j, i: (jnp.maximum(i * hb - 1, 0), j + off))
    vec = lambda rows, off: pl.BlockSpec((rows, tc), lambda j, i: (0, j + off))
    return pl.pallas_call(
        body, name=name, out_shape=jax.ShapeDtypeStruct((S, F), BF16), grid=(ncol, S // T),
        in_specs=[cur(0), cur(ncol), prev(0), prev(ncol), vec(3, 0), vec(3, ncol), vec(1, 0), vec(1, ncol)],
        out_specs=pl.BlockSpec((T, tc), lambda j, i: (i, j)),
        compiler_params=_cparams("parallel", "parallel"),
    )(z, z, z, z, conv_w, conv_w, conv_b, conv_b)


def _convglu_bwd(z, dact, conv_w, conv_b, *, name):
    S, F2 = z.shape
    F = F2 // 2
    tc = _pick(F, 1408)
    ncol = F // tc
    T = _pick(S, 256, 8)
    nrow = S // T
    hb = T // CONV_HALO
    TE = T + CONV_HALO

    def body(z_ref, zp_ref, zn_ref, da_ref, dan_ref, w_ref, b_ref, dz_ref, dw_ref):
        i = pl.program_id(0)
        first = (i > 0).astype(F32)
        last = (i < nrow - 1).astype(F32)

        @pl.when(i == 0)
        def _():
            dw_ref[...] = jnp.zeros_like(dw_ref)

        def ext_of(cols):
            return jnp.concatenate([zp_ref[:, cols].astype(F32) * first, z_ref[:, cols].astype(F32),
                                    zn_ref[:, cols].astype(F32)], axis=0)

        def back(d, ext, w, cols):
            dz = w[2:3, :] * d + w[1:2, :] * pltpu.roll(d, TE - 1, 0) + w[0:1, :] * pltpu.roll(d, TE - 2, 0)
            dz_ref[:, cols] = dz[:T, :].astype(BF16)
            dc = d[:T, :]
            z0 = ext[CONV_HALO:CONV_HALO + T, :]
            z1 = pltpu.roll(ext, 1, 0)[CONV_HALO:CONV_HALO + T, :]
            z2 = pltpu.roll(ext, 2, 0)[CONV_HALO:CONV_HALO + T, :]
            rows = [jnp.sum(dc * z2, axis=0, keepdims=True), jnp.sum(dc * z1, axis=0, keepdims=True),
                    jnp.sum(dc * z0, axis=0, keepdims=True), jnp.sum(dc, axis=0, keepdims=True)]
            dw_ref[0:4, cols] += jnp.concatenate(rows, axis=0)

        for jj in range(ncol):
            cg = slice(jj * tc, (jj + 1) * tc)
            cu = slice(F + jj * tc, F + (jj + 1) * tc)
            extg, extu = ext_of(cg), ext_of(cu)
            wg, wu = w_ref[:, cg], w_ref[:, cu]
            gc = _conv3(extg, wg, b_ref[:, cg])[CONV_HALO:, :]
            uc = _conv3(extu, wu, b_ref[:, cu])[CONV_HALO:, :]
            da = jnp.concatenate([da_ref[:, cg].astype(F32), dan_ref[:, cg].astype(F32) * last], axis=0)
            gl, dgl = _gelu_parts(gc)
            back(da * uc * dgl, extg, wg, cg)
            back(da * gl, extu, wu, cu)

    halo_rows = S // CONV_HALO
    return pl.pallas_call(
        body, name=name,
        out_shape=(jax.ShapeDtypeStruct((S, F2), BF16), jax.ShapeDtypeStruct((8, F2), F32)),
        grid=(nrow,),
        in_specs=[pl.BlockSpec((T, F2), lambda i: (i, 0)),
                  pl.BlockSpec((CONV_HALO, F2), lambda i: (jnp.maximum(i * hb - 1, 0), 0)),
                  pl.BlockSpec((CONV_HALO, F2), lambda i: (jnp.minimum((i + 1) * hb, halo_rows - 1), 0)),
                  pl.BlockSpec((T, F), lambda i: (i, 0)),
                  pl.BlockSpec((CONV_HALO, F), lambda i: (jnp.minimum((i + 1) * hb, halo_rows - 1), 0)),
                  pl.BlockSpec((3, F2), lambda i: (0, 0)), pl.BlockSpec((1, F2), lambda i: (0, 0))],
        out_specs=(pl.BlockSpec((T, F2), lambda i: (i, 0)), pl.BlockSpec((8, F2), lambda i: (0, 0))),
        compiler_params=_cparams("arbitrary"),
    )(z, z, z, dact, dact, conv_w, conv_b)


OFF_QKV = POOL_W
OFF_G = POOL_W + 3 * FOX_W


SHARD_BY_ROWS = {"w_in": False, "w_pool_br": False, "w_fox_br": False, "w_mix_out": True, "w_xq": True,
                 "w_xkv": True, "w_xo": False, "w_up": False, "w_down": True, "conv_w": False}


def _prep_layer_weights(w, l):
    n, _, D, b = w["w_in"].shape
    w_in = w["w_in"][:, l].transpose(1, 0, 2).reshape(D, n * b)
    off_f = OFF_G
    pad = jnp.zeros((D, LANES - N_FGATE), w_in.dtype)
    w_in_r = jnp.concatenate([w_in[:, :off_f], w_in[:, off_f + N_FGATE:], w_in[:, off_f:off_f + N_FGATE], pad], axis=1)
    wl = {k: _ChipMajor(w[k], l, SHARD_BY_ROWS[k]) for k in MATMUL_WEIGHTS if k != "w_in"}
    wl["w_in_r"] = w_in_r
    return wl


def _row(v):
    return v.reshape(1, -1)


def _layer_fwd(x, h1, mem, wl, p, l):
    S, D = x.shape
    n = lambda s: f"l{l}_{s}"
    sv = {"x0": x}
    w_in_r = wl["w_in_r"]
    og = OFF_G
    zu = _matmul(h1, w_in_r[:, :OFF_QKV], out_dtype=F32, name=n("mm_zu"))
    zqkv = _matmul(h1, w_in_r[:, OFF_QKV:og], out_dtype=BF16, name=n("mm_zqkv"))
    zg = _matmul(h1, w_in_r[:, og:og + 2 * D], out_dtype=BF16, name=n("mm_zg"))
    zf = _matmul(h1, w_in_r[:, og + 2 * D:], out_dtype=F32, name=n("mm_zf"))
    bf = jnp.pad(p["b_forget"][l], (0, LANES - N_FGATE)).reshape(1, LANES)
    augq, augk = _fgate_fwd(zf, bf, name=n("fgate"))
    o_fox, lse = _fox_fwd(zqkv, augq, augk, name=n("fox"))
    pooled, mixed = _pool_fwd(zu, p["pool_w"][l], _row(p["pool_scale"][l]), name=n("pool"))
    y_pool = _matmul(mixed, wl["w_pool_br"], out_dtype=BF16, name=n("mm_ypool"))
    y_fox = _matmul(o_fox, wl["w_fox_br"], out_dtype=BF16, name=n("mm_yfox"))
    merged = _merge_fwd(zg, y_pool, y_fox, name=n("merge"))
    r1 = _matmul(merged, wl["w_mix_out"], out_dtype=BF16, name=n("mm_r1"))
    x1, h2 = _add_rms(x, r1, _row(p["mix_post_g"][l]), _row(p["xa_pre_g"][l]), name=n("addrms1"))
    sv.update(h1=h1, zg=zg, zf=zf, bf=bf, zqkv=zqkv, augq=augq, augk=augk, o_fox=o_fox, lse=lse,
              pooled=pooled, mixed=mixed, y_pool=y_pool, y_fox=y_fox, merged=merged, r1=r1, x1=x1)
    mem_n = _rms_fwd(mem, _row(p["mem_g"][l]), out_dtype=BF16, name=n("rms_mem"))
    q2 = _matmul(h2, wl["w_xq"], out_dtype=BF16, name=n("mm_q2"))
    kv = _matmul(mem_n, wl["w_xkv"], out_dtype=BF16, name=n("mm_kv"))
    o2 = _xattn_fwd(q2, kv, name=n("xattn"))
    a2 = _matmul(o2, wl["w_xo"], out_dtype=BF16, name=n("mm_a2"))
    x2, h3 = _add_rms(x1, a2, _row(p["xa_post_g"][l]), _row(p["ffn_pre_g"][l]), name=n("addrms2"))
    sv.update(h2=h2, mem_n=mem_n, q2=q2, kv=kv, o2=o2, a2=a2, x2=x2)
    z3 = _matmul(h3, wl["w_up"], out_dtype=BF16, name=n("mm_z3"))
    act = _convglu_fwd(z3, p["conv_w"][l], _row(p["conv_b"][l]), name=n("convglu"))
    d3 = _matmul(act, wl["w_down"], out_dtype=BF16, name=n("mm_d3"))
    if l + 1 < p["mix_pre_g"].shape[0]:
        x3, h_next = _add_rms(x2, d3, _row(p["ffn_post_g"][l]), _row(p["mix_pre_g"][l + 1]), name=n("addrms3"))
    else:
        x3, h_next = _add_rms(x2, d3, _row(p["ffn_post_g"][l]), name=n("addrms3")), None
    sv.update(h3=h3, z3=z3, act=act, d3=d3)
    return x3, h_next, sv


def _layer_bwd(dx, mem, wl, p, l, sv, gbuf):
    S, D = dx.shape
    n_layers = p["mix_pre_g"].shape[0]
    n = lambda s: f"l{l}_b_{s}"
    g = {}
    red = lambda part: jnp.sum(part, axis=0)

    def wgrad(k, lhs, rhs, nm):
        how = dict(out_rows=N_CHIPS) if SHARD_BY_ROWS[k] else dict(out_chips=N_CHIPS)
        return _matmul(lhs, rhs, ta=True, out_dtype=F32, out_layer=(gbuf.get(k), l, n_layers), name=n(nm), **how)

    dd3, dg = _rms_bwd(sv["d3"], _row(p["ffn_post_g"][l]), dx, out_dtype=BF16, name=n("rms3post"))
    g["ffn_post_g"] = red(dg)
    dact = _matmul(dd3, wl["w_down"], tb=True, out_dtype=BF16, name=n("mm_dact"))
    g["w_down"] = wgrad("w_down", sv["act"], dd3, "mm_dwdown")
    dz3, dconv = _convglu_bwd(sv["z3"], dact, p["conv_w"][l], _row(p["conv_b"][l]), name=n("convglu"))
    g["conv_w"] = dconv[:3]
    g["conv_b"] = dconv[3]
    dh3 = _matmul(dz3, wl["w_up"], tb=True, out_dtype=BF16, name=n("mm_dh3"))
    g["w_up"] = wgrad("w_up", sv["h3"], dz3, "mm_dwup")
    dx, dg = _rms_bwd(sv["x2"], _row(p["ffn_pre_g"][l]), dh3, dx, out_dtype=F32, name=n("rms3pre"))
    g["ffn_pre_g"] = red(dg)
    da2, dg = _rms_bwd(sv["a2"], _row(p["xa_post_g"][l]), dx, out_dtype=BF16, name=n("rms2post"))
    g["xa_post_g"] = red(dg)
    do2 = _matmul(da2, wl["w_xo"], tb=True, out_dtype=BF16, name=n("mm_do2"))
    g["w_xo"] = wgrad("w_xo", sv["o2"], da2, "mm_dwxo")
    dq2, dkv = _xattn_bwd(sv["q2"], sv["kv"], do2, name=n("xattn"))
    dh2 = _matmul(dq2, wl["w_xq"], tb=True, out_dtype=BF16, name=n("mm_dh2"))
    g["w_xq"] = wgrad("w_xq", sv["h2"], dq2, "mm_dwxq")
    dmem_n = _matmul(dkv, wl["w_xkv"], tb=True, out_dtype=F32, name=n("mm_dmemn"))
    g["w_xkv"] = wgrad("w_xkv", sv["mem_n"], dkv, "mm_dwxkv")
    _, dg = _rms_bwd(mem, _row(p["mem_g"][l]), dmem_n, out_dtype=BF16, name=n("rms_mem"))
    g["mem_g"] = red(dg)
    dx, dg = _rms_bwd(sv["x1"], _row(p["xa_pre_g"][l]), dh2, dx, out_dtype=F32, name=n("rms2pre"))
    g["xa_pre_g"] = red(dg)
    dr1, dg = _rms_bwd(sv["r1"], _row(p["mix_post_g"][l]), dx, out_dtype=BF16, name=n("rms1post"))
    g["mix_post_g"] = red(dg)
    dmerged = _matmul(dr1, wl["w_mix_out"], tb=True, out_dtype=BF16, name=n("mm_dmerged"))
    g["w_mix_out"] = wgrad("w_mix_out", sv["merged"], dr1, "mm_dwmo")
    dyp, dyf, dzg = _merge_bwd(sv["zg"], sv["y_pool"], sv["y_fox"], dmerged, name=n("merge"))
    dmixed = _matmul(dyp, wl["w_pool_br"], tb=True, out_dtype=F32, name=n("mm_dmixed"))
    g["w_pool_br"] = wgrad("w_pool_br", sv["mixed"], dyp, "mm_dwpb")
    dofox = _matmul(dyf, wl["w_fox_br"], tb=True, out_dtype=F32, name=n("mm_dofox"))
    g["w_fox_br"] = wgrad("w_fox_br", sv["o_fox"], dyf, "mm_dwfb")
    dzu, dpw, dsc = _pool_bwd(sv["pooled"], dmixed, p["pool_w"][l], _row(p["pool_scale"][l]), name=n("pool"))
    g["pool_w"] = dpw
    g["pool_scale"] = red(dsc)
    delta = _head_rowsum(dofox, sv["o_fox"], name=n("delta"))
    dq, dk, dv, dck, dcq = _fox_bwd(sv["zqkv"], dofox.astype(BF16), sv["augq"], sv["augk"], sv["lse"], delta,
                                    name=n("fox"))
    dc_pad = jnp.pad((dcq + dck).reshape(FOX_HEADS, S).T, ((0, 0), (0, LANES - FOX_HEADS)))
    dzf, db = _fgate_bwd(sv["zf"], sv["bf"], dc_pad, name=n("fgate"))
    g["b_forget"] = red(db)[:N_FGATE]
    dz_cat = jnp.concatenate([dzu, dq.astype(BF16), dk, dv, dzg, dzf], axis=1)
    dh1 = _matmul(dz_cat, wl["w_in_r"], tb=True, out_dtype=BF16, name=n("mm_dh1"))
    dw_in_r = _matmul(sv["h1"], dz_cat, ta=True, out_dtype=F32, name=n("mm_dwin"))
    og = OFF_G
    g["w_in"] = jnp.concatenate([dw_in_r[:, :og], dw_in_r[:, og + 2 * D:og + 2 * D + N_FGATE],
                                 dw_in_r[:, og:og + 2 * D]], axis=1)
    dx, dg = _rms_bwd(sv["x0"], _row(p["mix_pre_g"][l]), dh1, dx, out_dtype=F32, name=n("rms1pre"))
    g["mix_pre_g"] = red(dg)
    return dx, g


MATMUL_WEIGHTS = ("w_in", "w_pool_br", "w_fox_br", "w_mix_out", "w_xq", "w_xkv", "w_xo", "w_up", "w_down")
WEIGHT_NAMES = ("mix_pre_g", "mix_post_g", "w_in", "b_forget", "pool_w", "pool_scale", "w_pool_br", "w_fox_br",
                "w_mix_out", "xa_pre_g", "xa_post_g", "mem_g", "w_xq", "w_xkv", "w_xo", "ffn_pre_g", "ffn_post_g",
                "w_up", "conv_w", "conv_b", "w_down")


def _local_step(x, mem, loss_target, wfull, p):
    L = p["mix_pre_g"].shape[0]
    saved, wls = [], []
    h = x
    hn = _rms_fwd(x, _row(p["mix_pre_g"][0]), out_dtype=BF16, name="rms_first")
    for l in range(L):
        wl = _prep_layer_weights(wfull, l)
        h, hn, sv = _layer_fwd(h, hn, mem, wl, p, l)
        saved.append(sv)
        wls.append(wl)
    D = x.shape[1]
    dy, sq = _loss_head(h, loss_target, name="loss_head")
    loss = 0.5 * jnp.sum(sq) / D
    grads = []
    gbuf = {}
    dx = dy
    for l in reversed(range(L)):
        dx, g = _layer_bwd(dx, mem, wls[l], p, l, saved[l], gbuf)
        gbuf = {k: g[k] for k in MATMUL_WEIGHTS if k != "w_in"}
        grads.append(g)
    grads = grads[::-1]

    def chip_major(g):
        return g.reshape(g.shape[0], N_CHIPS, g.shape[1] // N_CHIPS).transpose(1, 0, 2)

    gfull = {k: jnp.stack([grads[l][k] for l in range(L)]) for k in REPLICATED}
    gfull.update({k: jnp.stack([chip_major(grads[l][k]) for l in range(L)], axis=1) for k in ("w_in", "conv_w")})
    gfull.update(gbuf)
    return loss, dx, gfull


PACK_W = 512
N_CHIPS = 4
N_DEV = 8
REPLICATED = ("mix_pre_g", "mix_post_g", "b_forget", "pool_w", "pool_scale", "xa_pre_g", "xa_post_g", "mem_g",
              "ffn_pre_g", "ffn_post_g", "conv_b")


def _round_up(n, m):
    return -(-n // m) * m


def _pack(arrs, rows):
    flat = jnp.concatenate([a.reshape(-1) for a in arrs])
    return jnp.pad(flat, (0, rows * PACK_W - flat.shape[0])).reshape(rows, PACK_W)


def _unpack(buf, shapes):
    flat = buf.reshape(-1)
    out, off = [], 0
    for s in shapes:
        n = math.prod(s)
        out.append(flat[off:off + n].reshape(s))
        off += n
    return out


ANY = pl.BlockSpec(memory_space=pl.ANY)


def _remote(send_sems, recv_sems, k, src, dst, to):
    return pltpu.make_async_remote_copy(src_ref=src, dst_ref=dst, send_sem=send_sems.at[k], recv_sem=recv_sems.at[k],
                                        device_id=to, device_id_type=MESH)


def _my_place():
    return lax.axis_index("x"), lax.axis_index("y"), lax.axis_index("c")


def _rows_per_block(a, b, cap_bytes=1024 * 1024):
    if a % 8:
        return a
    return _pick(a, max(8, cap_bytes // (4 * b) // 8 * 8), 8)


def _place_shard(w, chip, dtype, *, name):
    L, a, b = w.shape
    ta = _rows_per_block(a, b)

    def body(chip_ref, w_ref, o_ref):
        o_ref[...] = w_ref[...].astype(dtype)

    return pl.pallas_call(
        body, name=name, out_shape=jax.ShapeDtypeStruct((N_CHIPS, L, a, b), dtype),
        grid_spec=pltpu.PrefetchScalarGridSpec(
            num_scalar_prefetch=1, grid=(L, a // ta),
            in_specs=[pl.BlockSpec((None, ta, b), lambda l, i, chip_ref: (l, i, 0))],
            out_specs=pl.BlockSpec((None, None, ta, b), lambda l, i, chip_ref: (chip_ref[0], l, i, 0))),
        compiler_params=_cparams("parallel", "parallel"),
    )(chip, w)


def _layer_halves(L, c):
    assert L % 2 == 0
    return pl.ds(c * (L // 2), L // 2), pl.ds((1 - c) * (L // 2), L // 2)


def _gather_weights(bufs):
    n = len(bufs)
    L = bufs[0].shape[1]

    def body(*refs):
        outs, (send_sems, recv_sems) = refs[n:2 * n], refs[2 * n:]
        x, y, c = _my_place()
        me = 2 * x + y
        sibling = (x, y, 1 - c)
        chips = [(1 - x, y), (x, 1 - y), (1 - x, 1 - y)]
        half, other = _layer_halves(L, c)
        rc = functools.partial(_remote, send_sems, recv_sems)
        first = [rc(6 * w + k, o.at[me, half], o.at[me, half], (px, py, c))
                 for w, o in enumerate(outs) for k, (px, py) in enumerate(chips)]
        for cp in first:
            cp.start()
        passed = []
        for k, (px, py) in enumerate(chips):
            src = 2 * px + py
            for w, o in enumerate(outs):
                rc(6 * w + k, o.at[src, half], o.at[src, half], (px, py, c)).wait_recv()
                fwd = rc(6 * w + 3 + k, o.at[src, half], o.at[src, half], sibling)
                fwd.start()
                passed.append(fwd)
        for k, (px, py) in enumerate(chips):
            src = 2 * px + py
            for w, o in enumerate(outs):
                rc(6 * w + 3 + k, o.at[src, other], o.at[src, other], sibling).wait_recv()
        for cp in first + passed:
            cp.wait_send()

    return pl.pallas_call(
        body, name="gather_weights",
        out_shape=tuple(jax.ShapeDtypeStruct(b.shape, b.dtype) for b in bufs),
        in_specs=[ANY] * n, out_specs=tuple([ANY] * n), input_output_aliases={i: i for i in range(n)},
        scratch_shapes=[pltpu.SemaphoreType.DMA((6 * n,)), pltpu.SemaphoreType.DMA((6 * n,))],
    )(*bufs)


def _exchange_halves(Gs, rep):
    n = len(Gs)
    L = Gs[0].shape[1]
    RR, W = rep.shape

    def body(*refs):
        g_refs, rep_ref, ra_refs, rall_ref = refs[:n], refs[n], refs[n + 1:2 * n + 1], refs[2 * n + 1]
        send_sems, recv_sems, local_sem = refs[2 * n + 2:]
        x, y, c = _my_place()
        me = 4 * x + 2 * y + c
        rc = functools.partial(_remote, send_sems, recv_sems)
        _, other = _layer_halves(L, c)

        def peer(idx):
            px = (1 - x) if (idx >> 2) & 1 else x
            py = (1 - y) if (idx >> 1) & 1 else y
            pc = (1 - c) if idx & 1 else c
            return px, py, pc

        loc = pltpu.make_async_copy(rep_ref, rall_ref.at[me], local_sem)
        loc.start()
        cps = [rc(N_DEV + w, g.at[:, other], ra, (x, y, 1 - c)) for w, (g, ra) in enumerate(zip(g_refs, ra_refs))]
        for idx in range(1, N_DEV):
            cps.append(rc(idx, rep_ref, rall_ref.at[me], peer(idx)))
        for cp in cps:
            cp.start()
        for w, (g, ra) in enumerate(zip(g_refs, ra_refs)):
            rc(N_DEV + w, g.at[:, other], ra, (x, y, 1 - c)).wait_recv()
        for idx in range(1, N_DEV):
            px, py, pc = peer(idx)
            rc(idx, rep_ref, rall_ref.at[4 * px + 2 * py + pc], (px, py, pc)).wait_recv()
        for cp in cps:
            cp.wait_send()
        loc.wait()

    halves = tuple(jax.ShapeDtypeStruct((g.shape[0], L // 2) + g.shape[2:], F32) for g in Gs)
    out = pl.pallas_call(
        body, name="exchange_halves",
        out_shape=halves + (jax.ShapeDtypeStruct((N_DEV, RR, W), F32),),
        in_specs=[ANY] * (n + 1), out_specs=tuple([ANY] * (n + 1)),
        scratch_shapes=[pltpu.SemaphoreType.DMA((N_DEV + n,)), pltpu.SemaphoreType.DMA((N_DEV + n,)),
                        pltpu.SemaphoreType.DMA],
    )(*Gs, rep)
    return out[:n], out[n]


def _exchange_chips(As):
    n = len(As)

    def body(*refs):
        a_refs, rb_refs, (send_sems, recv_sems) = refs[:n], refs[n:2 * n], refs[2 * n:]
        x, y, c = _my_place()
        me = 2 * x + y
        chips = [(1 - x, y), (x, 1 - y), (1 - x, 1 - y)]
        rc = functools.partial(_remote, send_sems, recv_sems)
        cps = [rc(3 * w + k, a.at[2 * px + py], rb.at[k], (px, py, c))
               for w, (a, rb) in enumerate(zip(a_refs, rb_refs)) for k, (px, py) in enumerate(chips)]
        for cp in cps:
            cp.start()
        for w, (a, rb) in enumerate(zip(a_refs, rb_refs)):
            for k, (px, py) in enumerate(chips):
                rc(3 * w + k, a.at[me], rb.at[k], (px, py, c)).wait_recv()
        for cp in cps:
            cp.wait_send()

    return pl.pallas_call(
        body, name="exchange_chips",
        out_shape=tuple(jax.ShapeDtypeStruct((N_CHIPS - 1,) + a.shape[1:], a.dtype) for a in As),
        in_specs=[ANY] * n, out_specs=tuple([ANY] * n),
        scratch_shapes=[pltpu.SemaphoreType.DMA((3 * n,)), pltpu.SemaphoreType.DMA((3 * n,))],
    )(*As)


def _exchange_sibling(gs):
    n = len(gs)
    L = gs[0].shape[0]

    def body(*refs):
        g_refs, (send_sems, recv_sems) = refs[n:2 * n], refs[2 * n:]
        x, y, c = _my_place()
        half, other = _layer_halves(L, c)
        rc = functools.partial(_remote, send_sems, recv_sems)
        cps = [rc(w, g.at[half], g.at[half], (x, y, 1 - c)) for w, g in enumerate(g_refs)]
        for cp in cps:
            cp.start()
        for w, g in enumerate(g_refs):
            rc(w, g.at[other], g.at[other], (x, y, 1 - c)).wait_recv()
        for cp in cps:
            cp.wait_send()

    return pl.pallas_call(
        body, name="exchange_sibling", out_shape=tuple(jax.ShapeDtypeStruct(g.shape, F32) for g in gs),
        in_specs=[ANY] * n, out_specs=tuple([ANY] * n), input_output_aliases={i: i for i in range(n)},
        scratch_shapes=[pltpu.SemaphoreType.DMA((n,)), pltpu.SemaphoreType.DMA((n,))],
    )(*gs)


def _add_halves(G, recv, core, *, name):
    n, L, a, b = G.shape
    Lh = L // 2
    ta = _rows_per_block(a, b)

    def body(core_ref, g_ref, r_ref, o_ref, o16_ref):
        s = g_ref[...] + r_ref[...]
        o_ref[...] = s
        o16_ref[...] = s.astype(BF16)

    blk = pl.BlockSpec((None, None, ta, b), lambda p, l, i, core_ref: (p, l, i, 0))
    return pl.pallas_call(
        body, name=name,
        out_shape=(jax.ShapeDtypeStruct((n, Lh, a, b), F32), jax.ShapeDtypeStruct((n, Lh, a, b), BF16)),
        grid_spec=pltpu.PrefetchScalarGridSpec(
            num_scalar_prefetch=1, grid=(n, Lh, a // ta),
            in_specs=[pl.BlockSpec((None, None, ta, b), lambda p, l, i, core_ref: (p, core_ref[0] * Lh + l, i, 0)), blk],
            out_specs=(blk, blk)),
        compiler_params=_cparams("parallel", "parallel", "parallel"),
    )(core, G, recv)


def _sum_chips(A, rb, place, *, name):
    _, Lh, a, b = A.shape
    ta = _rows_per_block(a, b, 512 * 1024)

    def body(place_ref, a_ref, r_ref, o_ref):
        o_ref[...] = ((a_ref[...] + r_ref[0].astype(F32)) + r_ref[1].astype(F32)) + r_ref[2].astype(F32)

    return pl.pallas_call(
        body, name=name, out_shape=jax.ShapeDtypeStruct((2 * Lh, a, b), F32),
        grid_spec=pltpu.PrefetchScalarGridSpec(
            num_scalar_prefetch=1, grid=(Lh, a // ta),
            in_specs=[pl.BlockSpec((None, None, ta, b), lambda l, i, place_ref: (place_ref[0], l, i, 0)),
                      pl.BlockSpec((N_CHIPS - 1, None, ta, b), lambda l, i, place_ref: (0, l, i, 0))],
            out_specs=pl.BlockSpec((None, ta, b), lambda l, i, place_ref: (place_ref[1] * Lh + l, i, 0))),
        compiler_params=_cparams("parallel", "parallel"),
    )(place, A, rb)


def _sum_slots(a, *, name):
    n, rows, W = a.shape
    tr = _pick(rows, 512, 8)

    def body(a_ref, o_ref):
        s = a_ref[0]
        for q in range(1, n):
            s = s + a_ref[q]
        o_ref[...] = s

    return pl.pallas_call(
        body, name=name, out_shape=jax.ShapeDtypeStruct((rows, W), F32), grid=(rows // tr,),
        in_specs=[pl.BlockSpec((n, tr, W), lambda i: (0, i, 0))], out_specs=pl.BlockSpec((tr, W), lambda i: (i, 0)),
        compiler_params=_cparams("parallel"),
    )(a)


def _adamw(w, g, m, v, *, name):
    L, a, b = w.shape
    ta = _rows_per_block(a, b, 512 * 1024)

    def body(w_ref, g_ref, m_ref, v_ref, d_ref, nm_ref, nv_ref):
        gg = g_ref[...]
        nm = ADAM_B1 * m_ref[...] + (1.0 - ADAM_B1) * gg
        nv = ADAM_B2 * v_ref[...] + (1.0 - ADAM_B2) * jnp.square(gg)
        m_hat = nm / (1.0 - ADAM_B1 ** ADAM_STEP)
        v_hat = nv / (1.0 - ADAM_B2 ** ADAM_STEP)
        d_ref[...] = -ADAM_LR * (m_hat / (jnp.sqrt(v_hat) + ADAM_EPS) + ADAM_WD * w_ref[...])
        nm_ref[...] = nm
        nv_ref[...] = nv

    blk = pl.BlockSpec((None, ta, b), lambda l, i: (l, i, 0))
    shp = jax.ShapeDtypeStruct((L, a, b), F32)
    return pl.pallas_call(
        body, name=name, out_shape=(shp, shp, shp), grid=(L, a // ta),
        in_specs=[blk, blk, blk, blk], out_specs=(blk, blk, blk),
        compiler_params=_cparams("parallel", "parallel"),
    )(w, g, m, v)


INPUT_NAMES = (("x", "mem") + WEIGHT_NAMES + ("loss_target",) + tuple("m_" + n for n in WEIGHT_NAMES)
               + tuple("v_" + n for n in WEIGHT_NAMES))


def kernel(*args):
    a = dict(zip(INPUT_NAMES, args, strict=True))
    x, mem, target = a["x"][0], a["mem"][0], a["loss_target"][0]
    sh_names = list(SHARD_BY_ROWS)
    core = lax.axis_index("c").astype(jnp.int32)
    chip = (2 * lax.axis_index("x") + lax.axis_index("y")).astype(jnp.int32)
    place = jnp.stack([chip, core])

    placed = [_place_shard(a[n], chip.reshape(1), F32 if n == "conv_w" else BF16, name="place_" + n) for n in sh_names]
    wfull = dict(zip(sh_names, _gather_weights(placed)))
    p = {n: a[n] for n in REPLICATED}
    cw = wfull.pop("conv_w")
    p["conv_w"] = cw.transpose(1, 2, 0, 3).reshape(cw.shape[1], cw.shape[2], N_CHIPS * cw.shape[3])

    loss, dx, gfull = _local_step(x, mem, target, wfull, p)
    loss = lax.psum(loss, ("x", "y", "c"))

    Gs = [gfull[n] for n in sh_names]
    rep_shapes = [a[n].shape for n in REPLICATED]
    rows_r = _round_up(-(-sum(math.prod(s) for s in rep_shapes) // PACK_W), 64)
    rep = _pack([gfull[n] for n in REPLICATED], rows_r)
    recvs, repall = _exchange_halves(Gs, rep)
    As = [_add_halves(g, r, core.reshape(1), name="add_halves_" + n) for n, g, r in zip(sh_names, Gs, recvs)]
    rbs = _exchange_chips([a16 for _, a16 in As])
    gsh = _exchange_sibling([_sum_chips(A, rb, place, name="sum_chips_" + n)
                             for n, (A, _), rb in zip(sh_names, As, rbs)])
    grep = _sum_slots(repall, name="sum_devices")

    got = {"g": dict(zip(sh_names, gsh)), "d": {}, "m": {}, "v": {}}
    for n, g in zip(sh_names, gsh):
        got["d"][n], got["m"][n], got["v"][n] = _adamw(a[n], g, a["m_" + n], a["v_" + n], name="adamw_" + n)
    packed = [_pack([a[pre + n] for n in REPLICATED], rows_r)[None] for pre in ("", "m_", "v_")]
    d_r, m_r, v_r = _adamw(packed[0], grep[None], packed[1], packed[2], name="adamw_replicated")
    for key, buf in (("g", grep), ("d", d_r[0]), ("m", m_r[0]), ("v", v_r[0])):
        got[key].update(zip(REPLICATED, _unpack(buf, rep_shapes)))
    outs = [got[key][n] for key in ("g", "d", "m", "v") for n in WEIGHT_NAMES]
    return (loss, dx[None], *outs)
```
